```python
import jax, jax.numpy as jnp
from jax import lax
import numpy as np

D_MODEL = 1024
BATCH = 8
SEQ = 2048
DEPTH = 1
DEC_BATCH = 128
DEC_SEQ = 1
PAST_LEN = 2048
PAGE_SIZE = 128

POOL_WIDTH = D_MODEL // 2
POOL_WINDOWS = (2, 4, 8, 16)
POOL_GROUPS = len(POOL_WINDOWS)
POOL_GC = POOL_WIDTH // POOL_GROUPS
POOL_STATE = max(POOL_WINDOWS) - 1
ATTN_WIDTH = D_MODEL - POOL_WIDTH
HEAD_DIM = 64
N_HEADS = ATTN_WIDTH // HEAD_DIM
MOBA_BLOCK = 256
MOBA_TOPK = 3
QBLK = 32
ROPE_THETA = 10000.0
MEM_LEN = 256
MEM_HEADS = 4
MEM_HEAD_DIM = D_MODEL // MEM_HEADS
D_FF = 4 * D_MODEL
EPS = 1e-6
NEG = -1e30

kernel_name = "hymba_pool_moba_memxattn_step"


def rmsnorm(x, g):
    xf = x.astype(jnp.float32)
    y = xf * lax.rsqrt(jnp.mean(xf * xf, axis=-1, keepdims=True) + EPS)
    return (y * g.astype(jnp.float32)).astype(x.dtype)


def rope(x, pos):
    half = x.shape[-1] // 2
    inv_freq = ROPE_THETA ** (-jnp.arange(half, dtype=jnp.float32) / half)
    ang = pos.astype(jnp.float32)[:, None] * inv_freq[None, :]
    cos = jnp.cos(ang)[None, :, None, :]
    sin = jnp.sin(ang)[None, :, None, :]
    xf = x.astype(jnp.float32)
    x1, x2 = xf[..., :half], xf[..., half:]
    return jnp.concatenate([x1 * cos - x2 * sin, x2 * cos + x1 * sin], axis=-1).astype(x.dtype)


def pool_mix(u, prev, pos0, w_pool, s_pool):
    B, S, _ = u.shape
    ext = jnp.concatenate([prev, u], axis=1)
    extf = ext.astype(jnp.float32)
    cs = jnp.concatenate([jnp.zeros_like(extf[:, :1]), jnp.cumsum(extf, axis=1)], axis=1)
    pos = pos0 + jnp.arange(S)
    outs = []
    for g, w in enumerate(POOL_WINDOWS):
        sl = slice(g * POOL_GC, (g + 1) * POOL_GC)
        wsum = cs[:, POOL_STATE + 1:, sl] - cs[:, POOL_STATE + 1 - w:POOL_STATE + 1 - w + S, sl]
        cnt = jnp.minimum(w, pos + 1).astype(jnp.float32)[None, :, None]
        outs.append(wsum / cnt - extf[:, POOL_STATE:, sl])
    d = jnp.stack(outs, axis=2)
    mixed = jnp.einsum('bsgc,gcd->bsgd', d, w_pool.astype(jnp.float32)).reshape(B, S, POOL_WIDTH)
    out = (mixed * s_pool.astype(jnp.float32)).astype(u.dtype)
    return out, ext[:, -POOL_STATE:]


def moba_attend(q, k, v, q_pos):
    B, Q, H, D = q.shape
    L = k.shape[1]
    nb = -(-L // MOBA_BLOCK)
    pad = nb * MOBA_BLOCK - L
    kb = jnp.pad(k, ((0, 0), (0, pad), (0, 0), (0, 0))).reshape(B, nb, MOBA_BLOCK, H, D).transpose(0, 3, 1, 2, 4)
    vb = jnp.pad(v, ((0, 0), (0, pad), (0, 0), (0, 0))).reshape(B, nb, MOBA_BLOCK, H, D).transpose(0, 3, 1, 2, 4)
    kmean = jnp.mean(kb.astype(jnp.float32), axis=3)
    ksel = min(MOBA_TOPK, nb)
    J = ksel + 1
    scale = D ** -0.5
    bi = jnp.arange(B)[:, None, None, None]
    hi = jnp.arange(H)[None, :, None, None]
    offs = jnp.arange(MOBA_BLOCK)

    def chunk(args):
        qc, pc = args
        C = qc.shape[1]
        qb = pc // MOBA_BLOCK
        s = jnp.einsum('bchd,bhnd->bhcn', qc.astype(jnp.float32), kmean)
        fully_past = jnp.arange(nb)[None, :] < qb[:, None]
        s = jnp.where(fully_past[None, None], s, NEG)
        _, sel = lax.top_k(s, ksel)
        own = jnp.broadcast_to(qb[None, None, :, None], sel.shape[:3] + (1,))
        idx = jnp.concatenate([sel, own], axis=-1)
        ok_blk = jnp.concatenate([sel < qb[None, None, :, None], jnp.ones_like(own, dtype=bool)], axis=-1)
        kg = kb[bi, hi, idx]
        vg = vb[bi, hi, idx]
        kpos = idx[..., None] * MOBA_BLOCK + offs
        ok = ok_blk[..., None] & (kpos <= pc[None, None, :, None, None])
        logits = jnp.einsum('bchd,bhcjkd->bhcjk', qc, kg).astype(jnp.float32) * scale
        logits = jnp.where(ok, logits, NEG)
        p = jax.nn.softmax(logits.reshape(B, H, C, J * MOBA_BLOCK), axis=-1).reshape(B, H, C, J, MOBA_BLOCK)
        return jnp.einsum('bhcjk,bhcjkd->bchd', p.astype(v.dtype), vg)

    C = min(QBLK, Q)
    nq = -(-Q // C)
    padq = nq * C - Q
    qp = jnp.pad(q, ((0, 0), (0, padq), (0, 0), (0, 0))).reshape(B, nq, C, H, D).transpose(1, 0, 2, 3, 4)
    pp = jnp.pad(q_pos, (0, padq), mode='edge').reshape(nq, C)
    out = lax.map(chunk, (qp, pp))
    return out.transpose(1, 0, 2, 3, 4).reshape(B, nq * C, H, D)[:, :Q]


def mem_kv(mem, g_mem, w_ck, w_cv):
    B = mem.shape[0]
    m = rmsnorm(mem, g_mem)
    mk = (m @ w_ck).reshape(B, MEM_LEN, MEM_HEADS, MEM_HEAD_DIM)
    mv = (m @ w_cv).reshape(B, MEM_LEN, MEM_HEADS, MEM_HEAD_DIM)
    return mk, mv


def cross_attend(h, mk, mv, w_cq, w_co):
    B, S, _ = h.shape
    q = (h @ w_cq).reshape(B, S, MEM_HEADS, MEM_HEAD_DIM)
    logits = jnp.einsum('bshd,bmhd->bhsm', q, mk).astype(jnp.float32) * (MEM_HEAD_DIM ** -0.5)
    p = jax.nn.softmax(logits, axis=-1)
    o = jnp.einsum('bhsm,bmhd->bshd', p.astype(mv.dtype), mv).reshape(B, S, D_MODEL)
    return o @ w_co


def decoder_layer(x, pos0, k_past, v_past, pool_prev, mk, mv,
                  g_mix, w_in, w_pool, s_pool, w_out, g_cross, w_cq, w_co, g_mlp, w_up, w_down):
    B, S, _ = x.shape
    pos = pos0 + jnp.arange(S)
    h = rmsnorm(x, g_mix)
    proj = h @ w_in
    o = POOL_WIDTH
    u = proj[..., :o]
    q = rope(proj[..., o:o + ATTN_WIDTH].reshape(B, S, N_HEADS, HEAD_DIM), pos)
    k = rope(proj[..., o + ATTN_WIDTH:o + 2 * ATTN_WIDTH].reshape(B, S, N_HEADS, HEAD_DIM), pos)
    v = proj[..., o + 2 * ATTN_WIDTH:].reshape(B, S, N_HEADS, HEAD_DIM)
    pool_out, pool_new = pool_mix(u, pool_prev, pos0, w_pool, s_pool)
    k_all = jnp.concatenate([k_past, k], axis=1)
    v_all = jnp.concatenate([v_past, v], axis=1)
    attn = moba_attend(q, k_all, v_all, pos).reshape(B, S, ATTN_WIDTH)
    x = x + jnp.concatenate([pool_out, attn], axis=-1) @ w_out
    x = x + cross_attend(rmsnorm(x, g_cross), mk, mv, w_cq, w_co)
    hm = rmsnorm(x, g_mlp)
    x = x + jnp.square(jax.nn.relu(hm @ w_up)) @ w_down
    return x, k, v, pool_new


def setup_inputs(seed: int = 0) -> dict:
    key = jax.random.key(seed)
    ks = jax.random.split(key, 32)
    f32 = jnp.float32

    def nrm(k, shape, scale=1.0):
        return jax.random.normal(k, shape, f32) * scale

    def gain(k, shape):
        return 1.0 + 0.02 * jax.random.normal(k, shape, f32)

    n_pages = PAST_LEN // PAGE_SIZE
    n_used = DEC_BATCH * n_pages
    n_phys = (n_used * 5) // 4
    perm = jax.random.permutation(ks[0], n_phys)
    page_table = perm[:n_used].reshape(DEC_BATCH, n_pages).astype(jnp.int32)
    d = D_MODEL
    return {
        "x_prompt": nrm(ks[1], (BATCH, SEQ, d)),
        "x_sample": nrm(ks[2], (DEC_BATCH, DEC_SEQ, d)),
        "cache_k": nrm(ks[3], (DEPTH, n_phys, PAGE_SIZE, N_HEADS, HEAD_DIM)),
        "cache_v": nrm(ks[4], (DEPTH, n_phys, PAGE_SIZE, N_HEADS, HEAD_DIM)),
        "cache_mem_k": nrm(ks[5], (DEPTH, DEC_BATCH, MEM_LEN, MEM_HEADS, MEM_HEAD_DIM)),
        "cache_mem_v": nrm(ks[6], (DEPTH, DEC_BATCH, MEM_LEN, MEM_HEADS, MEM_HEAD_DIM)),
        "state_pool": nrm(ks[7], (DEPTH, DEC_BATCH, POOL_STATE, POOL_WIDTH)),
        "page_table": page_table,
        "mem_prompt": nrm(ks[8], (BATCH, MEM_LEN, d)),
        "g_mix": gain(ks[9], (DEPTH, d)),
        "w_in": nrm(ks[10], (DEPTH, d, POOL_WIDTH + 3 * ATTN_WIDTH), d ** -0.5),
        "w_pool": nrm(ks[11], (DEPTH, POOL_GROUPS, POOL_GC, POOL_GC), POOL_GC ** -0.5),
        "s_pool": gain(ks[12], (DEPTH, POOL_WIDTH)),
        "w_out": nrm(ks[13], (DEPTH, d, d), d ** -0.5),
        "g_cross": gain(ks[14], (DEPTH, d)),
        "g_mem": gain(ks[15], (DEPTH, d)),
        "w_cq": nrm(ks[16], (DEPTH, d, d), d ** -0.5),
        "w_ck": nrm(ks[17], (DEPTH, d, d), d ** -0.5),
        "w_cv": nrm(ks[18], (DEPTH, d, d), d ** -0.5),
        "w_co": nrm(ks[19], (DEPTH, d, d), d ** -0.5),
        "g_mlp": gain(ks[20], (DEPTH, d)),
        "w_up": nrm(ks[21], (DEPTH, d, D_FF), d ** -0.5),
        "w_down": nrm(ks[22], (DEPTH, D_FF, d), D_FF ** -0.5),
        "g_final": gain(ks[23], (d,)),
    }


def reference(x_prompt, x_sample, cache_k, cache_v, cache_mem_k, cache_mem_v, state_pool, page_table,
              mem_prompt, g_mix, w_in, w_pool, s_pool, w_out, g_cross, g_mem, w_cq, w_ck, w_cv, w_co,
              g_mlp, w_up, w_down, g_final):
    bp = x_prompt.shape[0]
    bs = x_sample.shape[0]
    past_len = page_table.shape[1] * cache_k.shape[2]
    xp, xs = x_prompt, x_sample
    kp_l, vp_l, poolp_l, mkp_l, mvp_l, ks_l, vs_l, pools_l = [], [], [], [], [], [], [], []
    for l in range(DEPTH):
        lp = (g_mix[l], w_in[l], w_pool[l], s_pool[l], w_out[l], g_cross[l], w_cq[l], w_co[l],
              g_mlp[l], w_up[l], w_down[l])
        mk_p, mv_p = mem_kv(mem_prompt, g_mem[l], w_ck[l], w_cv[l])
        empty = jnp.zeros((bp, 0, N_HEADS, HEAD_DIM), xp.dtype)
        pool0 = jnp.zeros((bp, POOL_STATE, POOL_WIDTH), xp.dtype)
        xp, kp, vp, poolp = decoder_layer(xp, 0, empty, empty, pool0, mk_p, mv_p, *lp)
        past_k = cache_k[l][page_table].reshape(bs, past_len, N_HEADS, HEAD_DIM)
        past_v = cache_v[l][page_table].reshape(bs, past_len, N_HEADS, HEAD_DIM)
        xs, ks, vs, pools = decoder_layer(xs, past_len, past_k, past_v, state_pool[l],
                                          cache_mem_k[l], cache_mem_v[l], *lp)
        kp_l.append(kp); vp_l.append(vp); poolp_l.append(poolp); mkp_l.append(mk_p); mvp_l.append(mv_p)
        ks_l.append(ks); vs_l.append(vs); pools_l.append(pools)
    y_prompt = rmsnorm(xp, g_final)
    y_sample = rmsnorm(xs, g_final)
    return (y_prompt, y_sample, jnp.stack(kp_l), jnp.stack(vp_l), jnp.stack(poolp_l), jnp.stack(mkp_l),
            jnp.stack(mvp_l), jnp.stack(ks_l), jnp.stack(vs_l), jnp.stack(pools_l))
```

```python
import functools

import jax
import jax.numpy as jnp
from jax import lax
from jax.experimental import pallas as pl
from jax.experimental.pallas import tpu as pltpu

D_MODEL = 1024
POOL_WIDTH = 512
POOL_WINDOWS = (2, 4, 8, 16)
POOL_GC = POOL_WIDTH // len(POOL_WINDOWS)
POOL_STATE = max(POOL_WINDOWS) - 1
POOL_HIST = POOL_STATE + 1
ATTN_WIDTH = D_MODEL - POOL_WIDTH
HEAD_DIM = 64
N_HEADS = ATTN_WIDTH // HEAD_DIM
MOBA_BLOCK = 256
MOBA_TOPK = 3
ROPE_THETA = 10000.0
MEM_HEADS = 4
MEM_HEAD_DIM = D_MODEL // MEM_HEADS
D_FF = 4 * D_MODEL
FF_CHUNK = 1024
EPS = 1e-6
NEG = -1e30

VMEM_LIMIT_BYTES = 56 * 1024 * 1024

F32 = jnp.float32
BF16 = jnp.bfloat16
NT_DIMS = (((1,), (1,)), ((), ()))


def _params(*semantics):
    return pltpu.CompilerParams(dimension_semantics=semantics, vmem_limit_bytes=VMEM_LIMIT_BYTES)


def _resident(shape):
    zeros = (0,) * len(shape)
    return pl.BlockSpec(shape, lambda *_: zeros, pipeline_mode=pl.Buffered(1))


def _rms(x, g):
    return x * lax.rsqrt(jnp.mean(x * x, axis=-1, keepdims=True) + EPS) * g


def _rope(t, cos, sin_signed):
    width = t.shape[-1]
    lane = lax.broadcasted_iota(jnp.int32, t.shape, t.ndim - 1)
    first_half = (lane % HEAD_DIM) < (HEAD_DIM // 2)
    partner = jnp.where(first_half,
                        pltpu.roll(t, width - HEAD_DIM // 2, t.ndim - 1),
                        pltpu.roll(t, HEAD_DIM // 2, t.ndim - 1))
    return t * cos + partner * sin_signed


def _topk_member(s, valid, idx):
    axis = s.ndim - 1
    n = s.shape[axis]
    s = jnp.where(valid, s, NEG)
    rank = jnp.zeros(s.shape, jnp.int32)
    for j in range(n):
        sj = lax.slice_in_dim(s, j, j + 1, axis=axis)
        beats = (sj > s) | ((sj == s) & (j < idx))
        rank = rank + beats.astype(jnp.int32)
    return (rank < MOBA_TOPK) & valid


def _proj_prompt_kernel(x_ref, g_ref, w_ref, cos_ref, sin_ref, wp_ref, sp_ref,
                        q_ref, kt_ref, vt_ref, ktb_ref, vtb_ref, km_ref, po_ref, last_ref, ext_ref):
    i = pl.program_id(1)
    tm = x_ref.shape[1]
    h = _rms(x_ref[0], g_ref[...]).astype(BF16)
    proj = jnp.dot(h, w_ref[...], preferred_element_type=F32)
    u = proj[:, :POOL_WIDTH]
    cos = cos_ref[...]
    sin = sin_ref[...]
    q = _rope(proj[:, POOL_WIDTH:POOL_WIDTH + ATTN_WIDTH], cos, sin)
    k = _rope(proj[:, POOL_WIDTH + ATTN_WIDTH:POOL_WIDTH + 2 * ATTN_WIDTH], cos, sin)
    v = proj[:, POOL_WIDTH + 2 * ATTN_WIDTH:]
    q_ref[0] = (q * (HEAD_DIM ** -0.5)).astype(BF16)
    kt = k.T
    vt = v.T
    kt_ref[0] = kt
    vt_ref[0] = vt
    ktb_ref[0, 0] = kt.astype(BF16)
    vtb_ref[0, 0] = vt.astype(BF16)
    kmean = jnp.mean(kt, axis=1, keepdims=True)

    @pl.when(i == 0)
    def _():
        km_ref[0] = jnp.broadcast_to(kmean, km_ref.shape[1:])

    @pl.when(i > 0)
    def _():
        blk = lax.broadcasted_iota(jnp.int32, km_ref.shape[1:], 1)
        km_ref[0] = jnp.where(blk == i, kmean, km_ref[0])

    @pl.when(i == 0)
    def _():
        ext_ref[0:POOL_HIST, :] = jnp.zeros((POOL_HIST, POOL_WIDTH), F32)

    ext_ref[POOL_HIST:POOL_HIST + tm, :] = u
    pos = i * tm + lax.broadcasted_iota(jnp.int32, (tm, 1), 0)
    for g, w in enumerate(POOL_WINDOWS):
        cols = slice(g * POOL_GC, (g + 1) * POOL_GC)
        ug = u[:, cols]
        wsum = ug
        for j in range(1, w):
            wsum = wsum + ext_ref[POOL_HIST - j:POOL_HIST - j + tm, cols]
        cnt = jnp.minimum(w, pos + 1).astype(F32)
        d = wsum / cnt - ug
        mixed = jnp.dot(d.astype(BF16), wp_ref[g], preferred_element_type=F32)
        po_ref[0, :, cols] = (mixed * sp_ref[:, cols]).astype(BF16)
    tail = ext_ref[tm:tm + POOL_HIST, :]
    ext_ref[0:POOL_HIST, :] = tail
    last_ref[0] = tail


def _proj_prompt(x, g, w_in, cos, sin, w_pool, s_pool):
    b, s, d = x.shape
    tm = MOBA_BLOCK
    nq = s // tm
    row = lambda bi, i: (bi, i, 0)
    col = lambda bi, i: (bi, 0, i)
    blk = lambda bi, i: (bi, i, 0, 0)
    return pl.pallas_call(
        _proj_prompt_kernel,
        grid=(b, nq),
        in_specs=[
            pl.BlockSpec((1, tm, d), row),
            _resident((1, d)),
            _resident(w_in.shape),
            pl.BlockSpec((tm, ATTN_WIDTH), lambda bi, i: (i, 0)),
            pl.BlockSpec((tm, ATTN_WIDTH), lambda bi, i: (i, 0)),
            _resident(w_pool.shape),
            _resident((1, POOL_WIDTH)),
        ],
        out_specs=[
            pl.BlockSpec((1, tm, ATTN_WIDTH), row),
            pl.BlockSpec((1, ATTN_WIDTH, tm), col),
            pl.BlockSpec((1, ATTN_WIDTH, tm), col),
            pl.BlockSpec((1, 1, ATTN_WIDTH, tm), blk),
            pl.BlockSpec((1, 1, ATTN_WIDTH, tm), blk),
            pl.BlockSpec((1, ATTN_WIDTH, nq), lambda bi, i: (bi, 0, 0)),
            pl.BlockSpec((1, tm, POOL_WIDTH), row),
            pl.BlockSpec((1, POOL_HIST, POOL_WIDTH), lambda bi, i: (bi, 0, 0)),
        ],
        out_shape=[
            jax.ShapeDtypeStruct((b, s, ATTN_WIDTH), BF16),
            jax.ShapeDtypeStruct((b, ATTN_WIDTH, s), F32),
            jax.ShapeDtypeStruct((b, ATTN_WIDTH, s), F32),
            jax.ShapeDtypeStruct((b, nq, ATTN_WIDTH, tm), BF16),
            jax.ShapeDtypeStruct((b, nq, ATTN_WIDTH, tm), BF16),
            jax.ShapeDtypeStruct((b, ATTN_WIDTH, nq), F32),
            jax.ShapeDtypeStruct((b, s, POOL_WIDTH), BF16),
            jax.ShapeDtypeStruct((b, POOL_HIST, POOL_WIDTH), F32),
        ],
        scratch_shapes=[pltpu.VMEM((POOL_HIST + tm, POOL_WIDTH), F32)],
        compiler_params=_params("arbitrary", "arbitrary"),
        name="proj_prompt",
    )(x, g, w_in, cos, sin, w_pool, s_pool)


def _moba_prompt_kernel(q_ref, ktb_ref, vtb_ref, km_ref, o_ref):
    i = pl.program_id(1)
    tq = q_ref.shape[1]
    nb = km_ref.shape[2]
    blk_id = lax.broadcasted_iota(jnp.int32, (tq, nb), 1)
    fully_past = blk_id < i
    r = lax.broadcasted_iota(jnp.int32, (tq, MOBA_BLOCK), 0)
    c = lax.broadcasted_iota(jnp.int32, (tq, MOBA_BLOCK), 1)
    causal = c <= r
    for h in range(N_HEADS):
        sl = slice(h * HEAD_DIM, (h + 1) * HEAD_DIM)
        qh = q_ref[0, :, sl]
        s = jnp.dot(qh, km_ref[0, sl, :].astype(BF16), preferred_element_type=F32)
        sel = _topk_member(s, fully_past, blk_id).astype(F32)

        lg = jnp.dot(qh, ktb_ref[0, i, sl, :], preferred_element_type=F32)
        lg = jnp.where(causal, lg, NEG)
        m0 = jnp.max(lg, axis=-1, keepdims=True)
        p = jnp.exp(lg - m0)
        l0 = jnp.sum(p, axis=-1, keepdims=True)
        acc0 = lax.dot_general(p.astype(BF16), vtb_ref[0, i, sl, :], NT_DIMS, preferred_element_type=F32)

        def past_block(j, carry, qh=qh, sel=sel, sl=sl):
            m, l, acc = carry
            lg = jnp.dot(qh, ktb_ref[0, j, sl, :], preferred_element_type=F32)
            chosen = jnp.sum(jnp.where(blk_id == j, sel, 0.0), axis=-1, keepdims=True) > 0.0
            lg = jnp.where(chosen, lg, NEG)
            m_new = jnp.maximum(m, jnp.max(lg, axis=-1, keepdims=True))
            a = jnp.exp(m - m_new)
            p = jnp.exp(lg - m_new)
            l = a * l + jnp.sum(p, axis=-1, keepdims=True)
            acc = a * acc + lax.dot_general(p.astype(BF16), vtb_ref[0, j, sl, :], NT_DIMS,
                                            preferred_element_type=F32)
            return m_new, l, acc

        _, l, acc = lax.fori_loop(0, i, past_block, (m0, l0, acc0))
        o_ref[0, :, sl] = (acc / l).astype(BF16)


def _moba_prompt(qb, ktb, vtb, kmean_t):
    b, s, w = qb.shape
    nq = s // MOBA_BLOCK
    return pl.pallas_call(
        _moba_prompt_kernel,
        grid=(b, nq),
        in_specs=[
            pl.BlockSpec((1, MOBA_BLOCK, w), lambda bi, i: (bi, i, 0)),
            pl.BlockSpec((1, nq, w, MOBA_BLOCK), lambda bi, i: (bi, 0, 0, 0)),
            pl.BlockSpec((1, nq, w, MOBA_BLOCK), lambda bi, i: (bi, 0, 0, 0)),
            pl.BlockSpec((1, w, nq), lambda bi, i: (bi, 0, 0)),
        ],
        out_specs=pl.BlockSpec((1, MOBA_BLOCK, w), lambda bi, i: (bi, i, 0)),
        out_shape=jax.ShapeDtypeStruct((b, s, w), BF16),
        compiler_params=_params("arbitrary", "arbitrary"),
        name="moba_prompt",
    )(qb, ktb, vtb, kmean_t)


def _mem_kv_kernel(m_ref, g_ref, wk_ref, wv_ref, mk_ref, mv_ref, mkb_ref, mvb_ref):
    h = _rms(m_ref[...], g_ref[...]).astype(BF16)
    mk = jnp.dot(h, wk_ref[...], preferred_element_type=F32)
    mv = jnp.dot(h, wv_ref[...], preferred_element_type=F32)
    mk_ref[...] = mk
    mv_ref[...] = mv
    mkb_ref[...] = mk.astype(BF16)
    mvb_ref[...] = mv.astype(BF16)


def _mem_kv(mem, g, w_ck, w_cv):
    rows, d = mem.shape
    tm = 512
    row = lambda i: (i, 0)
    return pl.pallas_call(
        _mem_kv_kernel,
        grid=(rows // tm,),
        in_specs=[pl.BlockSpec((tm, d), row), _resident((1, d)), _resident(w_ck.shape), _resident(w_cv.shape)],
        out_specs=[pl.BlockSpec((tm, d), row)] * 4,
        out_shape=[jax.ShapeDtypeStruct((rows, d), F32)] * 2 + [jax.ShapeDtypeStruct((rows, d), BF16)] * 2,
        compiler_params=_params("arbitrary"),
        name="mem_kv",
    )(mem, g, w_ck, w_cv)


def _mix_out_kernel(x_ref, po_ref, at_ref, wo_ref, g_ref, wq_ref, x1_ref, qc_ref):
    cat = jnp.concatenate([po_ref[...], at_ref[...]], axis=-1)
    x1 = x_ref[...] + jnp.dot(cat, wo_ref[...], preferred_element_type=F32)
    x1_ref[...] = x1
    hq = _rms(x1, g_ref[...]).astype(BF16)
    qc = jnp.dot(hq, wq_ref[...], preferred_element_type=F32)
    qc_ref[...] = (qc * (MEM_HEAD_DIM ** -0.5)).astype(BF16)


def _mix_out(x, po, at, w_out, g_cross, w_cq):
    rows, d = x.shape
    tm = min(512, rows)
    row = lambda i: (i, 0)
    return pl.pallas_call(
        _mix_out_kernel,
        grid=(rows // tm,),
        in_specs=[pl.BlockSpec((tm, d), row), pl.BlockSpec((tm, POOL_WIDTH), row),
                  pl.BlockSpec((tm, ATTN_WIDTH), row), _resident(w_out.shape), _resident((1, d)),
                  _resident(w_cq.shape)],
        out_specs=[pl.BlockSpec((tm, d), row), pl.BlockSpec((tm, d), row)],
        out_shape=[jax.ShapeDtypeStruct((rows, d), F32), jax.ShapeDtypeStruct((rows, d), BF16)],
        compiler_params=_params("arbitrary"),
        name="mix_out",
    )(x, po, at, w_out, g_cross, w_cq)


def _cross_kernel(q_ref, mk_ref, mv_ref, o_ref):
    q = q_ref[0]
    for h in range(MEM_HEADS):
        sl = slice(h * MEM_HEAD_DIM, (h + 1) * MEM_HEAD_DIM)
        lg = lax.dot_general(q[:, sl], mk_ref[0, :, sl].astype(BF16), NT_DIMS, preferred_element_type=F32)
        p = jnp.exp(lg - jnp.max(lg, axis=-1, keepdims=True))
        l = jnp.sum(p, axis=-1, keepdims=True)
        o = jnp.dot(p.astype(BF16), mv_ref[0, :, sl].astype(BF16), preferred_element_type=F32)
        o_ref[0, :, sl] = (o / l).astype(BF16)


def _cross(qc, mk, mv, tm):
    b, s, d = qc.shape
    m = mk.shape[1]
    return pl.pallas_call(
        _cross_kernel,
        grid=(b, s // tm),
        in_specs=[pl.BlockSpec((1, tm, d), lambda bi, i: (bi, i, 0)),
                  pl.BlockSpec((1, m, d), lambda bi, i: (bi, 0, 0)),
                  pl.BlockSpec((1, m, d), lambda bi, i: (bi, 0, 0))],
        out_specs=pl.BlockSpec((1, tm, d), lambda bi, i: (bi, i, 0)),
        out_shape=jax.ShapeDtypeStruct((b, s, d), BF16),
        compiler_params=_params("arbitrary", "arbitrary"),
        name="cross_attn",
    )(qc, mk, mv)


def _mlp_kernel(x1_ref, o_ref, wco_ref, gm_ref, wu_ref, wd_ref, gf_ref, y_ref):
    x2 = x1_ref[...] + jnp.dot(o_ref[...], wco_ref[...], preferred_element_type=F32)
    hm = _rms(x2, gm_ref[...]).astype(BF16)
    acc = x2
    for c in range(D_FF // FF_CHUNK):
        cs = slice(c * FF_CHUNK, (c + 1) * FF_CHUNK)
        a = jnp.dot(hm, wu_ref[:, cs], preferred_element_type=F32)
        a = jnp.square(jnp.maximum(a, 0.0)).astype(BF16)
        acc = acc + jnp.dot(a, wd_ref[cs, :], preferred_element_type=F32)
    y_ref[...] = _rms(acc, gf_ref[...])


def _mlp(x1, o, w_co, g_mlp, w_up, w_down, g_final):
    rows, d = x1.shape
    tm = min(512, rows)
    row = lambda i: (i, 0)
    return pl.pallas_call(
        _mlp_kernel,
        grid=(rows // tm,),
        in_specs=[pl.BlockSpec((tm, d), row), pl.BlockSpec((tm, d), row), _resident(w_co.shape),
                  _resident((1, d)), _resident(w_up.shape), _resident(w_down.shape), _resident((1, d))],
        out_specs=pl.BlockSpec((tm, d), row),
        out_shape=jax.ShapeDtypeStruct((rows, d), F32),
        compiler_params=_params("arbitrary"),
        name="mlp",
    )(x1, o, w_co, g_mlp, w_up, w_down, g_final)


def _proj_sample_kernel(x_ref, g_ref, w_ref, cos_ref, sin_ref, st_ref, wp_ref, sp_ref,
                        q_ref, k_ref, v_ref, kt_ref, vt_ref, u_ref, po_ref, *, n_prev):
    h = _rms(x_ref[...], g_ref[...]).astype(BF16)
    proj = jnp.dot(h, w_ref[...], preferred_element_type=F32)
    u = proj[:, :POOL_WIDTH]
    cos = cos_ref[...]
    sin = sin_ref[...]
    q_ref[...] = _rope(proj[:, POOL_WIDTH:POOL_WIDTH + ATTN_WIDTH], cos, sin) * (HEAD_DIM ** -0.5)
    k = _rope(proj[:, POOL_WIDTH + ATTN_WIDTH:POOL_WIDTH + 2 * ATTN_WIDTH], cos, sin)
    v = proj[:, POOL_WIDTH + 2 * ATTN_WIDTH:]
    k_ref[...] = k
    v_ref[...] = v
    kt_ref[...] = k.T
    vt_ref[...] = v.T
    u_ref[...] = u
    for g, w in enumerate(POOL_WINDOWS):
        cols = slice(g * POOL_GC, (g + 1) * POOL_GC)
        ug = u[:, cols]
        wsum = ug
        for j in range(1, w):
            wsum = wsum + st_ref[POOL_STATE - j, :, cols]
        d = wsum / float(min(w, n_prev + 1)) - ug
        mixed = jnp.dot(d.astype(BF16), wp_ref[g], preferred_element_type=F32)
        po_ref[:, cols] = (mixed * sp_ref[:, cols]).astype(BF16)


def _proj_sample(x, g, w_in, cos, sin, state_t, w_pool, s_pool, n_prev):
    rows, d = x.shape
    wide = jax.ShapeDtypeStruct((rows, ATTN_WIDTH), F32)
    tall = jax.ShapeDtypeStruct((ATTN_WIDTH, rows), F32)
    full = lambda a: pl.BlockSpec(a.shape, lambda i, nd=len(a.shape): (0,) * nd)
    args = (x, g, w_in, cos, sin, state_t, w_pool, s_pool)
    outs = [wide, wide, wide, tall, tall, wide, jax.ShapeDtypeStruct((rows, POOL_WIDTH), BF16)]
    return pl.pallas_call(
        functools.partial(_proj_sample_kernel, n_prev=n_prev),
        grid=(1,),
        in_specs=[full(a) for a in args],
        out_specs=[full(o) for o in outs],
        out_shape=outs,
        compiler_params=_params("arbitrary"),
        name="proj_sample",
    )(*args)


def _moba_sample_kernel(pt_ref, q_ref, kn_ref, vn_ref, *rest, n_pages, page):
    del pt_ref
    k_pages = rest[:n_pages]
    v_pages = rest[n_pages:2 * n_pages]
    o_ref = rest[2 * n_pages]
    kb_ref, vb_ref = rest[2 * n_pages + 1:]
    past = n_pages * page
    nb = past // MOBA_BLOCK
    for p in range(n_pages):
        kb_ref[:, p * page:(p + 1) * page] = k_pages[p][0].astype(BF16)
        vb_ref[:, p * page:(p + 1) * page] = v_pages[p][0].astype(BF16)

    head_of_lane = lax.broadcasted_iota(jnp.int32, (N_HEADS, ATTN_WIDTH), 1) // HEAD_DIM
    head = lax.broadcasted_iota(jnp.int32, (N_HEADS, ATTN_WIDTH), 0)
    mine = head_of_lane == head
    q_heads = jnp.where(mine, q_ref[0], 0.0)
    lg = jnp.dot(q_heads.astype(BF16), kb_ref[...], preferred_element_type=F32)

    s = jnp.concatenate(
        [jnp.sum(lg[:, j * MOBA_BLOCK:(j + 1) * MOBA_BLOCK], axis=-1, keepdims=True) for j in range(nb)], axis=-1)
    blk_id = lax.broadcasted_iota(jnp.int32, (N_HEADS, nb), 1)
    sel = _topk_member(s, blk_id >= 0, blk_id)
    key_blk = lax.broadcasted_iota(jnp.int32, (N_HEADS, past), 1) // MOBA_BLOCK
    chosen = jnp.zeros((N_HEADS, past), jnp.bool_)
    for j in range(nb):
        chosen = chosen | ((key_blk == j) & sel[:, j:j + 1])
    lg = jnp.where(chosen, lg, NEG)

    lg_new = jnp.sum(q_heads * kn_ref[0], axis=-1, keepdims=True)
    m = jnp.maximum(jnp.max(lg, axis=-1, keepdims=True), lg_new)
    p = jnp.exp(lg - m)
    p_new = jnp.exp(lg_new - m)
    l = jnp.sum(p, axis=-1, keepdims=True) + p_new
    acc = lax.dot_general(p.astype(BF16), vb_ref[...], NT_DIMS, preferred_element_type=F32) + p_new * vn_ref[0]
    out = jnp.where(mine, acc / l, 0.0)
    o_ref[0] = jnp.sum(out, axis=0, keepdims=True).astype(BF16)


def _moba_sample(page_table, q, k_new, v_new, cache_k, cache_v):
    b, n_pages = page_table.shape
    _, w, page = cache_k.shape
    past = n_pages * page
    tok = pl.BlockSpec((1, 1, w), lambda bi, pt: (bi, 0, 0))
    page_spec = lambda p: pl.BlockSpec((1, w, page), lambda bi, pt: (pt[bi, p], 0, 0))
    grid_spec = pltpu.PrefetchScalarGridSpec(
        num_scalar_prefetch=1,
        grid=(b,),
        in_specs=[tok, tok, tok] + [page_spec(p) for p in range(n_pages)] * 2,
        out_specs=tok,
        scratch_shapes=[pltpu.VMEM((w, past), BF16), pltpu.VMEM((w, past), BF16)],
    )
    return pl.pallas_call(
        functools.partial(_moba_sample_kernel, n_pages=n_pages, page=page),
        grid_spec=grid_spec,
        out_shape=jax.ShapeDtypeStruct((b, 1, w), BF16),
        compiler_params=_params("arbitrary"),
        name="moba_sample",
    )(page_table, q, k_new, v_new, *([cache_k] * n_pages), *([cache_v] * n_pages))


def _rope_tables(pos):
    half = HEAD_DIM // 2
    inv_freq = ROPE_THETA ** (-jnp.arange(half, dtype=F32) / half)
    ang = pos.astype(F32)[:, None] * inv_freq[None, :]
    cos = jnp.cos(ang)
    sin = jnp.sin(ang)
    cos_full = jnp.tile(jnp.concatenate([cos, cos], axis=-1), (1, N_HEADS))
    sin_signed = jnp.tile(jnp.concatenate([-sin, sin], axis=-1), (1, N_HEADS))
    return cos_full, sin_signed


def _post_mixer(x, po, at, mk, mv, cross_tile, lp):
    b, s, d = x.shape
    x1, qc = _mix_out(x.reshape(b * s, d), po.reshape(b * s, -1), at.reshape(b * s, -1),
                      lp["w_out"], lp["g_cross"], lp["w_cq"])
    if s < cross_tile:
        qc3 = jnp.broadcast_to(qc.reshape(b, s, 1, d), (b, s, cross_tile // s, d)).reshape(b, cross_tile, d)
        o = _cross(qc3, mk, mv, cross_tile)[:, :s]
    else:
        o = _cross(qc.reshape(b, s, d), mk, mv, cross_tile)
    y = _mlp(x1, o.reshape(b * s, d), lp["w_co"], lp["g_mlp"], lp["w_up"], lp["w_down"], lp["g_final"])
    return y.reshape(b, s, d)


def kernel(x_prompt, x_sample, cache_k, cache_v, cache_mem_k, cache_mem_v, state_pool, page_table, mem_prompt,
           g_mix, w_in, w_pool, s_pool, w_out, g_cross, g_mem, w_cq, w_ck, w_cv, w_co, g_mlp, w_up, w_down,
           g_final):
    depth = w_in.shape[0]
    assert depth == 1, "one decoder layer"
    bp, seq, d = x_prompt.shape
    bs, dec_seq, _ = x_sample.shape
    assert dec_seq == 1 and seq % MOBA_BLOCK == 0
    n_phys, page = cache_k.shape[1], cache_k.shape[2]
    n_pages = page_table.shape[1]
    past_len = n_pages * page
    assert past_len % MOBA_BLOCK == 0 and past_len >= POOL_STATE
    mem_len = mem_prompt.shape[1]

    l = 0
    row = lambda a: a.reshape(1, -1)
    lp = dict(w_out=w_out[l].astype(BF16), g_cross=row(g_cross[l]), w_cq=w_cq[l].astype(BF16),
              w_co=w_co[l].astype(BF16), g_mlp=row(g_mlp[l]), w_up=w_up[l].astype(BF16),
              w_down=w_down[l].astype(BF16), g_final=row(g_final))
    w_in_b = w_in[l].astype(BF16)
    w_pool_b = w_pool[l].astype(BF16)
    g_mix_r = row(g_mix[l])
    s_pool_r = row(s_pool[l])

    cos_p, sin_p = _rope_tables(jnp.arange(seq))
    qb, kt_p, vt_p, ktb, vtb, kmean_t, po_p, last_p = _proj_prompt(x_prompt, g_mix_r, w_in_b, cos_p, sin_p,
                                                                   w_pool_b, s_pool_r)
    at_p = _moba_prompt(qb, ktb, vtb, kmean_t)
    mk_p, mv_p, mkb, mvb = _mem_kv(mem_prompt.reshape(bp * mem_len, d), row(g_mem[l]), w_ck[l].astype(BF16),
                                   w_cv[l].astype(BF16))
    y_prompt = _post_mixer(x_prompt, po_p, at_p, mkb.reshape(bp, mem_len, d), mvb.reshape(bp, mem_len, d), 512, lp)

    cos_s, sin_s = _rope_tables(jnp.full((1,), past_len))
    state_t = jnp.transpose(state_pool[l], (1, 0, 2))
    q_s, k_s, v_s, kt_s, vt_s, u_s, po_s = _proj_sample(x_sample.reshape(bs, d), g_mix_r, w_in_b, cos_s, sin_s,
                                                        state_t, w_pool_b, s_pool_r, past_len)
    tok = lambda a: a.reshape(bs, 1, -1)
    pages_t = lambda c: jnp.transpose(c, (0, 2, 3, 1)).reshape(n_phys, ATTN_WIDTH, page)
    at_s = _moba_sample(page_table, tok(q_s), tok(k_s), tok(v_s), pages_t(cache_k[l]), pages_t(cache_v[l]))
    y_sample = _post_mixer(x_sample, tok(po_s), at_s, cache_mem_k[l].reshape(bs, mem_len, d),
                           cache_mem_v[l].reshape(bs, mem_len, d), 8, lp)

    def heads(t, b_):
        return jnp.transpose(t.reshape(b_, N_HEADS, HEAD_DIM, -1), (0, 3, 1, 2))[None]

    mem_heads = lambda a: a.reshape(1, bp, mem_len, MEM_HEADS, MEM_HEAD_DIM)
    pool_sample = jnp.transpose(jnp.concatenate([state_t[1:], u_s[None]], axis=0), (1, 0, 2))[None]
    return (y_prompt, y_sample, heads(kt_p, bp), heads(vt_p, bp), last_p[:, 1:][None], mem_heads(mk_p),
            mem_heads(mv_p), heads(kt_s.T.reshape(bs, ATTN_WIDTH, 1), bs), heads(vt_s.T.reshape(bs, ATTN_WIDTH, 1), bs),
            pool_sample)
```

```python
import functools

import jax
import jax.numpy as jnp
from jax import lax
from jax.experimental import pallas as pl
from jax.experimental.pallas import tpu as pltpu

D_MODEL = 1024
POOL_WIDTH = 512
POOL_WINDOWS = (2, 4, 8, 16)
POOL_GC = POOL_WIDTH // len(POOL_WINDOWS)
POOL_STATE = max(POOL_WINDOWS) - 1
POOL_HIST = POOL_STATE + 1
ATTN_WIDTH = D_MODEL - POOL_WIDTH
HEAD_DIM = 64
N_HEADS = ATTN_WIDTH // HEAD_DIM
MOBA_BLOCK = 256
MOBA_TOPK = 3
ROPE_THETA = 10000.0
MEM_HEADS = 4
MEM_HEAD_DIM = D_MODEL // MEM_HEADS
D_FF = 4 * D_MODEL
FF_CHUNK = 1024
EPS = 1e-6
NEG = -1e30

VMEM_LIMIT_BYTES = 56 * 1024 * 1024

F32 = jnp.float32
BF16 = jnp.bfloat16
NT_DIMS = (((1,), (1,)), ((), ()))


def _params(*semantics):
    return pltpu.CompilerParams(dimension_semantics=semantics, vmem_limit_bytes=VMEM_LIMIT_BYTES)


def _resident(shape):
    zeros = (0,) * len(shape)
    return pl.BlockSpec(shape, lambda *_: zeros, pipeline_mode=pl.Buffered(1))


def _rms(x, g):
    return x * lax.rsqrt(jnp.mean(x * x, axis=-1, keepdims=True) + EPS) * g


def _rope(t, cos, sin_signed):
    width = t.shape[-1]
    lane = lax.broadcasted_iota(jnp.int32, t.shape, t.ndim - 1)
    first_half = (lane % HEAD_DIM) < (HEAD_DIM // 2)
    partner = jnp.where(first_half,
                        pltpu.roll(t, width - HEAD_DIM // 2, t.ndim - 1),
                        pltpu.roll(t, HEAD_DIM // 2, t.ndim - 1))
    return t * cos + partner * sin_signed


def _topk_member(s, valid, idx, axis):
    n = s.shape[axis]
    s = jnp.where(valid, s, NEG)
    rank = jnp.zeros(s.shape, jnp.int32)
    for j in range(n):
        sj = lax.slice_in_dim(s, j, j + 1, axis=axis)
        beats = (sj > s) | ((sj == s) & (j < idx))
        rank = rank + beats.astype(jnp.int32)
    return (rank < MOBA_TOPK) & valid


def _proj_prompt_kernel(x_ref, g_ref, w_ref, cos_ref, sin_ref, wp_ref, sp_ref,
                        qt_ref, kt_ref, vt_ref, kb_ref, vtb_ref, km_ref, po_ref, last_ref, ext_ref):
    i = pl.program_id(1)
    tm = x_ref.shape[1]
    h = _rms(x_ref[0], g_ref[...]).astype(BF16)
    proj = jnp.dot(h, w_ref[...], preferred_element_type=F32)
    u = proj[:, :POOL_WIDTH]
    cos = cos_ref[...]
    sin = sin_ref[...]
    q = _rope(proj[:, POOL_WIDTH:POOL_WIDTH + ATTN_WIDTH], cos, sin)
    k = _rope(proj[:, POOL_WIDTH + ATTN_WIDTH:POOL_WIDTH + 2 * ATTN_WIDTH], cos, sin)
    v = proj[:, POOL_WIDTH + 2 * ATTN_WIDTH:]
    qt_ref[0] = (q * (HEAD_DIM ** -0.5)).T.astype(BF16)
    vt = v.T
    kt_ref[0] = k.T
    vt_ref[0] = vt
    kb_ref[0] = k.astype(BF16)
    vtb_ref[0, 0] = vt.astype(BF16)
    km_ref[0, 0] = jnp.mean(k, axis=0, keepdims=True)

    @pl.when(i == 0)
    def _():
        ext_ref[0:POOL_HIST, :] = jnp.zeros((POOL_HIST, POOL_WIDTH), F32)

    ext_ref[POOL_HIST:POOL_HIST + tm, :] = u
    pos = i * tm + lax.broadcasted_iota(jnp.int32, (tm, 1), 0)
    for g, w in enumerate(POOL_WINDOWS):
        cols = slice(g * POOL_GC, (g + 1) * POOL_GC)
        ug = u[:, cols]
        wsum = ug
        for j in range(1, w):
            wsum = wsum + ext_ref[POOL_HIST - j:POOL_HIST - j + tm, cols]
        cnt = jnp.minimum(w, pos + 1).astype(F32)
        d = wsum / cnt - ug
        mixed = jnp.dot(d.astype(BF16), wp_ref[g], preferred_element_type=F32)
        po_ref[0, :, cols] = (mixed * sp_ref[:, cols]).astype(BF16)
    tail = ext_ref[tm:tm + POOL_HIST, :]
    ext_ref[0:POOL_HIST, :] = tail
    last_ref[0] = tail


def _proj_prompt(x, g, w_in, cos, sin, w_pool, s_pool):
    b, s, d = x.shape
    tm = MOBA_BLOCK
    nq = s // tm
    row = lambda bi, i: (bi, i, 0)
    col = lambda bi, i: (bi, 0, i)
    blk = lambda bi, i: (bi, i, 0, 0)
    return pl.pallas_call(
        _proj_prompt_kernel,
        grid=(b, nq),
        in_specs=[
            pl.BlockSpec((1, tm, d), row),
            _resident((1, d)),
            _resident(w_in.shape),
            pl.BlockSpec((tm, ATTN_WIDTH), lambda bi, i: (i, 0)),
            pl.BlockSpec((tm, ATTN_WIDTH), lambda bi, i: (i, 0)),
            _resident(w_pool.shape),
            _resident((1, POOL_WIDTH)),
        ],
        out_specs=[
            pl.BlockSpec((1, ATTN_WIDTH, tm), col),
            pl.BlockSpec((1, ATTN_WIDTH, tm), col),
            pl.BlockSpec((1, ATTN_WIDTH, tm), col),
            pl.BlockSpec((1, tm, ATTN_WIDTH), row),
            pl.BlockSpec((1, 1, ATTN_WIDTH, tm), blk),
            pl.BlockSpec((1, 1, 1, ATTN_WIDTH), blk),
            pl.BlockSpec((1, tm, POOL_WIDTH), row),
            pl.BlockSpec((1, POOL_HIST, POOL_WIDTH), lambda bi, i: (bi, 0, 0)),
        ],
        out_shape=[
            jax.ShapeDtypeStruct((b, ATTN_WIDTH, s), BF16),
            jax.ShapeDtypeStruct((b, ATTN_WIDTH, s), F32),
            jax.ShapeDtypeStruct((b, ATTN_WIDTH, s), F32),
            jax.ShapeDtypeStruct((b, s, ATTN_WIDTH), BF16),
            jax.ShapeDtypeStruct((b, nq, ATTN_WIDTH, tm), BF16),
            jax.ShapeDtypeStruct((b, nq, 1, ATTN_WIDTH), F32),
            jax.ShapeDtypeStruct((b, s, POOL_WIDTH), BF16),
            jax.ShapeDtypeStruct((b, POOL_HIST, POOL_WIDTH), F32),
        ],
        scratch_shapes=[pltpu.VMEM((POOL_HIST + tm, POOL_WIDTH), F32)],
        compiler_params=_params("arbitrary", "arbitrary"),
        name="proj_prompt",
    )(x, g, w_in, cos, sin, w_pool, s_pool)


SUBLANES = 8
KEY_CHUNK = 128


def _all_sublanes(t, combine):
    for s in (4, 2, 1):
        t = combine(t, pltpu.roll(t, s, 0))
    return t


def _moba_prompt_kernel(qt_ref, kb_ref, vtb_ref, km_ref, o_ref, acc_ref, m_ref, l_ref, take_ref, lg_ref):
    i = pl.program_id(1)
    tq = qt_ref.shape[2]
    nb = km_ref.shape[1]

    km_rows = jnp.concatenate([km_ref[0]] * N_HEADS, axis=0)
    row_head = lax.broadcasted_iota(jnp.int32, km_rows.shape, 0) // nb
    lane_head = lax.broadcasted_iota(jnp.int32, km_rows.shape, 1) // HEAD_DIM
    km_heads = jnp.where(row_head == lane_head, km_rows, 0.0).astype(BF16)
    s = jnp.dot(km_heads, qt_ref[0], preferred_element_type=F32).reshape(N_HEADS, nb, tq)
    blk = lax.broadcasted_iota(jnp.int32, s.shape, 1)
    sel = _topk_member(s, blk < i, blk, axis=1).astype(F32)

    for h in range(N_HEADS):
        for jb in range(nb):
            take_ref[h, jb] = jnp.broadcast_to(sel[h, jb:jb + 1, :], (SUBLANES, tq))

    def head_rows(h):
        return slice(h * HEAD_DIM, (h + 1) * HEAD_DIM)

    def stage_logits(h, j):
        off = pl.multiple_of(j * MOBA_BLOCK, MOBA_BLOCK)
        lg_ref[h % 2] = jnp.dot(kb_ref[0, pl.ds(off, MOBA_BLOCK), head_rows(h)], qt_ref[0, head_rows(h), :],
                                preferred_element_type=F32)

    def weighted_values(h, j, p3):
        p = p3.reshape(MOBA_BLOCK, tq).astype(BF16)
        pv = jnp.dot(vtb_ref[0, j, head_rows(h), :], p, preferred_element_type=F32)
        return pv.reshape(HEAD_DIM // SUBLANES, SUBLANES, tq)

    def tiles(t):
        return t.reshape(t.shape[0] // SUBLANES, SUBLANES, tq)

    def block_pass(j, own):
        stage_logits(0, j)
        for h in range(N_HEADS):
            if h + 1 < N_HEADS:
                stage_logits(h + 1, j)
            lg = lg_ref[h % 2]
            if own:
                kk = lax.broadcasted_iota(jnp.int32, lg.shape, 0)
                qq = lax.broadcasted_iota(jnp.int32, lg.shape, 1)
                lg3 = tiles(jnp.where(kk <= qq, lg, NEG))
                m = _all_sublanes(jnp.max(lg3, axis=0), jnp.maximum)
                p3 = jnp.exp(lg3 - m[None])
                m_ref[h] = m
                l_ref[h] = _all_sublanes(jnp.sum(p3, axis=0), jnp.add)
                acc_ref[head_rows(h), :] = weighted_values(h, j, p3).reshape(HEAD_DIM, tq)
            else:
                lg3 = tiles(lg)
                take = take_ref[h, j] > 0.0
                m_old = m_ref[h]
                l_old = l_ref[h]
                acc_old = tiles(acc_ref[head_rows(h), :])
                m = jnp.maximum(m_old, _all_sublanes(jnp.max(lg3, axis=0), jnp.maximum))
                a = jnp.exp(m_old - m)
                p3 = jnp.exp(lg3 - m[None])
                l = a * l_old + _all_sublanes(jnp.sum(p3, axis=0), jnp.add)
                acc3 = a[None] * acc_old + weighted_values(h, j, p3)
                m_ref[h] = jnp.where(take, m, m_old)
                l_ref[h] = jnp.where(take, l, l_old)
                acc_ref[head_rows(h), :] = jnp.where(take[None], acc3, acc_old).reshape(HEAD_DIM, tq)

    block_pass(i, own=True)

    def past_block(j, carry):
        block_pass(j, own=False)
        return carry

    lax.fori_loop(0, i, past_block, 0)
    for h in range(N_HEADS):
        rows = slice(h * HEAD_DIM, (h + 1) * HEAD_DIM)
        acc_ref[rows, :] = (tiles(acc_ref[rows, :]) / l_ref[h][None]).reshape(HEAD_DIM, tq)
    o_ref[0] = acc_ref[...].T.astype(BF16)


def _moba_prompt(qtb, kb, vtb, kmean):
    b, w, s = qtb.shape
    nq = s // MOBA_BLOCK
    return pl.pallas_call(
        _moba_prompt_kernel,
        grid=(b, nq),
        in_specs=[
            pl.BlockSpec((1, w, MOBA_BLOCK), lambda bi, i: (bi, 0, i)),
            pl.BlockSpec((1, s, w), lambda bi, i: (bi, 0, 0)),
            pl.BlockSpec((1, nq, w, MOBA_BLOCK), lambda bi, i: (bi, 0, 0, 0)),
            pl.BlockSpec((1, nq, w), lambda bi, i: (bi, 0, 0)),
        ],
        out_specs=pl.BlockSpec((1, MOBA_BLOCK, w), lambda bi, i: (bi, i, 0)),
        out_shape=jax.ShapeDtypeStruct((b, s, w), BF16),
        scratch_shapes=[pltpu.VMEM((w, MOBA_BLOCK), F32),
                        pltpu.VMEM((N_HEADS, SUBLANES, MOBA_BLOCK), F32),
                        pltpu.VMEM((N_HEADS, SUBLANES, MOBA_BLOCK), F32),
                        pltpu.VMEM((N_HEADS, nq, SUBLANES, MOBA_BLOCK), F32),
                        pltpu.VMEM((2, MOBA_BLOCK, MOBA_BLOCK), F32)],
        compiler_params=_params("arbitrary", "arbitrary"),
        name="moba_prompt",
    )(qtb, kb, vtb, kmean)


def _mem_kv_kernel(m_ref, g_ref, wk_ref, wv_ref, mk_ref, mv_ref, mkb_ref, mvb_ref):
    h = _rms(m_ref[...], g_ref[...]).astype(BF16)
    mk = jnp.dot(h, wk_ref[...], preferred_element_type=F32)
    mv = jnp.dot(h, wv_ref[...], preferred_element_type=F32)
    mk_ref[...] = mk
    mv_ref[...] = mv
    mkb_ref[...] = mk.astype(BF16)
    mvb_ref[...] = mv.astype(BF16)


def _mem_kv(mem, g, w_ck, w_cv):
    rows, d = mem.shape
    tm = 512
    row = lambda i: (i, 0)
    return pl.pallas_call(
        _mem_kv_kernel,
        grid=(rows // tm,),
        in_specs=[pl.BlockSpec((tm, d), row), _resident((1, d)), _resident(w_ck.shape), _resident(w_cv.shape)],
        out_specs=[pl.BlockSpec((tm, d), row)] * 4,
        out_shape=[jax.ShapeDtypeStruct((rows, d), F32)] * 2 + [jax.ShapeDtypeStruct((rows, d), BF16)] * 2,
        compiler_params=_params("arbitrary"),
        name="mem_kv",
    )(mem, g, w_ck, w_cv)


def _mix_out_kernel(x_ref, po_ref, at_ref, wo_ref, g_ref, wq_ref, x1_ref, qc_ref):
    cat = jnp.concatenate([po_ref[...], at_ref[...]], axis=-1)
    x1 = x_ref[...] + jnp.dot(cat, wo_ref[...], preferred_element_type=F32)
    x1_ref[...] = x1
    hq = _rms(x1, g_ref[...]).astype(BF16)
    qc = jnp.dot(hq, wq_ref[...], preferred_element_type=F32)
    qc_ref[...] = (qc * (MEM_HEAD_DIM ** -0.5)).astype(BF16)


def _mix_out(x, po, at, w_out, g_cross, w_cq):
    rows, d = x.shape
    tm = min(512, rows)
    row = lambda i: (i, 0)
    return pl.pallas_call(
        _mix_out_kernel,
        grid=(rows // tm,),
        in_specs=[pl.BlockSpec((tm, d), row), pl.BlockSpec((tm, POOL_WIDTH), row),
                  pl.BlockSpec((tm, ATTN_WIDTH), row), _resident(w_out.shape), _resident((1, d)),
                  _resident(w_cq.shape)],
        out_specs=[pl.BlockSpec((tm, d), row), pl.BlockSpec((tm, d), row)],
        out_shape=[jax.ShapeDtypeStruct((rows, d), F32), jax.ShapeDtypeStruct((rows, d), BF16)],
        compiler_params=_params("arbitrary"),
        name="mix_out",
    )(x, po, at, w_out, g_cross, w_cq)


def _cross_kernel(q_ref, mk_ref, mv_ref, o_ref):
    q = q_ref[0]
    for h in range(MEM_HEADS):
        sl = slice(h * MEM_HEAD_DIM, (h + 1) * MEM_HEAD_DIM)
        lg = lax.dot_general(q[:, sl], mk_ref[0, :, sl].astype(BF16), NT_DIMS, preferred_element_type=F32)
        p = jnp.exp(lg - jnp.max(lg, axis=-1, keepdims=True))
        l = jnp.sum(p, axis=-1, keepdims=True)
        o = jnp.dot(p.astype(BF16), mv_ref[0, :, sl].astype(BF16), preferred_element_type=F32)
        o_ref[0, :, sl] = (o / l).astype(BF16)


def _cross(qc, mk, mv, tm):
    b, s, d = qc.shape
    m = mk.shape[1]
    return pl.pallas_call(
        _cross_kernel,
        grid=(b, s // tm),
        in_specs=[pl.BlockSpec((1, tm, d), lambda bi, i: (bi, i, 0)),
                  pl.BlockSpec((1, m, d), lambda bi, i: (bi, 0, 0)),
                  pl.BlockSpec((1, m, d), lambda bi, i: (bi, 0, 0))],
        out_specs=pl.BlockSpec((1, tm, d), lambda bi, i: (bi, i, 0)),
        out_shape=jax.ShapeDtypeStruct((b, s, d), BF16),
        compiler_params=_params("arbitrary", "arbitrary"),
        name="cross_attn",
    )(qc, mk, mv)


MEM_LANES = 128
MEM_PARTS = MEM_HEAD_DIM // MEM_LANES
MEM_ROWS = MEM_PARTS * MEM_HEADS


def _cross_sample_kernel(q_ref, mk_ref, mv_ref, o_ref):
    n = mk_ref.shape[1]
    r = lax.dot_general(q_ref[0], mk_ref[0].astype(BF16), NT_DIMS, preferred_element_type=F32)
    row = lax.broadcasted_iota(jnp.int32, r.shape, 0)
    lane_row = lax.broadcasted_iota(jnp.int32, r.shape, 1) % MEM_ROWS
    r = jnp.where(lane_row == row, r, 0.0)

    def other_part(t, lane_shift):
        return pltpu.roll(pltpu.roll(t, MEM_HEADS, 0), lane_shift, 1)

    valid = (row < MEM_HEADS) & (lane_row == row)
    lg = jnp.where(valid, r + other_part(r, n - MEM_HEADS), NEG)
    p = jnp.where(valid, jnp.exp(lg - jnp.max(lg, axis=-1, keepdims=True)), 0.0)
    p = p + other_part(p, MEM_HEADS)
    l = jnp.sum(p, axis=-1, keepdims=True)
    o = jnp.dot(p.astype(BF16), mv_ref[0].astype(BF16), preferred_element_type=F32)
    o_ref[0] = (o / l).astype(BF16)


def _stored_rows(qc):
    rows = qc.shape[0]
    return jnp.transpose(qc.reshape(rows, MEM_HEADS, MEM_PARTS, MEM_LANES), (0, 2, 1, 3)).reshape(rows, MEM_ROWS, MEM_LANES)


def _cross_sample(qc, mem_k, mem_v):
    b, slots = mem_k.shape[:2]

    def stored(m):
        m = m.reshape(b, slots, MEM_HEADS, MEM_PARTS, MEM_LANES)
        return jnp.transpose(m, (0, 1, 3, 2, 4)).reshape(b, slots * MEM_ROWS, MEM_LANES)

    seq = lambda n: pl.BlockSpec((1, n, MEM_LANES), lambda bi: (bi, 0, 0))
    o = pl.pallas_call(
        _cross_sample_kernel,
        grid=(b,),
        in_specs=[seq(MEM_ROWS), seq(slots * MEM_ROWS), seq(slots * MEM_ROWS)],
        out_specs=seq(MEM_ROWS),
        out_shape=jax.ShapeDtypeStruct((b, MEM_ROWS, MEM_LANES), BF16),
        compiler_params=_params("arbitrary"),
        name="cross_sample",
    )(_stored_rows(qc), stored(mem_k), stored(mem_v))
    return jnp.transpose(o.reshape(b, MEM_PARTS, MEM_HEADS, MEM_LANES), (0, 2, 1, 3)).reshape(b, -1)


def _mlp_kernel(x1_ref, o_ref, wco_ref, gm_ref, wu_ref, wd_ref, gf_ref, y_ref):
    x2 = x1_ref[...] + jnp.dot(o_ref[...], wco_ref[...], preferred_element_type=F32)
    hm = _rms(x2, gm_ref[...]).astype(BF16)
    acc = x2
    for c in range(D_FF // FF_CHUNK):
        cs = slice(c * FF_CHUNK, (c + 1) * FF_CHUNK)
        a = jnp.dot(hm, wu_ref[:, cs], preferred_element_type=F32)
        a = jnp.square(jnp.maximum(a, 0.0)).astype(BF16)
        acc = acc + jnp.dot(a, wd_ref[cs, :], preferred_element_type=F32)
    y_ref[...] = _rms(acc, gf_ref[...])


def _mlp(x1, o, w_co, g_mlp, w_up, w_down, g_final):
    rows, d = x1.shape
    tm = min(512, rows)
    row = lambda i: (i, 0)
    return pl.pallas_call(
        _mlp_kernel,
        grid=(rows // tm,),
        in_specs=[pl.BlockSpec((tm, d), row), pl.BlockSpec((tm, d), row), _resident(w_co.shape),
                  _resident((1, d)), _resident(w_up.shape), _resident(w_down.shape), _resident((1, d))],
        out_specs=pl.BlockSpec((tm, d), row),
        out_shape=jax.ShapeDtypeStruct((rows, d), F32),
        compiler_params=_params("arbitrary"),
        name="mlp",
    )(x1, o, w_co, g_mlp, w_up, w_down, g_final)


def _proj_sample_kernel(x_ref, g_ref, w_ref, cos_ref, sin_ref, st_ref, wp_ref, sp_ref,
                        q_ref, k_ref, v_ref, kt_ref, vt_ref, u_ref, po_ref, *, n_prev):
    h = _rms(x_ref[...], g_ref[...]).astype(BF16)
    proj = jnp.dot(h, w_ref[...], preferred_element_type=F32)
    u = proj[:, :POOL_WIDTH]
    cos = cos_ref[...]
    sin = sin_ref[...]
    q_ref[...] = _rope(proj[:, POOL_WIDTH:POOL_WIDTH + ATTN_WIDTH], cos, sin) * (HEAD_DIM ** -0.5)
    k = _rope(proj[:, POOL_WIDTH + ATTN_WIDTH:POOL_WIDTH + 2 * ATTN_WIDTH], cos, sin)
    v = proj[:, POOL_WIDTH + 2 * ATTN_WIDTH:]
    k_ref[...] = k
    v_ref[...] = v
    kt_ref[...] = k.T
    vt_ref[...] = v.T
    u_ref[...] = u
    for g, w in enumerate(POOL_WINDOWS):
        cols = slice(g * POOL_GC, (g + 1) * POOL_GC)
        ug = u[:, cols]
        wsum = ug
        for j in range(1, w):
            wsum = wsum + st_ref[POOL_STATE - j, :, cols]
        d = wsum / float(min(w, n_prev + 1)) - ug
        mixed = jnp.dot(d.astype(BF16), wp_ref[g], preferred_element_type=F32)
        po_ref[:, cols] = (mixed * sp_ref[:, cols]).astype(BF16)


def _proj_sample(x, g, w_in, cos, sin, state_t, w_pool, s_pool, n_prev):
    rows, d = x.shape
    wide = jax.ShapeDtypeStruct((rows, ATTN_WIDTH), F32)
    tall = jax.ShapeDtypeStruct((ATTN_WIDTH, rows), F32)
    full = lambda a: pl.BlockSpec(a.shape, lambda i, nd=len(a.shape): (0,) * nd)
    args = (x, g, w_in, cos, sin, state_t, w_pool, s_pool)
    outs = [wide, wide, wide, tall, tall, wide, jax.ShapeDtypeStruct((rows, POOL_WIDTH), BF16)]
    return pl.pallas_call(
        functools.partial(_proj_sample_kernel, n_prev=n_prev),
        grid=(1,),
        in_specs=[full(a) for a in args],
        out_specs=[full(o) for o in outs],
        out_shape=outs,
        compiler_params=_params("arbitrary"),
        name="proj_sample",
    )(*args)


def _moba_sample_kernel(pt_ref, q_ref, kn_ref, vn_ref, *rest, n_pages, page):
    del pt_ref
    k_pages = rest[:n_pages]
    v_pages = rest[n_pages:2 * n_pages]
    o_ref = rest[2 * n_pages]
    kb_ref, vb_ref = rest[2 * n_pages + 1:]
    past = n_pages * page
    nb = past // MOBA_BLOCK
    for p in range(n_pages):
        kb_ref[:, p * page:(p + 1) * page] = k_pages[p][0].astype(BF16)
        vb_ref[:, p * page:(p + 1) * page] = v_pages[p][0].astype(BF16)

    head_of_lane = lax.broadcasted_iota(jnp.int32, (N_HEADS, ATTN_WIDTH), 1) // HEAD_DIM
    head = lax.broadcasted_iota(jnp.int32, (N_HEADS, ATTN_WIDTH), 0)
    mine = head_of_lane == head
    q_heads = jnp.where(mine, q_ref[0], 0.0)
    lg = jnp.dot(q_heads.astype(BF16), kb_ref[...], preferred_element_type=F32)

    s = jnp.concatenate(
        [jnp.sum(lg[:, j * MOBA_BLOCK:(j + 1) * MOBA_BLOCK], axis=-1, keepdims=True) for j in range(nb)], axis=-1)
    blk_id = lax.broadcasted_iota(jnp.int32, (N_HEADS, nb), 1)
    sel = _topk_member(s, blk_id >= 0, blk_id, axis=1)
    key_blk = lax.broadcasted_iota(jnp.int32, (N_HEADS, past), 1) // MOBA_BLOCK
    chosen = jnp.zeros((N_HEADS, past), jnp.bool_)
    for j in range(nb):
        chosen = chosen | ((key_blk == j) & sel[:, j:j + 1])
    lg = jnp.where(chosen, lg, NEG)

    lg_new = jnp.sum(q_heads * kn_ref[0], axis=-1, keepdims=True)
    m = jnp.maximum(jnp.max(lg, axis=-1, keepdims=True), lg_new)
    p = jnp.exp(lg - m)
    p_new = jnp.exp(lg_new - m)
    l = jnp.sum(p, axis=-1, keepdims=True) + p_new
    acc = lax.dot_general(p.astype(BF16), vb_ref[...], NT_DIMS, preferred_element_type=F32) + p_new * vn_ref[0]
    out = jnp.where(mine, acc / l, 0.0)
    o_ref[0] = jnp.sum(out, axis=0, keepdims=True).astype(BF16)


def _moba_sample(page_table, q, k_new, v_new, cache_k, cache_v):
    b, n_pages = page_table.shape
    _, w, page = cache_k.shape
    past = n_pages * page
    tok = pl.BlockSpec((1, 1, w), lambda bi, pt: (bi, 0, 0))
    page_spec = lambda p: pl.BlockSpec((1, w, page), lambda bi, pt: (pt[bi, p], 0, 0))
    grid_spec = pltpu.PrefetchScalarGridSpec(
        num_scalar_prefetch=1,
        grid=(b,),
        in_specs=[tok, tok, tok] + [page_spec(p) for p in range(n_pages)] * 2,
        out_specs=tok,
        scratch_shapes=[pltpu.VMEM((w, past), BF16), pltpu.VMEM((w, past), BF16)],
    )
    return pl.pallas_call(
        functools.partial(_moba_sample_kernel, n_pages=n_pages, page=page),
        grid_spec=grid_spec,
        out_shape=jax.ShapeDtypeStruct((b, 1, w), BF16),
        compiler_params=_params("arbitrary"),
        name="moba_sample",
    )(page_table, q, k_new, v_new, *([cache_k] * n_pages), *([cache_v] * n_pages))


def _rope_tables(pos):
    half = HEAD_DIM // 2
    inv_freq = ROPE_THETA ** (-jnp.arange(half, dtype=F32) / half)
    ang = pos.astype(F32)[:, None] * inv_freq[None, :]
    cos = jnp.cos(ang)
    sin = jnp.sin(ang)
    cos_full = jnp.tile(jnp.concatenate([cos, cos], axis=-1), (1, N_HEADS))
    sin_signed = jnp.tile(jnp.concatenate([-sin, sin], axis=-1), (1, N_HEADS))
    return cos_full, sin_signed


def _post_mixer(x, po, at, cross, lp):
    x1, qc = _mix_out(x, po, at, lp["w_out"], lp["g_cross"], lp["w_cq"])
    return _mlp(x1, cross(qc), lp["w_co"], lp["g_mlp"], lp["w_up"], lp["w_down"], lp["g_final"])


def kernel(x_prompt, x_sample, cache_k, cache_v, cache_mem_k, cache_mem_v, state_pool, page_table, mem_prompt,
           g_mix, w_in, w_pool, s_pool, w_out, g_cross, g_mem, w_cq, w_ck, w_cv, w_co, g_mlp, w_up, w_down,
           g_final):
    depth = w_in.shape[0]
    assert depth == 1, "one decoder layer"
    bp, seq, d = x_prompt.shape
    bs, dec_seq, _ = x_sample.shape
    assert dec_seq == 1 and seq % MOBA_BLOCK == 0
    n_phys, page = cache_k.shape[1], cache_k.shape[2]
    n_pages = page_table.shape[1]
    past_len = n_pages * page
    assert past_len % MOBA_BLOCK == 0 and past_len >= POOL_STATE
    mem_len = mem_prompt.shape[1]

    l = 0
    row = lambda a: a.reshape(1, -1)
    lp = dict(w_out=w_out[l].astype(BF16), g_cross=row(g_cross[l]), w_cq=w_cq[l].astype(BF16),
              w_co=w_co[l].astype(BF16), g_mlp=row(g_mlp[l]), w_up=w_up[l].astype(BF16),
              w_down=w_down[l].astype(BF16), g_final=row(g_final))
    w_in_b = w_in[l].astype(BF16)
    w_pool_b = w_pool[l].astype(BF16)
    g_mix_r = row(g_mix[l])
    s_pool_r = row(s_pool[l])

    cos_p, sin_p = _rope_tables(jnp.arange(seq))
    qtb, kt_p, vt_p, kb, vtb, kmean, po_p, last_p = _proj_prompt(x_prompt, g_mix_r, w_in_b, cos_p, sin_p,
                                                                 w_pool_b, s_pool_r)
    at_p = _moba_prompt(qtb, kb, vtb, kmean.reshape(bp, seq // MOBA_BLOCK, ATTN_WIDTH))
    mk_p, mv_p, mkb, mvb = _mem_kv(mem_prompt.reshape(bp * mem_len, d), row(g_mem[l]), w_ck[l].astype(BF16),
                                   w_cv[l].astype(BF16))
    cross_p = lambda qc: _cross(qc.reshape(bp, seq, d), mkb.reshape(bp, mem_len, d), mvb.reshape(bp, mem_len, d),
                                512).reshape(bp * seq, d)
    flat = lambda a: a.reshape(-1, a.shape[-1])
    y_prompt = _post_mixer(flat(x_prompt), flat(po_p), flat(at_p), cross_p, lp).reshape(bp, seq, d)

    cos_s, sin_s = _rope_tables(jnp.full((1,), past_len))
    state_t = jnp.transpose(state_pool[l], (1, 0, 2))
    q_s, k_s, v_s, kt_s, vt_s, u_s, po_s = _proj_sample(x_sample.reshape(bs, d), g_mix_r, w_in_b, cos_s, sin_s,
                                                        state_t, w_pool_b, s_pool_r, past_len)
    tok = lambda a: a.reshape(bs, 1, -1)
    pages_t = lambda c: jnp.transpose(c, (0, 2, 3, 1)).reshape(n_phys, ATTN_WIDTH, page)
    at_s = _moba_sample(page_table, tok(q_s), tok(k_s), tok(v_s), pages_t(cache_k[l]), pages_t(cache_v[l]))
    cross_s = lambda qc: _cross_sample(qc, cache_mem_k[l], cache_mem_v[l])
    y_sample = _post_mixer(flat(x_sample), po_s, flat(at_s), cross_s, lp).reshape(bs, 1, d)

    def heads(t, b_):
        return jnp.transpose(t.reshape(b_, N_HEADS, HEAD_DIM, -1), (0, 3, 1, 2))[None]

    mem_heads = lambda a: a.reshape(1, bp, mem_len, MEM_HEADS, MEM_HEAD_DIM)
    pool_sample = jnp.transpose(jnp.concatenate([state_t[1:], u_s[None]], axis=0), (1, 0, 2))[None]
    return (y_prompt, y_sample, heads(kt_p, bp), heads(vt_p, bp), last_p[:, 1:][None], mem_heads(mk_p),
            mem_heads(mv_p), heads(kt_s.T.reshape(bs, ATTN_WIDTH, 1), bs), heads(vt_s.T.reshape(bs, ATTN_WIDTH, 1), bs),
            pool_sample)
```

```python
import functools

import jax
import jax.numpy as jnp
from jax import lax
from jax.experimental import pallas as pl
from jax.experimental.pallas import tpu as pltpu

D_MODEL = 1024
POOL_WIDTH = 512
POOL_WINDOWS = (2, 4, 8, 16)
POOL_GC = POOL_WIDTH // len(POOL_WINDOWS)
POOL_STATE = max(POOL_WINDOWS) - 1
POOL_HIST = POOL_STATE + 1
ATTN_WIDTH = D_MODEL - POOL_WIDTH
HEAD_DIM = 64
N_HEADS = ATTN_WIDTH // HEAD_DIM
MOBA_BLOCK = 256
MOBA_TOPK = 3
ROPE_THETA = 10000.0
MEM_HEADS = 4
MEM_HEAD_DIM = D_MODEL // MEM_HEADS
D_FF = 4 * D_MODEL
FF_CHUNK = 1024
EPS = 1e-6
NEG = -1e30

VMEM_LIMIT_BYTES = 56 * 1024 * 1024

F32 = jnp.float32
BF16 = jnp.bfloat16
NT_DIMS = (((1,), (1,)), ((), ()))


def _params(*semantics):
    return pltpu.CompilerParams(dimension_semantics=semantics, vmem_limit_bytes=VMEM_LIMIT_BYTES)


def _resident(shape):
    zeros = (0,) * len(shape)
    return pl.BlockSpec(shape, lambda *_: zeros, pipeline_mode=pl.Buffered(1))


def _rms(x, g):
    return x * lax.rsqrt(jnp.mean(x * x, axis=-1, keepdims=True) + EPS) * g


def _rope(t, cos, sin_signed):
    width = t.shape[-1]
    lane = lax.broadcasted_iota(jnp.int32, t.shape, t.ndim - 1)
    first_half = (lane % HEAD_DIM) < (HEAD_DIM // 2)
    partner = jnp.where(first_half,
                        pltpu.roll(t, width - HEAD_DIM // 2, t.ndim - 1),
                        pltpu.roll(t, HEAD_DIM // 2, t.ndim - 1))
    return t * cos + partner * sin_signed


def _rope_t(t, cos, sin_signed):
    rows = t.shape[0]
    row = lax.broadcasted_iota(jnp.int32, t.shape, 0)
    first_half = (row % HEAD_DIM) < (HEAD_DIM // 2)
    partner = jnp.where(first_half, pltpu.roll(t, rows - HEAD_DIM // 2, 0), pltpu.roll(t, HEAD_DIM // 2, 0))
    return t * cos + partner * sin_signed


def _topk_member(s, valid, idx, axis):
    n = s.shape[axis]
    s = jnp.where(valid, s, NEG)
    rank = jnp.zeros(s.shape, jnp.int32)
    for j in range(n):
        sj = lax.slice_in_dim(s, j, j + 1, axis=axis)
        beats = (sj > s) | ((sj == s) & (j < idx))
        rank = rank + beats.astype(jnp.int32)
    return (rank < MOBA_TOPK) & valid


def _proj_prompt_kernel(x_ref, g_ref, w_ref, cos_ref, sin_ref, cost_ref, sint_ref, wp_ref, sp_ref,
                        qt_ref, kt_ref, vt_ref, kb_ref, vtb_ref, km_ref, po_ref, last_ref, ext_ref):
    i = pl.program_id(1)
    tm = x_ref.shape[1]

    @pl.when(i == 0)
    def _():
        ext_ref[0:POOL_HIST, :] = jnp.zeros((POOL_HIST, POOL_WIDTH), F32)

    h = _rms(x_ref[0], g_ref[...]).astype(BF16)

    def project(first):
        return jnp.dot(h, w_ref[:, first:first + ATTN_WIDTH], preferred_element_type=F32)

    u = project(0)
    k = _rope(project(POOL_WIDTH + ATTN_WIDTH), cos_ref[...], sin_ref[...])
    q = project(POOL_WIDTH)
    v = project(POOL_WIDTH + 2 * ATTN_WIDTH)
    qt = _rope_t(q.T, cost_ref[...], sint_ref[...])
    qt_ref[0] = (qt * (HEAD_DIM ** -0.5 * LOG2_E)).astype(BF16)
    vt = v.T
    kt_ref[0] = k.T
    vt_ref[0] = vt
    kb_ref[0] = k.astype(BF16)
    vtb_ref[0, 0] = vt.astype(BF16)
    km_ref[0, 0] = jnp.mean(k, axis=0, keepdims=True)

    ext_ref[POOL_HIST:POOL_HIST + tm, :] = u
    pos = i * tm + lax.broadcasted_iota(jnp.int32, (tm, 1), 0)
    for g, w in enumerate(POOL_WINDOWS):
        cols = slice(g * POOL_GC, (g + 1) * POOL_GC)
        ug = u[:, cols]
        wsum = ug
        for j in range(1, w):
            wsum = wsum + ext_ref[POOL_HIST - j:POOL_HIST - j + tm, cols]
        cnt = jnp.minimum(w, pos + 1).astype(F32)
        d = wsum / cnt - ug
        mixed = jnp.dot(d.astype(BF16), wp_ref[g], preferred_element_type=F32)
        po_ref[0, :, cols] = (mixed * sp_ref[:, cols]).astype(BF16)
    tail = ext_ref[tm:tm + POOL_HIST, :]
    ext_ref[0:POOL_HIST, :] = tail
    last_ref[0] = tail


def _proj_prompt(x, g, w_in, cos, sin, cos_t, sin_t, w_pool, s_pool):
    b, s, d = x.shape
    tm = MOBA_BLOCK
    nq = s // tm
    row = lambda bi, i: (bi, i, 0)
    col = lambda bi, i: (bi, 0, i)
    blk = lambda bi, i: (bi, i, 0, 0)
    return pl.pallas_call(
        _proj_prompt_kernel,
        grid=(b, nq),
        in_specs=[
            pl.BlockSpec((1, tm, d), row),
            _resident((1, d)),
            _resident(w_in.shape),
            pl.BlockSpec((tm, ATTN_WIDTH), lambda bi, i: (i, 0)),
            pl.BlockSpec((tm, ATTN_WIDTH), lambda bi, i: (i, 0)),
            pl.BlockSpec((ATTN_WIDTH, tm), lambda bi, i: (0, i)),
            pl.BlockSpec((ATTN_WIDTH, tm), lambda bi, i: (0, i)),
            _resident(w_pool.shape),
            _resident((1, POOL_WIDTH)),
        ],
        out_specs=[
            pl.BlockSpec((1, ATTN_WIDTH, tm), col),
            pl.BlockSpec((1, ATTN_WIDTH, tm), col),
            pl.BlockSpec((1, ATTN_WIDTH, tm), col),
            pl.BlockSpec((1, tm, ATTN_WIDTH), row),
            pl.BlockSpec((1, 1, ATTN_WIDTH, tm), blk),
            pl.BlockSpec((1, 1, 1, ATTN_WIDTH), blk),
            pl.BlockSpec((1, tm, POOL_WIDTH), row),
            pl.BlockSpec((1, POOL_HIST, POOL_WIDTH), lambda bi, i: (bi, 0, 0)),
        ],
        out_shape=[
            jax.ShapeDtypeStruct((b, ATTN_WIDTH, s), BF16),
            jax.ShapeDtypeStruct((b, ATTN_WIDTH, s), F32),
            jax.ShapeDtypeStruct((b, ATTN_WIDTH, s), F32),
            jax.ShapeDtypeStruct((b, s, ATTN_WIDTH), BF16),
            jax.ShapeDtypeStruct((b, nq, ATTN_WIDTH, tm), BF16),
            jax.ShapeDtypeStruct((b, nq, 1, ATTN_WIDTH), F32),
            jax.ShapeDtypeStruct((b, s, POOL_WIDTH), BF16),
            jax.ShapeDtypeStruct((b, POOL_HIST, POOL_WIDTH), F32),
        ],
        scratch_shapes=[pltpu.VMEM((POOL_HIST + tm, POOL_WIDTH), F32)],
        compiler_params=_params("arbitrary", "arbitrary"),
        name="proj_prompt",
    )(x, g, w_in, cos, sin, cos_t, sin_t, w_pool, s_pool)


SUBLANES = 8
LOGIT_AHEAD = 2
LOGIT_SLOTS = 4
LOG2_E = 1.4426950408889634


def _all_sublanes(t, combine):
    for s in (4, 2, 1):
        t = combine(t, pltpu.roll(t, s, 0))
    return t


def _moba_prompt_kernel(qt_ref, kb_ref, vtb_ref, km_ref, o_ref, acc_ref, m_ref, l_ref, take_ref, lg_ref):
    i = pl.program_id(1)
    tq = qt_ref.shape[2]
    nb = km_ref.shape[1]

    km_rows = jnp.concatenate([km_ref[0]] * N_HEADS, axis=0)
    row_head = lax.broadcasted_iota(jnp.int32, km_rows.shape, 0) // nb
    lane_head = lax.broadcasted_iota(jnp.int32, km_rows.shape, 1) // HEAD_DIM
    km_heads = jnp.where(row_head == lane_head, km_rows, 0.0).astype(BF16)
    s = jnp.dot(km_heads, qt_ref[0], preferred_element_type=F32).reshape(N_HEADS, nb, tq)
    blk = lax.broadcasted_iota(jnp.int32, s.shape, 1)
    sel = _topk_member(s, blk < i, blk, axis=1).astype(F32)

    for h in range(N_HEADS):
        for jb in range(nb):
            take_ref[h, jb] = jnp.broadcast_to(sel[h, jb:jb + 1, :], (SUBLANES, tq))

    def head_rows(h):
        return slice(h * HEAD_DIM, (h + 1) * HEAD_DIM)

    def stage_logits(h, j):
        off = pl.multiple_of(j * MOBA_BLOCK, MOBA_BLOCK)
        lg_ref[h % LOGIT_SLOTS] = jnp.dot(kb_ref[0, pl.ds(off, MOBA_BLOCK), head_rows(h)],
                                          qt_ref[0, head_rows(h), :], preferred_element_type=F32)

    def weighted_values(h, j, p3):
        p = p3.reshape(MOBA_BLOCK, tq).astype(BF16)
        pv = jnp.dot(vtb_ref[0, j, head_rows(h), :], p, preferred_element_type=F32)
        return pv.reshape(HEAD_DIM // SUBLANES, SUBLANES, tq)

    def tiles(t):
        return t.reshape(t.shape[0] // SUBLANES, SUBLANES, tq)

    def block_pass(j, j_next, own):
        for h in range(N_HEADS):
            ahead = h + LOGIT_AHEAD
            if ahead < N_HEADS:
                stage_logits(ahead, j)
            else:
                stage_logits(ahead - N_HEADS, j_next)
            lg = lg_ref[h % LOGIT_SLOTS]
            if own:
                kk = lax.broadcasted_iota(jnp.int32, lg.shape, 0)
                qq = lax.broadcasted_iota(jnp.int32, lg.shape, 1)
                lg3 = tiles(jnp.where(kk <= qq, lg, NEG))
                m = _all_sublanes(jnp.max(lg3, axis=0), jnp.maximum)
                p3 = jnp.exp2(lg3 - m[None])
                m_ref[h] = m
                l_ref[h] = _all_sublanes(jnp.sum(p3, axis=0), jnp.add)
                acc_ref[head_rows(h), :] = weighted_values(h, j, p3).reshape(HEAD_DIM, tq)
            else:
                lg3 = tiles(lg)
                take = take_ref[h, j] > 0.0
                m_old = m_ref[h]
                l_old = l_ref[h]
                acc_old = tiles(acc_ref[head_rows(h), :])
                m = jnp.maximum(m_old, _all_sublanes(jnp.max(lg3, axis=0), jnp.maximum))
                a = jnp.exp2(m_old - m)
                p3 = jnp.exp2(lg3 - m[None])
                l = a * l_old + _all_sublanes(jnp.sum(p3, axis=0), jnp.add)
                acc3 = a[None] * acc_old + weighted_values(h, j, p3)
                m_ref[h] = jnp.where(take, m, m_old)
                l_ref[h] = jnp.where(take, l, l_old)
                acc_ref[head_rows(h), :] = jnp.where(take[None], acc3, acc_old).reshape(HEAD_DIM, tq)

    last_past = jnp.maximum(i - 1, 0)
    for h in range(LOGIT_AHEAD):
        stage_logits(h, i)
    block_pass(i, 0, own=True)

    def past_block(j, carry):
        block_pass(j, jnp.minimum(j + 1, last_past), own=False)
        return carry

    lax.fori_loop(0, i, past_block, 0)
    for h in range(N_HEADS):
        rows = slice(h * HEAD_DIM, (h + 1) * HEAD_DIM)
        acc_ref[rows, :] = (tiles(acc_ref[rows, :]) / l_ref[h][None]).reshape(HEAD_DIM, tq)
    o_ref[0] = acc_ref[...].T.astype(BF16)


def _moba_prompt(qtb, kb, vtb, kmean):
    b, w, s = qtb.shape
    nq = s // MOBA_BLOCK
    return pl.pallas_call(
        _moba_prompt_kernel,
        grid=(b, nq),
        in_specs=[
            pl.BlockSpec((1, w, MOBA_BLOCK), lambda bi, i: (bi, 0, i)),
            pl.BlockSpec((1, s, w), lambda bi, i: (bi, 0, 0)),
            pl.BlockSpec((1, nq, w, MOBA_BLOCK), lambda bi, i: (bi, 0, 0, 0)),
            pl.BlockSpec((1, nq, w), lambda bi, i: (bi, 0, 0)),
        ],
        out_specs=pl.BlockSpec((1, MOBA_BLOCK, w), lambda bi, i: (bi, i, 0)),
        out_shape=jax.ShapeDtypeStruct((b, s, w), BF16),
        scratch_shapes=[pltpu.VMEM((w, MOBA_BLOCK), F32),
                        pltpu.VMEM((N_HEADS, SUBLANES, MOBA_BLOCK), F32),
                        pltpu.VMEM((N_HEADS, SUBLANES, MOBA_BLOCK), F32),
                        pltpu.VMEM((N_HEADS, nq, SUBLANES, MOBA_BLOCK), F32),
                        pltpu.VMEM((LOGIT_SLOTS, MOBA_BLOCK, MOBA_BLOCK), F32)],
        compiler_params=_params("arbitrary", "arbitrary"),
        name="moba_prompt",
    )(qtb, kb, vtb, kmean)


def _mem_kv_kernel(m_ref, g_ref, wk_ref, wv_ref, mk_ref, mv_ref, mkb_ref, mvb_ref):
    h = _rms(m_ref[...], g_ref[...]).astype(BF16)
    mk = jnp.dot(h, wk_ref[...], preferred_element_type=F32)
    mv = jnp.dot(h, wv_ref[...], preferred_element_type=F32)
    mk_ref[...] = mk
    mv_ref[...] = mv
    mkb_ref[...] = mk.astype(BF16)
    mvb_ref[...] = mv.astype(BF16)


def _mem_kv(mem, g, w_ck, w_cv):
    rows, d = mem.shape
    tm = 512
    row = lambda i: (i, 0)
    return pl.pallas_call(
        _mem_kv_kernel,
        grid=(rows // tm,),
        in_specs=[pl.BlockSpec((tm, d), row), _resident((1, d)), _resident(w_ck.shape), _resident(w_cv.shape)],
        out_specs=[pl.BlockSpec((tm, d), row)] * 4,
        out_shape=[jax.ShapeDtypeStruct((rows, d), F32)] * 2 + [jax.ShapeDtypeStruct((rows, d), BF16)] * 2,
        compiler_params=_params("arbitrary"),
        name="mem_kv",
    )(mem, g, w_ck, w_cv)


def _mix_out_kernel(x_ref, po_ref, at_ref, wo_ref, g_ref, wq_ref, x1_ref, qc_ref):
    cat = jnp.concatenate([po_ref[...], at_ref[...]], axis=-1)
    x1 = x_ref[...] + jnp.dot(cat, wo_ref[...], preferred_element_type=F32)
    x1_ref[...] = x1
    hq = _rms(x1, g_ref[...]).astype(BF16)
    qc = jnp.dot(hq, wq_ref[...], preferred_element_type=F32)
    qc_ref[...] = (qc * (MEM_HEAD_DIM ** -0.5)).astype(BF16)


def _mix_out(x, po, at, w_out, g_cross, w_cq):
    rows, d = x.shape
    tm = min(512, rows)
    row = lambda i: (i, 0)
    return pl.pallas_call(
        _mix_out_kernel,
        grid=(rows // tm,),
        in_specs=[pl.BlockSpec((tm, d), row), pl.BlockSpec((tm, POOL_WIDTH), row),
                  pl.BlockSpec((tm, ATTN_WIDTH), row), _resident(w_out.shape), _resident((1, d)),
                  _resident(w_cq.shape)],
        out_specs=[pl.BlockSpec((tm, d), row), pl.BlockSpec((tm, d), row)],
        out_shape=[jax.ShapeDtypeStruct((rows, d), F32), jax.ShapeDtypeStruct((rows, d), BF16)],
        compiler_params=_params("arbitrary"),
        name="mix_out",
    )(x, po, at, w_out, g_cross, w_cq)


def _cross_kernel(q_ref, mk_ref, mv_ref, o_ref):
    q = q_ref[0]
    for h in range(MEM_HEADS):
        sl = slice(h * MEM_HEAD_DIM, (h + 1) * MEM_HEAD_DIM)
        lg = lax.dot_general(q[:, sl], mk_ref[0, :, sl].astype(BF16), NT_DIMS, preferred_element_type=F32)
        p = jnp.exp(lg - jnp.max(lg, axis=-1, keepdims=True))
        l = jnp.sum(p, axis=-1, keepdims=True)
        o = jnp.dot(p.astype(BF16), mv_ref[0, :, sl].astype(BF16), preferred_element_type=F32)
        o_ref[0, :, sl] = (o / l).astype(BF16)


def _cross(qc, mk, mv, tm):
    b, s, d = qc.shape
    m = mk.shape[1]
    return pl.pallas_call(
        _cross_kernel,
        grid=(b, s // tm),
        in_specs=[pl.BlockSpec((1, tm, d), lambda bi, i: (bi, i, 0)),
                  pl.BlockSpec((1, m, d), lambda bi, i: (bi, 0, 0)),
                  pl.BlockSpec((1, m, d), lambda bi, i: (bi, 0, 0))],
        out_specs=pl.BlockSpec((1, tm, d), lambda bi, i: (bi, i, 0)),
        out_shape=jax.ShapeDtypeStruct((b, s, d), BF16),
        compiler_params=_params("arbitrary", "arbitrary"),
        name="cross_attn",
    )(qc, mk, mv)


MEM_LANES = 128
MEM_PARTS = MEM_HEAD_DIM // MEM_LANES
MEM_ROWS = MEM_PARTS * MEM_HEADS


def _cross_sample_kernel(q_ref, mk_ref, mv_ref, o_ref):
    n = mk_ref.shape[1]
    row = lax.broadcasted_iota(jnp.int32, (MEM_ROWS, n), 0)
    lane_row = lax.broadcasted_iota(jnp.int32, (MEM_ROWS, n), 1) % MEM_ROWS
    own = lane_row == row
    valid = (row < MEM_HEADS) & own

    def other_part(t, lane_shift):
        return pltpu.roll(pltpu.roll(t, MEM_HEADS, 0), lane_shift, 1)

    for s in range(q_ref.shape[0]):
        r = lax.dot_general(q_ref[s], mk_ref[s].astype(BF16), NT_DIMS, preferred_element_type=F32)
        r = jnp.where(own, r, 0.0)
        lg = jnp.where(valid, r + other_part(r, n - MEM_HEADS), NEG)
        p = jnp.where(valid, jnp.exp(lg - jnp.max(lg, axis=-1, keepdims=True)), 0.0)
        p = p + other_part(p, MEM_HEADS)
        l = jnp.sum(p, axis=-1, keepdims=True)
        o = jnp.dot(p.astype(BF16), mv_ref[s].astype(BF16), preferred_element_type=F32)
        o_ref[s] = (o / l).astype(BF16)


def _stored_rows(qc):
    rows = qc.shape[0]
    return jnp.transpose(qc.reshape(rows, MEM_HEADS, MEM_PARTS, MEM_LANES), (0, 2, 1, 3)).reshape(rows, MEM_ROWS, MEM_LANES)


def _cross_sample(qc, mem_k, mem_v):
    b, slots = mem_k.shape[:2]

    def stored(m):
        m = m.reshape(b, slots, MEM_HEADS, MEM_PARTS, MEM_LANES)
        return jnp.transpose(m, (0, 1, 3, 2, 4)).reshape(b, slots * MEM_ROWS, MEM_LANES)

    per_step = 4
    assert b % per_step == 0
    seq = lambda n: pl.BlockSpec((per_step, n, MEM_LANES), lambda bi: (bi, 0, 0))
    o = pl.pallas_call(
        _cross_sample_kernel,
        grid=(b // per_step,),
        in_specs=[seq(MEM_ROWS), seq(slots * MEM_ROWS), seq(slots * MEM_ROWS)],
        out_specs=seq(MEM_ROWS),
        out_shape=jax.ShapeDtypeStruct((b, MEM_ROWS, MEM_LANES), BF16),
        compiler_params=_params("arbitrary"),
        name="cross_sample",
    )(_stored_rows(qc), stored(mem_k), stored(mem_v))
    return jnp.transpose(o.reshape(b, MEM_PARTS, MEM_HEADS, MEM_LANES), (0, 2, 1, 3)).reshape(b, -1)


def _mlp_kernel(x1_ref, o_ref, wco_ref, gm_ref, wu_ref, wd_ref, gf_ref, y_ref):
    x2 = x1_ref[...] + jnp.dot(o_ref[...], wco_ref[...], preferred_element_type=F32)
    hm = _rms(x2, gm_ref[...]).astype(BF16)
    acc = x2
    for c in range(D_FF // FF_CHUNK):
        cs = slice(c * FF_CHUNK, (c + 1) * FF_CHUNK)
        a = jnp.dot(hm, wu_ref[:, cs], preferred_element_type=F32)
        a = jnp.square(jnp.maximum(a, 0.0)).astype(BF16)
        acc = acc + jnp.dot(a, wd_ref[cs, :], preferred_element_type=F32)
    y_ref[...] = _rms(acc, gf_ref[...])


def _mlp(x1, o, w_co, g_mlp, w_up, w_down, g_final):
    rows, d = x1.shape
    tm = min(512, rows)
    row = lambda i: (i, 0)
    return pl.pallas_call(
        _mlp_kernel,
        grid=(rows // tm,),
        in_specs=[pl.BlockSpec((tm, d), row), pl.BlockSpec((tm, d), row), _resident(w_co.shape),
                  _resident((1, d)), _resident(w_up.shape), _resident(w_down.shape), _resident((1, d))],
        out_specs=pl.BlockSpec((tm, d), row),
        out_shape=jax.ShapeDtypeStruct((rows, d), F32),
        compiler_params=_params("arbitrary"),
        name="mlp",
    )(x1, o, w_co, g_mlp, w_up, w_down, g_final)


def _proj_sample_kernel(x_ref, g_ref, w_ref, cos_ref, sin_ref, st_ref, wp_ref, sp_ref,
                        q_ref, k_ref, v_ref, kt_ref, vt_ref, u_ref, po_ref, *, n_prev):
    h = _rms(x_ref[...], g_ref[...]).astype(BF16)
    proj = jnp.dot(h, w_ref[...], preferred_element_type=F32)
    u = proj[:, :POOL_WIDTH]
    cos = cos_ref[...]
    sin = sin_ref[...]
    q_ref[...] = _rope(proj[:, POOL_WIDTH:POOL_WIDTH + ATTN_WIDTH], cos, sin) * (HEAD_DIM ** -0.5)
    k = _rope(proj[:, POOL_WIDTH + ATTN_WIDTH:POOL_WIDTH + 2 * ATTN_WIDTH], cos, sin)
    v = proj[:, POOL_WIDTH + 2 * ATTN_WIDTH:]
    k_ref[...] = k
    v_ref[...] = v
    kt_ref[...] = k.T
    vt_ref[...] = v.T
    u_ref[...] = u
    for g, w in enumerate(POOL_WINDOWS):
        cols = slice(g * POOL_GC, (g + 1) * POOL_GC)
        ug = u[:, cols]
        wsum = ug
        for j in range(1, w):
            wsum = wsum + st_ref[POOL_STATE - j, :, cols]
        d = wsum / float(min(w, n_prev + 1)) - ug
        mixed = jnp.dot(d.astype(BF16), wp_ref[g], preferred_element_type=F32)
        po_ref[:, cols] = (mixed * sp_ref[:, cols]).astype(BF16)


def _proj_sample(x, g, w_in, cos, sin, state_t, w_pool, s_pool, n_prev):
    rows, d = x.shape
    wide = jax.ShapeDtypeStruct((rows, ATTN_WIDTH), F32)
    tall = jax.ShapeDtypeStruct((ATTN_WIDTH, rows), F32)
    full = lambda a: pl.BlockSpec(a.shape, lambda i, nd=len(a.shape): (0,) * nd)
    args = (x, g, w_in, cos, sin, state_t, w_pool, s_pool)
    outs = [wide, wide, wide, tall, tall, wide, jax.ShapeDtypeStruct((rows, POOL_WIDTH), BF16)]
    return pl.pallas_call(
        functools.partial(_proj_sample_kernel, n_prev=n_prev),
        grid=(1,),
        in_specs=[full(a) for a in args],
        out_specs=[full(o) for o in outs],
        out_shape=outs,
        compiler_params=_params("arbitrary"),
        name="proj_sample",
    )(*args)


def _moba_sample_kernel(pt_ref, q_ref, kn_ref, vn_ref, *rest, n_pages, page):
    del pt_ref
    k_pages = rest[:n_pages]
    v_pages = rest[n_pages:2 * n_pages]
    o_ref = rest[2 * n_pages]
    kb_ref, vb_ref = rest[2 * n_pages + 1:]
    past = n_pages * page
    nb = past // MOBA_BLOCK
    for p in range(n_pages):
        kb_ref[:, p * page:(p + 1) * page] = k_pages[p][0].astype(BF16)
        vb_ref[:, p * page:(p + 1) * page] = v_pages[p][0].astype(BF16)

    head_of_lane = lax.broadcasted_iota(jnp.int32, (N_HEADS, ATTN_WIDTH), 1) // HEAD_DIM
    head = lax.broadcasted_iota(jnp.int32, (N_HEADS, ATTN_WIDTH), 0)
    mine = head_of_lane == head
    q_heads = jnp.where(mine, q_ref[0], 0.0)
    lg = jnp.dot(q_heads.astype(BF16), kb_ref[...], preferred_element_type=F32)

    s = jnp.concatenate(
        [jnp.sum(lg[:, j * MOBA_BLOCK:(j + 1) * MOBA_BLOCK], axis=-1, keepdims=True) for j in range(nb)], axis=-1)
    blk_id = lax.broadcasted_iota(jnp.int32, (N_HEADS, nb), 1)
    sel = _topk_member(s, blk_id >= 0, blk_id, axis=1)
    key_blk = lax.broadcasted_iota(jnp.int32, (N_HEADS, past), 1) // MOBA_BLOCK
    chosen = jnp.zeros((N_HEADS, past), jnp.bool_)
    for j in range(nb):
        chosen = chosen | ((key_blk == j) & sel[:, j:j + 1])
    lg = jnp.where(chosen, lg, NEG)

    lg_new = jnp.sum(q_heads * kn_ref[0], axis=-1, keepdims=True)
    m = jnp.maximum(jnp.max(lg, axis=-1, keepdims=True), lg_new)
    p = jnp.exp(lg - m)
    p_new = jnp.exp(lg_new - m)
    l = jnp.sum(p, axis=-1, keepdims=True) + p_new
    acc = lax.dot_general(p.astype(BF16), vb_ref[...], NT_DIMS, preferred_element_type=F32) + p_new * vn_ref[0]
    out = jnp.where(mine, acc / l, 0.0)
    o_ref[0] = jnp.sum(out, axis=0, keepdims=True).astype(BF16)


def _moba_sample(page_table, q, k_new, v_new, cache_k, cache_v):
    b, n_pages = page_table.shape
    _, w, page = cache_k.shape
    past = n_pages * page
    tok = pl.BlockSpec((1, 1, w), lambda bi, pt: (bi, 0, 0))
    page_spec = lambda p: pl.BlockSpec((1, w, page), lambda bi, pt: (pt[bi, p], 0, 0))
    grid_spec = pltpu.PrefetchScalarGridSpec(
        num_scalar_prefetch=1,
        grid=(b,),
        in_specs=[tok, tok, tok] + [page_spec(p) for p in range(n_pages)] * 2,
        out_specs=tok,
        scratch_shapes=[pltpu.VMEM((w, past), BF16), pltpu.VMEM((w, past), BF16)],
    )
    return pl.pallas_call(
        functools.partial(_moba_sample_kernel, n_pages=n_pages, page=page),
        grid_spec=grid_spec,
        out_shape=jax.ShapeDtypeStruct((b, 1, w), BF16),
        compiler_params=_params("arbitrary"),
        name="moba_sample",
    )(page_table, q, k_new, v_new, *([cache_k] * n_pages), *([cache_v] * n_pages))


def _rope_tables(pos):
    half = HEAD_DIM // 2
    inv_freq = ROPE_THETA ** (-jnp.arange(half, dtype=F32) / half)
    ang = pos.astype(F32)[:, None] * inv_freq[None, :]
    cos = jnp.cos(ang)
    sin = jnp.sin(ang)
    cos_full = jnp.tile(jnp.concatenate([cos, cos], axis=-1), (1, N_HEADS))
    sin_signed = jnp.tile(jnp.concatenate([-sin, sin], axis=-1), (1, N_HEADS))
    return cos_full, sin_signed


def _post_mixer(x, po, at, cross, lp):
    x1, qc = _mix_out(x, po, at, lp["w_out"], lp["g_cross"], lp["w_cq"])
    return _mlp(x1, cross(qc), lp["w_co"], lp["g_mlp"], lp["w_up"], lp["w_down"], lp["g_final"])


def kernel(x_prompt, x_sample, cache_k, cache_v, cache_mem_k, cache_mem_v, state_pool, page_table, mem_prompt,
           g_mix, w_in, w_pool, s_pool, w_out, g_cross, g_mem, w_cq, w_ck, w_cv, w_co, g_mlp, w_up, w_down,
           g_final):
    depth = w_in.shape[0]
    assert depth == 1, "one decoder layer"
    bp, seq, d = x_prompt.shape
    bs, dec_seq, _ = x_sample.shape
    assert dec_seq == 1 and seq % MOBA_BLOCK == 0
    n_phys, page = cache_k.shape[1], cache_k.shape[2]
    n_pages = page_table.shape[1]
    past_len = n_pages * page
    assert past_len % MOBA_BLOCK == 0 and past_len >= POOL_STATE
    mem_len = mem_prompt.shape[1]

    l = 0
    row = lambda a: a.reshape(1, -1)
    lp = dict(w_out=w_out[l].astype(BF16), g_cross=row(g_cross[l]), w_cq=w_cq[l].astype(BF16),
              w_co=w_co[l].astype(BF16), g_mlp=row(g_mlp[l]), w_up=w_up[l].astype(BF16),
              w_down=w_down[l].astype(BF16), g_final=row(g_final))
    w_in_b = w_in[l].astype(BF16)
    w_pool_b = w_pool[l].astype(BF16)
    g_mix_r = row(g_mix[l])
    s_pool_r = row(s_pool[l])

    cos_p, sin_p = _rope_tables(jnp.arange(seq))
    qtb, kt_p, vt_p, kb, vtb, kmean, po_p, last_p = _proj_prompt(x_prompt, g_mix_r, w_in_b, cos_p, sin_p, cos_p.T,
                                                                 sin_p.T, w_pool_b, s_pool_r)
    at_p = _moba_prompt(qtb, kb, vtb, kmean.reshape(bp, seq // MOBA_BLOCK, ATTN_WIDTH))
    mk_p, mv_p, mkb, mvb = _mem_kv(mem_prompt.reshape(bp * mem_len, d), row(g_mem[l]), w_ck[l].astype(BF16),
                                   w_cv[l].astype(BF16))
    cross_p = lambda qc: _cross(qc.reshape(bp, seq, d), mkb.reshape(bp, mem_len, d), mvb.reshape(bp, mem_len, d),
                                512).reshape(bp * seq, d)
    flat = lambda a: a.reshape(-1, a.shape[-1])
    y_prompt = _post_mixer(flat(x_prompt), flat(po_p), flat(at_p), cross_p, lp).reshape(bp, seq, d)

    cos_s, sin_s = _rope_tables(jnp.full((1,), past_len))
    state_t = jnp.transpose(state_pool[l], (1, 0, 2))
    q_s, k_s, v_s, kt_s, vt_s, u_s, po_s = _proj_sample(x_sample.reshape(bs, d), g_mix_r, w_in_b, cos_s, sin_s,
                                                        state_t, w_pool_b, s_pool_r, past_len)
    tok = lambda a: a.reshape(bs, 1, -1)
    pages_t = lambda c: jnp.transpose(c, (0, 2, 3, 1)).reshape(n_phys, ATTN_WIDTH, page)
    at_s = _moba_sample(page_table, tok(q_s), tok(k_s), tok(v_s), pages_t(cache_k[l]), pages_t(cache_v[l]))
    cross_s = lambda qc: _cross_sample(qc, cache_mem_k[l], cache_mem_v[l])
    y_sample = _post_mixer(flat(x_sample), po_s, flat(at_s), cross_s, lp).reshape(bs, 1, d)

    def heads(t, b_):
        return jnp.transpose(t.reshape(b_, N_HEADS, HEAD_DIM, -1), (0, 3, 1, 2))[None]

    mem_heads = lambda a: a.reshape(1, bp, mem_len, MEM_HEADS, MEM_HEAD_DIM)
    pool_sample = jnp.transpose(jnp.concatenate([state_t[1:], u_s[None]], axis=0), (1, 0, 2))[None]
    return (y_prompt, y_sample, heads(kt_p, bp), heads(vt_p, bp), last_p[:, 1:][None], mem_heads(mk_p),
            mem_heads(mv_p), heads(kt_s.T.reshape(bs, ATTN_WIDTH, 1), bs), heads(vt_s.T.reshape(bs, ATTN_WIDTH, 1), bs),
            pool_sample)
```

```python
import functools

import jax
import jax.numpy as jnp
from jax import lax
from jax.experimental import pallas as pl
from jax.experimental.pallas import tpu as pltpu

D_MODEL = 1024
POOL_WIDTH = 512
POOL_WINDOWS = (2, 4, 8, 16)
POOL_GC = POOL_WIDTH // len(POOL_WINDOWS)
POOL_STATE = max(POOL_WINDOWS) - 1
POOL_HIST = POOL_STATE + 1
ATTN_WIDTH = D_MODEL - POOL_WIDTH
HEAD_DIM = 64
N_HEADS = ATTN_WIDTH // HEAD_DIM
MOBA_BLOCK = 256
MOBA_TOPK = 3
ROPE_THETA = 10000.0
MEM_HEADS = 4
MEM_HEAD_DIM = D_MODEL // MEM_HEADS
D_FF = 4 * D_MODEL
FF_CHUNK = 1024
EPS = 1e-6
NEG = -1e30

VMEM_LIMIT_BYTES = 56 * 1024 * 1024

F32 = jnp.float32
BF16 = jnp.bfloat16
NT_DIMS = (((1,), (1,)), ((), ()))


def _params(*semantics):
    return pltpu.CompilerParams(dimension_semantics=semantics, vmem_limit_bytes=VMEM_LIMIT_BYTES)


def _resident(shape):
    zeros = (0,) * len(shape)
    return pl.BlockSpec(shape, lambda *_: zeros, pipeline_mode=pl.Buffered(1))


def _rms(x, g):
    return x * lax.rsqrt(jnp.mean(x * x, axis=-1, keepdims=True) + EPS) * g


def _rope(t, cos, sin_signed):
    width = t.shape[-1]
    lane = lax.broadcasted_iota(jnp.int32, t.shape, t.ndim - 1)
    first_half = (lane % HEAD_DIM) < (HEAD_DIM // 2)
    partner = jnp.where(first_half,
                        pltpu.roll(t, width - HEAD_DIM // 2, t.ndim - 1),
                        pltpu.roll(t, HEAD_DIM // 2, t.ndim - 1))
    return t * cos + partner * sin_signed


def _rope_t(t, cos, sin_signed):
    rows = t.shape[0]
    row = lax.broadcasted_iota(jnp.int32, t.shape, 0)
    first_half = (row % HEAD_DIM) < (HEAD_DIM // 2)
    partner = jnp.where(first_half, pltpu.roll(t, rows - HEAD_DIM // 2, 0), pltpu.roll(t, HEAD_DIM // 2, 0))
    return t * cos + partner * sin_signed


def _topk_member(s, valid, idx, axis):
    n = s.shape[axis]
    s = jnp.where(valid, s, NEG)
    rank = jnp.zeros(s.shape, jnp.int32)
    for j in range(n):
        sj = lax.slice_in_dim(s, j, j + 1, axis=axis)
        beats = (sj > s) | ((sj == s) & (j < idx))
        rank = rank + beats.astype(jnp.int32)
    return (rank < MOBA_TOPK) & valid


def _proj_prompt_kernel(x_ref, g_ref, w_ref, cos_ref, sin_ref, cost_ref, sint_ref, wp_ref, sp_ref,
                        qt_ref, kt_ref, vt_ref, kb_ref, vtb_ref, km_ref, po_ref, last_ref, ext_ref):
    i = pl.program_id(1)
    tm = x_ref.shape[1]

    @pl.when(i == 0)
    def _():
        ext_ref[0:POOL_HIST, :] = jnp.zeros((POOL_HIST, POOL_WIDTH), F32)

    h = _rms(x_ref[0], g_ref[...]).astype(BF16)

    def project(first):
        return jnp.dot(h, w_ref[:, first:first + ATTN_WIDTH], preferred_element_type=F32)

    lane_reps = ATTN_WIDTH // cos_ref.shape[1]
    u = project(0)
    k = _rope(project(POOL_WIDTH + ATTN_WIDTH), jnp.tile(cos_ref[...], (1, lane_reps)),
              jnp.tile(sin_ref[...], (1, lane_reps)))
    q = project(POOL_WIDTH)
    v = project(POOL_WIDTH + 2 * ATTN_WIDTH)
    qt = _rope_t(q.T, jnp.tile(cost_ref[...], (N_HEADS, 1)), jnp.tile(sint_ref[...], (N_HEADS, 1)))
    qt_ref[0] = (qt * (HEAD_DIM ** -0.5 * LOG2_E)).astype(BF16)
    vt = v.T
    kt_ref[0] = k.T
    vt_ref[0] = vt
    kb_ref[0] = k.astype(BF16)
    vtb_ref[0, 0] = vt.astype(BF16)
    km_ref[0, 0] = jnp.mean(k, axis=0, keepdims=True)

    ext_ref[POOL_HIST:POOL_HIST + tm, :] = u
    pos = i * tm + lax.broadcasted_iota(jnp.int32, (tm, 1), 0)
    for g, w in enumerate(POOL_WINDOWS):
        cols = slice(g * POOL_GC, (g + 1) * POOL_GC)
        ug = u[:, cols]
        wsum = ug
        for j in range(1, w):
            wsum = wsum + ext_ref[POOL_HIST - j:POOL_HIST - j + tm, cols]
        cnt = jnp.minimum(w, pos + 1).astype(F32)
        d = wsum / cnt - ug
        mixed = jnp.dot(d.astype(BF16), wp_ref[g], preferred_element_type=F32)
        po_ref[0, :, cols] = (mixed * sp_ref[:, cols]).astype(BF16)
    tail = ext_ref[tm:tm + POOL_HIST, :]
    ext_ref[0:POOL_HIST, :] = tail
    last_ref[0] = tail


def _proj_prompt(x, g, w_in, cos, sin, cos_t, sin_t, w_pool, s_pool):
    b, s, d = x.shape
    tm = MOBA_BLOCK
    nq = s // tm
    row = lambda bi, i: (bi, i, 0)
    col = lambda bi, i: (bi, 0, i)
    blk = lambda bi, i: (bi, i, 0, 0)
    return pl.pallas_call(
        _proj_prompt_kernel,
        grid=(b, nq),
        in_specs=[
            pl.BlockSpec((1, tm, d), row),
            _resident((1, d)),
            _resident(w_in.shape),
            pl.BlockSpec((tm, cos.shape[1]), lambda bi, i: (i, 0)),
            pl.BlockSpec((tm, cos.shape[1]), lambda bi, i: (i, 0)),
            pl.BlockSpec((HEAD_DIM, tm), lambda bi, i: (0, i)),
            pl.BlockSpec((HEAD_DIM, tm), lambda bi, i: (0, i)),
            _resident(w_pool.shape),
            _resident((1, POOL_WIDTH)),
        ],
        out_specs=[
            pl.BlockSpec((1, ATTN_WIDTH, tm), col),
            pl.BlockSpec((1, ATTN_WIDTH, tm), col),
            pl.BlockSpec((1, ATTN_WIDTH, tm), col),
            pl.BlockSpec((1, tm, ATTN_WIDTH), row),
            pl.BlockSpec((1, 1, ATTN_WIDTH, tm), blk),
            pl.BlockSpec((1, 1, 1, ATTN_WIDTH), blk),
            pl.BlockSpec((1, tm, POOL_WIDTH), row),
            pl.BlockSpec((1, POOL_HIST, POOL_WIDTH), lambda bi, i: (bi, 0, 0)),
        ],
        out_shape=[
            jax.ShapeDtypeStruct((b, ATTN_WIDTH, s), BF16),
            jax.ShapeDtypeStruct((b, ATTN_WIDTH, s), F32),
            jax.ShapeDtypeStruct((b, ATTN_WIDTH, s), F32),
            jax.ShapeDtypeStruct((b, s, ATTN_WIDTH), BF16),
            jax.ShapeDtypeStruct((b, nq, ATTN_WIDTH, tm), BF16),
            jax.ShapeDtypeStruct((b, nq, 1, ATTN_WIDTH), F32),
            jax.ShapeDtypeStruct((b, s, POOL_WIDTH), BF16),
            jax.ShapeDtypeStruct((b, POOL_HIST, POOL_WIDTH), F32),
        ],
        scratch_shapes=[pltpu.VMEM((POOL_HIST + tm, POOL_WIDTH), F32)],
        compiler_params=_params("arbitrary", "arbitrary"),
        name="proj_prompt",
    )(x, g, w_in, cos, sin, cos_t, sin_t, w_pool, s_pool)


SUBLANES = 8
LOGIT_AHEAD = 2
LOGIT_SLOTS = 4
LOG2_E = 1.4426950408889634


def _all_sublanes(t, combine):
    for s in (4, 2, 1):
        t = combine(t, pltpu.roll(t, s, 0))
    return t


def _moba_kernel(pt_ref, qt_ref, kb_ref, vtb_ref, km_ref, qs_ref, kn_ref, vn_ref, *rest, n_pages, per_step):
    del pt_ref
    pages = rest[:2 * n_pages * per_step]
    o_ref, os_ref, acc_ref, m_ref, l_ref, take_ref, lg_ref = rest[2 * n_pages * per_step:]
    i = pl.program_id(1)
    tq = qt_ref.shape[2]
    nb = km_ref.shape[1]


    km_rows = jnp.concatenate([km_ref[0]] * N_HEADS, axis=0)
    row_head = lax.broadcasted_iota(jnp.int32, km_rows.shape, 0) // nb
    lane_head = lax.broadcasted_iota(jnp.int32, km_rows.shape, 1) // HEAD_DIM
    km_heads = jnp.where(row_head == lane_head, km_rows, 0.0).astype(BF16)
    s = jnp.dot(km_heads, qt_ref[0], preferred_element_type=F32).reshape(N_HEADS, nb, tq)
    blk = lax.broadcasted_iota(jnp.int32, s.shape, 1)
    sel = _topk_member(s, blk < i, blk, axis=1).astype(F32)

    for h in range(N_HEADS):
        for jb in range(nb):
            take_ref[h, jb] = jnp.broadcast_to(sel[h, jb:jb + 1, :], (SUBLANES, tq))

    def head_rows(h):
        return slice(h * HEAD_DIM, (h + 1) * HEAD_DIM)

    def stage_logits(h, j):
        off = pl.multiple_of(j * MOBA_BLOCK, MOBA_BLOCK)
        lg_ref[h % LOGIT_SLOTS] = jnp.dot(kb_ref[0, pl.ds(off, MOBA_BLOCK), head_rows(h)],
                                          qt_ref[0, head_rows(h), :], preferred_element_type=F32)

    def weighted_values(h, j, p3):
        p = p3.reshape(MOBA_BLOCK, tq).astype(BF16)
        pv = jnp.dot(vtb_ref[0, j, head_rows(h), :], p, preferred_element_type=F32)
        return pv.reshape(HEAD_DIM // SUBLANES, SUBLANES, tq)

    def tiles(t):
        return t.reshape(t.shape[0] // SUBLANES, SUBLANES, tq)

    def block_pass(j, j_next, own):
        for h in range(N_HEADS):
            ahead = h + LOGIT_AHEAD
            if ahead < N_HEADS:
                stage_logits(ahead, j)
            else:
                stage_logits(ahead - N_HEADS, j_next)
            lg = lg_ref[h % LOGIT_SLOTS]
            if own:
                kk = lax.broadcasted_iota(jnp.int32, lg.shape, 0)
                qq = lax.broadcasted_iota(jnp.int32, lg.shape, 1)
                lg3 = tiles(jnp.where(kk <= qq, lg, NEG))
                m = _all_sublanes(jnp.max(lg3, axis=0), jnp.maximum)
                p3 = jnp.exp2(lg3 - m[None])
                m_ref[h] = m
                l_ref[h] = _all_sublanes(jnp.sum(p3, axis=0), jnp.add)
                acc_ref[head_rows(h), :] = weighted_values(h, j, p3).reshape(HEAD_DIM, tq)
            else:
                lg3 = tiles(lg)
                take = take_ref[h, j] > 0.0
                m_old = m_ref[h]
                l_old = l_ref[h]
                acc_old = tiles(acc_ref[head_rows(h), :])
                m = jnp.maximum(m_old, _all_sublanes(jnp.max(lg3, axis=0), jnp.maximum))
                a = jnp.exp2(m_old - m)
                p3 = jnp.exp2(lg3 - m[None])
                l = a * l_old + _all_sublanes(jnp.sum(p3, axis=0), jnp.add)
                acc3 = a[None] * acc_old + weighted_values(h, j, p3)
                m_ref[h] = jnp.where(take, m, m_old)
                l_ref[h] = jnp.where(take, l, l_old)
                acc_ref[head_rows(h), :] = jnp.where(take[None], acc3, acc_old).reshape(HEAD_DIM, tq)

    last_past = jnp.maximum(i - 1, 0)
    for h in range(LOGIT_AHEAD):
        stage_logits(h, i)
    block_pass(i, 0, own=True)

    for s in range(per_step):
        k_pages = pages[2 * s * n_pages:(2 * s + 1) * n_pages]
        v_pages = pages[(2 * s + 1) * n_pages:(2 * s + 2) * n_pages]
        os_ref[s] = _sample_attend(qs_ref[s], kn_ref[s], vn_ref[s], k_pages, v_pages).astype(BF16)

    def past_block(j, carry):
        block_pass(j, jnp.minimum(j + 1, last_past), own=False)
        return carry

    lax.fori_loop(0, i, past_block, 0)
    for h in range(N_HEADS):
        rows = slice(h * HEAD_DIM, (h + 1) * HEAD_DIM)
        acc_ref[rows, :] = (tiles(acc_ref[rows, :]) / l_ref[h][None]).reshape(HEAD_DIM, tq)
    o_ref[0] = acc_ref[...].T.astype(BF16)


def _moba(qtb, kb, vtb, kmean, page_table, q_s, k_s, v_s, cache_kt, cache_vt):
    b, w, s = qtb.shape
    nq = s // MOBA_BLOCK
    bs, n_pages = page_table.shape
    page = cache_kt.shape[2]
    steps = b * nq
    assert bs % steps == 0
    per_step = bs // steps
    step_seq = lambda bi, i, pt: (bi * nq + i, 0, 0)
    tok = pl.BlockSpec((per_step, 1, w), step_seq)

    def page_spec(s_, p):
        return pl.BlockSpec((1, w, page), lambda bi, i, pt: (pt[(bi * nq + i) * per_step + s_, p], 0, 0))

    page_specs = [page_spec(s_, p) for s_ in range(per_step) for _ in range(2) for p in range(n_pages)]
    page_args = [c for _ in range(per_step) for c in (cache_kt, cache_vt) for _ in range(n_pages)]
    grid_spec = pltpu.PrefetchScalarGridSpec(
        num_scalar_prefetch=1,
        grid=(b, nq),
        in_specs=[
            pl.BlockSpec((1, w, MOBA_BLOCK), lambda bi, i, pt: (bi, 0, i)),
            pl.BlockSpec((1, s, w), lambda bi, i, pt: (bi, 0, 0)),
            pl.BlockSpec((1, nq, w, MOBA_BLOCK), lambda bi, i, pt: (bi, 0, 0, 0)),
            pl.BlockSpec((1, nq, w), lambda bi, i, pt: (bi, 0, 0)),
            tok, tok, tok,
        ] + page_specs,
        out_specs=[pl.BlockSpec((1, MOBA_BLOCK, w), lambda bi, i, pt: (bi, i, 0)), tok],
        scratch_shapes=[pltpu.VMEM((w, MOBA_BLOCK), F32),
                        pltpu.VMEM((N_HEADS, SUBLANES, MOBA_BLOCK), F32),
                        pltpu.VMEM((N_HEADS, SUBLANES, MOBA_BLOCK), F32),
                        pltpu.VMEM((N_HEADS, nq, SUBLANES, MOBA_BLOCK), F32),
                        pltpu.VMEM((LOGIT_SLOTS, MOBA_BLOCK, MOBA_BLOCK), F32)],
    )
    return pl.pallas_call(
        functools.partial(_moba_kernel, n_pages=n_pages, per_step=per_step),
        grid_spec=grid_spec,
        out_shape=[jax.ShapeDtypeStruct((b, s, w), BF16), jax.ShapeDtypeStruct((bs, 1, w), BF16)],
        compiler_params=_params("arbitrary", "arbitrary"),
        name="moba",
    )(page_table, qtb, kb, vtb, kmean, q_s, k_s, v_s, *page_args)


def _mem_kv_kernel(m_ref, g_ref, wk_ref, wv_ref, mk_ref, mv_ref, mkb_ref, mvb_ref):
    h = _rms(m_ref[...], g_ref[...]).astype(BF16)
    mk = jnp.dot(h, wk_ref[...], preferred_element_type=F32)
    mv = jnp.dot(h, wv_ref[...], preferred_element_type=F32)
    mk_ref[...] = mk
    mv_ref[...] = mv
    mkb_ref[...] = mk.astype(BF16)
    mvb_ref[...] = mv.astype(BF16)


def _mem_kv(mem, g, w_ck, w_cv):
    rows, d = mem.shape
    tm = 512
    row = lambda i: (i, 0)
    return pl.pallas_call(
        _mem_kv_kernel,
        grid=(rows // tm,),
        in_specs=[pl.BlockSpec((tm, d), row), _resident((1, d)), _resident(w_ck.shape), _resident(w_cv.shape)],
        out_specs=[pl.BlockSpec((tm, d), row)] * 4,
        out_shape=[jax.ShapeDtypeStruct((rows, d), F32)] * 2 + [jax.ShapeDtypeStruct((rows, d), BF16)] * 2,
        compiler_params=_params("arbitrary"),
        name="mem_kv",
    )(mem, g, w_ck, w_cv)


def _mix_out_kernel(x_ref, po_ref, at_ref, wo_ref, g_ref, wq_ref, x1_ref, qc_ref):
    cat = jnp.concatenate([po_ref[...], at_ref[...]], axis=-1)
    x1 = x_ref[...] + jnp.dot(cat, wo_ref[...], preferred_element_type=F32)
    x1_ref[...] = x1
    hq = _rms(x1, g_ref[...]).astype(BF16)
    qc = jnp.dot(hq, wq_ref[...], preferred_element_type=F32)
    qc_ref[...] = (qc * (MEM_HEAD_DIM ** -0.5)).astype(BF16)


def _mix_out(x, po, at, w_out, g_cross, w_cq):
    rows, d = x.shape
    tm = min(512, rows)
    row = lambda i: (i, 0)
    return pl.pallas_call(
        _mix_out_kernel,
        grid=(rows // tm,),
        in_specs=[pl.BlockSpec((tm, d), row), pl.BlockSpec((tm, POOL_WIDTH), row),
                  pl.BlockSpec((tm, ATTN_WIDTH), row), _resident(w_out.shape), _resident((1, d)),
                  _resident(w_cq.shape)],
        out_specs=[pl.BlockSpec((tm, d), row), pl.BlockSpec((tm, d), row)],
        out_shape=[jax.ShapeDtypeStruct((rows, d), F32), jax.ShapeDtypeStruct((rows, d), BF16)],
        compiler_params=_params("arbitrary"),
        name="mix_out",
    )(x, po, at, w_out, g_cross, w_cq)


def _cross_kernel(q_ref, mk_ref, mv_ref, o_ref):
    q = q_ref[0]
    for h in range(MEM_HEADS):
        sl = slice(h * MEM_HEAD_DIM, (h + 1) * MEM_HEAD_DIM)
        lg = lax.dot_general(q[:, sl], mk_ref[0, :, sl].astype(BF16), NT_DIMS, preferred_element_type=F32)
        p = jnp.exp(lg - jnp.max(lg, axis=-1, keepdims=True))
        l = jnp.sum(p, axis=-1, keepdims=True)
        o = jnp.dot(p.astype(BF16), mv_ref[0, :, sl].astype(BF16), preferred_element_type=F32)
        o_ref[0, :, sl] = (o / l).astype(BF16)


def _cross(qc, mk, mv, tm):
    b, s, d = qc.shape
    m = mk.shape[1]
    return pl.pallas_call(
        _cross_kernel,
        grid=(b, s // tm),
        in_specs=[pl.BlockSpec((1, tm, d), lambda bi, i: (bi, i, 0)),
                  pl.BlockSpec((1, m, d), lambda bi, i: (bi, 0, 0)),
                  pl.BlockSpec((1, m, d), lambda bi, i: (bi, 0, 0))],
        out_specs=pl.BlockSpec((1, tm, d), lambda bi, i: (bi, i, 0)),
        out_shape=jax.ShapeDtypeStruct((b, s, d), BF16),
        compiler_params=_params("arbitrary", "arbitrary"),
        name="cross_attn",
    )(qc, mk, mv)


MEM_LANES = 128
MEM_PARTS = MEM_HEAD_DIM // MEM_LANES
MEM_ROWS = MEM_PARTS * MEM_HEADS


def _cross_sample_kernel(q_ref, mk_ref, mv_ref, o_ref):
    n = mk_ref.shape[1]
    row = lax.broadcasted_iota(jnp.int32, (MEM_ROWS, n), 0)
    lane_row = lax.broadcasted_iota(jnp.int32, (MEM_ROWS, n), 1) % MEM_ROWS
    own = lane_row == row
    valid = (row < MEM_HEADS) & own

    def other_part(t, lane_shift):
        return pltpu.roll(pltpu.roll(t, MEM_HEADS, 0), lane_shift, 1)

    for s in range(q_ref.shape[0]):
        r = lax.dot_general(q_ref[s], mk_ref[s].astype(BF16), NT_DIMS, preferred_element_type=F32)
        r = jnp.where(own, r, 0.0)
        lg = jnp.where(valid, r + other_part(r, n - MEM_HEADS), NEG)
        p = jnp.where(valid, jnp.exp(lg - jnp.max(lg, axis=-1, keepdims=True)), 0.0)
        p = p + other_part(p, MEM_HEADS)
        l = jnp.sum(p, axis=-1, keepdims=True)
        o = jnp.dot(p.astype(BF16), mv_ref[s].astype(BF16), preferred_element_type=F32)
        o_ref[s] = (o / l).astype(BF16)


def _stored_rows(qc):
    rows = qc.shape[0]
    return jnp.transpose(qc.reshape(rows, MEM_HEADS, MEM_PARTS, MEM_LANES), (0, 2, 1, 3)).reshape(rows, MEM_ROWS, MEM_LANES)


def _cross_sample(qc, mem_k, mem_v):
    b, slots = mem_k.shape[:2]

    def stored(m):
        m = m.reshape(b, slots, MEM_HEADS, MEM_PARTS, MEM_LANES)
        return jnp.transpose(m, (0, 1, 3, 2, 4)).reshape(b, slots * MEM_ROWS, MEM_LANES)

    per_step = 4
    assert b % per_step == 0
    seq = lambda n: pl.BlockSpec((per_step, n, MEM_LANES), lambda bi: (bi, 0, 0))
    o = pl.pallas_call(
        _cross_sample_kernel,
        grid=(b // per_step,),
        in_specs=[seq(MEM_ROWS), seq(slots * MEM_ROWS), seq(slots * MEM_ROWS)],
        out_specs=seq(MEM_ROWS),
        out_shape=jax.ShapeDtypeStruct((b, MEM_ROWS, MEM_LANES), BF16),
        compiler_params=_params("arbitrary"),
        name="cross_sample",
    )(_stored_rows(qc), stored(mem_k), stored(mem_v))
    return jnp.transpose(o.reshape(b, MEM_PARTS, MEM_HEADS, MEM_LANES), (0, 2, 1, 3)).reshape(b, -1)


def _mlp_kernel(x1_ref, o_ref, wco_ref, gm_ref, wu_ref, wd_ref, gf_ref, y_ref):
    x2 = x1_ref[...] + jnp.dot(o_ref[...], wco_ref[...], preferred_element_type=F32)
    hm = _rms(x2, gm_ref[...]).astype(BF16)
    acc = x2
    for c in range(D_FF // FF_CHUNK):
        cs = slice(c * FF_CHUNK, (c + 1) * FF_CHUNK)
        a = jnp.dot(hm, wu_ref[:, cs], preferred_element_type=F32)
        a = jnp.square(jnp.maximum(a, 0.0)).astype(BF16)
        acc = acc + jnp.dot(a, wd_ref[cs, :], preferred_element_type=F32)
    y_ref[...] = _rms(acc, gf_ref[...])


def _mlp(x1, o, w_co, g_mlp, w_up, w_down, g_final):
    rows, d = x1.shape
    tm = min(512, rows)
    row = lambda i: (i, 0)
    return pl.pallas_call(
        _mlp_kernel,
        grid=(rows // tm,),
        in_specs=[pl.BlockSpec((tm, d), row), pl.BlockSpec((tm, d), row), _resident(w_co.shape),
                  _resident((1, d)), _resident(w_up.shape), _resident(w_down.shape), _resident((1, d))],
        out_specs=pl.BlockSpec((tm, d), row),
        out_shape=jax.ShapeDtypeStruct((rows, d), F32),
        compiler_params=_params("arbitrary"),
        name="mlp",
    )(x1, o, w_co, g_mlp, w_up, w_down, g_final)


def _proj_sample_kernel(x_ref, g_ref, w_ref, cos_ref, sin_ref, st_ref, wp_ref, sp_ref,
                        q_ref, k_ref, v_ref, kt_ref, vt_ref, u_ref, po_ref, *, n_prev):
    h = _rms(x_ref[...], g_ref[...]).astype(BF16)
    proj = jnp.dot(h, w_ref[...], preferred_element_type=F32)
    u = proj[:, :POOL_WIDTH]
    cos = cos_ref[...]
    sin = sin_ref[...]
    q_ref[...] = _rope(proj[:, POOL_WIDTH:POOL_WIDTH + ATTN_WIDTH], cos, sin) * (HEAD_DIM ** -0.5)
    k = _rope(proj[:, POOL_WIDTH + ATTN_WIDTH:POOL_WIDTH + 2 * ATTN_WIDTH], cos, sin)
    v = proj[:, POOL_WIDTH + 2 * ATTN_WIDTH:]
    k_ref[...] = k
    v_ref[...] = v
    kt_ref[...] = k.T
    vt_ref[...] = v.T
    u_ref[...] = u
    for g, w in enumerate(POOL_WINDOWS):
        cols = slice(g * POOL_GC, (g + 1) * POOL_GC)
        ug = u[:, cols]
        wsum = ug
        for j in range(1, w):
            wsum = wsum + st_ref[POOL_STATE - j, :, cols]
        d = wsum / float(min(w, n_prev + 1)) - ug
        mixed = jnp.dot(d.astype(BF16), wp_ref[g], preferred_element_type=F32)
        po_ref[:, cols] = (mixed * sp_ref[:, cols]).astype(BF16)


def _proj_sample(x, g, w_in, cos, sin, state_t, w_pool, s_pool, n_prev):
    rows, d = x.shape
    wide = jax.ShapeDtypeStruct((rows, ATTN_WIDTH), F32)
    tall = jax.ShapeDtypeStruct((ATTN_WIDTH, rows), F32)
    full = lambda a: pl.BlockSpec(a.shape, lambda i, nd=len(a.shape): (0,) * nd)
    args = (x, g, w_in, cos, sin, state_t, w_pool, s_pool)
    outs = [wide, wide, wide, tall, tall, wide, jax.ShapeDtypeStruct((rows, POOL_WIDTH), BF16)]
    return pl.pallas_call(
        functools.partial(_proj_sample_kernel, n_prev=n_prev),
        grid=(1,),
        in_specs=[full(a) for a in args],
        out_specs=[full(o) for o in outs],
        out_shape=outs,
        compiler_params=_params("arbitrary"),
        name="proj_sample",
    )(*args)


def _sample_attend(q, k_new, v_new, k_pages, v_pages):
    n_pages = len(k_pages)
    page = k_pages[0].shape[2]
    past = n_pages * page
    nb = past // MOBA_BLOCK
    head_of_lane = lax.broadcasted_iota(jnp.int32, (N_HEADS, ATTN_WIDTH), 1) // HEAD_DIM
    head = lax.broadcasted_iota(jnp.int32, (N_HEADS, ATTN_WIDTH), 0)
    mine = head_of_lane == head
    q_heads = jnp.where(mine, q, 0.0)
    q_heads_b = q_heads.astype(BF16)
    lg = jnp.concatenate([jnp.dot(q_heads_b, kp[0].astype(BF16), preferred_element_type=F32) for kp in k_pages],
                         axis=-1)

    s = jnp.concatenate(
        [jnp.sum(lg[:, j * MOBA_BLOCK:(j + 1) * MOBA_BLOCK], axis=-1, keepdims=True) for j in range(nb)], axis=-1)
    blk_id = lax.broadcasted_iota(jnp.int32, (N_HEADS, nb), 1)
    sel = _topk_member(s, blk_id >= 0, blk_id, axis=1)
    key_blk = lax.broadcasted_iota(jnp.int32, (N_HEADS, past), 1) // MOBA_BLOCK
    chosen = jnp.zeros((N_HEADS, past), jnp.bool_)
    for j in range(nb):
        chosen = chosen | ((key_blk == j) & sel[:, j:j + 1])
    lg = jnp.where(chosen, lg, NEG)

    lg_new = jnp.sum(q_heads * k_new, axis=-1, keepdims=True)
    m = jnp.maximum(jnp.max(lg, axis=-1, keepdims=True), lg_new)
    p = jnp.exp(lg - m)
    p_new = jnp.exp(lg_new - m)
    l = jnp.sum(p, axis=-1, keepdims=True) + p_new

    lanes = k_pages[0].shape[2]
    sub = lax.broadcasted_iota(jnp.int32, (HEAD_DIM, lanes), 0)
    lane = lax.broadcasted_iota(jnp.int32, (HEAD_DIM, lanes), 1)
    tiles = []
    for first in range(0, N_HEADS, lanes // HEAD_DIM):
        row = jnp.zeros((1, lanes), F32)
        for h in range(first, first + lanes // HEAD_DIM):
            acc = jnp.zeros((HEAD_DIM, lanes), F32)
            for n, vp in enumerate(v_pages):
                w = jnp.broadcast_to(p[h:h + 1, n * page:(n + 1) * page], (HEAD_DIM, page))
                acc = acc + vp[0, h * HEAD_DIM:(h + 1) * HEAD_DIM, :] * w
            total = jnp.sum(acc, axis=1, keepdims=True)
            on_diag = lane == sub + (h - first) * HEAD_DIM
            row = row + jnp.sum(jnp.where(on_diag, total, 0.0), axis=0, keepdims=True)
        tiles.append(row)
    pv = jnp.concatenate(tiles, axis=-1)
    per_lane = lambda t: jnp.sum(jnp.where(mine, t, 0.0), axis=0, keepdims=True)
    return (pv + per_lane(p_new) * v_new) / per_lane(l)


def _rope_tables(pos, heads):
    half = HEAD_DIM // 2
    inv_freq = ROPE_THETA ** (-jnp.arange(half, dtype=F32) / half)
    ang = pos.astype(F32)[:, None] * inv_freq[None, :]
    cos = jnp.cos(ang)
    sin = jnp.sin(ang)
    cos_full = jnp.tile(jnp.concatenate([cos, cos], axis=-1), (1, heads))
    sin_signed = jnp.tile(jnp.concatenate([-sin, sin], axis=-1), (1, heads))
    return cos_full, sin_signed


def _post_mixer(x, po, at, cross, lp):
    x1, qc = _mix_out(x, po, at, lp["w_out"], lp["g_cross"], lp["w_cq"])
    return _mlp(x1, cross(qc), lp["w_co"], lp["g_mlp"], lp["w_up"], lp["w_down"], lp["g_final"])


def kernel(x_prompt, x_sample, cache_k, cache_v, cache_mem_k, cache_mem_v, state_pool, page_table, mem_prompt,
           g_mix, w_in, w_pool, s_pool, w_out, g_cross, g_mem, w_cq, w_ck, w_cv, w_co, g_mlp, w_up, w_down,
           g_final):
    depth = w_in.shape[0]
    assert depth == 1, "one decoder layer"
    bp, seq, d = x_prompt.shape
    bs, dec_seq, _ = x_sample.shape
    assert dec_seq == 1 and seq % MOBA_BLOCK == 0
    n_phys, page = cache_k.shape[1], cache_k.shape[2]
    n_pages = page_table.shape[1]
    past_len = n_pages * page
    assert past_len % MOBA_BLOCK == 0 and past_len >= POOL_STATE
    mem_len = mem_prompt.shape[1]

    l = 0
    row = lambda a: a.reshape(1, -1)
    lp = dict(w_out=w_out[l].astype(BF16), g_cross=row(g_cross[l]), w_cq=w_cq[l].astype(BF16),
              w_co=w_co[l].astype(BF16), g_mlp=row(g_mlp[l]), w_up=w_up[l].astype(BF16),
              w_down=w_down[l].astype(BF16), g_final=row(g_final))
    w_in_b = w_in[l].astype(BF16)
    w_pool_b = w_pool[l].astype(BF16)
    g_mix_r = row(g_mix[l])
    s_pool_r = row(s_pool[l])

    lane_heads = 128 // HEAD_DIM
    cos_p, sin_p = _rope_tables(jnp.arange(seq), lane_heads)
    qtb, kt_p, vt_p, kb, vtb, kmean, po_p, last_p = _proj_prompt(
        x_prompt, g_mix_r, w_in_b, cos_p, sin_p, cos_p[:, :HEAD_DIM].T, sin_p[:, :HEAD_DIM].T, w_pool_b, s_pool_r)
    cos_s, sin_s = _rope_tables(jnp.full((1,), past_len), N_HEADS)
    state_t = jnp.transpose(state_pool[l], (1, 0, 2))
    q_s, k_s, v_s, kt_s, vt_s, u_s, po_s = _proj_sample(x_sample.reshape(bs, d), g_mix_r, w_in_b, cos_s, sin_s,
                                                        state_t, w_pool_b, s_pool_r, past_len)

    tok = lambda a: a.reshape(bs, 1, -1)
    pages_t = lambda c: jnp.transpose(c, (0, 2, 3, 1)).reshape(n_phys, ATTN_WIDTH, page)
    at_p, at_s = _moba(qtb, kb, vtb, kmean.reshape(bp, seq // MOBA_BLOCK, ATTN_WIDTH), page_table, tok(q_s),
                       tok(k_s), tok(v_s), pages_t(cache_k[l]), pages_t(cache_v[l]))

    mk_p, mv_p, mkb, mvb = _mem_kv(mem_prompt.reshape(bp * mem_len, d), row(g_mem[l]), w_ck[l].astype(BF16),
                                   w_cv[l].astype(BF16))
    cross_p = lambda qc: _cross(qc.reshape(bp, seq, d), mkb.reshape(bp, mem_len, d), mvb.reshape(bp, mem_len, d),
                                512).reshape(bp * seq, d)
    flat = lambda a: a.reshape(-1, a.shape[-1])
    y_prompt = _post_mixer(flat(x_prompt), flat(po_p), flat(at_p), cross_p, lp).reshape(bp, seq, d)
    cross_s = lambda qc: _cross_sample(qc, cache_mem_k[l], cache_mem_v[l])
    y_sample = _post_mixer(flat(x_sample), po_s, flat(at_s), cross_s, lp).reshape(bs, 1, d)

    def heads(t, b_):
        return jnp.transpose(t.reshape(b_, N_HEADS, HEAD_DIM, -1), (0, 3, 1, 2))[None]

    mem_heads = lambda a: a.reshape(1, bp, mem_len, MEM_HEADS, MEM_HEAD_DIM)
    pool_sample = jnp.transpose(jnp.concatenate([state_t[1:], u_s[None]], axis=0), (1, 0, 2))[None]
    return (y_prompt, y_sample, heads(kt_p, bp), heads(vt_p, bp), last_p[:, 1:][None], mem_heads(mk_p),
            mem_heads(mv_p), heads(kt_s.T.reshape(bs, ATTN_WIDTH, 1), bs), heads(vt_s.T.reshape(bs, ATTN_WIDTH, 1), bs),
            pool_sample)
```

```python
import functools

import jax
import jax.numpy as jnp
from jax import lax
from jax.experimental import pallas as pl
from jax.experimental.pallas import tpu as pltpu

D_MODEL = 1024
POOL_WIDTH = 512
POOL_WINDOWS = (2, 4, 8, 16)
POOL_GC = POOL_WIDTH // len(POOL_WINDOWS)
POOL_STATE = max(POOL_WINDOWS) - 1
POOL_HIST = POOL_STATE + 1
ATTN_WIDTH = D_MODEL - POOL_WIDTH
HEAD_DIM = 64
N_HEADS = ATTN_WIDTH // HEAD_DIM
MOBA_BLOCK = 256
MOBA_TOPK = 3
ROPE_THETA = 10000.0
MEM_HEADS = 4
MEM_HEAD_DIM = D_MODEL // MEM_HEADS
D_FF = 4 * D_MODEL
FF_CHUNK = 1024
EPS = 1e-6
NEG = -1e30

VMEM_LIMIT_BYTES = 56 * 1024 * 1024

F32 = jnp.float32
BF16 = jnp.bfloat16
NT_DIMS = (((1,), (1,)), ((), ()))


def _params(*semantics):
    return pltpu.CompilerParams(dimension_semantics=semantics, vmem_limit_bytes=VMEM_LIMIT_BYTES)


def _resident(shape):
    zeros = (0,) * len(shape)
    return pl.BlockSpec(shape, lambda *_: zeros, pipeline_mode=pl.Buffered(1))


def _rms(x, g):
    return x * lax.rsqrt(jnp.mean(x * x, axis=-1, keepdims=True) + EPS) * g


def _rope(t, cos, sin_signed):
    width = t.shape[-1]
    lane = lax.broadcasted_iota(jnp.int32, t.shape, t.ndim - 1)
    first_half = (lane % HEAD_DIM) < (HEAD_DIM // 2)
    partner = jnp.where(first_half,
                        pltpu.roll(t, width - HEAD_DIM // 2, t.ndim - 1),
                        pltpu.roll(t, HEAD_DIM // 2, t.ndim - 1))
    return t * cos + partner * sin_signed


def _rope_t(t, cos, sin_signed):
    rows = t.shape[0]
    row = lax.broadcasted_iota(jnp.int32, t.shape, 0)
    first_half = (row % HEAD_DIM) < (HEAD_DIM // 2)
    partner = jnp.where(first_half, pltpu.roll(t, rows - HEAD_DIM // 2, 0), pltpu.roll(t, HEAD_DIM // 2, 0))
    return t * cos + partner * sin_signed


def _topk_member(s, valid, idx, axis):
    n = s.shape[axis]
    s = jnp.where(valid, s, NEG)
    rank = jnp.zeros(s.shape, jnp.int32)
    for j in range(n):
        sj = lax.slice_in_dim(s, j, j + 1, axis=axis)
        beats = (sj > s) | ((sj == s) & (j < idx))
        rank = rank + beats.astype(jnp.int32)
    return (rank < MOBA_TOPK) & valid


def _proj_prompt_kernel(x_ref, g_ref, w_ref, cos_ref, sin_ref, cost_ref, sint_ref, wp_ref, sp_ref,
                        qt_ref, kt_ref, vt_ref, kb_ref, vtb_ref, km_ref, po_ref, last_ref, ext_ref):
    i = pl.program_id(1)
    tm = x_ref.shape[1]

    @pl.when(i == 0)
    def _():
        ext_ref[0:POOL_HIST, :] = jnp.zeros((POOL_HIST, POOL_WIDTH), F32)

    h = _rms(x_ref[0], g_ref[...]).astype(BF16)

    def project(first):
        return jnp.dot(h, w_ref[:, first:first + ATTN_WIDTH], preferred_element_type=F32)

    lane_reps = ATTN_WIDTH // cos_ref.shape[1]
    u = project(0)
    k = _rope(project(POOL_WIDTH + ATTN_WIDTH), jnp.tile(cos_ref[...], (1, lane_reps)),
              jnp.tile(sin_ref[...], (1, lane_reps)))
    q = project(POOL_WIDTH)
    v = project(POOL_WIDTH + 2 * ATTN_WIDTH)
    qt = _rope_t(q.T, jnp.tile(cost_ref[...], (N_HEADS, 1)), jnp.tile(sint_ref[...], (N_HEADS, 1)))
    qt_ref[0] = (qt * (HEAD_DIM ** -0.5 * LOG2_E)).astype(BF16)
    vt = v.T
    kt_ref[0] = k.T
    vt_ref[0] = vt
    kb_ref[0] = k.astype(BF16)
    vtb_ref[0, 0] = vt.astype(BF16)
    km_ref[0, 0] = jnp.mean(k, axis=0, keepdims=True)

    ext_ref[POOL_HIST:POOL_HIST + tm, :] = u
    pos = i * tm + lax.broadcasted_iota(jnp.int32, (tm, 1), 0)
    for g, w in enumerate(POOL_WINDOWS):
        cols = slice(g * POOL_GC, (g + 1) * POOL_GC)
        ug = u[:, cols]
        wsum = ug
        for j in range(1, w):
            wsum = wsum + ext_ref[POOL_HIST - j:POOL_HIST - j + tm, cols]
        cnt = jnp.minimum(w, pos + 1).astype(F32)
        d = wsum / cnt - ug
        mixed = jnp.dot(d.astype(BF16), wp_ref[g], preferred_element_type=F32)
        po_ref[0, :, cols] = (mixed * sp_ref[:, cols]).astype(BF16)
    tail = ext_ref[tm:tm + POOL_HIST, :]
    ext_ref[0:POOL_HIST, :] = tail
    last_ref[0] = tail


def _proj_prompt(x, g, w_in, cos, sin, cos_t, sin_t, w_pool, s_pool):
    b, s, d = x.shape
    tm = MOBA_BLOCK
    nq = s // tm
    row = lambda bi, i: (bi, i, 0)
    col = lambda bi, i: (bi, 0, i)
    blk = lambda bi, i: (bi, i, 0, 0)
    return pl.pallas_call(
        _proj_prompt_kernel,
        grid=(b, nq),
        in_specs=[
            pl.BlockSpec((1, tm, d), row),
            _resident((1, d)),
            _resident(w_in.shape),
            pl.BlockSpec((tm, cos.shape[1]), lambda bi, i: (i, 0)),
            pl.BlockSpec((tm, cos.shape[1]), lambda bi, i: (i, 0)),
            pl.BlockSpec((HEAD_DIM, tm), lambda bi, i: (0, i)),
            pl.BlockSpec((HEAD_DIM, tm), lambda bi, i: (0, i)),
            _resident(w_pool.shape),
            _resident((1, POOL_WIDTH)),
        ],
        out_specs=[
            pl.BlockSpec((1, ATTN_WIDTH, tm), col),
            pl.BlockSpec((1, ATTN_WIDTH, tm), col),
            pl.BlockSpec((1, ATTN_WIDTH, tm), col),
            pl.BlockSpec((1, tm, ATTN_WIDTH), row),
            pl.BlockSpec((1, 1, ATTN_WIDTH, tm), blk),
            pl.BlockSpec((1, 1, 1, ATTN_WIDTH), blk),
            pl.BlockSpec((1, tm, POOL_WIDTH), row),
            pl.BlockSpec((1, POOL_HIST, POOL_WIDTH), lambda bi, i: (bi, 0, 0)),
        ],
        out_shape=[
            jax.ShapeDtypeStruct((b, ATTN_WIDTH, s), BF16),
            jax.ShapeDtypeStruct((b, ATTN_WIDTH, s), F32),
            jax.ShapeDtypeStruct((b, ATTN_WIDTH, s), F32),
            jax.ShapeDtypeStruct((b, s, ATTN_WIDTH), BF16),
            jax.ShapeDtypeStruct((b, nq, ATTN_WIDTH, tm), BF16),
            jax.ShapeDtypeStruct((b, nq, 1, ATTN_WIDTH), F32),
            jax.ShapeDtypeStruct((b, s, POOL_WIDTH), BF16),
            jax.ShapeDtypeStruct((b, POOL_HIST, POOL_WIDTH), F32),
        ],
        scratch_shapes=[pltpu.VMEM((POOL_HIST + tm, POOL_WIDTH), F32)],
        compiler_params=_params("arbitrary", "arbitrary"),
        name="proj_prompt",
    )(x, g, w_in, cos, sin, cos_t, sin_t, w_pool, s_pool)


SUBLANES = 8
LOGIT_AHEAD = 2
LOGIT_SLOTS = 4
LOG2_E = 1.4426950408889634


def _all_sublanes(t, combine):
    for s in (4, 2, 1):
        t = combine(t, pltpu.roll(t, s, 0))
    return t


def _moba_kernel(pt_ref, qt_ref, kb_ref, vtb_ref, km_ref, qs_ref, kn_ref, vn_ref, ck_ref, cv_ref,
                 o_ref, os_ref, acc_ref, m_ref, l_ref, take_ref, lg_ref, kbuf_ref, vbuf_ref, page_sem,
                 *, n_pages, per_step):
    i = pl.program_id(1)
    step = pl.program_id(0) * pl.num_programs(1) + i
    n_steps = pl.num_programs(0) * pl.num_programs(1)
    slot = step % 2
    tq = qt_ref.shape[2]
    nb = km_ref.shape[1]

    def page_copies(page_id, half):
        copies = []
        for s in range(per_step):
            for p in range(n_pages):
                pid = page_id(s, p)
                dst = s * n_pages + p
                copies.append(pltpu.make_async_copy(ck_ref.at[pid], kbuf_ref.at[half, dst], page_sem.at[half, 0]))
                copies.append(pltpu.make_async_copy(cv_ref.at[pid], vbuf_ref.at[half, dst], page_sem.at[half, 1]))
        return copies

    def start_step(t, half):
        for c in page_copies(lambda s, p: pt_ref[t * per_step + s, p], half):
            c.start()

    @pl.when(step == 0)
    def _():
        start_step(0, 0)

    @pl.when(step + 1 < n_steps)
    def _():
        start_step(step + 1, 1 - slot)


    km_rows = jnp.concatenate([km_ref[0]] * N_HEADS, axis=0)
    row_head = lax.broadcasted_iota(jnp.int32, km_rows.shape, 0) // nb
    lane_head = lax.broadcasted_iota(jnp.int32, km_rows.shape, 1) // HEAD_DIM
    km_heads = jnp.where(row_head == lane_head, km_rows, 0.0).astype(BF16)
    s = jnp.dot(km_heads, qt_ref[0], preferred_element_type=F32).reshape(N_HEADS, nb, tq)
    blk = lax.broadcasted_iota(jnp.int32, s.shape, 1)
    sel = _topk_member(s, blk < i, blk, axis=1).astype(F32)

    for h in range(N_HEADS):
        for jb in range(nb):
            take_ref[h, jb] = jnp.broadcast_to(sel[h, jb:jb + 1, :], (SUBLANES, tq))

    def head_rows(h):
        return slice(h * HEAD_DIM, (h + 1) * HEAD_DIM)

    def stage_logits(h, j):
        off = pl.multiple_of(j * MOBA_BLOCK, MOBA_BLOCK)
        lg_ref[h % LOGIT_SLOTS] = jnp.dot(kb_ref[0, pl.ds(off, MOBA_BLOCK), head_rows(h)],
                                          qt_ref[0, head_rows(h), :], preferred_element_type=F32)

    def weighted_values(h, j, p3):
        p = p3.reshape(MOBA_BLOCK, tq).astype(BF16)
        pv = jnp.dot(vtb_ref[0, j, head_rows(h), :], p, preferred_element_type=F32)
        return pv.reshape(HEAD_DIM // SUBLANES, SUBLANES, tq)

    def tiles(t):
        return t.reshape(t.shape[0] // SUBLANES, SUBLANES, tq)

    def block_pass(j, j_next, own):
        for h in range(N_HEADS):
            ahead = h + LOGIT_AHEAD
            if ahead < N_HEADS:
                stage_logits(ahead, j)
            else:
                stage_logits(ahead - N_HEADS, j_next)
            lg = lg_ref[h % LOGIT_SLOTS]
            if own:
                kk = lax.broadcasted_iota(jnp.int32, lg.shape, 0)
                qq = lax.broadcasted_iota(jnp.int32, lg.shape, 1)
                lg3 = tiles(jnp.where(kk <= qq, lg, NEG))
                m = _all_sublanes(jnp.max(lg3, axis=0), jnp.maximum)
                p3 = jnp.exp2(lg3 - m[None])
                m_ref[h] = m
                l_ref[h] = _all_sublanes(jnp.sum(p3, axis=0), jnp.add)
                acc_ref[head_rows(h), :] = weighted_values(h, j, p3).reshape(HEAD_DIM, tq)
            else:
                lg3 = tiles(lg)
                take = take_ref[h, j] > 0.0
                m_old = m_ref[h]
                l_old = l_ref[h]
                acc_old = tiles(acc_ref[head_rows(h), :])
                m = jnp.maximum(m_old, _all_sublanes(jnp.max(lg3, axis=0), jnp.maximum))
                a = jnp.exp2(m_old - m)
                p3 = jnp.exp2(lg3 - m[None])
                l = a * l_old + _all_sublanes(jnp.sum(p3, axis=0), jnp.add)
                acc3 = a[None] * acc_old + weighted_values(h, j, p3)
                m_ref[h] = jnp.where(take, m, m_old)
                l_ref[h] = jnp.where(take, l, l_old)
                acc_ref[head_rows(h), :] = jnp.where(take[None], acc3, acc_old).reshape(HEAD_DIM, tq)

    last_past = jnp.maximum(i - 1, 0)
    for h in range(LOGIT_AHEAD):
        stage_logits(h, i)
    block_pass(i, 0, own=True)

    for c in page_copies(lambda s, p: 0, slot):
        c.wait()
    for s in range(per_step):
        k_pages = [kbuf_ref.at[slot, s * n_pages + p] for p in range(n_pages)]
        v_pages = [vbuf_ref.at[slot, s * n_pages + p] for p in range(n_pages)]
        os_ref[s] = _sample_attend(qs_ref[s], kn_ref[s], vn_ref[s], k_pages, v_pages).astype(BF16)

    def past_block(j, carry):
        block_pass(j, jnp.minimum(j + 1, last_past), own=False)
        return carry

    lax.fori_loop(0, i, past_block, 0)
    for h in range(N_HEADS):
        rows = slice(h * HEAD_DIM, (h + 1) * HEAD_DIM)
        acc_ref[rows, :] = (tiles(acc_ref[rows, :]) / l_ref[h][None]).reshape(HEAD_DIM, tq)
    o_ref[0] = acc_ref[...].T.astype(BF16)


def _moba(qtb, kb, vtb, kmean, page_table, q_s, k_s, v_s, cache_kt, cache_vt):
    b, w, s = qtb.shape
    nq = s // MOBA_BLOCK
    bs, n_pages = page_table.shape
    page = cache_kt.shape[2]
    steps = b * nq
    assert bs % steps == 0
    per_step = bs // steps
    step_seq = lambda bi, i, pt: (bi * nq + i, 0, 0)
    tok = pl.BlockSpec((per_step, 1, w), step_seq)
    in_hbm = pl.BlockSpec(memory_space=pl.ANY)
    page_buf = pltpu.VMEM((2, per_step * n_pages, w, page), cache_kt.dtype)
    grid_spec = pltpu.PrefetchScalarGridSpec(
        num_scalar_prefetch=1,
        grid=(b, nq),
        in_specs=[
            pl.BlockSpec((1, w, MOBA_BLOCK), lambda bi, i, pt: (bi, 0, i)),
            pl.BlockSpec((1, s, w), lambda bi, i, pt: (bi, 0, 0)),
            pl.BlockSpec((1, nq, w, MOBA_BLOCK), lambda bi, i, pt: (bi, 0, 0, 0)),
            pl.BlockSpec((1, nq, w), lambda bi, i, pt: (bi, 0, 0)),
            tok, tok, tok, in_hbm, in_hbm,
        ],
        out_specs=[pl.BlockSpec((1, MOBA_BLOCK, w), lambda bi, i, pt: (bi, i, 0)), tok],
        scratch_shapes=[pltpu.VMEM((w, MOBA_BLOCK), F32),
                        pltpu.VMEM((N_HEADS, SUBLANES, MOBA_BLOCK), F32),
                        pltpu.VMEM((N_HEADS, SUBLANES, MOBA_BLOCK), F32),
                        pltpu.VMEM((N_HEADS, nq, SUBLANES, MOBA_BLOCK), F32),
                        pltpu.VMEM((LOGIT_SLOTS, MOBA_BLOCK, MOBA_BLOCK), F32),
                        page_buf, page_buf, pltpu.SemaphoreType.DMA((2, 2))],
    )
    return pl.pallas_call(
        functools.partial(_moba_kernel, n_pages=n_pages, per_step=per_step),
        grid_spec=grid_spec,
        out_shape=[jax.ShapeDtypeStruct((b, s, w), BF16), jax.ShapeDtypeStruct((bs, 1, w), BF16)],
        compiler_params=_params("arbitrary", "arbitrary"),
        name="moba",
    )(page_table, qtb, kb, vtb, kmean, q_s, k_s, v_s, cache_kt, cache_vt)


def _mem_kv_kernel(m_ref, g_ref, wk_ref, wv_ref, mk_ref, mv_ref, mkb_ref, mvb_ref):
    h = _rms(m_ref[...], g_ref[...]).astype(BF16)
    mk = jnp.dot(h, wk_ref[...], preferred_element_type=F32)
    mv = jnp.dot(h, wv_ref[...], preferred_element_type=F32)
    mk_ref[...] = mk
    mv_ref[...] = mv
    mkb_ref[...] = mk.astype(BF16)
    mvb_ref[...] = mv.astype(BF16)


def _mem_kv(mem, g, w_ck, w_cv):
    rows, d = mem.shape
    tm = 512
    row = lambda i: (i, 0)
    return pl.pallas_call(
        _mem_kv_kernel,
        grid=(rows // tm,),
        in_specs=[pl.BlockSpec((tm, d), row), _resident((1, d)), _resident(w_ck.shape), _resident(w_cv.shape)],
        out_specs=[pl.BlockSpec((tm, d), row)] * 4,
        out_shape=[jax.ShapeDtypeStruct((rows, d), F32)] * 2 + [jax.ShapeDtypeStruct((rows, d), BF16)] * 2,
        compiler_params=_params("arbitrary"),
        name="mem_kv",
    )(mem, g, w_ck, w_cv)


def _mix_out_kernel(x_ref, po_ref, at_ref, wo_ref, g_ref, wq_ref, x1_ref, qc_ref):
    cat = jnp.concatenate([po_ref[...], at_ref[...]], axis=-1)
    x1 = x_ref[...] + jnp.dot(cat, wo_ref[...], preferred_element_type=F32)
    x1_ref[...] = x1
    hq = _rms(x1, g_ref[...]).astype(BF16)
    qc = jnp.dot(hq, wq_ref[...], preferred_element_type=F32)
    qc_ref[...] = (qc * (MEM_HEAD_DIM ** -0.5)).astype(BF16)


def _mix_out(x, po, at, w_out, g_cross, w_cq):
    rows, d = x.shape
    tm = min(512, rows)
    row = lambda i: (i, 0)
    return pl.pallas_call(
        _mix_out_kernel,
        grid=(rows // tm,),
        in_specs=[pl.BlockSpec((tm, d), row), pl.BlockSpec((tm, POOL_WIDTH), row),
                  pl.BlockSpec((tm, ATTN_WIDTH), row), _resident(w_out.shape), _resident((1, d)),
                  _resident(w_cq.shape)],
        out_specs=[pl.BlockSpec((tm, d), row), pl.BlockSpec((tm, d), row)],
        out_shape=[jax.ShapeDtypeStruct((rows, d), F32), jax.ShapeDtypeStruct((rows, d), BF16)],
        compiler_params=_params("arbitrary"),
        name="mix_out",
    )(x, po, at, w_out, g_cross, w_cq)


def _cross_kernel(q_ref, mk_ref, mv_ref, o_ref):
    q = q_ref[0]
    for h in range(MEM_HEADS):
        sl = slice(h * MEM_HEAD_DIM, (h + 1) * MEM_HEAD_DIM)
        lg = lax.dot_general(q[:, sl], mk_ref[0, :, sl].astype(BF16), NT_DIMS, preferred_element_type=F32)
        p = jnp.exp(lg - jnp.max(lg, axis=-1, keepdims=True))
        l = jnp.sum(p, axis=-1, keepdims=True)
        o = jnp.dot(p.astype(BF16), mv_ref[0, :, sl].astype(BF16), preferred_element_type=F32)
        o_ref[0, :, sl] = (o / l).astype(BF16)


def _cross(qc, mk, mv, tm):
    b, s, d = qc.shape
    m = mk.shape[1]
    return pl.pallas_call(
        _cross_kernel,
        grid=(b, s // tm),
        in_specs=[pl.BlockSpec((1, tm, d), lambda bi, i: (bi, i, 0)),
                  pl.BlockSpec((1, m, d), lambda bi, i: (bi, 0, 0)),
                  pl.BlockSpec((1, m, d), lambda bi, i: (bi, 0, 0))],
        out_specs=pl.BlockSpec((1, tm, d), lambda bi, i: (bi, i, 0)),
        out_shape=jax.ShapeDtypeStruct((b, s, d), BF16),
        compiler_params=_params("arbitrary", "arbitrary"),
        name="cross_attn",
    )(qc, mk, mv)


MEM_LANES = 128
MEM_PARTS = MEM_HEAD_DIM // MEM_LANES
MEM_ROWS = MEM_PARTS * MEM_HEADS


def _cross_sample_kernel(q_ref, mk_ref, mv_ref, o_ref):
    n = mk_ref.shape[1]
    row = lax.broadcasted_iota(jnp.int32, (MEM_ROWS, n), 0)
    lane_row = lax.broadcasted_iota(jnp.int32, (MEM_ROWS, n), 1) % MEM_ROWS
    own = lane_row == row
    valid = (row < MEM_HEADS) & own

    def other_part(t, lane_shift):
        return pltpu.roll(pltpu.roll(t, MEM_HEADS, 0), lane_shift, 1)

    for s in range(q_ref.shape[0]):
        r = lax.dot_general(q_ref[s], mk_ref[s].astype(BF16), NT_DIMS, preferred_element_type=F32)
        r = jnp.where(own, r, 0.0)
        lg = jnp.where(valid, r + other_part(r, n - MEM_HEADS), NEG)
        p = jnp.where(valid, jnp.exp(lg - jnp.max(lg, axis=-1, keepdims=True)), 0.0)
        p = p + other_part(p, MEM_HEADS)
        l = jnp.sum(p, axis=-1, keepdims=True)
        o = jnp.dot(p.astype(BF16), mv_ref[s].astype(BF16), preferred_element_type=F32)
        o_ref[s] = (o / l).astype(BF16)


def _stored_rows(qc):
    rows = qc.shape[0]
    return jnp.transpose(qc.reshape(rows, MEM_HEADS, MEM_PARTS, MEM_LANES), (0, 2, 1, 3)).reshape(rows, MEM_ROWS, MEM_LANES)


def _cross_sample(qc, mem_k, mem_v):
    b, slots = mem_k.shape[:2]

    def stored(m):
        m = m.reshape(b, slots, MEM_HEADS, MEM_PARTS, MEM_LANES)
        return jnp.transpose(m, (0, 1, 3, 2, 4)).reshape(b, slots * MEM_ROWS, MEM_LANES)

    per_step = 4
    assert b % per_step == 0
    seq = lambda n: pl.BlockSpec((per_step, n, MEM_LANES), lambda bi: (bi, 0, 0))
    o = pl.pallas_call(
        _cross_sample_kernel,
        grid=(b // per_step,),
        in_specs=[seq(MEM_ROWS), seq(slots * MEM_ROWS), seq(slots * MEM_ROWS)],
        out_specs=seq(MEM_ROWS),
        out_shape=jax.ShapeDtypeStruct((b, MEM_ROWS, MEM_LANES), BF16),
        compiler_params=_params("arbitrary"),
        name="cross_sample",
    )(_stored_rows(qc), stored(mem_k), stored(mem_v))
    return jnp.transpose(o.reshape(b, MEM_PARTS, MEM_HEADS, MEM_LANES), (0, 2, 1, 3)).reshape(b, -1)


def _mlp_kernel(x1_ref, o_ref, wco_ref, gm_ref, wu_ref, wd_ref, gf_ref, y_ref):
    x2 = x1_ref[...] + jnp.dot(o_ref[...], wco_ref[...], preferred_element_type=F32)
    hm = _rms(x2, gm_ref[...]).astype(BF16)
    acc = x2
    for c in range(D_FF // FF_CHUNK):
        cs = slice(c * FF_CHUNK, (c + 1) * FF_CHUNK)
        a = jnp.dot(hm, wu_ref[:, cs], preferred_element_type=F32)
        a = jnp.square(jnp.maximum(a, 0.0)).astype(BF16)
        acc = acc + jnp.dot(a, wd_ref[cs, :], preferred_element_type=F32)
    y_ref[...] = _rms(acc, gf_ref[...])


def _mlp(x1, o, w_co, g_mlp, w_up, w_down, g_final):
    rows, d = x1.shape
    tm = min(512, rows)
    row = lambda i: (i, 0)
    return pl.pallas_call(
        _mlp_kernel,
        grid=(rows // tm,),
        in_specs=[pl.BlockSpec((tm, d), row), pl.BlockSpec((tm, d), row), _resident(w_co.shape),
                  _resident((1, d)), _resident(w_up.shape), _resident(w_down.shape), _resident((1, d))],
        out_specs=pl.BlockSpec((tm, d), row),
        out_shape=jax.ShapeDtypeStruct((rows, d), F32),
        compiler_params=_params("arbitrary"),
        name="mlp",
    )(x1, o, w_co, g_mlp, w_up, w_down, g_final)


def _proj_sample_kernel(x_ref, g_ref, w_ref, cos_ref, sin_ref, st_ref, wp_ref, sp_ref,
                        q_ref, k_ref, v_ref, kt_ref, vt_ref, u_ref, po_ref, *, n_prev):
    h = _rms(x_ref[...], g_ref[...]).astype(BF16)
    proj = jnp.dot(h, w_ref[...], preferred_element_type=F32)
    u = proj[:, :POOL_WIDTH]
    cos = cos_ref[...]
    sin = sin_ref[...]
    q_ref[...] = _rope(proj[:, POOL_WIDTH:POOL_WIDTH + ATTN_WIDTH], cos, sin) * (HEAD_DIM ** -0.5)
    k = _rope(proj[:, POOL_WIDTH + ATTN_WIDTH:POOL_WIDTH + 2 * ATTN_WIDTH], cos, sin)
    v = proj[:, POOL_WIDTH + 2 * ATTN_WIDTH:]
    k_ref[...] = k
    v_ref[...] = v
    kt_ref[...] = k.T
    vt_ref[...] = v.T
    u_ref[...] = u
    for g, w in enumerate(POOL_WINDOWS):
        cols = slice(g * POOL_GC, (g + 1) * POOL_GC)
        ug = u[:, cols]
        wsum = ug
        for j in range(1, w):
            wsum = wsum + st_ref[POOL_STATE - j, :, cols]
        d = wsum / float(min(w, n_prev + 1)) - ug
        mixed = jnp.dot(d.astype(BF16), wp_ref[g], preferred_element_type=F32)
        po_ref[:, cols] = (mixed * sp_ref[:, cols]).astype(BF16)


def _proj_sample(x, g, w_in, cos, sin, state_t, w_pool, s_pool, n_prev):
    rows, d = x.shape
    wide = jax.ShapeDtypeStruct((rows, ATTN_WIDTH), F32)
    tall = jax.ShapeDtypeStruct((ATTN_WIDTH, rows), F32)
    full = lambda a: pl.BlockSpec(a.shape, lambda i, nd=len(a.shape): (0,) * nd)
    args = (x, g, w_in, cos, sin, state_t, w_pool, s_pool)
    outs = [wide, wide, wide, tall, tall, wide, jax.ShapeDtypeStruct((rows, POOL_WIDTH), BF16)]
    return pl.pallas_call(
        functools.partial(_proj_sample_kernel, n_prev=n_prev),
        grid=(1,),
        in_specs=[full(a) for a in args],
        out_specs=[full(o) for o in outs],
        out_shape=outs,
        compiler_params=_params("arbitrary"),
        name="proj_sample",
    )(*args)


def _sample_attend(q, k_new, v_new, k_pages, v_pages):
    n_pages = len(k_pages)
    page = k_pages[0].shape[1]
    past = n_pages * page
    nb = past // MOBA_BLOCK
    head_of_lane = lax.broadcasted_iota(jnp.int32, (N_HEADS, ATTN_WIDTH), 1) // HEAD_DIM
    head = lax.broadcasted_iota(jnp.int32, (N_HEADS, ATTN_WIDTH), 0)
    mine = head_of_lane == head
    q_heads = jnp.where(mine, q, 0.0)
    q_heads_b = q_heads.astype(BF16)
    lg = jnp.concatenate([jnp.dot(q_heads_b, kp[...].astype(BF16), preferred_element_type=F32) for kp in k_pages],
                         axis=-1)

    s = jnp.concatenate(
        [jnp.sum(lg[:, j * MOBA_BLOCK:(j + 1) * MOBA_BLOCK], axis=-1, keepdims=True) for j in range(nb)], axis=-1)
    blk_id = lax.broadcasted_iota(jnp.int32, (N_HEADS, nb), 1)
    sel = _topk_member(s, blk_id >= 0, blk_id, axis=1)
    key_blk = lax.broadcasted_iota(jnp.int32, (N_HEADS, past), 1) // MOBA_BLOCK
    chosen = jnp.zeros((N_HEADS, past), jnp.bool_)
    for j in range(nb):
        chosen = chosen | ((key_blk == j) & sel[:, j:j + 1])
    lg = jnp.where(chosen, lg, NEG)

    lg_new = jnp.sum(q_heads * k_new, axis=-1, keepdims=True)
    m = jnp.maximum(jnp.max(lg, axis=-1, keepdims=True), lg_new)
    p = jnp.exp(lg - m)
    p_new = jnp.exp(lg_new - m)
    l = jnp.sum(p, axis=-1, keepdims=True) + p_new

    lanes = page
    sub = lax.broadcasted_iota(jnp.int32, (HEAD_DIM, lanes), 0)
    lane = lax.broadcasted_iota(jnp.int32, (HEAD_DIM, lanes), 1)
    tiles = []
    for first in range(0, N_HEADS, lanes // HEAD_DIM):
        row = jnp.zeros((1, lanes), F32)
        for h in range(first, first + lanes // HEAD_DIM):
            acc = jnp.zeros((HEAD_DIM, lanes), F32)
            for n, vp in enumerate(v_pages):
                w = jnp.broadcast_to(p[h:h + 1, n * page:(n + 1) * page], (HEAD_DIM, page))
                acc = acc + vp[h * HEAD_DIM:(h + 1) * HEAD_DIM, :] * w
            total = jnp.sum(acc, axis=1, keepdims=True)
            on_diag = lane == sub + (h - first) * HEAD_DIM
            row = row + jnp.sum(jnp.where(on_diag, total, 0.0), axis=0, keepdims=True)
        tiles.append(row)
    pv = jnp.concatenate(tiles, axis=-1)
    per_lane = lambda t: jnp.sum(jnp.where(mine, t, 0.0), axis=0, keepdims=True)
    return (pv + per_lane(p_new) * v_new) / per_lane(l)


def _rope_tables(pos, heads):
    half = HEAD_DIM // 2
    inv_freq = ROPE_THETA ** (-jnp.arange(half, dtype=F32) / half)
    ang = pos.astype(F32)[:, None] * inv_freq[None, :]
    cos = jnp.cos(ang)
    sin = jnp.sin(ang)
    cos_full = jnp.tile(jnp.concatenate([cos, cos], axis=-1), (1, heads))
    sin_signed = jnp.tile(jnp.concatenate([-sin, sin], axis=-1), (1, heads))
    return cos_full, sin_signed


def _post_mixer(x, po, at, cross, lp):
    x1, qc = _mix_out(x, po, at, lp["w_out"], lp["g_cross"], lp["w_cq"])
    return _mlp(x1, cross(qc), lp["w_co"], lp["g_mlp"], lp["w_up"], lp["w_down"], lp["g_final"])


def kernel(x_prompt, x_sample, cache_k, cache_v, cache_mem_k, cache_mem_v, state_pool, page_table, mem_prompt,
           g_mix, w_in, w_pool, s_pool, w_out, g_cross, g_mem, w_cq, w_ck, w_cv, w_co, g_mlp, w_up, w_down,
           g_final):
    depth = w_in.shape[0]
    assert depth == 1, "one decoder layer"
    bp, seq, d = x_prompt.shape
    bs, dec_seq, _ = x_sample.shape
    assert dec_seq == 1 and seq % MOBA_BLOCK == 0
    n_phys, page = cache_k.shape[1], cache_k.shape[2]
    n_pages = page_table.shape[1]
    past_len = n_pages * page
    assert past_len % MOBA_BLOCK == 0 and past_len >= POOL_STATE
    mem_len = mem_prompt.shape[1]

    l = 0
    row = lambda a: a.reshape(1, -1)
    lp = dict(w_out=w_out[l].astype(BF16), g_cross=row(g_cross[l]), w_cq=w_cq[l].astype(BF16),
              w_co=w_co[l].astype(BF16), g_mlp=row(g_mlp[l]), w_up=w_up[l].astype(BF16),
              w_down=w_down[l].astype(BF16), g_final=row(g_final))
    w_in_b = w_in[l].astype(BF16)
    w_pool_b = w_pool[l].astype(BF16)
    g_mix_r = row(g_mix[l])
    s_pool_r = row(s_pool[l])

    lane_heads = 128 // HEAD_DIM
    cos_p, sin_p = _rope_tables(jnp.arange(seq), lane_heads)
    qtb, kt_p, vt_p, kb, vtb, kmean, po_p, last_p = _proj_prompt(
        x_prompt, g_mix_r, w_in_b, cos_p, sin_p, cos_p[:, :HEAD_DIM].T, sin_p[:, :HEAD_DIM].T, w_pool_b, s_pool_r)
    cos_s, sin_s = _rope_tables(jnp.full((1,), past_len), N_HEADS)
    state_t = jnp.transpose(state_pool[l], (1, 0, 2))
    q_s, k_s, v_s, kt_s, vt_s, u_s, po_s = _proj_sample(x_sample.reshape(bs, d), g_mix_r, w_in_b, cos_s, sin_s,
                                                        state_t, w_pool_b, s_pool_r, past_len)

    tok = lambda a: a.reshape(bs, 1, -1)
    pages_t = lambda c: jnp.transpose(c, (0, 2, 3, 1)).reshape(n_phys, ATTN_WIDTH, page)
    at_p, at_s = _moba(qtb, kb, vtb, kmean.reshape(bp, seq // MOBA_BLOCK, ATTN_WIDTH), page_table, tok(q_s),
                       tok(k_s), tok(v_s), pages_t(cache_k[l]), pages_t(cache_v[l]))

    mk_p, mv_p, mkb, mvb = _mem_kv(mem_prompt.reshape(bp * mem_len, d), row(g_mem[l]), w_ck[l].astype(BF16),
                                   w_cv[l].astype(BF16))
    cross_p = lambda qc: _cross(qc.reshape(bp, seq, d), mkb.reshape(bp, mem_len, d), mvb.reshape(bp, mem_len, d),
                                512).reshape(bp * seq, d)
    flat = lambda a: a.reshape(-1, a.shape[-1])
    y_prompt = _post_mixer(flat(x_prompt), flat(po_p), flat(at_p), cross_p, lp).reshape(bp, seq, d)
    cross_s = lambda qc: _cross_sample(qc, cache_mem_k[l], cache_mem_v[l])
    y_sample = _post_mixer(flat(x_sample), po_s, flat(at_s), cross_s, lp).reshape(bs, 1, d)

    def heads(t, b_):
        return jnp.transpose(t.reshape(b_, N_HEADS, HEAD_DIM, -1), (0, 3, 1, 2))[None]

    mem_heads = lambda a: a.reshape(1, bp, mem_len, MEM_HEADS, MEM_HEAD_DIM)
    pool_sample = jnp.transpose(jnp.concatenate([state_t[1:], u_s[None]], axis=0), (1, 0, 2))[None]
    return (y_prompt, y_sample, heads(kt_p, bp), heads(vt_p, bp), last_p[:, 1:][None], mem_heads(mk_p),
            mem_heads(mv_p), heads(kt_s.T.reshape(bs, ATTN_WIDTH, 1), bs), heads(vt_s.T.reshape(bs, ATTN_WIDTH, 1), bs),
            pool_sample)
```

```python
import functools

import jax
import jax.numpy as jnp
from jax import lax
from jax.experimental import pallas as pl
from jax.experimental.pallas import tpu as pltpu

D_MODEL = 1024
POOL_WIDTH = 512
POOL_WINDOWS = (2, 4, 8, 16)
POOL_GC = POOL_WIDTH // len(POOL_WINDOWS)
POOL_STATE = max(POOL_WINDOWS) - 1
POOL_HIST = POOL_STATE + 1
ATTN_WIDTH = D_MODEL - POOL_WIDTH
HEAD_DIM = 64
N_HEADS = ATTN_WIDTH // HEAD_DIM
MOBA_BLOCK = 256
MOBA_TOPK = 3
ROPE_THETA = 10000.0
MEM_HEADS = 4
MEM_HEAD_DIM = D_MODEL // MEM_HEADS
D_FF = 4 * D_MODEL
FF_CHUNK = 1024
EPS = 1e-6
NEG = -1e30

VMEM_LIMIT_BYTES = 56 * 1024 * 1024

F32 = jnp.float32
BF16 = jnp.bfloat16
NT_DIMS = (((1,), (1,)), ((), ()))


def _params(*semantics):
    return pltpu.CompilerParams(dimension_semantics=semantics, vmem_limit_bytes=VMEM_LIMIT_BYTES)


def _resident(shape):
    zeros = (0,) * len(shape)
    return pl.BlockSpec(shape, lambda *_: zeros, pipeline_mode=pl.Buffered(1))


def _rms(x, g):
    return x * lax.rsqrt(jnp.mean(x * x, axis=-1, keepdims=True) + EPS) * g


def _rope(t, cos, sin_signed):
    width = t.shape[-1]
    lane = lax.broadcasted_iota(jnp.int32, t.shape, t.ndim - 1)
    first_half = (lane % HEAD_DIM) < (HEAD_DIM // 2)
    partner = jnp.where(first_half,
                        pltpu.roll(t, width - HEAD_DIM // 2, t.ndim - 1),
                        pltpu.roll(t, HEAD_DIM // 2, t.ndim - 1))
    return t * cos + partner * sin_signed


def _rope_t(t, cos, sin_signed):
    rows = t.shape[0]
    row = lax.broadcasted_iota(jnp.int32, t.shape, 0)
    first_half = (row % HEAD_DIM) < (HEAD_DIM // 2)
    partner = jnp.where(first_half, pltpu.roll(t, rows - HEAD_DIM // 2, 0), pltpu.roll(t, HEAD_DIM // 2, 0))
    return t * cos + partner * sin_signed


def _topk_member(s, valid, idx, axis):
    n = s.shape[axis]
    s = jnp.where(valid, s, NEG)
    rank = jnp.zeros(s.shape, jnp.int32)
    for j in range(n):
        sj = lax.slice_in_dim(s, j, j + 1, axis=axis)
        beats = (sj > s) | ((sj == s) & (j < idx))
        rank = rank + beats.astype(jnp.int32)
    return (rank < MOBA_TOPK) & valid


def _proj_prompt_kernel(x_ref, g_ref, w_ref, cos_ref, sin_ref, cost_ref, sint_ref, wp_ref, sp_ref,
                        qt_ref, kt_ref, vt_ref, kb_ref, vtb_ref, km_ref, po_ref, last_ref, ext_ref):
    i = pl.program_id(1)
    tm = x_ref.shape[1]

    @pl.when(i == 0)
    def _():
        ext_ref[0:POOL_HIST, :] = jnp.zeros((POOL_HIST, POOL_WIDTH), F32)

    h = _rms(x_ref[0], g_ref[...]).astype(BF16)

    def project(first):
        return jnp.dot(h, w_ref[:, first:first + ATTN_WIDTH], preferred_element_type=F32)

    lane_reps = ATTN_WIDTH // cos_ref.shape[1]
    u = project(0)
    k = _rope(project(POOL_WIDTH + ATTN_WIDTH), jnp.tile(cos_ref[...], (1, lane_reps)),
              jnp.tile(sin_ref[...], (1, lane_reps)))
    q = project(POOL_WIDTH)
    v = project(POOL_WIDTH + 2 * ATTN_WIDTH)
    qt = _rope_t(q.T, jnp.tile(cost_ref[...], (N_HEADS, 1)), jnp.tile(sint_ref[...], (N_HEADS, 1)))
    qt_ref[0] = (qt * (HEAD_DIM ** -0.5 * LOG2_E)).astype(BF16)
    vt = v.T
    kt_ref[0] = k.T
    vt_ref[0] = vt
    kb_ref[0] = k.astype(BF16)
    vtb_ref[0, 0] = vt.astype(BF16)
    km_ref[0, 0] = jnp.mean(k, axis=0, keepdims=True)

    pos = i * tm + lax.broadcasted_iota(jnp.int32, (tm, 1), 0)
    for g, w in enumerate(POOL_WINDOWS):
        cols = slice(g * POOL_GC, (g + 1) * POOL_GC)
        ug = u[:, cols]
        wsum = jnp.concatenate([ext_ref[:, cols], ug], axis=0)
        shift = 1
        while shift < w:
            wsum = wsum + pltpu.roll(wsum, shift, 0)
            shift *= 2
        cnt = jnp.minimum(w, pos + 1).astype(F32)
        d = wsum[POOL_HIST:, :] / cnt - ug
        mixed = jnp.dot(d.astype(BF16), wp_ref[g], preferred_element_type=F32)
        po_ref[0, :, cols] = (mixed * sp_ref[:, cols]).astype(BF16)
    tail = u[tm - POOL_HIST:, :]
    ext_ref[...] = tail
    last_ref[0] = tail


def _proj_prompt(x, g, w_in, cos, sin, cos_t, sin_t, w_pool, s_pool):
    b, s, d = x.shape
    tm = MOBA_BLOCK
    nq = s // tm
    row = lambda bi, i: (bi, i, 0)
    col = lambda bi, i: (bi, 0, i)
    blk = lambda bi, i: (bi, i, 0, 0)
    return pl.pallas_call(
        _proj_prompt_kernel,
        grid=(b, nq),
        in_specs=[
            pl.BlockSpec((1, tm, d), row),
            _resident((1, d)),
            _resident(w_in.shape),
            pl.BlockSpec((tm, cos.shape[1]), lambda bi, i: (i, 0)),
            pl.BlockSpec((tm, cos.shape[1]), lambda bi, i: (i, 0)),
            pl.BlockSpec((HEAD_DIM, tm), lambda bi, i: (0, i)),
            pl.BlockSpec((HEAD_DIM, tm), lambda bi, i: (0, i)),
            _resident(w_pool.shape),
            _resident((1, POOL_WIDTH)),
        ],
        out_specs=[
            pl.BlockSpec((1, ATTN_WIDTH, tm), col),
            pl.BlockSpec((1, ATTN_WIDTH, tm), col),
            pl.BlockSpec((1, ATTN_WIDTH, tm), col),
            pl.BlockSpec((1, tm, ATTN_WIDTH), row),
            pl.BlockSpec((1, 1, ATTN_WIDTH, tm), blk),
            pl.BlockSpec((1, 1, 1, ATTN_WIDTH), blk),
            pl.BlockSpec((1, tm, POOL_WIDTH), row),
            pl.BlockSpec((1, POOL_HIST, POOL_WIDTH), lambda bi, i: (bi, 0, 0)),
        ],
        out_shape=[
            jax.ShapeDtypeStruct((b, ATTN_WIDTH, s), BF16),
            jax.ShapeDtypeStruct((b, ATTN_WIDTH, s), F32),
            jax.ShapeDtypeStruct((b, ATTN_WIDTH, s), F32),
            jax.ShapeDtypeStruct((b, s, ATTN_WIDTH), BF16),
            jax.ShapeDtypeStruct((b, nq, ATTN_WIDTH, tm), BF16),
            jax.ShapeDtypeStruct((b, nq, 1, ATTN_WIDTH), F32),
            jax.ShapeDtypeStruct((b, s, POOL_WIDTH), BF16),
            jax.ShapeDtypeStruct((b, POOL_HIST, POOL_WIDTH), F32),
        ],
        scratch_shapes=[pltpu.VMEM((POOL_HIST, POOL_WIDTH), F32)],
        compiler_params=_params("arbitrary", "arbitrary"),
        name="proj_prompt",
    )(x, g, w_in, cos, sin, cos_t, sin_t, w_pool, s_pool)


SUBLANES = 8
LOGIT_AHEAD = 2
LOGIT_SLOTS = 4
LOG2_E = 1.4426950408889634


def _all_sublanes(t, combine):
    for s in (4, 2, 1):
        t = combine(t, pltpu.roll(t, s, 0))
    return t


def _moba_kernel(pt_ref, qt_ref, kb_ref, vtb_ref, km_ref, qs_ref, kn_ref, vn_ref, ck_ref, cv_ref,
                 o_ref, os_ref, acc_ref, m_ref, l_ref, take_ref, lg_ref, kbuf_ref, vbuf_ref, page_sem,
                 *, n_pages, per_step):
    i = pl.program_id(1)
    step = pl.program_id(0) * pl.num_programs(1) + i
    n_steps = pl.num_programs(0) * pl.num_programs(1)
    slot = step % 2
    tq = qt_ref.shape[2]
    nb = km_ref.shape[1]

    def page_copies(page_id, half):
        copies = []
        for s in range(per_step):
            for p in range(n_pages):
                pid = page_id(s, p)
                dst = s * n_pages + p
                copies.append(pltpu.make_async_copy(ck_ref.at[pid], kbuf_ref.at[half, dst], page_sem.at[half, 0]))
                copies.append(pltpu.make_async_copy(cv_ref.at[pid], vbuf_ref.at[half, dst], page_sem.at[half, 1]))
        return copies

    def start_step(t, half):
        for c in page_copies(lambda s, p: pt_ref[t * per_step + s, p], half):
            c.start()

    @pl.when(step == 0)
    def _():
        start_step(0, 0)

    @pl.when(step + 1 < n_steps)
    def _():
        start_step(step + 1, 1 - slot)


    km_rows = jnp.concatenate([km_ref[0]] * N_HEADS, axis=0)
    row_head = lax.broadcasted_iota(jnp.int32, km_rows.shape, 0) // nb
    lane_head = lax.broadcasted_iota(jnp.int32, km_rows.shape, 1) // HEAD_DIM
    km_heads = jnp.where(row_head == lane_head, km_rows, 0.0).astype(BF16)
    s = jnp.dot(km_heads, qt_ref[0], preferred_element_type=F32).reshape(N_HEADS, nb, tq)
    blk = lax.broadcasted_iota(jnp.int32, s.shape, 1)
    sel = _topk_member(s, blk < i, blk, axis=1).astype(F32)

    for h in range(N_HEADS):
        for jb in range(nb):
            take_ref[h, jb] = jnp.broadcast_to(sel[h, jb:jb + 1, :], (SUBLANES, tq))

    def head_rows(h):
        return slice(h * HEAD_DIM, (h + 1) * HEAD_DIM)

    def stage_logits(h, j):
        off = pl.multiple_of(j * MOBA_BLOCK, MOBA_BLOCK)
        lg_ref[h % LOGIT_SLOTS] = jnp.dot(kb_ref[0, pl.ds(off, MOBA_BLOCK), head_rows(h)],
                                          qt_ref[0, head_rows(h), :], preferred_element_type=F32)

    def weighted_values(h, j, p3):
        p = p3.reshape(MOBA_BLOCK, tq).astype(BF16)
        pv = jnp.dot(vtb_ref[0, j, head_rows(h), :], p, preferred_element_type=F32)
        return pv.reshape(HEAD_DIM // SUBLANES, SUBLANES, tq)

    def tiles(t):
        return t.reshape(t.shape[0] // SUBLANES, SUBLANES, tq)

    def block_pass(j, j_next, own, between=None):
        for h in range(N_HEADS):
            ahead = h + LOGIT_AHEAD
            if ahead < N_HEADS:
                stage_logits(ahead, j)
            else:
                stage_logits(ahead - N_HEADS, j_next)
            if between is not None:
                between(h)
            lg = lg_ref[h % LOGIT_SLOTS]
            if own:
                kk = lax.broadcasted_iota(jnp.int32, lg.shape, 0)
                qq = lax.broadcasted_iota(jnp.int32, lg.shape, 1)
                lg3 = tiles(jnp.where(kk <= qq, lg, NEG))
                m = _all_sublanes(jnp.max(lg3, axis=0), jnp.maximum)
                p3 = jnp.exp2(lg3 - m[None])
                m_ref[h] = m
                l_ref[h] = _all_sublanes(jnp.sum(p3, axis=0), jnp.add)
                acc_ref[head_rows(h), :] = weighted_values(h, j, p3).reshape(HEAD_DIM, tq)
            else:
                lg3 = tiles(lg)
                take = take_ref[h, j] > 0.0
                m_old = m_ref[h]
                l_old = l_ref[h]
                acc_old = tiles(acc_ref[head_rows(h), :])
                m = jnp.maximum(m_old, _all_sublanes(jnp.max(lg3, axis=0), jnp.maximum))
                a = jnp.exp2(m_old - m)
                p3 = jnp.exp2(lg3 - m[None])
                l = a * l_old + _all_sublanes(jnp.sum(p3, axis=0), jnp.add)
                acc3 = a[None] * acc_old + weighted_values(h, j, p3)
                m_ref[h] = jnp.where(take, m, m_old)
                l_ref[h] = jnp.where(take, l, l_old)
                acc_ref[head_rows(h), :] = jnp.where(take[None], acc3, acc_old).reshape(HEAD_DIM, tq)

    last_past = jnp.maximum(i - 1, 0)
    for h in range(LOGIT_AHEAD):
        stage_logits(h, i)

    for c in page_copies(lambda s, p: 0, slot):
        c.wait()
    sample_q = [_sample_queries(qs_ref[s]) for s in range(per_step)]
    jobs = [(s, p) for s in range(per_step) for p in range(n_pages)]
    per_head = -(-len(jobs) // N_HEADS)
    page_logits = {}

    def sample_page_logits(h):
        for s, p in jobs[h * per_head:(h + 1) * per_head]:
            page_logits[s, p] = _sample_page_logits(sample_q[s], kbuf_ref.at[slot, s * n_pages + p])

    block_pass(i, 0, own=True, between=sample_page_logits)
    for s in range(per_step):
        v_pages = [vbuf_ref.at[slot, s * n_pages + p] for p in range(n_pages)]
        lg = jnp.concatenate([page_logits[s, p] for p in range(n_pages)], axis=-1)
        os_ref[s] = _sample_finish(sample_q[s], lg, kn_ref[s], vn_ref[s], v_pages).astype(BF16)

    def two_past_blocks(jj, carry):
        j = 2 * jj
        block_pass(j, j + 1, own=False)
        block_pass(j + 1, jnp.minimum(j + 2, last_past), own=False)
        return carry

    lax.fori_loop(0, i // 2, two_past_blocks, 0)

    @pl.when(i % 2 == 1)
    def _():
        block_pass(last_past, last_past, own=False)

    for h in range(N_HEADS):
        rows = slice(h * HEAD_DIM, (h + 1) * HEAD_DIM)
        acc_ref[rows, :] = (tiles(acc_ref[rows, :]) / l_ref[h][None]).reshape(HEAD_DIM, tq)
    o_ref[0] = acc_ref[...].T.astype(BF16)


def _moba(qtb, kb, vtb, kmean, page_table, q_s, k_s, v_s, cache_kt, cache_vt):
    b, w, s = qtb.shape
    nq = s // MOBA_BLOCK
    bs, n_pages = page_table.shape
    page = cache_kt.shape[2]
    steps = b * nq
    assert bs % steps == 0
    per_step = bs // steps
    step_seq = lambda bi, i, pt: (bi * nq + i, 0, 0)
    tok = pl.BlockSpec((per_step, 1, w), step_seq)
    in_hbm = pl.BlockSpec(memory_space=pl.ANY)
    page_buf = pltpu.VMEM((2, per_step * n_pages, w, page), cache_kt.dtype)
    grid_spec = pltpu.PrefetchScalarGridSpec(
        num_scalar_prefetch=1,
        grid=(b, nq),
        in_specs=[
            pl.BlockSpec((1, w, MOBA_BLOCK), lambda bi, i, pt: (bi, 0, i)),
            pl.BlockSpec((1, s, w), lambda bi, i, pt: (bi, 0, 0)),
            pl.BlockSpec((1, nq, w, MOBA_BLOCK), lambda bi, i, pt: (bi, 0, 0, 0)),
            pl.BlockSpec((1, nq, w), lambda bi, i, pt: (bi, 0, 0)),
            tok, tok, tok, in_hbm, in_hbm,
        ],
        out_specs=[pl.BlockSpec((1, MOBA_BLOCK, w), lambda bi, i, pt: (bi, i, 0)), tok],
        scratch_shapes=[pltpu.VMEM((w, MOBA_BLOCK), F32),
                        pltpu.VMEM((N_HEADS, SUBLANES, MOBA_BLOCK), F32),
                        pltpu.VMEM((N_HEADS, SUBLANES, MOBA_BLOCK), F32),
                        pltpu.VMEM((N_HEADS, nq, SUBLANES, MOBA_BLOCK), F32),
                        pltpu.VMEM((LOGIT_SLOTS, MOBA_BLOCK, MOBA_BLOCK), F32),
                        page_buf, page_buf, pltpu.SemaphoreType.DMA((2, 2))],
    )
    return pl.pallas_call(
        functools.partial(_moba_kernel, n_pages=n_pages, per_step=per_step),
        grid_spec=grid_spec,
        out_shape=[jax.ShapeDtypeStruct((b, s, w), BF16), jax.ShapeDtypeStruct((bs, 1, w), BF16)],
        compiler_params=_params("arbitrary", "arbitrary"),
        name="moba",
    )(page_table, qtb, kb, vtb, kmean, q_s, k_s, v_s, cache_kt, cache_vt)


def _mem_kv_kernel(m_ref, g_ref, wk_ref, wv_ref, mk_ref, mv_ref, mkb_ref, mvb_ref):
    h = _rms(m_ref[...], g_ref[...]).astype(BF16)
    mk = jnp.dot(h, wk_ref[...], preferred_element_type=F32)
    mv = jnp.dot(h, wv_ref[...], preferred_element_type=F32)
    mk_ref[...] = mk
    mv_ref[...] = mv
    mkb_ref[...] = mk.astype(BF16)
    mvb_ref[...] = mv.astype(BF16)


def _mem_kv(mem, g, w_ck, w_cv):
    rows, d = mem.shape
    tm = 512
    row = lambda i: (i, 0)
    return pl.pallas_call(
        _mem_kv_kernel,
        grid=(rows // tm,),
        in_specs=[pl.BlockSpec((tm, d), row), _resident((1, d)), _resident(w_ck.shape), _resident(w_cv.shape)],
        out_specs=[pl.BlockSpec((tm, d), row)] * 4,
        out_shape=[jax.ShapeDtypeStruct((rows, d), F32)] * 2 + [jax.ShapeDtypeStruct((rows, d), BF16)] * 2,
        compiler_params=_params("arbitrary"),
        name="mem_kv",
    )(mem, g, w_ck, w_cv)


def _mix_out_kernel(x_ref, po_ref, at_ref, wo_ref, g_ref, wq_ref, x1_ref, qc_ref):
    cat = jnp.concatenate([po_ref[...], at_ref[...]], axis=-1)
    x1 = x_ref[...] + jnp.dot(cat, wo_ref[...], preferred_element_type=F32)
    x1_ref[...] = x1
    hq = _rms(x1, g_ref[...]).astype(BF16)
    qc = jnp.dot(hq, wq_ref[...], preferred_element_type=F32)
    qc_ref[...] = (qc * (MEM_HEAD_DIM ** -0.5)).astype(BF16)


def _mix_out(x, po, at, w_out, g_cross, w_cq):
    rows, d = x.shape
    tm = min(512, rows)
    row = lambda i: (i, 0)
    return pl.pallas_call(
        _mix_out_kernel,
        grid=(rows // tm,),
        in_specs=[pl.BlockSpec((tm, d), row), pl.BlockSpec((tm, POOL_WIDTH), row),
                  pl.BlockSpec((tm, ATTN_WIDTH), row), _resident(w_out.shape), _resident((1, d)),
                  _resident(w_cq.shape)],
        out_specs=[pl.BlockSpec((tm, d), row), pl.BlockSpec((tm, d), row)],
        out_shape=[jax.ShapeDtypeStruct((rows, d), F32), jax.ShapeDtypeStruct((rows, d), BF16)],
        compiler_params=_params("arbitrary"),
        name="mix_out",
    )(x, po, at, w_out, g_cross, w_cq)


def _cross_kernel(q_ref, mk_ref, mv_ref, o_ref):
    q = q_ref[0]
    for h in range(MEM_HEADS):
        sl = slice(h * MEM_HEAD_DIM, (h + 1) * MEM_HEAD_DIM)
        lg = lax.dot_general(q[:, sl], mk_ref[0, :, sl].astype(BF16), NT_DIMS, preferred_element_type=F32)
        p = jnp.exp(lg - jnp.max(lg, axis=-1, keepdims=True))
        l = jnp.sum(p, axis=-1, keepdims=True)
        o = jnp.dot(p.astype(BF16), mv_ref[0, :, sl].astype(BF16), preferred_element_type=F32)
        o_ref[0, :, sl] = (o / l).astype(BF16)


def _cross(qc, mk, mv, tm):
    b, s, d = qc.shape
    m = mk.shape[1]
    return pl.pallas_call(
        _cross_kernel,
        grid=(b, s // tm),
        in_specs=[pl.BlockSpec((1, tm, d), lambda bi, i: (bi, i, 0)),
                  pl.BlockSpec((1, m, d), lambda bi, i: (bi, 0, 0)),
                  pl.BlockSpec((1, m, d), lambda bi, i: (bi, 0, 0))],
        out_specs=pl.BlockSpec((1, tm, d), lambda bi, i: (bi, i, 0)),
        out_shape=jax.ShapeDtypeStruct((b, s, d), BF16),
        compiler_params=_params("arbitrary", "arbitrary"),
        name="cross_attn",
    )(qc, mk, mv)


MEM_LANES = 128
MEM_PARTS = MEM_HEAD_DIM // MEM_LANES
MEM_ROWS = MEM_PARTS * MEM_HEADS


def _cross_sample_kernel(q_ref, mk_ref, mv_ref, o_ref):
    n = mk_ref.shape[1]
    row = lax.broadcasted_iota(jnp.int32, (MEM_ROWS, n), 0)
    lane_row = lax.broadcasted_iota(jnp.int32, (MEM_ROWS, n), 1) % MEM_ROWS
    own = lane_row == row
    valid = (row < MEM_HEADS) & own

    def other_part(t, lane_shift):
        return pltpu.roll(pltpu.roll(t, MEM_HEADS, 0), lane_shift, 1)

    for s in range(q_ref.shape[0]):
        r = lax.dot_general(q_ref[s], mk_ref[s].astype(BF16), NT_DIMS, preferred_element_type=F32)
        r = jnp.where(own, r, 0.0)
        lg = jnp.where(valid, r + other_part(r, n - MEM_HEADS), NEG)
        p = jnp.where(valid, jnp.exp(lg - jnp.max(lg, axis=-1, keepdims=True)), 0.0)
        p = p + other_part(p, MEM_HEADS)
        l = jnp.sum(p, axis=-1, keepdims=True)
        o = jnp.dot(p.astype(BF16), mv_ref[s].astype(BF16), preferred_element_type=F32)
        o_ref[s] = (o / l).astype(BF16)


def _stored_rows(qc):
    rows = qc.shape[0]
    return jnp.transpose(qc.reshape(rows, MEM_HEADS, MEM_PARTS, MEM_LANES), (0, 2, 1, 3)).reshape(rows, MEM_ROWS, MEM_LANES)


def _cross_sample(qc, mem_k, mem_v):
    b, slots = mem_k.shape[:2]

    def stored(m):
        m = m.reshape(b, slots, MEM_HEADS, MEM_PARTS, MEM_LANES)
        return jnp.transpose(m, (0, 1, 3, 2, 4)).reshape(b, slots * MEM_ROWS, MEM_LANES)

    per_step = 4
    assert b % per_step == 0
    seq = lambda n: pl.BlockSpec((per_step, n, MEM_LANES), lambda bi: (bi, 0, 0))
    o = pl.pallas_call(
        _cross_sample_kernel,
        grid=(b // per_step,),
        in_specs=[seq(MEM_ROWS), seq(slots * MEM_ROWS), seq(slots * MEM_ROWS)],
        out_specs=seq(MEM_ROWS),
        out_shape=jax.ShapeDtypeStruct((b, MEM_ROWS, MEM_LANES), BF16),
        compiler_params=_params("arbitrary"),
        name="cross_sample",
    )(_stored_rows(qc), stored(mem_k), stored(mem_v))
    return jnp.transpose(o.reshape(b, MEM_PARTS, MEM_HEADS, MEM_LANES), (0, 2, 1, 3)).reshape(b, -1)


def _mlp_kernel(x1_ref, o_ref, wco_ref, gm_ref, wu_ref, wd_ref, gf_ref, y_ref):
    x2 = x1_ref[...] + jnp.dot(o_ref[...], wco_ref[...], preferred_element_type=F32)
    hm = _rms(x2, gm_ref[...]).astype(BF16)
    acc = x2
    for c in range(D_FF // FF_CHUNK):
        cs = slice(c * FF_CHUNK, (c + 1) * FF_CHUNK)
        a = jnp.dot(hm, wu_ref[:, cs], preferred_element_type=F32)
        a = jnp.square(jnp.maximum(a, 0.0)).astype(BF16)
        acc = acc + jnp.dot(a, wd_ref[cs, :], preferred_element_type=F32)
    y_ref[...] = _rms(acc, gf_ref[...])


def _mlp(x1, o, w_co, g_mlp, w_up, w_down, g_final):
    rows, d = x1.shape
    tm = min(512, rows)
    row = lambda i: (i, 0)
    return pl.pallas_call(
        _mlp_kernel,
        grid=(rows // tm,),
        in_specs=[pl.BlockSpec((tm, d), row), pl.BlockSpec((tm, d), row), _resident(w_co.shape),
                  _resident((1, d)), _resident(w_up.shape), _resident(w_down.shape), _resident((1, d))],
        out_specs=pl.BlockSpec((tm, d), row),
        out_shape=jax.ShapeDtypeStruct((rows, d), F32),
        compiler_params=_params("arbitrary"),
        name="mlp",
    )(x1, o, w_co, g_mlp, w_up, w_down, g_final)


def _proj_sample_kernel(x_ref, g_ref, w_ref, cos_ref, sin_ref, st_ref, wp_ref, sp_ref,
                        q_ref, k_ref, v_ref, kt_ref, vt_ref, u_ref, po_ref, *, n_prev):
    h = _rms(x_ref[...], g_ref[...]).astype(BF16)
    proj = jnp.dot(h, w_ref[...], preferred_element_type=F32)
    u = proj[:, :POOL_WIDTH]
    cos = cos_ref[...]
    sin = sin_ref[...]
    q_ref[...] = _rope(proj[:, POOL_WIDTH:POOL_WIDTH + ATTN_WIDTH], cos, sin) * (HEAD_DIM ** -0.5)
    k = _rope(proj[:, POOL_WIDTH + ATTN_WIDTH:POOL_WIDTH + 2 * ATTN_WIDTH], cos, sin)
    v = proj[:, POOL_WIDTH + 2 * ATTN_WIDTH:]
    k_ref[...] = k
    v_ref[...] = v
    kt_ref[...] = k.T
    vt_ref[...] = v.T
    u_ref[...] = u
    for g, w in enumerate(POOL_WINDOWS):
        cols = slice(g * POOL_GC, (g + 1) * POOL_GC)
        ug = u[:, cols]
        wsum = ug
        for j in range(1, w):
            wsum = wsum + st_ref[POOL_STATE - j, :, cols]
        d = wsum / float(min(w, n_prev + 1)) - ug
        mixed = jnp.dot(d.astype(BF16), wp_ref[g], preferred_element_type=F32)
        po_ref[:, cols] = (mixed * sp_ref[:, cols]).astype(BF16)


def _proj_sample(x, g, w_in, cos, sin, state_t, w_pool, s_pool, n_prev):
    rows, d = x.shape
    wide = jax.ShapeDtypeStruct((rows, ATTN_WIDTH), F32)
    tall = jax.ShapeDtypeStruct((ATTN_WIDTH, rows), F32)
    full = lambda a: pl.BlockSpec(a.shape, lambda i, nd=len(a.shape): (0,) * nd)
    args = (x, g, w_in, cos, sin, state_t, w_pool, s_pool)
    outs = [wide, wide, wide, tall, tall, wide, jax.ShapeDtypeStruct((rows, POOL_WIDTH), BF16)]
    return pl.pallas_call(
        functools.partial(_proj_sample_kernel, n_prev=n_prev),
        grid=(1,),
        in_specs=[full(a) for a in args],
        out_specs=[full(o) for o in outs],
        out_shape=outs,
        compiler_params=_params("arbitrary"),
        name="proj_sample",
    )(*args)


def _own_lanes():
    head_of_lane = lax.broadcasted_iota(jnp.int32, (N_HEADS, ATTN_WIDTH), 1) // HEAD_DIM
    head = lax.broadcasted_iota(jnp.int32, (N_HEADS, ATTN_WIDTH), 0)
    return head_of_lane == head


def _sample_queries(q):
    return jnp.where(_own_lanes(), q, 0.0)


def _sample_page_logits(q_heads, k_page):
    return jnp.dot(q_heads.astype(BF16), k_page[...].astype(BF16), preferred_element_type=F32)


def _sample_finish(q_heads, lg, k_new, v_new, v_pages):
    page = v_pages[0].shape[1]
    past = lg.shape[1]
    nb = past // MOBA_BLOCK
    mine = _own_lanes()

    s = jnp.concatenate(
        [jnp.sum(lg[:, j * MOBA_BLOCK:(j + 1) * MOBA_BLOCK], axis=-1, keepdims=True) for j in range(nb)], axis=-1)
    blk_id = lax.broadcasted_iota(jnp.int32, (N_HEADS, nb), 1)
    sel = _topk_member(s, blk_id >= 0, blk_id, axis=1)
    key_blk = lax.broadcasted_iota(jnp.int32, (N_HEADS, past), 1) // MOBA_BLOCK
    chosen = jnp.zeros((N_HEADS, past), jnp.bool_)
    for j in range(nb):
        chosen = chosen | ((key_blk == j) & sel[:, j:j + 1])
    lg = jnp.where(chosen, lg, NEG)

    lg_new = jnp.sum(q_heads * k_new, axis=-1, keepdims=True)
    m = jnp.maximum(jnp.max(lg, axis=-1, keepdims=True), lg_new)
    p = jnp.exp(lg - m)
    p_new = jnp.exp(lg_new - m)
    l = jnp.sum(p, axis=-1, keepdims=True) + p_new

    lanes = page
    sub = lax.broadcasted_iota(jnp.int32, (HEAD_DIM, lanes), 0)
    lane = lax.broadcasted_iota(jnp.int32, (HEAD_DIM, lanes), 1)
    tiles = []
    for first in range(0, N_HEADS, lanes // HEAD_DIM):
        row = jnp.zeros((1, lanes), F32)
        for h in range(first, first + lanes // HEAD_DIM):
            acc = jnp.zeros((HEAD_DIM, lanes), F32)
            for n, vp in enumerate(v_pages):
                w = jnp.broadcast_to(p[h:h + 1, n * page:(n + 1) * page], (HEAD_DIM, page))
                acc = acc + vp[h * HEAD_DIM:(h + 1) * HEAD_DIM, :] * w
            total = jnp.sum(acc, axis=1, keepdims=True)
            on_diag = lane == sub + (h - first) * HEAD_DIM
            row = row + jnp.sum(jnp.where(on_diag, total, 0.0), axis=0, keepdims=True)
        tiles.append(row)
    pv = jnp.concatenate(tiles, axis=-1)
    per_lane = lambda t: jnp.sum(jnp.where(mine, t, 0.0), axis=0, keepdims=True)
    return (pv + per_lane(p_new) * v_new) / per_lane(l)


def _rope_tables(pos, heads):
    half = HEAD_DIM // 2
    inv_freq = ROPE_THETA ** (-jnp.arange(half, dtype=F32) / half)
    ang = pos.astype(F32)[:, None] * inv_freq[None, :]
    cos = jnp.cos(ang)
    sin = jnp.sin(ang)
    cos_full = jnp.tile(jnp.concatenate([cos, cos], axis=-1), (1, heads))
    sin_signed = jnp.tile(jnp.concatenate([-sin, sin], axis=-1), (1, heads))
    return cos_full, sin_signed


def _post_mixer(x, po, at, cross, lp):
    x1, qc = _mix_out(x, po, at, lp["w_out"], lp["g_cross"], lp["w_cq"])
    return _mlp(x1, cross(qc), lp["w_co"], lp["g_mlp"], lp["w_up"], lp["w_down"], lp["g_final"])


def kernel(x_prompt, x_sample, cache_k, cache_v, cache_mem_k, cache_mem_v, state_pool, page_table, mem_prompt,
           g_mix, w_in, w_pool, s_pool, w_out, g_cross, g_mem, w_cq, w_ck, w_cv, w_co, g_mlp, w_up, w_down,
           g_final):
    depth = w_in.shape[0]
    assert depth == 1, "one decoder layer"
    bp, seq, d = x_prompt.shape
    bs, dec_seq, _ = x_sample.shape
    assert dec_seq == 1 and seq % MOBA_BLOCK == 0
    n_phys, page = cache_k.shape[1], cache_k.shape[2]
    n_pages = page_table.shape[1]
    past_len = n_pages * page
    assert past_len % MOBA_BLOCK == 0 and past_len >= POOL_STATE
    mem_len = mem_prompt.shape[1]

    l = 0
    row = lambda a: a.reshape(1, -1)
    lp = dict(w_out=w_out[l].astype(BF16), g_cross=row(g_cross[l]), w_cq=w_cq[l].astype(BF16),
              w_co=w_co[l].astype(BF16), g_mlp=row(g_mlp[l]), w_up=w_up[l].astype(BF16),
              w_down=w_down[l].astype(BF16), g_final=row(g_final))
    w_in_b = w_in[l].astype(BF16)
    w_pool_b = w_pool[l].astype(BF16)
    g_mix_r = row(g_mix[l])
    s_pool_r = row(s_pool[l])

    lane_heads = 128 // HEAD_DIM
    cos_p, sin_p = _rope_tables(jnp.arange(seq), lane_heads)
    qtb, kt_p, vt_p, kb, vtb, kmean, po_p, last_p = _proj_prompt(
        x_prompt, g_mix_r, w_in_b, cos_p, sin_p, cos_p[:, :HEAD_DIM].T, sin_p[:, :HEAD_DIM].T, w_pool_b, s_pool_r)
    cos_s, sin_s = _rope_tables(jnp.full((1,), past_len), N_HEADS)
    state_t = jnp.transpose(state_pool[l], (1, 0, 2))
    q_s, k_s, v_s, kt_s, vt_s, u_s, po_s = _proj_sample(x_sample.reshape(bs, d), g_mix_r, w_in_b, cos_s, sin_s,
                                                        state_t, w_pool_b, s_pool_r, past_len)

    tok = lambda a: a.reshape(bs, 1, -1)
    pages_t = lambda c: jnp.transpose(c, (0, 2, 3, 1)).reshape(n_phys, ATTN_WIDTH, page)
    at_p, at_s = _moba(qtb, kb, vtb, kmean.reshape(bp, seq // MOBA_BLOCK, ATTN_WIDTH), page_table, tok(q_s),
                       tok(k_s), tok(v_s), pages_t(cache_k[l]), pages_t(cache_v[l]))

    mk_p, mv_p, mkb, mvb = _mem_kv(mem_prompt.reshape(bp * mem_len, d), row(g_mem[l]), w_ck[l].astype(BF16),
                                   w_cv[l].astype(BF16))
    cross_p = lambda qc: _cross(qc.reshape(bp, seq, d), mkb.reshape(bp, mem_len, d), mvb.reshape(bp, mem_len, d),
                                512).reshape(bp * seq, d)
    flat = lambda a: a.reshape(-1, a.shape[-1])
    y_prompt = _post_mixer(flat(x_prompt), flat(po_p), flat(at_p), cross_p, lp).reshape(bp, seq, d)
    cross_s = lambda qc: _cross_sample(qc, cache_mem_k[l], cache_mem_v[l])
    y_sample = _post_mixer(flat(x_sample), po_s, flat(at_s), cross_s, lp).reshape(bs, 1, d)

    def heads(t, b_):
        return jnp.transpose(t.reshape(b_, N_HEADS, HEAD_DIM, -1), (0, 3, 1, 2))[None]

    mem_heads = lambda a: a.reshape(1, bp, mem_len, MEM_HEADS, MEM_HEAD_DIM)
    pool_sample = jnp.transpose(jnp.concatenate([state_t[1:], u_s[None]], axis=0), (1, 0, 2))[None]
    return (y_prompt, y_sample, heads(kt_p, bp), heads(vt_p, bp), last_p[:, 1:][None], mem_heads(mk_p),
            mem_heads(mv_p), heads(kt_s.T.reshape(bs, ATTN_WIDTH, 1), bs), heads(vt_s.T.reshape(bs, ATTN_WIDTH, 1), bs),
            pool_sample)
```

```python
import functools

import jax
import jax.numpy as jnp
from jax import lax
from jax.experimental import pallas as pl
from jax.experimental.pallas import tpu as pltpu

D_MODEL = 1024
POOL_WIDTH = 512
POOL_WINDOWS = (2, 4, 8, 16)
POOL_GC = POOL_WIDTH // len(POOL_WINDOWS)
POOL_STATE = max(POOL_WINDOWS) - 1
POOL_HIST = POOL_STATE + 1
ATTN_WIDTH = D_MODEL - POOL_WIDTH
HEAD_DIM = 64
N_HEADS = ATTN_WIDTH // HEAD_DIM
MOBA_BLOCK = 256
MOBA_TOPK = 3
ROPE_THETA = 10000.0
MEM_HEADS = 4
MEM_HEAD_DIM = D_MODEL // MEM_HEADS
D_FF = 4 * D_MODEL
FF_CHUNK = 1024
EPS = 1e-6
NEG = -1e30

VMEM_LIMIT_BYTES = 56 * 1024 * 1024

F32 = jnp.float32
BF16 = jnp.bfloat16
NT_DIMS = (((1,), (1,)), ((), ()))


def _params(*semantics):
    return pltpu.CompilerParams(dimension_semantics=semantics, vmem_limit_bytes=VMEM_LIMIT_BYTES)


def _resident(shape):
    zeros = (0,) * len(shape)
    return pl.BlockSpec(shape, lambda *_: zeros, pipeline_mode=pl.Buffered(1))


def _rms(x, g):
    return x * lax.rsqrt(jnp.mean(x * x, axis=-1, keepdims=True) + EPS) * g


def _rope(t, cos, sin_signed):
    width = t.shape[-1]
    lane = lax.broadcasted_iota(jnp.int32, t.shape, t.ndim - 1)
    first_half = (lane % HEAD_DIM) < (HEAD_DIM // 2)
    partner = jnp.where(first_half,
                        pltpu.roll(t, width - HEAD_DIM // 2, t.ndim - 1),
                        pltpu.roll(t, HEAD_DIM // 2, t.ndim - 1))
    return t * cos + partner * sin_signed


def _rope_t(t, cos, sin_signed):
    rows = t.shape[0]
    row = lax.broadcasted_iota(jnp.int32, t.shape, 0)
    first_half = (row % HEAD_DIM) < (HEAD_DIM // 2)
    partner = jnp.where(first_half, pltpu.roll(t, rows - HEAD_DIM // 2, 0), pltpu.roll(t, HEAD_DIM // 2, 0))
    return t * cos + partner * sin_signed


def _topk_member(s, valid, idx, axis):
    n = s.shape[axis]
    s = jnp.where(valid, s, NEG)
    rank = jnp.zeros(s.shape, jnp.int32)
    for j in range(n):
        sj = lax.slice_in_dim(s, j, j + 1, axis=axis)
        beats = (sj > s) | ((sj == s) & (j < idx))
        rank = rank + beats.astype(jnp.int32)
    return (rank < MOBA_TOPK) & valid


def _proj_prompt_kernel(x_ref, g_ref, w_ref, cos_ref, sin_ref, cost_ref, sint_ref, wp_ref, sp_ref,
                        qt_ref, kt_ref, vt_ref, kb_ref, vtb_ref, km_ref, po_ref, last_ref, ext_ref):
    i = pl.program_id(1)
    tm = x_ref.shape[1]

    @pl.when(i == 0)
    def _():
        ext_ref[0:POOL_HIST, :] = jnp.zeros((POOL_HIST, POOL_WIDTH), F32)

    h = _rms(x_ref[0], g_ref[...]).astype(BF16)

    def project(first):
        return jnp.dot(h, w_ref[:, first:first + ATTN_WIDTH], preferred_element_type=F32)

    lane_reps = ATTN_WIDTH // cos_ref.shape[1]
    u = project(0)
    k = _rope(project(POOL_WIDTH + ATTN_WIDTH), jnp.tile(cos_ref[...], (1, lane_reps)),
              jnp.tile(sin_ref[...], (1, lane_reps)))
    q = project(POOL_WIDTH)
    v = project(POOL_WIDTH + 2 * ATTN_WIDTH)
    qt = _rope_t(q.T, jnp.tile(cost_ref[...], (N_HEADS, 1)), jnp.tile(sint_ref[...], (N_HEADS, 1)))
    qt_ref[0] = (qt * (HEAD_DIM ** -0.5 * LOG2_E)).astype(BF16)
    vt = v.T
    kt_ref[0] = k.T
    vt_ref[0] = vt
    kb_ref[0] = k.astype(BF16)
    vtb_ref[0, 0] = vt.astype(BF16)
    km_ref[0, 0] = jnp.mean(k, axis=0, keepdims=True)

    pos = i * tm + lax.broadcasted_iota(jnp.int32, (tm, 1), 0)
    for g, w in enumerate(POOL_WINDOWS):
        cols = slice(g * POOL_GC, (g + 1) * POOL_GC)
        ug = u[:, cols]
        wsum = jnp.concatenate([ext_ref[:, cols], ug], axis=0)
        shift = 1
        while shift < w:
            wsum = wsum + pltpu.roll(wsum, shift, 0)
            shift *= 2
        cnt = jnp.minimum(w, pos + 1).astype(F32)
        d = wsum[POOL_HIST:, :] / cnt - ug
        mixed = jnp.dot(d.astype(BF16), wp_ref[g], preferred_element_type=F32)
        po_ref[0, :, cols] = (mixed * sp_ref[:, cols]).astype(BF16)
    tail = u[tm - POOL_HIST:, :]
    ext_ref[...] = tail
    last_ref[0] = tail


def _proj_prompt(x, g, w_in, cos, sin, cos_t, sin_t, w_pool, s_pool):
    b, s, d = x.shape
    tm = MOBA_BLOCK
    nq = s // tm
    row = lambda bi, i: (bi, i, 0)
    col = lambda bi, i: (bi, 0, i)
    blk = lambda bi, i: (bi, i, 0, 0)
    return pl.pallas_call(
        _proj_prompt_kernel,
        grid=(b, nq),
        in_specs=[
            pl.BlockSpec((1, tm, d), row),
            _resident((1, d)),
            _resident(w_in.shape),
            pl.BlockSpec((tm, cos.shape[1]), lambda bi, i: (i, 0)),
            pl.BlockSpec((tm, cos.shape[1]), lambda bi, i: (i, 0)),
            pl.BlockSpec((HEAD_DIM, tm), lambda bi, i: (0, i)),
            pl.BlockSpec((HEAD_DIM, tm), lambda bi, i: (0, i)),
            _resident(w_pool.shape),
            _resident((1, POOL_WIDTH)),
        ],
        out_specs=[
            pl.BlockSpec((1, ATTN_WIDTH, tm), col),
            pl.BlockSpec((1, ATTN_WIDTH, tm), col),
            pl.BlockSpec((1, ATTN_WIDTH, tm), col),
            pl.BlockSpec((1, tm, ATTN_WIDTH), row),
            pl.BlockSpec((1, 1, ATTN_WIDTH, tm), blk),
            pl.BlockSpec((1, 1, 1, ATTN_WIDTH), blk),
            pl.BlockSpec((1, tm, POOL_WIDTH), row),
            pl.BlockSpec((1, POOL_HIST, POOL_WIDTH), lambda bi, i: (bi, 0, 0)),
        ],
        out_shape=[
            jax.ShapeDtypeStruct((b, ATTN_WIDTH, s), BF16),
            jax.ShapeDtypeStruct((b, ATTN_WIDTH, s), F32),
            jax.ShapeDtypeStruct((b, ATTN_WIDTH, s), F32),
            jax.ShapeDtypeStruct((b, s, ATTN_WIDTH), BF16),
            jax.ShapeDtypeStruct((b, nq, ATTN_WIDTH, tm), BF16),
            jax.ShapeDtypeStruct((b, nq, 1, ATTN_WIDTH), F32),
            jax.ShapeDtypeStruct((b, s, POOL_WIDTH), BF16),
            jax.ShapeDtypeStruct((b, POOL_HIST, POOL_WIDTH), F32),
        ],
        scratch_shapes=[pltpu.VMEM((POOL_HIST, POOL_WIDTH), F32)],
        compiler_params=_params("arbitrary", "arbitrary"),
        name="proj_prompt",
    )(x, g, w_in, cos, sin, cos_t, sin_t, w_pool, s_pool)


SUBLANES = 8
LOGIT_AHEAD = 2
LOGIT_SLOTS = 4
LOG2_E = 1.4426950408889634


def _all_sublanes(t, combine):
    for s in (4, 2, 1):
        t = combine(t, pltpu.roll(t, s, 0))
    return t


def _moba_kernel(pt_ref, qt_ref, kb_ref, vtb_ref, km_ref, qs_ref, kn_ref, vn_ref, ck_ref, cv_ref,
                 o_ref, os_ref, acc_ref, m_ref, l_ref, take_ref, lg_ref, kbuf_ref, vbuf_ref, page_sem,
                 *, n_pages, per_step):
    i = pl.program_id(1)
    step = pl.program_id(0) * pl.num_programs(1) + i
    n_steps = pl.num_programs(0) * pl.num_programs(1)
    slot = step % 2
    tq = qt_ref.shape[2]
    nb = km_ref.shape[1]

    def page_copies(page_id, half):
        copies = []
        for s in range(per_step):
            for p in range(n_pages):
                pid = page_id(s, p)
                dst = s * n_pages + p
                copies.append(pltpu.make_async_copy(ck_ref.at[pid], kbuf_ref.at[half, dst], page_sem.at[half, 0]))
                copies.append(pltpu.make_async_copy(cv_ref.at[pid], vbuf_ref.at[half, dst], page_sem.at[half, 1]))
        return copies

    def start_step(t, half):
        for c in page_copies(lambda s, p: pt_ref[t * per_step + s, p], half):
            c.start()

    @pl.when(step == 0)
    def _():
        start_step(0, 0)

    @pl.when(step + 1 < n_steps)
    def _():
        start_step(step + 1, 1 - slot)


    km_rows = jnp.concatenate([km_ref[0]] * N_HEADS, axis=0)
    row_head = lax.broadcasted_iota(jnp.int32, km_rows.shape, 0) // nb
    lane_head = lax.broadcasted_iota(jnp.int32, km_rows.shape, 1) // HEAD_DIM
    km_heads = jnp.where(row_head == lane_head, km_rows, 0.0).astype(BF16)
    s = jnp.dot(km_heads, qt_ref[0], preferred_element_type=F32).reshape(N_HEADS, nb, tq)
    blk = lax.broadcasted_iota(jnp.int32, s.shape, 1)
    sel = _topk_member(s, blk < i, blk, axis=1).astype(F32)

    for h in range(N_HEADS):
        for jb in range(nb):
            take_ref[h, jb] = jnp.broadcast_to(sel[h, jb:jb + 1, :], (SUBLANES, tq))

    def head_rows(h):
        return slice(h * HEAD_DIM, (h + 1) * HEAD_DIM)

    def stage_logits(h, j):
        off = pl.multiple_of(j * MOBA_BLOCK, MOBA_BLOCK)
        lg_ref[h % LOGIT_SLOTS] = jnp.dot(kb_ref[0, pl.ds(off, MOBA_BLOCK), head_rows(h)],
                                          qt_ref[0, head_rows(h), :], preferred_element_type=F32)

    def weighted_values(h, j, p3):
        p = p3.reshape(MOBA_BLOCK, tq).astype(BF16)
        pv = jnp.dot(vtb_ref[0, j, head_rows(h), :], p, preferred_element_type=F32)
        return pv.reshape(HEAD_DIM // SUBLANES, SUBLANES, tq)

    def tiles(t):
        return t.reshape(t.shape[0] // SUBLANES, SUBLANES, tq)

    def block_pass(j, j_next, own, between=None):
        for h in range(N_HEADS):
            ahead = h + LOGIT_AHEAD
            if ahead < N_HEADS:
                stage_logits(ahead, j)
            else:
                stage_logits(ahead - N_HEADS, j_next)
            if between is not None:
                between(h)
            lg = lg_ref[h % LOGIT_SLOTS]
            if own:
                kk = lax.broadcasted_iota(jnp.int32, lg.shape, 0)
                qq = lax.broadcasted_iota(jnp.int32, lg.shape, 1)
                lg3 = tiles(jnp.where(kk <= qq, lg, NEG))
                m = _all_sublanes(jnp.max(lg3, axis=0), jnp.maximum)
                p3 = jnp.exp2(lg3 - m[None])
                m_ref[h] = m
                l_ref[h] = _all_sublanes(jnp.sum(p3, axis=0), jnp.add)
                acc_ref[head_rows(h), :] = weighted_values(h, j, p3).reshape(HEAD_DIM, tq)
            else:
                lg3 = tiles(lg)
                take = take_ref[h, j] > 0.0
                m_old = m_ref[h]
                l_old = l_ref[h]
                acc_old = tiles(acc_ref[head_rows(h), :])
                m = jnp.maximum(m_old, _all_sublanes(jnp.max(lg3, axis=0), jnp.maximum))
                a = jnp.exp2(m_old - m)
                p3 = jnp.exp2(lg3 - m[None])
                l = a * l_old + _all_sublanes(jnp.sum(p3, axis=0), jnp.add)
                acc3 = a[None] * acc_old + weighted_values(h, j, p3)
                m_ref[h] = jnp.where(take, m, m_old)
                l_ref[h] = jnp.where(take, l, l_old)
                acc_ref[head_rows(h), :] = jnp.where(take[None], acc3, acc_old).reshape(HEAD_DIM, tq)

    last_past = jnp.maximum(i - 1, 0)
    for h in range(LOGIT_AHEAD):
        stage_logits(h, i)

    for c in page_copies(lambda s, p: 0, slot):
        c.wait()
    sample_q = [_sample_queries(qs_ref[s]) for s in range(per_step)]
    jobs = [(s, p) for s in range(per_step) for p in range(n_pages)]
    per_head = -(-len(jobs) // N_HEADS)
    page_logits = {}

    def sample_page_logits(h):
        for s, p in jobs[h * per_head:(h + 1) * per_head]:
            page_logits[s, p] = _sample_page_logits(sample_q[s], kbuf_ref.at[slot, s * n_pages + p])

    block_pass(i, 0, own=True, between=sample_page_logits)
    for s in range(per_step):
        v_pages = [vbuf_ref.at[slot, s * n_pages + p] for p in range(n_pages)]
        lg = jnp.concatenate([page_logits[s, p] for p in range(n_pages)], axis=-1)
        os_ref[s] = _sample_finish(sample_q[s], lg, kn_ref[s], vn_ref[s], v_pages).astype(BF16)

    def two_past_blocks(jj, carry):
        j = 2 * jj
        block_pass(j, j + 1, own=False)
        block_pass(j + 1, jnp.minimum(j + 2, last_past), own=False)
        return carry

    lax.fori_loop(0, i // 2, two_past_blocks, 0)

    @pl.when(i % 2 == 1)
    def _():
        block_pass(last_past, last_past, own=False)

    for h in range(N_HEADS):
        rows = slice(h * HEAD_DIM, (h + 1) * HEAD_DIM)
        acc_ref[rows, :] = (tiles(acc_ref[rows, :]) / l_ref[h][None]).reshape(HEAD_DIM, tq)
    o_ref[0] = acc_ref[...].T.astype(BF16)


def _moba(qtb, kb, vtb, kmean, page_table, q_s, k_s, v_s, cache_kt, cache_vt):
    b, w, s = qtb.shape
    nq = s // MOBA_BLOCK
    bs, n_pages = page_table.shape
    page = cache_kt.shape[2]
    steps = b * nq
    assert bs % steps == 0
    per_step = bs // steps
    step_seq = lambda bi, i, pt: (bi * nq + i, 0, 0)
    tok = pl.BlockSpec((per_step, 1, w), step_seq)
    in_hbm = pl.BlockSpec(memory_space=pl.ANY)
    page_buf = pltpu.VMEM((2, per_step * n_pages, w, page), cache_kt.dtype)
    grid_spec = pltpu.PrefetchScalarGridSpec(
        num_scalar_prefetch=1,
        grid=(b, nq),
        in_specs=[
            pl.BlockSpec((1, w, MOBA_BLOCK), lambda bi, i, pt: (bi, 0, i)),
            pl.BlockSpec((1, s, w), lambda bi, i, pt: (bi, 0, 0)),
            pl.BlockSpec((1, nq, w, MOBA_BLOCK), lambda bi, i, pt: (bi, 0, 0, 0)),
            pl.BlockSpec((1, nq, w), lambda bi, i, pt: (bi, 0, 0)),
            tok, tok, tok, in_hbm, in_hbm,
        ],
        out_specs=[pl.BlockSpec((1, MOBA_BLOCK, w), lambda bi, i, pt: (bi, i, 0)), tok],
        scratch_shapes=[pltpu.VMEM((w, MOBA_BLOCK), F32),
                        pltpu.VMEM((N_HEADS, SUBLANES, MOBA_BLOCK), F32),
                        pltpu.VMEM((N_HEADS, SUBLANES, MOBA_BLOCK), F32),
                        pltpu.VMEM((N_HEADS, nq, SUBLANES, MOBA_BLOCK), F32),
                        pltpu.VMEM((LOGIT_SLOTS, MOBA_BLOCK, MOBA_BLOCK), F32),
                        page_buf, page_buf, pltpu.SemaphoreType.DMA((2, 2))],
    )
    return pl.pallas_call(
        functools.partial(_moba_kernel, n_pages=n_pages, per_step=per_step),
        grid_spec=grid_spec,
        out_shape=[jax.ShapeDtypeStruct((b, s, w), BF16), jax.ShapeDtypeStruct((bs, 1, w), BF16)],
        compiler_params=_params("arbitrary", "arbitrary"),
        name="moba",
    )(page_table, qtb, kb, vtb, kmean, q_s, k_s, v_s, cache_kt, cache_vt)


def _mem_kv_kernel(m_ref, g_ref, wk_ref, wv_ref, mk_ref, mv_ref, mkb_ref, mvb_ref):
    h = _rms(m_ref[...], g_ref[...]).astype(BF16)
    mk = jnp.dot(h, wk_ref[...], preferred_element_type=F32)
    mv = jnp.dot(h, wv_ref[...], preferred_element_type=F32)
    mk_ref[...] = mk
    mv_ref[...] = mv
    mkb_ref[...] = mk.astype(BF16)
    mvb_ref[...] = mv.astype(BF16)


def _mem_kv(mem, g, w_ck, w_cv):
    rows, d = mem.shape
    tm = 512
    row = lambda i: (i, 0)
    return pl.pallas_call(
        _mem_kv_kernel,
        grid=(rows // tm,),
        in_specs=[pl.BlockSpec((tm, d), row), _resident((1, d)), _resident(w_ck.shape), _resident(w_cv.shape)],
        out_specs=[pl.BlockSpec((tm, d), row)] * 4,
        out_shape=[jax.ShapeDtypeStruct((rows, d), F32)] * 2 + [jax.ShapeDtypeStruct((rows, d), BF16)] * 2,
        compiler_params=_params("arbitrary"),
        name="mem_kv",
    )(mem, g, w_ck, w_cv)


def _mix_out_kernel(x_ref, po_ref, at_ref, wo_ref, g_ref, wq_ref, x1_ref, qc_ref):
    cat = jnp.concatenate([po_ref[...], at_ref[...]], axis=-1)
    x1 = x_ref[...] + jnp.dot(cat, wo_ref[...], preferred_element_type=F32)
    x1_ref[...] = x1
    hq = _rms(x1, g_ref[...]).astype(BF16)
    qc = jnp.dot(hq, wq_ref[...], preferred_element_type=F32)
    qc_ref[...] = (qc * (MEM_HEAD_DIM ** -0.5)).astype(BF16)


def _mix_out(x, po, at, w_out, g_cross, w_cq):
    rows, d = x.shape
    tm = min(512, rows)
    row = lambda i: (i, 0)
    return pl.pallas_call(
        _mix_out_kernel,
        grid=(rows // tm,),
        in_specs=[pl.BlockSpec((tm, d), row), pl.BlockSpec((tm, POOL_WIDTH), row),
                  pl.BlockSpec((tm, ATTN_WIDTH), row), _resident(w_out.shape), _resident((1, d)),
                  _resident(w_cq.shape)],
        out_specs=[pl.BlockSpec((tm, d), row), pl.BlockSpec((tm, d), row)],
        out_shape=[jax.ShapeDtypeStruct((rows, d), F32), jax.ShapeDtypeStruct((rows, d), BF16)],
        compiler_params=_params("arbitrary"),
        name="mix_out",
    )(x, po, at, w_out, g_cross, w_cq)


def _cross_kernel(q_ref, mk_ref, mv_ref, o_ref):
    q = q_ref[0]
    for h in range(MEM_HEADS):
        sl = slice(h * MEM_HEAD_DIM, (h + 1) * MEM_HEAD_DIM)
        lg = lax.dot_general(q[:, sl], mk_ref[0, :, sl].astype(BF16), NT_DIMS, preferred_element_type=F32)
        p = jnp.exp(lg - jnp.max(lg, axis=-1, keepdims=True))
        l = jnp.sum(p, axis=-1, keepdims=True)
        o = jnp.dot(p.astype(BF16), mv_ref[0, :, sl].astype(BF16), preferred_element_type=F32)
        o_ref[0, :, sl] = (o / l).astype(BF16)


def _cross(qc, mk, mv, tm):
    b, s, d = qc.shape
    m = mk.shape[1]
    return pl.pallas_call(
        _cross_kernel,
        grid=(b, s // tm),
        in_specs=[pl.BlockSpec((1, tm, d), lambda bi, i: (bi, i, 0)),
                  pl.BlockSpec((1, m, d), lambda bi, i: (bi, 0, 0)),
                  pl.BlockSpec((1, m, d), lambda bi, i: (bi, 0, 0))],
        out_specs=pl.BlockSpec((1, tm, d), lambda bi, i: (bi, i, 0)),
        out_shape=jax.ShapeDtypeStruct((b, s, d), BF16),
        compiler_params=_params("arbitrary", "arbitrary"),
        name="cross_attn",
    )(qc, mk, mv)


MEM_LANES = 128
MEM_PARTS = MEM_HEAD_DIM // MEM_LANES
MEM_ROWS = MEM_PARTS * MEM_HEADS


def _cross_sample_products(q, mk_ref):
    return lax.dot_general(q, mk_ref[...].astype(BF16), NT_DIMS, preferred_element_type=F32)


def _cross_sample_attend(r, mv_ref):
    n = r.shape[1]
    row = lax.broadcasted_iota(jnp.int32, (MEM_ROWS, n), 0)
    lane_row = lax.broadcasted_iota(jnp.int32, (MEM_ROWS, n), 1) % MEM_ROWS
    own = lane_row == row
    valid = (row < MEM_HEADS) & own

    def other_part(t, lane_shift):
        return pltpu.roll(pltpu.roll(t, MEM_HEADS, 0), lane_shift, 1)

    r = jnp.where(own, r, 0.0)
    lg = jnp.where(valid, r + other_part(r, n - MEM_HEADS), NEG)
    p = jnp.where(valid, jnp.exp(lg - jnp.max(lg, axis=-1, keepdims=True)), 0.0)
    p = p + other_part(p, MEM_HEADS)
    l = jnp.sum(p, axis=-1, keepdims=True)
    return jnp.dot(p.astype(BF16), mv_ref[...].astype(BF16), preferred_element_type=F32) / l


def _stored_rows(qc):
    rows = qc.shape[0]
    return jnp.transpose(qc.reshape(rows, MEM_HEADS, MEM_PARTS, MEM_LANES), (0, 2, 1, 3)).reshape(rows, MEM_ROWS, MEM_LANES)


def _stored_slots(m):
    b, slots = m.shape[:2]
    m = m.reshape(b, slots, MEM_HEADS, MEM_PARTS, MEM_LANES)
    return jnp.transpose(m, (0, 1, 3, 2, 4)).reshape(b, slots * MEM_ROWS, MEM_LANES)


def _unstored_rows(o):
    rows = o.shape[0]
    return jnp.transpose(o.reshape(rows, MEM_PARTS, MEM_HEADS, MEM_LANES), (0, 2, 1, 3)).reshape(rows, -1)


def _mlp_kernel(x1_ref, o_ref, wco_ref, gm_ref, wu_ref, wd_ref, gf_ref, *rest):
    if len(rest) == 1:
        (y_ref,) = rest
        n_seq = 0
    else:
        qs_ref, mk_ref, mv_ref, y_ref, os_ref = rest
        n_seq = qs_ref.shape[0]
    products = [_cross_sample_products(qs_ref[s], mk_ref.at[s]) for s in range(n_seq)]
    x2 = x1_ref[...] + jnp.dot(o_ref[...], wco_ref[...], preferred_element_type=F32)
    hm = _rms(x2, gm_ref[...]).astype(BF16)
    acc = x2
    for c in range(D_FF // FF_CHUNK):
        if c == 1:
            for s in range(n_seq):
                os_ref[s] = _cross_sample_attend(products[s], mv_ref.at[s]).astype(BF16)
        cs = slice(c * FF_CHUNK, (c + 1) * FF_CHUNK)
        a = jnp.dot(hm, wu_ref[:, cs], preferred_element_type=F32)
        a = jnp.square(jnp.maximum(a, 0.0)).astype(BF16)
        acc = acc + jnp.dot(a, wd_ref[cs, :], preferred_element_type=F32)
    y_ref[...] = _rms(acc, gf_ref[...])


def _mlp(x1, o, w_co, g_mlp, w_up, w_down, g_final, sample_cross=None):
    rows, d = x1.shape
    tm = min(512, rows)
    steps = rows // tm
    row = lambda i: (i, 0)
    in_specs = [pl.BlockSpec((tm, d), row), pl.BlockSpec((tm, d), row), _resident(w_co.shape),
                _resident((1, d)), _resident(w_up.shape), _resident(w_down.shape), _resident((1, d))]
    args = [x1, o, w_co, g_mlp, w_up, w_down, g_final]
    out_specs = [pl.BlockSpec((tm, d), row)]
    out_shape = [jax.ShapeDtypeStruct((rows, d), F32)]
    if sample_cross is not None:
        qc, mem_k, mem_v = sample_cross
        b, slots = mem_k.shape[:2]
        assert b % steps == 0
        seq = lambda n: pl.BlockSpec((b // steps, n, MEM_LANES), lambda i: (i, 0, 0))
        in_specs += [seq(MEM_ROWS), seq(slots * MEM_ROWS), seq(slots * MEM_ROWS)]
        args += [_stored_rows(qc), _stored_slots(mem_k), _stored_slots(mem_v)]
        out_specs.append(seq(MEM_ROWS))
        out_shape.append(jax.ShapeDtypeStruct((b, MEM_ROWS, MEM_LANES), BF16))
    outs = pl.pallas_call(
        _mlp_kernel,
        grid=(steps,),
        in_specs=in_specs,
        out_specs=out_specs,
        out_shape=out_shape,
        compiler_params=_params("arbitrary"),
        name="mlp",
    )(*args)
    if sample_cross is None:
        return outs[0]
    return outs[0], _unstored_rows(outs[1])


def _proj_sample_kernel(x_ref, g_ref, w_ref, cos_ref, sin_ref, st_ref, wp_ref, sp_ref,
                        q_ref, k_ref, v_ref, kt_ref, vt_ref, u_ref, po_ref, *, n_prev):
    h = _rms(x_ref[...], g_ref[...]).astype(BF16)
    proj = jnp.dot(h, w_ref[...], preferred_element_type=F32)
    u = proj[:, :POOL_WIDTH]
    cos = cos_ref[...]
    sin = sin_ref[...]
    q_ref[...] = _rope(proj[:, POOL_WIDTH:POOL_WIDTH + ATTN_WIDTH], cos, sin) * (HEAD_DIM ** -0.5)
    k = _rope(proj[:, POOL_WIDTH + ATTN_WIDTH:POOL_WIDTH + 2 * ATTN_WIDTH], cos, sin)
    v = proj[:, POOL_WIDTH + 2 * ATTN_WIDTH:]
    k_ref[...] = k
    v_ref[...] = v
    kt_ref[...] = k.T
    vt_ref[...] = v.T
    u_ref[...] = u
    for g, w in enumerate(POOL_WINDOWS):
        cols = slice(g * POOL_GC, (g + 1) * POOL_GC)
        ug = u[:, cols]
        wsum = ug
        for j in range(1, w):
            wsum = wsum + st_ref[POOL_STATE - j, :, cols]
        d = wsum / float(min(w, n_prev + 1)) - ug
        mixed = jnp.dot(d.astype(BF16), wp_ref[g], preferred_element_type=F32)
        po_ref[:, cols] = (mixed * sp_ref[:, cols]).astype(BF16)


def _proj_sample(x, g, w_in, cos, sin, state_t, w_pool, s_pool, n_prev):
    rows, d = x.shape
    wide = jax.ShapeDtypeStruct((rows, ATTN_WIDTH), F32)
    tall = jax.ShapeDtypeStruct((ATTN_WIDTH, rows), F32)
    full = lambda a: pl.BlockSpec(a.shape, lambda i, nd=len(a.shape): (0,) * nd)
    args = (x, g, w_in, cos, sin, state_t, w_pool, s_pool)
    outs = [wide, wide, wide, tall, tall, wide, jax.ShapeDtypeStruct((rows, POOL_WIDTH), BF16)]
    return pl.pallas_call(
        functools.partial(_proj_sample_kernel, n_prev=n_prev),
        grid=(1,),
        in_specs=[full(a) for a in args],
        out_specs=[full(o) for o in outs],
        out_shape=outs,
        compiler_params=_params("arbitrary"),
        name="proj_sample",
    )(*args)


def _own_lanes():
    head_of_lane = lax.broadcasted_iota(jnp.int32, (N_HEADS, ATTN_WIDTH), 1) // HEAD_DIM
    head = lax.broadcasted_iota(jnp.int32, (N_HEADS, ATTN_WIDTH), 0)
    return head_of_lane == head


def _sample_queries(q):
    return jnp.where(_own_lanes(), q, 0.0)


def _sample_page_logits(q_heads, k_page):
    return jnp.dot(q_heads.astype(BF16), k_page[...].astype(BF16), preferred_element_type=F32)


def _sample_finish(q_heads, lg, k_new, v_new, v_pages):
    page = v_pages[0].shape[1]
    past = lg.shape[1]
    nb = past // MOBA_BLOCK
    mine = _own_lanes()

    s = jnp.concatenate(
        [jnp.sum(lg[:, j * MOBA_BLOCK:(j + 1) * MOBA_BLOCK], axis=-1, keepdims=True) for j in range(nb)], axis=-1)
    blk_id = lax.broadcasted_iota(jnp.int32, (N_HEADS, nb), 1)
    sel = _topk_member(s, blk_id >= 0, blk_id, axis=1)
    key_blk = lax.broadcasted_iota(jnp.int32, (N_HEADS, past), 1) // MOBA_BLOCK
    chosen = jnp.zeros((N_HEADS, past), jnp.bool_)
    for j in range(nb):
        chosen = chosen | ((key_blk == j) & sel[:, j:j + 1])
    lg = jnp.where(chosen, lg, NEG)

    lg_new = jnp.sum(q_heads * k_new, axis=-1, keepdims=True)
    m = jnp.maximum(jnp.max(lg, axis=-1, keepdims=True), lg_new)
    p = jnp.exp(lg - m)
    p_new = jnp.exp(lg_new - m)
    l = jnp.sum(p, axis=-1, keepdims=True) + p_new

    lanes = page
    sub = lax.broadcasted_iota(jnp.int32, (HEAD_DIM, lanes), 0)
    lane = lax.broadcasted_iota(jnp.int32, (HEAD_DIM, lanes), 1)
    tiles = []
    for first in range(0, N_HEADS, lanes // HEAD_DIM):
        row = jnp.zeros((1, lanes), F32)
        for h in range(first, first + lanes // HEAD_DIM):
            acc = jnp.zeros((HEAD_DIM, lanes), F32)
            for n, vp in enumerate(v_pages):
                w = jnp.broadcast_to(p[h:h + 1, n * page:(n + 1) * page], (HEAD_DIM, page))
                acc = acc + vp[h * HEAD_DIM:(h + 1) * HEAD_DIM, :] * w
            total = jnp.sum(acc, axis=1, keepdims=True)
            on_diag = lane == sub + (h - first) * HEAD_DIM
            row = row + jnp.sum(jnp.where(on_diag, total, 0.0), axis=0, keepdims=True)
        tiles.append(row)
    pv = jnp.concatenate(tiles, axis=-1)
    per_lane = lambda t: jnp.sum(jnp.where(mine, t, 0.0), axis=0, keepdims=True)
    return (pv + per_lane(p_new) * v_new) / per_lane(l)


def _rope_tables(pos, heads):
    half = HEAD_DIM // 2
    inv_freq = ROPE_THETA ** (-jnp.arange(half, dtype=F32) / half)
    ang = pos.astype(F32)[:, None] * inv_freq[None, :]
    cos = jnp.cos(ang)
    sin = jnp.sin(ang)
    cos_full = jnp.tile(jnp.concatenate([cos, cos], axis=-1), (1, heads))
    sin_signed = jnp.tile(jnp.concatenate([-sin, sin], axis=-1), (1, heads))
    return cos_full, sin_signed


def kernel(x_prompt, x_sample, cache_k, cache_v, cache_mem_k, cache_mem_v, state_pool, page_table, mem_prompt,
           g_mix, w_in, w_pool, s_pool, w_out, g_cross, g_mem, w_cq, w_ck, w_cv, w_co, g_mlp, w_up, w_down,
           g_final):
    depth = w_in.shape[0]
    assert depth == 1, "one decoder layer"
    bp, seq, d = x_prompt.shape
    bs, dec_seq, _ = x_sample.shape
    assert dec_seq == 1 and seq % MOBA_BLOCK == 0
    n_phys, page = cache_k.shape[1], cache_k.shape[2]
    n_pages = page_table.shape[1]
    past_len = n_pages * page
    assert past_len % MOBA_BLOCK == 0 and past_len >= POOL_STATE
    mem_len = mem_prompt.shape[1]

    l = 0
    row = lambda a: a.reshape(1, -1)
    lp = dict(w_out=w_out[l].astype(BF16), g_cross=row(g_cross[l]), w_cq=w_cq[l].astype(BF16),
              w_co=w_co[l].astype(BF16), g_mlp=row(g_mlp[l]), w_up=w_up[l].astype(BF16),
              w_down=w_down[l].astype(BF16), g_final=row(g_final))
    w_in_b = w_in[l].astype(BF16)
    w_pool_b = w_pool[l].astype(BF16)
    g_mix_r = row(g_mix[l])
    s_pool_r = row(s_pool[l])

    lane_heads = 128 // HEAD_DIM
    cos_p, sin_p = _rope_tables(jnp.arange(seq), lane_heads)
    qtb, kt_p, vt_p, kb, vtb, kmean, po_p, last_p = _proj_prompt(
        x_prompt, g_mix_r, w_in_b, cos_p, sin_p, cos_p[:, :HEAD_DIM].T, sin_p[:, :HEAD_DIM].T, w_pool_b, s_pool_r)
    cos_s, sin_s = _rope_tables(jnp.full((1,), past_len), N_HEADS)
    state_t = jnp.transpose(state_pool[l], (1, 0, 2))
    q_s, k_s, v_s, kt_s, vt_s, u_s, po_s = _proj_sample(x_sample.reshape(bs, d), g_mix_r, w_in_b, cos_s, sin_s,
                                                        state_t, w_pool_b, s_pool_r, past_len)

    tok = lambda a: a.reshape(bs, 1, -1)
    pages_t = lambda c: jnp.transpose(c, (0, 2, 3, 1)).reshape(n_phys, ATTN_WIDTH, page)
    at_p, at_s = _moba(qtb, kb, vtb, kmean.reshape(bp, seq // MOBA_BLOCK, ATTN_WIDTH), page_table, tok(q_s),
                       tok(k_s), tok(v_s), pages_t(cache_k[l]), pages_t(cache_v[l]))

    mk_p, mv_p, mkb, mvb = _mem_kv(mem_prompt.reshape(bp * mem_len, d), row(g_mem[l]), w_ck[l].astype(BF16),
                                   w_cv[l].astype(BF16))
    flat = lambda a: a.reshape(-1, a.shape[-1])
    mix = lambda x, po, at: _mix_out(flat(x), flat(po), flat(at), lp["w_out"], lp["g_cross"], lp["w_cq"])
    tail = (lp["w_co"], lp["g_mlp"], lp["w_up"], lp["w_down"], lp["g_final"])
    x1_p, qc_p = mix(x_prompt, po_p, at_p)
    x1_s, qc_s = mix(x_sample, po_s, at_s)
    o_p = _cross(qc_p.reshape(bp, seq, d), mkb.reshape(bp, mem_len, d), mvb.reshape(bp, mem_len, d), 512)
    y_prompt, o_s = _mlp(x1_p, flat(o_p), *tail, sample_cross=(qc_s, cache_mem_k[l], cache_mem_v[l]))
    y_prompt = y_prompt.reshape(bp, seq, d)
    y_sample = _mlp(x1_s, o_s, *tail).reshape(bs, 1, d)

    def heads(t, b_):
        return jnp.transpose(t.reshape(b_, N_HEADS, HEAD_DIM, -1), (0, 3, 1, 2))[None]

    mem_heads = lambda a: a.reshape(1, bp, mem_len, MEM_HEADS, MEM_HEAD_DIM)
    pool_sample = jnp.transpose(jnp.concatenate([state_t[1:], u_s[None]], axis=0), (1, 0, 2))[None]
    return (y_prompt, y_sample, heads(kt_p, bp), heads(vt_p, bp), last_p[:, 1:][None], mem_heads(mk_p),
            mem_heads(mv_p), heads(kt_s.T.reshape(bs, ATTN_WIDTH, 1), bs), heads(vt_s.T.reshape(bs, ATTN_WIDTH, 1), bs),
            pool_sample)
```

```python
import functools

import jax
import jax.numpy as jnp
from jax import lax
from jax.experimental import pallas as pl
from jax.experimental.pallas import tpu as pltpu

D_MODEL = 1024
POOL_WIDTH = 512
POOL_WINDOWS = (2, 4, 8, 16)
POOL_GC = POOL_WIDTH // len(POOL_WINDOWS)
POOL_STATE = max(POOL_WINDOWS) - 1
POOL_HIST = POOL_STATE + 1
ATTN_WIDTH = D_MODEL - POOL_WIDTH
HEAD_DIM = 64
N_HEADS = ATTN_WIDTH // HEAD_DIM
MOBA_BLOCK = 256
MOBA_TOPK = 3
ROPE_THETA = 10000.0
MEM_HEADS = 4
MEM_HEAD_DIM = D_MODEL // MEM_HEADS
D_FF = 4 * D_MODEL
FF_CHUNK = 1024
EPS = 1e-6
NEG = -1e30

VMEM_LIMIT_BYTES = 56 * 1024 * 1024

F32 = jnp.float32
BF16 = jnp.bfloat16
NT_DIMS = (((1,), (1,)), ((), ()))


def _params(*semantics):
    return pltpu.CompilerParams(dimension_semantics=semantics, vmem_limit_bytes=VMEM_LIMIT_BYTES)


def _resident(shape):
    zeros = (0,) * len(shape)
    return pl.BlockSpec(shape, lambda *_: zeros, pipeline_mode=pl.Buffered(1))


def _rms(x, g):
    return x * lax.rsqrt(jnp.mean(x * x, axis=-1, keepdims=True) + EPS) * g


def _rope(t, cos, sin_signed):
    width = t.shape[-1]
    lane = lax.broadcasted_iota(jnp.int32, t.shape, t.ndim - 1)
    first_half = (lane % HEAD_DIM) < (HEAD_DIM // 2)
    partner = jnp.where(first_half,
                        pltpu.roll(t, width - HEAD_DIM // 2, t.ndim - 1),
                        pltpu.roll(t, HEAD_DIM // 2, t.ndim - 1))
    return t * cos + partner * sin_signed


def _rope_t(t, cos, sin_signed):
    rows = t.shape[0]
    row = lax.broadcasted_iota(jnp.int32, t.shape, 0)
    first_half = (row % HEAD_DIM) < (HEAD_DIM // 2)
    partner = jnp.where(first_half, pltpu.roll(t, rows - HEAD_DIM // 2, 0), pltpu.roll(t, HEAD_DIM // 2, 0))
    return t * cos + partner * sin_signed


def _topk_member(s, valid, idx, axis):
    n = s.shape[axis]
    s = jnp.where(valid, s, NEG)
    rank = jnp.zeros(s.shape, jnp.int32)
    for j in range(n):
        sj = lax.slice_in_dim(s, j, j + 1, axis=axis)
        beats = (sj > s) | ((sj == s) & (j < idx))
        rank = rank + beats.astype(jnp.int32)
    return (rank < MOBA_TOPK) & valid


def _proj_prompt_kernel(x_ref, g_ref, w_ref, cos_ref, sin_ref, cost_ref, sint_ref, wp_ref, sp_ref,
                        qt_ref, kt_ref, vt_ref, kb_ref, vtb_ref, km_ref, po_ref, last_ref, ext_ref):
    i = pl.program_id(1)
    tm = x_ref.shape[1]

    @pl.when(i == 0)
    def _():
        ext_ref[0:POOL_HIST, :] = jnp.zeros((POOL_HIST, POOL_WIDTH), F32)

    h = _rms(x_ref[0], g_ref[...]).astype(BF16)

    def project(first):
        return jnp.dot(h, w_ref[:, first:first + ATTN_WIDTH], preferred_element_type=F32)

    lane_reps = ATTN_WIDTH // cos_ref.shape[1]
    u = project(0)
    k = _rope(project(POOL_WIDTH + ATTN_WIDTH), jnp.tile(cos_ref[...], (1, lane_reps)),
              jnp.tile(sin_ref[...], (1, lane_reps)))
    q = project(POOL_WIDTH)
    v = project(POOL_WIDTH + 2 * ATTN_WIDTH)
    qt = _rope_t(q.T, jnp.tile(cost_ref[...], (N_HEADS, 1)), jnp.tile(sint_ref[...], (N_HEADS, 1)))
    qt_ref[0] = (qt * (HEAD_DIM ** -0.5 * LOG2_E)).astype(BF16)
    vt = v.T
    kt_ref[0] = k.T
    vt_ref[0] = vt
    kb_ref[0] = k.astype(BF16)
    vtb_ref[0, 0] = vt.astype(BF16)
    km_ref[0, 0] = jnp.mean(k, axis=0, keepdims=True)

    pos = i * tm + lax.broadcasted_iota(jnp.int32, (tm, 1), 0)
    for g, w in enumerate(POOL_WINDOWS):
        cols = slice(g * POOL_GC, (g + 1) * POOL_GC)
        ug = u[:, cols]
        wsum = jnp.concatenate([ext_ref[:, cols], ug], axis=0)
        shift = 1
        while shift < w:
            wsum = wsum + pltpu.roll(wsum, shift, 0)
            shift *= 2
        cnt = jnp.minimum(w, pos + 1).astype(F32)
        d = wsum[POOL_HIST:, :] / cnt - ug
        mixed = jnp.dot(d.astype(BF16), wp_ref[g], preferred_element_type=F32)
        po_ref[0, :, cols] = (mixed * sp_ref[:, cols]).astype(BF16)
    tail = u[tm - POOL_HIST:, :]
    ext_ref[...] = tail
    last_ref[0] = tail


def _proj_prompt(x, g, w_in, cos, sin, cos_t, sin_t, w_pool, s_pool):
    b, s, d = x.shape
    tm = MOBA_BLOCK
    nq = s // tm
    row = lambda bi, i: (bi, i, 0)
    col = lambda bi, i: (bi, 0, i)
    blk = lambda bi, i: (bi, i, 0, 0)
    return pl.pallas_call(
        _proj_prompt_kernel,
        grid=(b, nq),
        in_specs=[
            pl.BlockSpec((1, tm, d), row),
            _resident((1, d)),
            _resident(w_in.shape),
            pl.BlockSpec((tm, cos.shape[1]), lambda bi, i: (i, 0)),
            pl.BlockSpec((tm, cos.shape[1]), lambda bi, i: (i, 0)),
            pl.BlockSpec((HEAD_DIM, tm), lambda bi, i: (0, i)),
            pl.BlockSpec((HEAD_DIM, tm), lambda bi, i: (0, i)),
            _resident(w_pool.shape),
            _resident((1, POOL_WIDTH)),
        ],
        out_specs=[
            pl.BlockSpec((1, ATTN_WIDTH, tm), col),
            pl.BlockSpec((1, ATTN_WIDTH, tm), col),
            pl.BlockSpec((1, ATTN_WIDTH, tm), col),
            pl.BlockSpec((1, tm, ATTN_WIDTH), row),
            pl.BlockSpec((1, 1, ATTN_WIDTH, tm), blk),
            pl.BlockSpec((1, 1, 1, ATTN_WIDTH), blk),
            pl.BlockSpec((1, tm, POOL_WIDTH), row),
            pl.BlockSpec((1, POOL_HIST, POOL_WIDTH), lambda bi, i: (bi, 0, 0)),
        ],
        out_shape=[
            jax.ShapeDtypeStruct((b, ATTN_WIDTH, s), BF16),
            jax.ShapeDtypeStruct((b, ATTN_WIDTH, s), F32),
            jax.ShapeDtypeStruct((b, ATTN_WIDTH, s), F32),
            jax.ShapeDtypeStruct((b, s, ATTN_WIDTH), BF16),
            jax.ShapeDtypeStruct((b, nq, ATTN_WIDTH, tm), BF16),
            jax.ShapeDtypeStruct((b, nq, 1, ATTN_WIDTH), F32),
            jax.ShapeDtypeStruct((b, s, POOL_WIDTH), BF16),
            jax.ShapeDtypeStruct((b, POOL_HIST, POOL_WIDTH), F32),
        ],
        scratch_shapes=[pltpu.VMEM((POOL_HIST, POOL_WIDTH), F32)],
        compiler_params=_params("arbitrary", "arbitrary"),
        name="proj_prompt",
    )(x, g, w_in, cos, sin, cos_t, sin_t, w_pool, s_pool)


SUBLANES = 8
LOGIT_AHEAD = 2
LOGIT_SLOTS = 4
LOG2_E = 1.4426950408889634


def _all_sublanes(t, combine):
    for s in (4, 2, 1):
        t = combine(t, pltpu.roll(t, s, 0))
    return t


def _moba_kernel(pt_ref, qt_ref, kb_ref, vtb_ref, km_ref, qs_ref, kn_ref, vn_ref, ck_ref, cv_ref,
                 o_ref, os_ref, acc_ref, m_ref, l_ref, take_ref, lg_ref, kbuf_ref, vbuf_ref, page_sem,
                 *, n_pages, per_step):
    i = pl.program_id(1)
    step = pl.program_id(0) * pl.num_programs(1) + i
    n_steps = pl.num_programs(0) * pl.num_programs(1)
    slot = step % 2
    tq = qt_ref.shape[2]
    nb = km_ref.shape[1]

    def page_copies(page_id, half):
        copies = []
        for s in range(per_step):
            for p in range(n_pages):
                pid = page_id(s, p)
                dst = s * n_pages + p
                copies.append(pltpu.make_async_copy(ck_ref.at[pid], kbuf_ref.at[half, dst], page_sem.at[half, 0]))
                copies.append(pltpu.make_async_copy(cv_ref.at[pid], vbuf_ref.at[half, dst], page_sem.at[half, 1]))
        return copies

    def start_step(t, half):
        for c in page_copies(lambda s, p: pt_ref[t * per_step + s, p], half):
            c.start()

    @pl.when(step == 0)
    def _():
        start_step(0, 0)

    @pl.when(step + 1 < n_steps)
    def _():
        start_step(step + 1, 1 - slot)


    km_rows = jnp.concatenate([km_ref[0]] * N_HEADS, axis=0)
    row_head = lax.broadcasted_iota(jnp.int32, km_rows.shape, 0) // nb
    lane_head = lax.broadcasted_iota(jnp.int32, km_rows.shape, 1) // HEAD_DIM
    km_heads = jnp.where(row_head == lane_head, km_rows, 0.0).astype(BF16)
    s = jnp.dot(km_heads, qt_ref[0], preferred_element_type=F32).reshape(N_HEADS, nb, tq)
    blk = lax.broadcasted_iota(jnp.int32, s.shape, 1)
    sel = _topk_member(s, blk < i, blk, axis=1).astype(F32)

    for h in range(N_HEADS):
        for jb in range(nb):
            take_ref[h, jb] = jnp.broadcast_to(sel[h, jb:jb + 1, :], (SUBLANES, tq))

    def head_rows(h):
        return slice(h * HEAD_DIM, (h + 1) * HEAD_DIM)

    def stage_logits(h, j):
        off = pl.multiple_of(j * MOBA_BLOCK, MOBA_BLOCK)
        lg_ref[h % LOGIT_SLOTS] = jnp.dot(kb_ref[0, pl.ds(off, MOBA_BLOCK), head_rows(h)],
                                          qt_ref[0, head_rows(h), :], preferred_element_type=F32)

    def weighted_values(h, j, p3):
        p = p3.reshape(MOBA_BLOCK, tq).astype(BF16)
        pv = jnp.dot(vtb_ref[0, j, head_rows(h), :], p, preferred_element_type=F32)
        return pv.reshape(HEAD_DIM // SUBLANES, SUBLANES, tq)

    def tiles(t):
        return t.reshape(t.shape[0] // SUBLANES, SUBLANES, tq)

    def block_pass(j, j_next, own, between=None):
        for h in range(N_HEADS):
            ahead = h + LOGIT_AHEAD
            if ahead < N_HEADS:
                stage_logits(ahead, j)
            else:
                stage_logits(ahead - N_HEADS, j_next)
            if between is not None:
                between(h)
            lg = lg_ref[h % LOGIT_SLOTS]
            if own:
                kk = lax.broadcasted_iota(jnp.int32, lg.shape, 0)
                qq = lax.broadcasted_iota(jnp.int32, lg.shape, 1)
                lg3 = tiles(jnp.where(kk <= qq, lg, NEG))
                m = _all_sublanes(jnp.max(lg3, axis=0), jnp.maximum)
                p3 = jnp.exp2(lg3 - m[None])
                m_ref[h] = m
                l_ref[h] = _all_sublanes(jnp.sum(p3, axis=0), jnp.add)
                acc_ref[head_rows(h), :] = weighted_values(h, j, p3).reshape(HEAD_DIM, tq)
            else:
                lg3 = tiles(lg)
                take = take_ref[h, j] > 0.0
                m_old = m_ref[h]
                l_old = l_ref[h]
                acc_old = tiles(acc_ref[head_rows(h), :])
                m = jnp.maximum(m_old, _all_sublanes(jnp.max(lg3, axis=0), jnp.maximum))
                a = jnp.exp2(m_old - m)
                p3 = jnp.exp2(lg3 - m[None])
                l = a * l_old + _all_sublanes(jnp.sum(p3, axis=0), jnp.add)
                acc3 = a[None] * acc_old + weighted_values(h, j, p3)
                m_ref[h] = jnp.where(take, m, m_old)
                l_ref[h] = jnp.where(take, l, l_old)
                acc_ref[head_rows(h), :] = jnp.where(take[None], acc3, acc_old).reshape(HEAD_DIM, tq)

    last_past = jnp.maximum(i - 1, 0)
    for h in range(LOGIT_AHEAD):
        stage_logits(h, i)

    for c in page_copies(lambda s, p: 0, slot):
        c.wait()
    sample_q = [_sample_queries(qs_ref[s]) for s in range(per_step)]
    jobs = [(s, p) for s in range(per_step) for p in range(n_pages)]
    per_head = -(-len(jobs) // N_HEADS)
    page_logits = {}

    def sample_page_logits(h):
        for s, p in jobs[h * per_head:(h + 1) * per_head]:
            page_logits[s, p] = _sample_page_logits(sample_q[s], kbuf_ref.at[slot, s * n_pages + p])

    block_pass(i, 0, own=True, between=sample_page_logits)
    for s in range(per_step):
        v_pages = [vbuf_ref.at[slot, s * n_pages + p] for p in range(n_pages)]
        lg = jnp.concatenate([page_logits[s, p] for p in range(n_pages)], axis=-1)
        os_ref[s] = _sample_finish(sample_q[s], lg, kn_ref[s], vn_ref[s], v_pages).astype(BF16)

    def two_past_blocks(jj, carry):
        j = 2 * jj
        block_pass(j, j + 1, own=False)
        block_pass(j + 1, jnp.minimum(j + 2, last_past), own=False)
        return carry

    lax.fori_loop(0, i // 2, two_past_blocks, 0)

    @pl.when(i % 2 == 1)
    def _():
        block_pass(last_past, last_past, own=False)

    for h in range(N_HEADS):
        rows = slice(h * HEAD_DIM, (h + 1) * HEAD_DIM)
        acc_ref[rows, :] = (tiles(acc_ref[rows, :]) / l_ref[h][None]).reshape(HEAD_DIM, tq)
    o_ref[0] = acc_ref[...].T.astype(BF16)


def _moba(qtb, kb, vtb, kmean, page_table, q_s, k_s, v_s, cache_kt, cache_vt):
    b, w, s = qtb.shape
    nq = s // MOBA_BLOCK
    bs, n_pages = page_table.shape
    page = cache_kt.shape[2]
    steps = b * nq
    assert bs % steps == 0
    per_step = bs // steps
    step_seq = lambda bi, i, pt: (bi * nq + i, 0, 0)
    tok = pl.BlockSpec((per_step, 1, w), step_seq)
    in_hbm = pl.BlockSpec(memory_space=pl.ANY)
    page_buf = pltpu.VMEM((2, per_step * n_pages, w, page), cache_kt.dtype)
    grid_spec = pltpu.PrefetchScalarGridSpec(
        num_scalar_prefetch=1,
        grid=(b, nq),
        in_specs=[
            pl.BlockSpec((1, w, MOBA_BLOCK), lambda bi, i, pt: (bi, 0, i)),
            pl.BlockSpec((1, s, w), lambda bi, i, pt: (bi, 0, 0)),
            pl.BlockSpec((1, nq, w, MOBA_BLOCK), lambda bi, i, pt: (bi, 0, 0, 0)),
            pl.BlockSpec((1, nq, w), lambda bi, i, pt: (bi, 0, 0)),
            tok, tok, tok, in_hbm, in_hbm,
        ],
        out_specs=[pl.BlockSpec((1, MOBA_BLOCK, w), lambda bi, i, pt: (bi, i, 0)), tok],
        scratch_shapes=[pltpu.VMEM((w, MOBA_BLOCK), F32),
                        pltpu.VMEM((N_HEADS, SUBLANES, MOBA_BLOCK), F32),
                        pltpu.VMEM((N_HEADS, SUBLANES, MOBA_BLOCK), F32),
                        pltpu.VMEM((N_HEADS, nq, SUBLANES, MOBA_BLOCK), F32),
                        pltpu.VMEM((LOGIT_SLOTS, MOBA_BLOCK, MOBA_BLOCK), F32),
                        page_buf, page_buf, pltpu.SemaphoreType.DMA((2, 2))],
    )
    return pl.pallas_call(
        functools.partial(_moba_kernel, n_pages=n_pages, per_step=per_step),
        grid_spec=grid_spec,
        out_shape=[jax.ShapeDtypeStruct((b, s, w), BF16), jax.ShapeDtypeStruct((bs, 1, w), BF16)],
        compiler_params=_params("arbitrary", "arbitrary"),
        name="moba",
    )(page_table, qtb, kb, vtb, kmean, q_s, k_s, v_s, cache_kt, cache_vt)


def _mem_kv_kernel(m_ref, g_ref, wk_ref, wv_ref, mk_ref, mv_ref, mkb_ref, mvb_ref):
    h = _rms(m_ref[...], g_ref[...]).astype(BF16)
    mk = jnp.dot(h, wk_ref[...], preferred_element_type=F32)
    mv = jnp.dot(h, wv_ref[...], preferred_element_type=F32)
    mk_ref[...] = mk
    mv_ref[...] = mv
    mkb_ref[...] = mk.astype(BF16)
    mvb_ref[...] = mv.astype(BF16)


def _mem_kv(mem, g, w_ck, w_cv):
    rows, d = mem.shape
    tm = 512
    row = lambda i: (i, 0)
    return pl.pallas_call(
        _mem_kv_kernel,
        grid=(rows // tm,),
        in_specs=[pl.BlockSpec((tm, d), row), _resident((1, d)), _resident(w_ck.shape), _resident(w_cv.shape)],
        out_specs=[pl.BlockSpec((tm, d), row)] * 4,
        out_shape=[jax.ShapeDtypeStruct((rows, d), F32)] * 2 + [jax.ShapeDtypeStruct((rows, d), BF16)] * 2,
        compiler_params=_params("arbitrary"),
        name="mem_kv",
    )(mem, g, w_ck, w_cv)


MIX_CHAINS = 2


def _mix_out_kernel(x_ref, po_ref, at_ref, wo_ref, g_ref, wq_ref, *rest):
    with_memory = len(rest) == 4
    if with_memory:
        mk_ref, mv_ref, x1_ref, out_ref = rest
    else:
        x1_ref, out_ref = rest
    tm = x_ref.shape[0]
    chains = MIX_CHAINS if with_memory else 1
    rows_of = lambda c: slice(c * (tm // chains), (c + 1) * (tm // chains))

    def residual(c):
        rows = rows_of(c)
        cat = jnp.concatenate([po_ref[rows, :], at_ref[rows, :]], axis=-1)
        x1 = x_ref[rows, :] + jnp.dot(cat, wo_ref[...], preferred_element_type=F32)
        x1_ref[rows, :] = x1
        return _rms(x1, g_ref[...]).astype(BF16)

    def queries(hq):
        return (jnp.dot(hq, wq_ref[...], preferred_element_type=F32) * (MEM_HEAD_DIM ** -0.5)).astype(BF16)

    def attend(c, q):
        for h in range(MEM_HEADS):
            sl = slice(h * MEM_HEAD_DIM, (h + 1) * MEM_HEAD_DIM)
            lg = lax.dot_general(q[:, sl], mk_ref[0, :, sl], NT_DIMS, preferred_element_type=F32)
            p = jnp.exp(lg - jnp.max(lg, axis=-1, keepdims=True))
            l = jnp.sum(p, axis=-1, keepdims=True)
            o = jnp.dot(p.astype(BF16), mv_ref[0, :, sl], preferred_element_type=F32)
            out_ref[rows_of(c), sl] = (o / l).astype(BF16)

    if not with_memory:
        out_ref[...] = queries(residual(0))
        return
    q = queries(residual(0))
    for c in range(chains):
        hq_next = residual(c + 1) if c + 1 < chains else None
        attend(c, q)
        if hq_next is not None:
            q = queries(hq_next)


def _mix_out(x, po, at, w_out, g_cross, w_cq, memory=None):
    rows, d = x.shape
    tm = min(512, rows)
    row = lambda i: (i, 0)
    in_specs = [pl.BlockSpec((tm, d), row), pl.BlockSpec((tm, POOL_WIDTH), row), pl.BlockSpec((tm, ATTN_WIDTH), row),
                _resident(w_out.shape), _resident((1, d)), _resident(w_cq.shape)]
    args = [x, po, at, w_out, g_cross, w_cq]
    if memory is not None:
        mk, mv = memory
        tiles_per_batch = rows // mk.shape[0] // tm
        assert tiles_per_batch * tm * mk.shape[0] == rows and tm % MIX_CHAINS == 0
        of_batch = pl.BlockSpec((1,) + mk.shape[1:], lambda i: (i // tiles_per_batch, 0, 0))
        in_specs += [of_batch, of_batch]
        args += [mk, mv]
    return pl.pallas_call(
        _mix_out_kernel,
        grid=(rows // tm,),
        in_specs=in_specs,
        out_specs=[pl.BlockSpec((tm, d), row), pl.BlockSpec((tm, d), row)],
        out_shape=[jax.ShapeDtypeStruct((rows, d), F32), jax.ShapeDtypeStruct((rows, d), BF16)],
        compiler_params=_params("arbitrary"),
        name="mix_out",
    )(*args)


MEM_LANES = 128
MEM_PARTS = MEM_HEAD_DIM // MEM_LANES
MEM_ROWS = MEM_PARTS * MEM_HEADS


def _cross_sample_products(q, mk_ref):
    return lax.dot_general(q, mk_ref[...].astype(BF16), NT_DIMS, preferred_element_type=F32)


def _cross_sample_attend(r, mv_ref):
    n = r.shape[1]
    row = lax.broadcasted_iota(jnp.int32, (MEM_ROWS, n), 0)
    lane_row = lax.broadcasted_iota(jnp.int32, (MEM_ROWS, n), 1) % MEM_ROWS
    own = lane_row == row
    valid = (row < MEM_HEADS) & own

    def other_part(t, lane_shift):
        return pltpu.roll(pltpu.roll(t, MEM_HEADS, 0), lane_shift, 1)

    r = jnp.where(own, r, 0.0)
    lg = jnp.where(valid, r + other_part(r, n - MEM_HEADS), NEG)
    p = jnp.where(valid, jnp.exp(lg - jnp.max(lg, axis=-1, keepdims=True)), 0.0)
    p = p + other_part(p, MEM_HEADS)
    l = jnp.sum(p, axis=-1, keepdims=True)
    return jnp.dot(p.astype(BF16), mv_ref[...].astype(BF16), preferred_element_type=F32) / l


def _stored_rows(qc):
    rows = qc.shape[0]
    return jnp.transpose(qc.reshape(rows, MEM_HEADS, MEM_PARTS, MEM_LANES), (0, 2, 1, 3)).reshape(rows, MEM_ROWS, MEM_LANES)


def _stored_slots(m):
    b, slots = m.shape[:2]
    m = m.reshape(b, slots, MEM_HEADS, MEM_PARTS, MEM_LANES)
    return jnp.transpose(m, (0, 1, 3, 2, 4)).reshape(b, slots * MEM_ROWS, MEM_LANES)


def _unstored_rows(o):
    rows = o.shape[0]
    return jnp.transpose(o.reshape(rows, MEM_PARTS, MEM_HEADS, MEM_LANES), (0, 2, 1, 3)).reshape(rows, -1)


def _mlp_kernel(x1_ref, o_ref, wco_ref, gm_ref, wu_ref, wd_ref, gf_ref, *rest):
    if len(rest) == 1:
        (y_ref,) = rest
        n_seq = 0
    else:
        qs_ref, mk_ref, mv_ref, y_ref, os_ref = rest
        n_seq = qs_ref.shape[0]
    products = [_cross_sample_products(qs_ref[s], mk_ref.at[s]) for s in range(n_seq)]
    x2 = x1_ref[...] + jnp.dot(o_ref[...], wco_ref[...], preferred_element_type=F32)
    hm = _rms(x2, gm_ref[...]).astype(BF16)
    acc = x2
    for c in range(D_FF // FF_CHUNK):
        if c == 1:
            for s in range(n_seq):
                os_ref[s] = _cross_sample_attend(products[s], mv_ref.at[s]).astype(BF16)
        cs = slice(c * FF_CHUNK, (c + 1) * FF_CHUNK)
        a = jnp.dot(hm, wu_ref[:, cs], preferred_element_type=F32)
        a = jnp.square(jnp.maximum(a, 0.0)).astype(BF16)
        acc = acc + jnp.dot(a, wd_ref[cs, :], preferred_element_type=F32)
    y_ref[...] = _rms(acc, gf_ref[...])


def _mlp(x1, o, w_co, g_mlp, w_up, w_down, g_final, sample_cross=None):
    rows, d = x1.shape
    tm = min(512, rows)
    steps = rows // tm
    row = lambda i: (i, 0)
    in_specs = [pl.BlockSpec((tm, d), row), pl.BlockSpec((tm, d), row), _resident(w_co.shape),
                _resident((1, d)), _resident(w_up.shape), _resident(w_down.shape), _resident((1, d))]
    args = [x1, o, w_co, g_mlp, w_up, w_down, g_final]
    out_specs = [pl.BlockSpec((tm, d), row)]
    out_shape = [jax.ShapeDtypeStruct((rows, d), F32)]
    if sample_cross is not None:
        qc, mem_k, mem_v = sample_cross
        b, slots = mem_k.shape[:2]
        assert b % steps == 0
        seq = lambda n: pl.BlockSpec((b // steps, n, MEM_LANES), lambda i: (i, 0, 0))
        in_specs += [seq(MEM_ROWS), seq(slots * MEM_ROWS), seq(slots * MEM_ROWS)]
        args += [_stored_rows(qc), _stored_slots(mem_k), _stored_slots(mem_v)]
        out_specs.append(seq(MEM_ROWS))
        out_shape.append(jax.ShapeDtypeStruct((b, MEM_ROWS, MEM_LANES), BF16))
    outs = pl.pallas_call(
        _mlp_kernel,
        grid=(steps,),
        in_specs=in_specs,
        out_specs=out_specs,
        out_shape=out_shape,
        compiler_params=_params("arbitrary"),
        name="mlp",
    )(*args)
    if sample_cross is None:
        return outs[0]
    return outs[0], _unstored_rows(outs[1])


def _proj_sample_kernel(x_ref, g_ref, w_ref, cos_ref, sin_ref, st_ref, wp_ref, sp_ref,
                        q_ref, k_ref, v_ref, kt_ref, vt_ref, u_ref, po_ref, *, n_prev):
    h = _rms(x_ref[...], g_ref[...]).astype(BF16)
    proj = jnp.dot(h, w_ref[...], preferred_element_type=F32)
    u = proj[:, :POOL_WIDTH]
    cos = cos_ref[...]
    sin = sin_ref[...]
    q_ref[...] = _rope(proj[:, POOL_WIDTH:POOL_WIDTH + ATTN_WIDTH], cos, sin) * (HEAD_DIM ** -0.5)
    k = _rope(proj[:, POOL_WIDTH + ATTN_WIDTH:POOL_WIDTH + 2 * ATTN_WIDTH], cos, sin)
    v = proj[:, POOL_WIDTH + 2 * ATTN_WIDTH:]
    k_ref[...] = k
    v_ref[...] = v
    kt_ref[...] = k.T
    vt_ref[...] = v.T
    u_ref[...] = u
    for g, w in enumerate(POOL_WINDOWS):
        cols = slice(g * POOL_GC, (g + 1) * POOL_GC)
        ug = u[:, cols]
        wsum = ug
        for j in range(1, w):
            wsum = wsum + st_ref[POOL_STATE - j, :, cols]
        d = wsum / float(min(w, n_prev + 1)) - ug
        mixed = jnp.dot(d.astype(BF16), wp_ref[g], preferred_element_type=F32)
        po_ref[:, cols] = (mixed * sp_ref[:, cols]).astype(BF16)


def _proj_sample(x, g, w_in, cos, sin, state_t, w_pool, s_pool, n_prev):
    rows, d = x.shape
    wide = jax.ShapeDtypeStruct((rows, ATTN_WIDTH), F32)
    tall = jax.ShapeDtypeStruct((ATTN_WIDTH, rows), F32)
    full = lambda a: pl.BlockSpec(a.shape, lambda i, nd=len(a.shape): (0,) * nd)
    args = (x, g, w_in, cos, sin, state_t, w_pool, s_pool)
    outs = [wide, wide, wide, tall, tall, wide, jax.ShapeDtypeStruct((rows, POOL_WIDTH), BF16)]
    return pl.pallas_call(
        functools.partial(_proj_sample_kernel, n_prev=n_prev),
        grid=(1,),
        in_specs=[full(a) for a in args],
        out_specs=[full(o) for o in outs],
        out_shape=outs,
        compiler_params=_params("arbitrary"),
        name="proj_sample",
    )(*args)


def _own_lanes():
    head_of_lane = lax.broadcasted_iota(jnp.int32, (N_HEADS, ATTN_WIDTH), 1) // HEAD_DIM
    head = lax.broadcasted_iota(jnp.int32, (N_HEADS, ATTN_WIDTH), 0)
    return head_of_lane == head


def _sample_queries(q):
    return jnp.where(_own_lanes(), q, 0.0)


def _sample_page_logits(q_heads, k_page):
    return jnp.dot(q_heads.astype(BF16), k_page[...].astype(BF16), preferred_element_type=F32)


def _sample_finish(q_heads, lg, k_new, v_new, v_pages):
    page = v_pages[0].shape[1]
    past = lg.shape[1]
    nb = past // MOBA_BLOCK
    mine = _own_lanes()

    s = jnp.concatenate(
        [jnp.sum(lg[:, j * MOBA_BLOCK:(j + 1) * MOBA_BLOCK], axis=-1, keepdims=True) for j in range(nb)], axis=-1)
    blk_id = lax.broadcasted_iota(jnp.int32, (N_HEADS, nb), 1)
    sel = _topk_member(s, blk_id >= 0, blk_id, axis=1)
    key_blk = lax.broadcasted_iota(jnp.int32, (N_HEADS, past), 1) // MOBA_BLOCK
    chosen = jnp.zeros((N_HEADS, past), jnp.bool_)
    for j in range(nb):
        chosen = chosen | ((key_blk == j) & sel[:, j:j + 1])
    lg = jnp.where(chosen, lg, NEG)

    lg_new = jnp.sum(q_heads * k_new, axis=-1, keepdims=True)
    m = jnp.maximum(jnp.max(lg, axis=-1, keepdims=True), lg_new)
    p = jnp.exp(lg - m)
    p_new = jnp.exp(lg_new - m)
    l = jnp.sum(p, axis=-1, keepdims=True) + p_new

    lanes = page
    sub = lax.broadcasted_iota(jnp.int32, (HEAD_DIM, lanes), 0)
    lane = lax.broadcasted_iota(jnp.int32, (HEAD_DIM, lanes), 1)
    tiles = []
    for first in range(0, N_HEADS, lanes // HEAD_DIM):
        row = jnp.zeros((1, lanes), F32)
        for h in range(first, first + lanes // HEAD_DIM):
            acc = jnp.zeros((HEAD_DIM, lanes), F32)
            for n, vp in enumerate(v_pages):
                w = jnp.broadcast_to(p[h:h + 1, n * page:(n + 1) * page], (HEAD_DIM, page))
                acc = acc + vp[h * HEAD_DIM:(h + 1) * HEAD_DIM, :] * w
            total = jnp.sum(acc, axis=1, keepdims=True)
            on_diag = lane == sub + (h - first) * HEAD_DIM
            row = row + jnp.sum(jnp.where(on_diag, total, 0.0), axis=0, keepdims=True)
        tiles.append(row)
    pv = jnp.concatenate(tiles, axis=-1)
    per_lane = lambda t: jnp.sum(jnp.where(mine, t, 0.0), axis=0, keepdims=True)
    return (pv + per_lane(p_new) * v_new) / per_lane(l)


def _rope_tables(pos, heads):
    half = HEAD_DIM // 2
    inv_freq = ROPE_THETA ** (-jnp.arange(half, dtype=F32) / half)
    ang = pos.astype(F32)[:, None] * inv_freq[None, :]
    cos = jnp.cos(ang)
    sin = jnp.sin(ang)
    cos_full = jnp.tile(jnp.concatenate([cos, cos], axis=-1), (1, heads))
    sin_signed = jnp.tile(jnp.concatenate([-sin, sin], axis=-1), (1, heads))
    return cos_full, sin_signed


def kernel(x_prompt, x_sample, cache_k, cache_v, cache_mem_k, cache_mem_v, state_pool, page_table, mem_prompt,
           g_mix, w_in, w_pool, s_pool, w_out, g_cross, g_mem, w_cq, w_ck, w_cv, w_co, g_mlp, w_up, w_down,
           g_final):
    depth = w_in.shape[0]
    assert depth == 1, "one decoder layer"
    bp, seq, d = x_prompt.shape
    bs, dec_seq, _ = x_sample.shape
    assert dec_seq == 1 and seq % MOBA_BLOCK == 0
    n_phys, page = cache_k.shape[1], cache_k.shape[2]
    n_pages = page_table.shape[1]
    past_len = n_pages * page
    assert past_len % MOBA_BLOCK == 0 and past_len >= POOL_STATE
    mem_len = mem_prompt.shape[1]

    l = 0
    row = lambda a: a.reshape(1, -1)
    lp = dict(w_out=w_out[l].astype(BF16), g_cross=row(g_cross[l]), w_cq=w_cq[l].astype(BF16),
              w_co=w_co[l].astype(BF16), g_mlp=row(g_mlp[l]), w_up=w_up[l].astype(BF16),
              w_down=w_down[l].astype(BF16), g_final=row(g_final))
    w_in_b = w_in[l].astype(BF16)
    w_pool_b = w_pool[l].astype(BF16)
    g_mix_r = row(g_mix[l])
    s_pool_r = row(s_pool[l])

    lane_heads = 128 // HEAD_DIM
    cos_p, sin_p = _rope_tables(jnp.arange(seq), lane_heads)
    qtb, kt_p, vt_p, kb, vtb, kmean, po_p, last_p = _proj_prompt(
        x_prompt, g_mix_r, w_in_b, cos_p, sin_p, cos_p[:, :HEAD_DIM].T, sin_p[:, :HEAD_DIM].T, w_pool_b, s_pool_r)
    cos_s, sin_s = _rope_tables(jnp.full((1,), past_len), N_HEADS)
    state_t = jnp.transpose(state_pool[l], (1, 0, 2))
    q_s, k_s, v_s, kt_s, vt_s, u_s, po_s = _proj_sample(x_sample.reshape(bs, d), g_mix_r, w_in_b, cos_s, sin_s,
                                                        state_t, w_pool_b, s_pool_r, past_len)

    tok = lambda a: a.reshape(bs, 1, -1)
    pages_t = lambda c: jnp.transpose(c, (0, 2, 3, 1)).reshape(n_phys, ATTN_WIDTH, page)
    at_p, at_s = _moba(qtb, kb, vtb, kmean.reshape(bp, seq // MOBA_BLOCK, ATTN_WIDTH), page_table, tok(q_s),
                       tok(k_s), tok(v_s), pages_t(cache_k[l]), pages_t(cache_v[l]))

    mk_p, mv_p, mkb, mvb = _mem_kv(mem_prompt.reshape(bp * mem_len, d), row(g_mem[l]), w_ck[l].astype(BF16),
                                   w_cv[l].astype(BF16))
    flat = lambda a: a.reshape(-1, a.shape[-1])
    mix = lambda x, po, at, memory=None: _mix_out(flat(x), flat(po), flat(at), lp["w_out"], lp["g_cross"],
                                                  lp["w_cq"], memory)
    tail = (lp["w_co"], lp["g_mlp"], lp["w_up"], lp["w_down"], lp["g_final"])
    x1_p, o_p = mix(x_prompt, po_p, at_p, (mkb.reshape(bp, mem_len, d), mvb.reshape(bp, mem_len, d)))
    x1_s, qc_s = mix(x_sample, po_s, at_s)
    y_prompt, o_s = _mlp(x1_p, o_p, *tail, sample_cross=(qc_s, cache_mem_k[l], cache_mem_v[l]))
    y_prompt = y_prompt.reshape(bp, seq, d)
    y_sample = _mlp(x1_s, o_s, *tail).reshape(bs, 1, d)

    def heads(t, b_):
        return jnp.transpose(t.reshape(b_, N_HEADS, HEAD_DIM, -1), (0, 3, 1, 2))[None]

    mem_heads = lambda a: a.reshape(1, bp, mem_len, MEM_HEADS, MEM_HEAD_DIM)
    pool_sample = jnp.transpose(jnp.concatenate([state_t[1:], u_s[None]], axis=0), (1, 0, 2))[None]
    return (y_prompt, y_sample, heads(kt_p, bp), heads(vt_p, bp), last_p[:, 1:][None], mem_heads(mk_p),
            mem_heads(mv_p), heads(kt_s.T.reshape(bs, ATTN_WIDTH, 1), bs), heads(vt_s.T.reshape(bs, ATTN_WIDTH, 1), bs),
            pool_sample)
```

```python
import functools

import jax
import jax.numpy as jnp
from jax import lax
from jax.experimental import pallas as pl
from jax.experimental.pallas import tpu as pltpu

D_MODEL = 1024
POOL_WIDTH = 512
POOL_WINDOWS = (2, 4, 8, 16)
POOL_GC = POOL_WIDTH // len(POOL_WINDOWS)
POOL_STATE = max(POOL_WINDOWS) - 1
POOL_HIST = POOL_STATE + 1
ATTN_WIDTH = D_MODEL - POOL_WIDTH
HEAD_DIM = 64
N_HEADS = ATTN_WIDTH // HEAD_DIM
MOBA_BLOCK = 256
MOBA_TOPK = 3
ROPE_THETA = 10000.0
MEM_HEADS = 4
MEM_HEAD_DIM = D_MODEL // MEM_HEADS
D_FF = 4 * D_MODEL
FF_CHUNK = 1024
EPS = 1e-6
NEG = -1e30

VMEM_LIMIT_BYTES = 56 * 1024 * 1024

F32 = jnp.float32
BF16 = jnp.bfloat16
NT_DIMS = (((1,), (1,)), ((), ()))


def _params(*semantics):
    return pltpu.CompilerParams(dimension_semantics=semantics, vmem_limit_bytes=VMEM_LIMIT_BYTES)


def _resident(shape):
    zeros = (0,) * len(shape)
    return pl.BlockSpec(shape, lambda *_: zeros, pipeline_mode=pl.Buffered(1))


def _rms(x, g):
    return x * lax.rsqrt(jnp.mean(x * x, axis=-1, keepdims=True) + EPS) * g


def _rope(t, cos, sin_signed):
    width = t.shape[-1]
    lane = lax.broadcasted_iota(jnp.int32, t.shape, t.ndim - 1)
    first_half = (lane % HEAD_DIM) < (HEAD_DIM // 2)
    partner = jnp.where(first_half,
                        pltpu.roll(t, width - HEAD_DIM // 2, t.ndim - 1),
                        pltpu.roll(t, HEAD_DIM // 2, t.ndim - 1))
    return t * cos + partner * sin_signed


def _rope_t(t, cos, sin_signed):
    rows = t.shape[0]
    row = lax.broadcasted_iota(jnp.int32, t.shape, 0)
    first_half = (row % HEAD_DIM) < (HEAD_DIM // 2)
    partner = jnp.where(first_half, pltpu.roll(t, rows - HEAD_DIM // 2, 0), pltpu.roll(t, HEAD_DIM // 2, 0))
    return t * cos + partner * sin_signed


def _topk_member(s, valid, idx, axis):
    n = s.shape[axis]
    s = jnp.where(valid, s, NEG)
    rank = jnp.zeros(s.shape, jnp.int32)
    for j in range(n):
        sj = lax.slice_in_dim(s, j, j + 1, axis=axis)
        beats = (sj > s) | ((sj == s) & (j < idx))
        rank = rank + beats.astype(jnp.int32)
    return (rank < MOBA_TOPK) & valid


def _proj_prompt_kernel(x_ref, g_ref, w_ref, cos_ref, sin_ref, cost_ref, sint_ref, wp_ref, sp_ref,
                        qt_ref, kt_ref, vt_ref, kb_ref, vtb_ref, km_ref, po_ref, last_ref, ext_ref):
    i = pl.program_id(1)
    tm = x_ref.shape[1]

    @pl.when(i == 0)
    def _():
        ext_ref[0:POOL_HIST, :] = jnp.zeros((POOL_HIST, POOL_WIDTH), F32)

    h = _rms(x_ref[0], g_ref[...]).astype(BF16)

    def project(first):
        return jnp.dot(h, w_ref[:, first:first + ATTN_WIDTH], preferred_element_type=F32)

    lane_reps = ATTN_WIDTH // cos_ref.shape[1]
    u = project(0)
    k = _rope(project(POOL_WIDTH + ATTN_WIDTH), jnp.tile(cos_ref[...], (1, lane_reps)),
              jnp.tile(sin_ref[...], (1, lane_reps)))
    q = project(POOL_WIDTH)
    v = project(POOL_WIDTH + 2 * ATTN_WIDTH)
    qt = _rope_t(q.T, jnp.tile(cost_ref[...], (N_HEADS, 1)), jnp.tile(sint_ref[...], (N_HEADS, 1)))
    qt_ref[0] = (qt * (HEAD_DIM ** -0.5 * LOG2_E)).astype(BF16)
    vt = v.T
    kt_ref[0] = k.T
    vt_ref[0] = vt
    kb_ref[0] = k.astype(BF16)
    vtb_ref[0, 0] = vt.astype(BF16)
    km_ref[0, 0] = jnp.mean(k, axis=0, keepdims=True)

    pos = i * tm + lax.broadcasted_iota(jnp.int32, (tm, 1), 0)
    for g, w in enumerate(POOL_WINDOWS):
        cols = slice(g * POOL_GC, (g + 1) * POOL_GC)
        ug = u[:, cols]
        wsum = jnp.concatenate([ext_ref[:, cols], ug], axis=0)
        shift = 1
        while shift < w:
            wsum = wsum + pltpu.roll(wsum, shift, 0)
            shift *= 2
        cnt = jnp.minimum(w, pos + 1).astype(F32)
        d = wsum[POOL_HIST:, :] / cnt - ug
        mixed = jnp.dot(d.astype(BF16), wp_ref[g], preferred_element_type=F32)
        po_ref[0, :, cols] = (mixed * sp_ref[:, cols]).astype(BF16)
    tail = u[tm - POOL_HIST:, :]
    ext_ref[...] = tail
    last_ref[0] = tail


def _proj_prompt(x, g, w_in, cos, sin, cos_t, sin_t, w_pool, s_pool):
    b, s, d = x.shape
    tm = MOBA_BLOCK
    nq = s // tm
    row = lambda bi, i: (bi, i, 0)
    col = lambda bi, i: (bi, 0, i)
    blk = lambda bi, i: (bi, i, 0, 0)
    return pl.pallas_call(
        _proj_prompt_kernel,
        grid=(b, nq),
        in_specs=[
            pl.BlockSpec((1, tm, d), row),
            _resident((1, d)),
            _resident(w_in.shape),
            pl.BlockSpec((tm, cos.shape[1]), lambda bi, i: (i, 0)),
            pl.BlockSpec((tm, cos.shape[1]), lambda bi, i: (i, 0)),
            pl.BlockSpec((HEAD_DIM, tm), lambda bi, i: (0, i)),
            pl.BlockSpec((HEAD_DIM, tm), lambda bi, i: (0, i)),
            _resident(w_pool.shape),
            _resident((1, POOL_WIDTH)),
        ],
        out_specs=[
            pl.BlockSpec((1, ATTN_WIDTH, tm), col),
            pl.BlockSpec((1, ATTN_WIDTH, tm), col),
            pl.BlockSpec((1, ATTN_WIDTH, tm), col),
            pl.BlockSpec((1, tm, ATTN_WIDTH), row),
            pl.BlockSpec((1, 1, ATTN_WIDTH, tm), blk),
            pl.BlockSpec((1, 1, 1, ATTN_WIDTH), blk),
            pl.BlockSpec((1, tm, POOL_WIDTH), row),
            pl.BlockSpec((1, POOL_HIST, POOL_WIDTH), lambda bi, i: (bi, 0, 0)),
        ],
        out_shape=[
            jax.ShapeDtypeStruct((b, ATTN_WIDTH, s), BF16),
            jax.ShapeDtypeStruct((b, ATTN_WIDTH, s), F32),
            jax.ShapeDtypeStruct((b, ATTN_WIDTH, s), F32),
            jax.ShapeDtypeStruct((b, s, ATTN_WIDTH), BF16),
            jax.ShapeDtypeStruct((b, nq, ATTN_WIDTH, tm), BF16),
            jax.ShapeDtypeStruct((b, nq, 1, ATTN_WIDTH), F32),
            jax.ShapeDtypeStruct((b, s, POOL_WIDTH), BF16),
            jax.ShapeDtypeStruct((b, POOL_HIST, POOL_WIDTH), F32),
        ],
        scratch_shapes=[pltpu.VMEM((POOL_HIST, POOL_WIDTH), F32)],
        compiler_params=_params("arbitrary", "arbitrary"),
        name="proj_prompt",
    )(x, g, w_in, cos, sin, cos_t, sin_t, w_pool, s_pool)


SUBLANES = 8
LOGIT_AHEAD = 2
LOGIT_SLOTS = 4
LOG2_E = 1.4426950408889634


def _all_sublanes(t, combine):
    for s in (4, 2, 1):
        t = combine(t, pltpu.roll(t, s, 0))
    return t


def _moba_kernel(pt_ref, qt_ref, kb_ref, vtb_ref, km_ref, qs_ref, qn_ref, kn_ref, vn_ref, ck_ref, cv_ref,
                 o_ref, os_ref, acc_ref, m_ref, l_ref, take_ref, lg_ref, kbuf_ref, vbuf_ref, slg_ref, page_sem,
                 *, n_pages, per_step):
    i = pl.program_id(1)
    step = pl.program_id(0) * pl.num_programs(1) + i
    n_steps = pl.num_programs(0) * pl.num_programs(1)
    slot = step % 2
    tq = qt_ref.shape[2]
    nb = km_ref.shape[1]

    def page_copies(cache_ref, buf_ref, which, page_id, half):
        return [pltpu.make_async_copy(cache_ref.at[page_id(s, p)], buf_ref.at[half, s * n_pages + p],
                                      page_sem.at[half, which])
                for s in range(per_step) for p in range(n_pages)]

    def start_keys(t, half):
        for c in page_copies(ck_ref, kbuf_ref, 0, lambda s, p: pt_ref[t * per_step + s, p], half):
            c.start()

    def start_values(t, half):
        for c in page_copies(cv_ref, vbuf_ref, 1, lambda s, p: pt_ref[t * per_step + s, p], half):
            c.start()

    def wait_half(cache_ref, buf_ref, which, half):
        pltpu.make_async_copy(cache_ref.at[pl.ds(0, per_step * n_pages)], buf_ref.at[half],
                              page_sem.at[half, which]).wait()

    def wait_keys(half):
        wait_half(ck_ref, kbuf_ref, 0, half)

    def wait_values(half):
        wait_half(cv_ref, vbuf_ref, 1, half)

    def key_logits(q_rows, half):
        return [[_sample_page_logits(q_rows[s], kbuf_ref.at[half, s * n_pages + p]) for p in range(n_pages)]
                for s in range(per_step)]

    @pl.when(step == 0)
    def _():
        start_keys(0, 0)
        start_values(0, 0)
        if n_steps > 1:
            start_keys(1, 1)
            start_values(1, 1)
        wait_keys(0)
        first = key_logits([_sample_queries(qs_ref[s]) for s in range(per_step)], 0)
        for s in range(per_step):
            slg_ref[s] = jnp.concatenate(first[s], axis=-1)
        if n_steps > 2:
            start_keys(2, 0)

    @pl.when((step >= 1) & (step + 2 < n_steps))
    def _():
        start_keys(step + 2, slot)

    @pl.when((step >= 1) & (step + 1 < n_steps))
    def _():
        start_values(step + 1, 1 - slot)

    @pl.when(step + 1 < n_steps)
    def _():
        wait_keys(1 - slot)

    wait_values(slot)


    km_rows = jnp.concatenate([km_ref[0]] * N_HEADS, axis=0)
    row_head = lax.broadcasted_iota(jnp.int32, km_rows.shape, 0) // nb
    lane_head = lax.broadcasted_iota(jnp.int32, km_rows.shape, 1) // HEAD_DIM
    km_heads = jnp.where(row_head == lane_head, km_rows, 0.0).astype(BF16)
    s = jnp.dot(km_heads, qt_ref[0], preferred_element_type=F32).reshape(N_HEADS, nb, tq)
    blk = lax.broadcasted_iota(jnp.int32, s.shape, 1)
    sel = _topk_member(s, blk < i, blk, axis=1).astype(F32)

    for h in range(N_HEADS):
        for jb in range(nb):
            take_ref[h, jb] = jnp.broadcast_to(sel[h, jb:jb + 1, :], (SUBLANES, tq))

    def head_rows(h):
        return slice(h * HEAD_DIM, (h + 1) * HEAD_DIM)

    def stage_logits(h, j):
        off = pl.multiple_of(j * MOBA_BLOCK, MOBA_BLOCK)
        lg_ref[h % LOGIT_SLOTS] = jnp.dot(kb_ref[0, pl.ds(off, MOBA_BLOCK), head_rows(h)],
                                          qt_ref[0, head_rows(h), :], preferred_element_type=F32)

    def weighted_values(h, j, p3):
        p = p3.reshape(MOBA_BLOCK, tq).astype(BF16)
        pv = jnp.dot(vtb_ref[0, j, head_rows(h), :], p, preferred_element_type=F32)
        return pv.reshape(HEAD_DIM // SUBLANES, SUBLANES, tq)

    def tiles(t):
        return t.reshape(t.shape[0] // SUBLANES, SUBLANES, tq)

    def block_pass(j, j_next, own, between=None):
        for h in range(N_HEADS):
            ahead = h + LOGIT_AHEAD
            if ahead < N_HEADS:
                stage_logits(ahead, j)
            else:
                stage_logits(ahead - N_HEADS, j_next)
            if between is not None:
                between(h)
            lg = lg_ref[h % LOGIT_SLOTS]
            if own:
                kk = lax.broadcasted_iota(jnp.int32, lg.shape, 0)
                qq = lax.broadcasted_iota(jnp.int32, lg.shape, 1)
                lg3 = tiles(jnp.where(kk <= qq, lg, NEG))
                m = _all_sublanes(jnp.max(lg3, axis=0), jnp.maximum)
                p3 = jnp.exp2(lg3 - m[None])
                m_ref[h] = m
                l_ref[h] = _all_sublanes(jnp.sum(p3, axis=0), jnp.add)
                acc_ref[head_rows(h), :] = weighted_values(h, j, p3).reshape(HEAD_DIM, tq)
            else:
                lg3 = tiles(lg)
                take = take_ref[h, j] > 0.0
                m_old = m_ref[h]
                l_old = l_ref[h]
                acc_old = tiles(acc_ref[head_rows(h), :])
                m = jnp.maximum(m_old, _all_sublanes(jnp.max(lg3, axis=0), jnp.maximum))
                a = jnp.exp2(m_old - m)
                p3 = jnp.exp2(lg3 - m[None])
                l = a * l_old + _all_sublanes(jnp.sum(p3, axis=0), jnp.add)
                acc3 = a[None] * acc_old + weighted_values(h, j, p3)
                m_ref[h] = jnp.where(take, m, m_old)
                l_ref[h] = jnp.where(take, l, l_old)
                acc_ref[head_rows(h), :] = jnp.where(take[None], acc3, acc_old).reshape(HEAD_DIM, tq)

    last_past = jnp.maximum(i - 1, 0)
    for h in range(LOGIT_AHEAD):
        stage_logits(h, i)

    for s in range(per_step):
        v_pages = [vbuf_ref.at[slot, s * n_pages + p] for p in range(n_pages)]
        os_ref[s] = _sample_finish(_sample_queries(qs_ref[s]), slg_ref[s], kn_ref[s], vn_ref[s], v_pages).astype(BF16)

    next_q = [_sample_queries(qn_ref[s]) for s in range(per_step)]
    jobs = [(s, p) for s in range(per_step) for p in range(n_pages)]
    per_head = -(-len(jobs) // N_HEADS)
    page_logits = {}

    def sample_page_logits(h):
        for s, p in jobs[h * per_head:(h + 1) * per_head]:
            page_logits[s, p] = _sample_page_logits(next_q[s], kbuf_ref.at[1 - slot, s * n_pages + p])

    block_pass(i, 0, own=True, between=sample_page_logits)
    for s in range(per_step):
        slg_ref[s] = jnp.concatenate([page_logits[s, p] for p in range(n_pages)], axis=-1)

    def two_past_blocks(jj, carry):
        j = 2 * jj
        block_pass(j, j + 1, own=False)
        block_pass(j + 1, jnp.minimum(j + 2, last_past), own=False)
        return carry

    lax.fori_loop(0, i // 2, two_past_blocks, 0)

    @pl.when(i % 2 == 1)
    def _():
        block_pass(last_past, last_past, own=False)

    for h in range(N_HEADS):
        rows = slice(h * HEAD_DIM, (h + 1) * HEAD_DIM)
        acc_ref[rows, :] = (tiles(acc_ref[rows, :]) / l_ref[h][None]).reshape(HEAD_DIM, tq)
    o_ref[0] = acc_ref[...].T.astype(BF16)


def _moba(qtb, kb, vtb, kmean, page_table, q_s, k_s, v_s, cache_kt, cache_vt):
    b, w, s = qtb.shape
    nq = s // MOBA_BLOCK
    bs, n_pages = page_table.shape
    page = cache_kt.shape[2]
    steps = b * nq
    assert bs % steps == 0
    per_step = bs // steps
    step_seq = lambda bi, i, pt: (bi * nq + i, 0, 0)
    tok = pl.BlockSpec((per_step, 1, w), step_seq)
    tok_next = pl.BlockSpec((per_step, 1, w), lambda bi, i, pt: (jnp.minimum(bi * nq + i + 1, steps - 1), 0, 0))
    in_hbm = pl.BlockSpec(memory_space=pl.ANY)
    page_buf = pltpu.VMEM((2, per_step * n_pages, w, page), cache_kt.dtype)
    grid_spec = pltpu.PrefetchScalarGridSpec(
        num_scalar_prefetch=1,
        grid=(b, nq),
        in_specs=[
            pl.BlockSpec((1, w, MOBA_BLOCK), lambda bi, i, pt: (bi, 0, i)),
            pl.BlockSpec((1, s, w), lambda bi, i, pt: (bi, 0, 0)),
            pl.BlockSpec((1, nq, w, MOBA_BLOCK), lambda bi, i, pt: (bi, 0, 0, 0)),
            pl.BlockSpec((1, nq, w), lambda bi, i, pt: (bi, 0, 0)),
            tok, tok_next, tok, tok, in_hbm, in_hbm,
        ],
        out_specs=[pl.BlockSpec((1, MOBA_BLOCK, w), lambda bi, i, pt: (bi, i, 0)), tok],
        scratch_shapes=[pltpu.VMEM((w, MOBA_BLOCK), F32),
                        pltpu.VMEM((N_HEADS, SUBLANES, MOBA_BLOCK), F32),
                        pltpu.VMEM((N_HEADS, SUBLANES, MOBA_BLOCK), F32),
                        pltpu.VMEM((N_HEADS, nq, SUBLANES, MOBA_BLOCK), F32),
                        pltpu.VMEM((LOGIT_SLOTS, MOBA_BLOCK, MOBA_BLOCK), F32),
                        page_buf, page_buf, pltpu.VMEM((per_step, N_HEADS, n_pages * page), F32),
                        pltpu.SemaphoreType.DMA((2, 2))],
    )
    return pl.pallas_call(
        functools.partial(_moba_kernel, n_pages=n_pages, per_step=per_step),
        grid_spec=grid_spec,
        out_shape=[jax.ShapeDtypeStruct((b, s, w), BF16), jax.ShapeDtypeStruct((bs, 1, w), BF16)],
        compiler_params=_params("arbitrary", "arbitrary"),
        name="moba",
    )(page_table, qtb, kb, vtb, kmean, q_s, q_s, k_s, v_s, cache_kt, cache_vt)


def _mem_kv_kernel(m_ref, g_ref, wk_ref, wv_ref, mk_ref, mv_ref, mkb_ref, mvb_ref):
    h = _rms(m_ref[...], g_ref[...]).astype(BF16)
    mk = jnp.dot(h, wk_ref[...], preferred_element_type=F32)
    mv = jnp.dot(h, wv_ref[...], preferred_element_type=F32)
    mk_ref[...] = mk
    mv_ref[...] = mv
    mkb_ref[...] = mk.astype(BF16)
    mvb_ref[...] = mv.astype(BF16)


def _mem_kv(mem, g, w_ck, w_cv):
    rows, d = mem.shape
    tm = 512
    row = lambda i: (i, 0)
    return pl.pallas_call(
        _mem_kv_kernel,
        grid=(rows // tm,),
        in_specs=[pl.BlockSpec((tm, d), row), _resident((1, d)), _resident(w_ck.shape), _resident(w_cv.shape)],
        out_specs=[pl.BlockSpec((tm, d), row)] * 4,
        out_shape=[jax.ShapeDtypeStruct((rows, d), F32)] * 2 + [jax.ShapeDtypeStruct((rows, d), BF16)] * 2,
        compiler_params=_params("arbitrary"),
        name="mem_kv",
    )(mem, g, w_ck, w_cv)


MIX_CHAINS = 2


def _mix_out_kernel(x_ref, po_ref, at_ref, wo_ref, g_ref, wq_ref, *rest):
    with_memory = len(rest) == 4
    if with_memory:
        mk_ref, mv_ref, x1_ref, out_ref = rest
    else:
        x1_ref, out_ref = rest
    tm = x_ref.shape[0]
    chains = MIX_CHAINS if with_memory else 1
    rows_of = lambda c: slice(c * (tm // chains), (c + 1) * (tm // chains))

    def residual(c):
        rows = rows_of(c)
        cat = jnp.concatenate([po_ref[rows, :], at_ref[rows, :]], axis=-1)
        x1 = x_ref[rows, :] + jnp.dot(cat, wo_ref[...], preferred_element_type=F32)
        x1_ref[rows, :] = x1
        return _rms(x1, g_ref[...]).astype(BF16)

    def queries(hq):
        return (jnp.dot(hq, wq_ref[...], preferred_element_type=F32) * (MEM_HEAD_DIM ** -0.5)).astype(BF16)

    def attend(c, q):
        for h in range(MEM_HEADS):
            sl = slice(h * MEM_HEAD_DIM, (h + 1) * MEM_HEAD_DIM)
            lg = lax.dot_general(q[:, sl], mk_ref[0, :, sl], NT_DIMS, preferred_element_type=F32)
            p = jnp.exp(lg - jnp.max(lg, axis=-1, keepdims=True))
            l = jnp.sum(p, axis=-1, keepdims=True)
            o = jnp.dot(p.astype(BF16), mv_ref[0, :, sl], preferred_element_type=F32)
            out_ref[rows_of(c), sl] = (o / l).astype(BF16)

    if not with_memory:
        out_ref[...] = queries(residual(0))
        return
    q = queries(residual(0))
    for c in range(chains):
        hq_next = residual(c + 1) if c + 1 < chains else None
        attend(c, q)
        if hq_next is not None:
            q = queries(hq_next)


def _mix_out(x, po, at, w_out, g_cross, w_cq, memory=None):
    rows, d = x.shape
    tm = min(512, rows)
    row = lambda i: (i, 0)
    in_specs = [pl.BlockSpec((tm, d), row), pl.BlockSpec((tm, POOL_WIDTH), row), pl.BlockSpec((tm, ATTN_WIDTH), row),
                _resident(w_out.shape), _resident((1, d)), _resident(w_cq.shape)]
    args = [x, po, at, w_out, g_cross, w_cq]
    if memory is not None:
        mk, mv = memory
        tiles_per_batch = rows // mk.shape[0] // tm
        assert tiles_per_batch * tm * mk.shape[0] == rows and tm % MIX_CHAINS == 0
        of_batch = pl.BlockSpec((1,) + mk.shape[1:], lambda i: (i // tiles_per_batch, 0, 0))
        in_specs += [of_batch, of_batch]
        args += [mk, mv]
    return pl.pallas_call(
        _mix_out_kernel,
        grid=(rows // tm,),
        in_specs=in_specs,
        out_specs=[pl.BlockSpec((tm, d), row), pl.BlockSpec((tm, d), row)],
        out_shape=[jax.ShapeDtypeStruct((rows, d), F32), jax.ShapeDtypeStruct((rows, d), BF16)],
        compiler_params=_params("arbitrary"),
        name="mix_out",
    )(*args)


MEM_LANES = 128
MEM_PARTS = MEM_HEAD_DIM // MEM_LANES
MEM_ROWS = MEM_PARTS * MEM_HEADS


def _cross_sample_products(q, mk_ref):
    return lax.dot_general(q, mk_ref[...].astype(BF16), NT_DIMS, preferred_element_type=F32)


def _cross_sample_attend(r, mv_ref):
    n = r.shape[1]
    row = lax.broadcasted_iota(jnp.int32, (MEM_ROWS, n), 0)
    lane_row = lax.broadcasted_iota(jnp.int32, (MEM_ROWS, n), 1) % MEM_ROWS
    own = lane_row == row
    valid = (row < MEM_HEADS) & own

    def other_part(t, lane_shift):
        return pltpu.roll(pltpu.roll(t, MEM_HEADS, 0), lane_shift, 1)

    r = jnp.where(own, r, 0.0)
    lg = jnp.where(valid, r + other_part(r, n - MEM_HEADS), NEG)
    p = jnp.where(valid, jnp.exp(lg - jnp.max(lg, axis=-1, keepdims=True)), 0.0)
    p = p + other_part(p, MEM_HEADS)
    l = jnp.sum(p, axis=-1, keepdims=True)
    return jnp.dot(p.astype(BF16), mv_ref[...].astype(BF16), preferred_element_type=F32) / l


def _stored_rows(qc):
    rows = qc.shape[0]
    return jnp.transpose(qc.reshape(rows, MEM_HEADS, MEM_PARTS, MEM_LANES), (0, 2, 1, 3)).reshape(rows, MEM_ROWS, MEM_LANES)


def _stored_slots(m):
    b, slots = m.shape[:2]
    m = m.reshape(b, slots, MEM_HEADS, MEM_PARTS, MEM_LANES)
    return jnp.transpose(m, (0, 1, 3, 2, 4)).reshape(b, slots * MEM_ROWS, MEM_LANES)


def _unstored_rows(o):
    rows = o.shape[0]
    return jnp.transpose(o.reshape(rows, MEM_PARTS, MEM_HEADS, MEM_LANES), (0, 2, 1, 3)).reshape(rows, -1)


def _mlp_kernel(x1_ref, o_ref, wco_ref, gm_ref, wu_ref, wd_ref, gf_ref, *rest):
    if len(rest) == 1:
        (y_ref,) = rest
        n_seq = 0
    else:
        qs_ref, mk_ref, mv_ref, y_ref, os_ref = rest
        n_seq = qs_ref.shape[0]
    products = [_cross_sample_products(qs_ref[s], mk_ref.at[s]) for s in range(n_seq)]
    x2 = x1_ref[...] + jnp.dot(o_ref[...], wco_ref[...], preferred_element_type=F32)
    hm = _rms(x2, gm_ref[...]).astype(BF16)
    acc = x2
    for c in range(D_FF // FF_CHUNK):
        if c == 1:
            for s in range(n_seq):
                os_ref[s] = _cross_sample_attend(products[s], mv_ref.at[s]).astype(BF16)
        cs = slice(c * FF_CHUNK, (c + 1) * FF_CHUNK)
        a = jnp.dot(hm, wu_ref[:, cs], preferred_element_type=F32)
        a = jnp.square(jnp.maximum(a, 0.0)).astype(BF16)
        acc = acc + jnp.dot(a, wd_ref[cs, :], preferred_element_type=F32)
    y_ref[...] = _rms(acc, gf_ref[...])


def _mlp(x1, o, w_co, g_mlp, w_up, w_down, g_final, sample_cross=None):
    rows, d = x1.shape
    tm = min(512, rows)
    steps = rows // tm
    row = lambda i: (i, 0)
    in_specs = [pl.BlockSpec((tm, d), row), pl.BlockSpec((tm, d), row), _resident(w_co.shape),
                _resident((1, d)), _resident(w_up.shape), _resident(w_down.shape), _resident((1, d))]
    args = [x1, o, w_co, g_mlp, w_up, w_down, g_final]
    out_specs = [pl.BlockSpec((tm, d), row)]
    out_shape = [jax.ShapeDtypeStruct((rows, d), F32)]
    if sample_cross is not None:
        qc, mem_k, mem_v = sample_cross
        b, slots = mem_k.shape[:2]
        assert b % steps == 0
        seq = lambda n: pl.BlockSpec((b // steps, n, MEM_LANES), lambda i: (i, 0, 0))
        in_specs += [seq(MEM_ROWS), seq(slots * MEM_ROWS), seq(slots * MEM_ROWS)]
        args += [_stored_rows(qc), _stored_slots(mem_k), _stored_slots(mem_v)]
        out_specs.append(seq(MEM_ROWS))
        out_shape.append(jax.ShapeDtypeStruct((b, MEM_ROWS, MEM_LANES), BF16))
    outs = pl.pallas_call(
        _mlp_kernel,
        grid=(steps,),
        in_specs=in_specs,
        out_specs=out_specs,
        out_shape=out_shape,
        compiler_params=_params("arbitrary"),
        name="mlp",
    )(*args)
    if sample_cross is None:
        return outs[0]
    return outs[0], _unstored_rows(outs[1])


def _proj_sample_kernel(x_ref, g_ref, w_ref, cos_ref, sin_ref, st_ref, wp_ref, sp_ref,
                        q_ref, k_ref, v_ref, kt_ref, vt_ref, u_ref, po_ref, *, n_prev):
    h = _rms(x_ref[...], g_ref[...]).astype(BF16)
    proj = jnp.dot(h, w_ref[...], preferred_element_type=F32)
    u = proj[:, :POOL_WIDTH]
    cos = cos_ref[...]
    sin = sin_ref[...]
    q_ref[...] = _rope(proj[:, POOL_WIDTH:POOL_WIDTH + ATTN_WIDTH], cos, sin) * (HEAD_DIM ** -0.5)
    k = _rope(proj[:, POOL_WIDTH + ATTN_WIDTH:POOL_WIDTH + 2 * ATTN_WIDTH], cos, sin)
    v = proj[:, POOL_WIDTH + 2 * ATTN_WIDTH:]
    k_ref[...] = k
    v_ref[...] = v
    kt_ref[...] = k.T
    vt_ref[...] = v.T
    u_ref[...] = u
    for g, w in enumerate(POOL_WINDOWS):
        cols = slice(g * POOL_GC, (g + 1) * POOL_GC)
        ug = u[:, cols]
        wsum = ug
        for j in range(1, w):
            wsum = wsum + st_ref[POOL_STATE - j, :, cols]
        d = wsum / float(min(w, n_prev + 1)) - ug
        mixed = jnp.dot(d.astype(BF16), wp_ref[g], preferred_element_type=F32)
        po_ref[:, cols] = (mixed * sp_ref[:, cols]).astype(BF16)


def _proj_sample(x, g, w_in, cos, sin, state_t, w_pool, s_pool, n_prev):
    rows, d = x.shape
    wide = jax.ShapeDtypeStruct((rows, ATTN_WIDTH), F32)
    tall = jax.ShapeDtypeStruct((ATTN_WIDTH, rows), F32)
    full = lambda a: pl.BlockSpec(a.shape, lambda i, nd=len(a.shape): (0,) * nd)
    args = (x, g, w_in, cos, sin, state_t, w_pool, s_pool)
    outs = [wide, wide, wide, tall, tall, wide, jax.ShapeDtypeStruct((rows, POOL_WIDTH), BF16)]
    return pl.pallas_call(
        functools.partial(_proj_sample_kernel, n_prev=n_prev),
        grid=(1,),
        in_specs=[full(a) for a in args],
        out_specs=[full(o) for o in outs],
        out_shape=outs,
        compiler_params=_params("arbitrary"),
        name="proj_sample",
    )(*args)


def _own_lanes():
    head_of_lane = lax.broadcasted_iota(jnp.int32, (N_HEADS, ATTN_WIDTH), 1) // HEAD_DIM
    head = lax.broadcasted_iota(jnp.int32, (N_HEADS, ATTN_WIDTH), 0)
    return head_of_lane == head


def _sample_queries(q):
    return jnp.where(_own_lanes(), q, 0.0)


def _sample_page_logits(q_heads, k_page):
    return jnp.dot(q_heads.astype(BF16), k_page[...].astype(BF16), preferred_element_type=F32)


def _sample_finish(q_heads, lg, k_new, v_new, v_pages):
    page = v_pages[0].shape[1]
    past = lg.shape[1]
    nb = past // MOBA_BLOCK
    mine = _own_lanes()

    s = jnp.concatenate(
        [jnp.sum(lg[:, j * MOBA_BLOCK:(j + 1) * MOBA_BLOCK], axis=-1, keepdims=True) for j in range(nb)], axis=-1)
    blk_id = lax.broadcasted_iota(jnp.int32, (N_HEADS, nb), 1)
    sel = _topk_member(s, blk_id >= 0, blk_id, axis=1)
    key_blk = lax.broadcasted_iota(jnp.int32, (N_HEADS, past), 1) // MOBA_BLOCK
    chosen = jnp.zeros((N_HEADS, past), jnp.bool_)
    for j in range(nb):
        chosen = chosen | ((key_blk == j) & sel[:, j:j + 1])
    lg = jnp.where(chosen, lg, NEG)

    lg_new = jnp.sum(q_heads * k_new, axis=-1, keepdims=True)
    m = jnp.maximum(jnp.max(lg, axis=-1, keepdims=True), lg_new)
    p = jnp.exp(lg - m)
    p_new = jnp.exp(lg_new - m)
    l = jnp.sum(p, axis=-1, keepdims=True) + p_new

    lanes = page
    sub = lax.broadcasted_iota(jnp.int32, (HEAD_DIM, lanes), 0)
    lane = lax.broadcasted_iota(jnp.int32, (HEAD_DIM, lanes), 1)
    tiles = []
    for first in range(0, N_HEADS, lanes // HEAD_DIM):
        row = jnp.zeros((1, lanes), F32)
        for h in range(first, first + lanes // HEAD_DIM):
            acc = jnp.zeros((HEAD_DIM, lanes), F32)
            for n, vp in enumerate(v_pages):
                w = jnp.broadcast_to(p[h:h + 1, n * page:(n + 1) * page], (HEAD_DIM, page))
                acc = acc + vp[h * HEAD_DIM:(h + 1) * HEAD_DIM, :] * w
            total = jnp.sum(acc, axis=1, keepdims=True)
            on_diag = lane == sub + (h - first) * HEAD_DIM
            row = row + jnp.sum(jnp.where(on_diag, total, 0.0), axis=0, keepdims=True)
        tiles.append(row)
    pv = jnp.concatenate(tiles, axis=-1)
    per_lane = lambda t: jnp.sum(jnp.where(mine, t, 0.0), axis=0, keepdims=True)
    return (pv + per_lane(p_new) * v_new) / per_lane(l)


def _rope_tables(pos, heads):
    half = HEAD_DIM // 2
    inv_freq = ROPE_THETA ** (-jnp.arange(half, dtype=F32) / half)
    ang = pos.astype(F32)[:, None] * inv_freq[None, :]
    cos = jnp.cos(ang)
    sin = jnp.sin(ang)
    cos_full = jnp.tile(jnp.concatenate([cos, cos], axis=-1), (1, heads))
    sin_signed = jnp.tile(jnp.concatenate([-sin, sin], axis=-1), (1, heads))
    return cos_full, sin_signed


def kernel(x_prompt, x_sample, cache_k, cache_v, cache_mem_k, cache_mem_v, state_pool, page_table, mem_prompt,
           g_mix, w_in, w_pool, s_pool, w_out, g_cross, g_mem, w_cq, w_ck, w_cv, w_co, g_mlp, w_up, w_down,
           g_final):
    depth = w_in.shape[0]
    assert depth == 1, "one decoder layer"
    bp, seq, d = x_prompt.shape
    bs, dec_seq, _ = x_sample.shape
    assert dec_seq == 1 and seq % MOBA_BLOCK == 0
    n_phys, page = cache_k.shape[1], cache_k.shape[2]
    n_pages = page_table.shape[1]
    past_len = n_pages * page
    assert past_len % MOBA_BLOCK == 0 and past_len >= POOL_STATE
    mem_len = mem_prompt.shape[1]

    l = 0
    row = lambda a: a.reshape(1, -1)
    lp = dict(w_out=w_out[l].astype(BF16), g_cross=row(g_cross[l]), w_cq=w_cq[l].astype(BF16),
              w_co=w_co[l].astype(BF16), g_mlp=row(g_mlp[l]), w_up=w_up[l].astype(BF16),
              w_down=w_down[l].astype(BF16), g_final=row(g_final))
    w_in_b = w_in[l].astype(BF16)
    w_pool_b = w_pool[l].astype(BF16)
    g_mix_r = row(g_mix[l])
    s_pool_r = row(s_pool[l])

    lane_heads = 128 // HEAD_DIM
    cos_p, sin_p = _rope_tables(jnp.arange(seq), lane_heads)
    qtb, kt_p, vt_p, kb, vtb, kmean, po_p, last_p = _proj_prompt(
        x_prompt, g_mix_r, w_in_b, cos_p, sin_p, cos_p[:, :HEAD_DIM].T, sin_p[:, :HEAD_DIM].T, w_pool_b, s_pool_r)
    cos_s, sin_s = _rope_tables(jnp.full((1,), past_len), N_HEADS)
    state_t = jnp.transpose(state_pool[l], (1, 0, 2))
    q_s, k_s, v_s, kt_s, vt_s, u_s, po_s = _proj_sample(x_sample.reshape(bs, d), g_mix_r, w_in_b, cos_s, sin_s,
                                                        state_t, w_pool_b, s_pool_r, past_len)

    tok = lambda a: a.reshape(bs, 1, -1)
    pages_t = lambda c: jnp.transpose(c, (0, 2, 3, 1)).reshape(n_phys, ATTN_WIDTH, page)
    at_p, at_s = _moba(qtb, kb, vtb, kmean.reshape(bp, seq // MOBA_BLOCK, ATTN_WIDTH), page_table, tok(q_s),
                       tok(k_s), tok(v_s), pages_t(cache_k[l]), pages_t(cache_v[l]))

    mk_p, mv_p, mkb, mvb = _mem_kv(mem_prompt.reshape(bp * mem_len, d), row(g_mem[l]), w_ck[l].astype(BF16),
                                   w_cv[l].astype(BF16))
    flat = lambda a: a.reshape(-1, a.shape[-1])
    mix = lambda x, po, at, memory=None: _mix_out(flat(x), flat(po), flat(at), lp["w_out"], lp["g_cross"],
                                                  lp["w_cq"], memory)
    tail = (lp["w_co"], lp["g_mlp"], lp["w_up"], lp["w_down"], lp["g_final"])
    x1_p, o_p = mix(x_prompt, po_p, at_p, (mkb.reshape(bp, mem_len, d), mvb.reshape(bp, mem_len, d)))
    x1_s, qc_s = mix(x_sample, po_s, at_s)
    y_prompt, o_s = _mlp(x1_p, o_p, *tail, sample_cross=(qc_s, cache_mem_k[l], cache_mem_v[l]))
    y_prompt = y_prompt.reshape(bp, seq, d)
    y_sample = _mlp(x1_s, o_s, *tail).reshape(bs, 1, d)

    def heads(t, b_):
        return jnp.transpose(t.reshape(b_, N_HEADS, HEAD_DIM, -1), (0, 3, 1, 2))[None]

    mem_heads = lambda a: a.reshape(1, bp, mem_len, MEM_HEADS, MEM_HEAD_DIM)
    pool_sample = jnp.transpose(jnp.concatenate([state_t[1:], u_s[None]], axis=0), (1, 0, 2))[None]
    return (y_prompt, y_sample, heads(kt_p, bp), heads(vt_p, bp), last_p[:, 1:][None], mem_heads(mk_p),
            mem_heads(mv_p), heads(kt_s.T.reshape(bs, ATTN_WIDTH, 1), bs), heads(vt_s.T.reshape(bs, ATTN_WIDTH, 1), bs),
            pool_sample)
```

```python
import functools

import jax
import jax.numpy as jnp
from jax import lax
from jax.experimental import pallas as pl
from jax.experimental.pallas import tpu as pltpu

D_MODEL = 1024
POOL_WIDTH = 512
POOL_WINDOWS = (2, 4, 8, 16)
POOL_GC = POOL_WIDTH // len(POOL_WINDOWS)
POOL_STATE = max(POOL_WINDOWS) - 1
POOL_HIST = POOL_STATE + 1
ATTN_WIDTH = D_MODEL - POOL_WIDTH
HEAD_DIM = 64
N_HEADS = ATTN_WIDTH // HEAD_DIM
MOBA_BLOCK = 256
MOBA_TOPK = 3
ROPE_THETA = 10000.0
MEM_HEADS = 4
MEM_HEAD_DIM = D_MODEL // MEM_HEADS
D_FF = 4 * D_MODEL
FF_CHUNK = 1024
EPS = 1e-6
NEG = -1e30

VMEM_LIMIT_BYTES = 56 * 1024 * 1024

F32 = jnp.float32
BF16 = jnp.bfloat16
NT_DIMS = (((1,), (1,)), ((), ()))


def _params(*semantics):
    return pltpu.CompilerParams(dimension_semantics=semantics, vmem_limit_bytes=VMEM_LIMIT_BYTES)


def _resident(shape):
    zeros = (0,) * len(shape)
    return pl.BlockSpec(shape, lambda *_: zeros, pipeline_mode=pl.Buffered(1))


def _rms(x, g):
    return x * lax.rsqrt(jnp.mean(x * x, axis=-1, keepdims=True) + EPS) * g


def _rope(t, cos, sin_signed):
    width = t.shape[-1]
    lane = lax.broadcasted_iota(jnp.int32, t.shape, t.ndim - 1)
    first_half = (lane % HEAD_DIM) < (HEAD_DIM // 2)
    partner = jnp.where(first_half,
                        pltpu.roll(t, width - HEAD_DIM // 2, t.ndim - 1),
                        pltpu.roll(t, HEAD_DIM // 2, t.ndim - 1))
    return t * cos + partner * sin_signed


def _rope_t(t, cos, sin_signed):
    rows = t.shape[0]
    row = lax.broadcasted_iota(jnp.int32, t.shape, 0)
    first_half = (row % HEAD_DIM) < (HEAD_DIM // 2)
    partner = jnp.where(first_half, pltpu.roll(t, rows - HEAD_DIM // 2, 0), pltpu.roll(t, HEAD_DIM // 2, 0))
    return t * cos + partner * sin_signed


def _topk_member(s, valid, idx, axis):
    n = s.shape[axis]
    s = jnp.where(valid, s, NEG)
    rank = jnp.zeros(s.shape, jnp.int32)
    for j in range(n):
        sj = lax.slice_in_dim(s, j, j + 1, axis=axis)
        beats = (sj > s) | ((sj == s) & (j < idx))
        rank = rank + beats.astype(jnp.int32)
    return (rank < MOBA_TOPK) & valid


def _proj_prompt_kernel(x_ref, g_ref, w_ref, cos_ref, sin_ref, cost_ref, sint_ref, wp_ref, sp_ref,
                        qt_ref, kt_ref, vt_ref, kb_ref, vtb_ref, km_ref, po_ref, last_ref, ext_ref):
    i = pl.program_id(1)
    tm = x_ref.shape[1]

    @pl.when(i == 0)
    def _():
        ext_ref[0:POOL_HIST, :] = jnp.zeros((POOL_HIST, POOL_WIDTH), F32)

    h = _rms(x_ref[0], g_ref[...]).astype(BF16)

    def project(first):
        return jnp.dot(h, w_ref[:, first:first + ATTN_WIDTH], preferred_element_type=F32)

    lane_reps = ATTN_WIDTH // cos_ref.shape[1]
    u = project(0)
    k = _rope(project(POOL_WIDTH + ATTN_WIDTH), jnp.tile(cos_ref[...], (1, lane_reps)),
              jnp.tile(sin_ref[...], (1, lane_reps)))
    q = project(POOL_WIDTH)
    v = project(POOL_WIDTH + 2 * ATTN_WIDTH)
    qt = _rope_t(q.T, jnp.tile(cost_ref[...], (N_HEADS, 1)), jnp.tile(sint_ref[...], (N_HEADS, 1)))
    qt_ref[0] = (qt * (HEAD_DIM ** -0.5 * LOG2_E)).astype(BF16)
    vt = v.T
    kt_ref[0] = k.T
    vt_ref[0] = vt
    kb_ref[0] = k.astype(BF16)
    vtb_ref[0, 0] = vt.astype(BF16)
    km_ref[0, 0] = jnp.mean(k, axis=0, keepdims=True)

    pos = i * tm + lax.broadcasted_iota(jnp.int32, (tm, 1), 0)
    for g, w in enumerate(POOL_WINDOWS):
        cols = slice(g * POOL_GC, (g + 1) * POOL_GC)
        ug = u[:, cols]
        wsum = jnp.concatenate([ext_ref[:, cols], ug], axis=0)
        shift = 1
        while shift < w:
            wsum = wsum + pltpu.roll(wsum, shift, 0)
            shift *= 2
        cnt = jnp.minimum(w, pos + 1).astype(F32)
        d = wsum[POOL_HIST:, :] / cnt - ug
        mixed = jnp.dot(d.astype(BF16), wp_ref[g], preferred_element_type=F32)
        po_ref[0, :, cols] = (mixed * sp_ref[:, cols]).astype(BF16)
    tail = u[tm - POOL_HIST:, :]
    ext_ref[...] = tail
    last_ref[0] = tail


def _proj_prompt(x, g, w_in, cos, sin, cos_t, sin_t, w_pool, s_pool):
    b, s, d = x.shape
    tm = MOBA_BLOCK
    nq = s // tm
    row = lambda bi, i: (bi, i, 0)
    col = lambda bi, i: (bi, 0, i)
    blk = lambda bi, i: (bi, i, 0, 0)
    return pl.pallas_call(
        _proj_prompt_kernel,
        grid=(b, nq),
        in_specs=[
            pl.BlockSpec((1, tm, d), row),
            _resident((1, d)),
            _resident(w_in.shape),
            pl.BlockSpec((tm, cos.shape[1]), lambda bi, i: (i, 0)),
            pl.BlockSpec((tm, cos.shape[1]), lambda bi, i: (i, 0)),
            pl.BlockSpec((HEAD_DIM, tm), lambda bi, i: (0, i)),
            pl.BlockSpec((HEAD_DIM, tm), lambda bi, i: (0, i)),
            _resident(w_pool.shape),
            _resident((1, POOL_WIDTH)),
        ],
        out_specs=[
            pl.BlockSpec((1, ATTN_WIDTH, tm), col),
            pl.BlockSpec((1, ATTN_WIDTH, tm), col),
            pl.BlockSpec((1, ATTN_WIDTH, tm), col),
            pl.BlockSpec((1, tm, ATTN_WIDTH), row),
            pl.BlockSpec((1, 1, ATTN_WIDTH, tm), blk),
            pl.BlockSpec((1, 1, 1, ATTN_WIDTH), blk),
            pl.BlockSpec((1, tm, POOL_WIDTH), row),
            pl.BlockSpec((1, POOL_HIST, POOL_WIDTH), lambda bi, i: (bi, 0, 0)),
        ],
        out_shape=[
            jax.ShapeDtypeStruct((b, ATTN_WIDTH, s), BF16),
            jax.ShapeDtypeStruct((b, ATTN_WIDTH, s), F32),
            jax.ShapeDtypeStruct((b, ATTN_WIDTH, s), F32),
            jax.ShapeDtypeStruct((b, s, ATTN_WIDTH), BF16),
            jax.ShapeDtypeStruct((b, nq, ATTN_WIDTH, tm), BF16),
            jax.ShapeDtypeStruct((b, nq, 1, ATTN_WIDTH), F32),
            jax.ShapeDtypeStruct((b, s, POOL_WIDTH), BF16),
            jax.ShapeDtypeStruct((b, POOL_HIST, POOL_WIDTH), F32),
        ],
        scratch_shapes=[pltpu.VMEM((POOL_HIST, POOL_WIDTH), F32)],
        compiler_params=_params("arbitrary", "arbitrary"),
        name="proj_prompt",
    )(x, g, w_in, cos, sin, cos_t, sin_t, w_pool, s_pool)


SUBLANES = 8
LOGIT_AHEAD = 2
LOGIT_SLOTS = 4
LOG2_E = 1.4426950408889634
KEY_RING = 3
VALUE_RING = 2


def _query_block(visit, n):
    return jnp.where(visit % 2 == 0, visit // 2, n - 1 - visit // 2)


def _all_sublanes(t, combine):
    for s in (4, 2, 1):
        t = combine(t, pltpu.roll(t, s, 0))
    return t


def _moba_kernel(pt_ref, qt_ref, kb_ref, vtb_ref, km_ref, qs_ref, qn_ref, kn_ref, vn_ref, ck_ref, cv_ref,
                 o_ref, os_ref, acc_ref, m_ref, l_ref, take_ref, lg_ref, kbuf_ref, vbuf_ref, slg_ref, key_sem, val_sem,
                 *, n_pages, per_step):
    nq = pl.num_programs(1)
    i = _query_block(pl.program_id(1), nq)
    step = pl.program_id(0) * nq + pl.program_id(1)
    n_steps = pl.num_programs(0) * nq
    key_ring = kbuf_ref.shape[0]
    val_ring = vbuf_ref.shape[0]
    tq = qt_ref.shape[2]
    nb = km_ref.shape[1]

    def page_copies(cache_ref, buf_ref, sem, page_id, part):
        return [pltpu.make_async_copy(cache_ref.at[page_id(s, p)], buf_ref.at[part, s * n_pages + p], sem.at[part])
                for s in range(per_step) for p in range(n_pages)]

    def start_keys(t):
        for c in page_copies(ck_ref, kbuf_ref, key_sem, lambda s, p: pt_ref[t * per_step + s, p], t % key_ring):
            c.start()

    def start_values(t):
        for c in page_copies(cv_ref, vbuf_ref, val_sem, lambda s, p: pt_ref[t * per_step + s, p], t % val_ring):
            c.start()

    def wait_part(cache_ref, buf_ref, sem, part):
        pltpu.make_async_copy(cache_ref.at[pl.ds(0, per_step * n_pages)], buf_ref.at[part], sem.at[part]).wait()

    def key_logits(q_rows, part):
        return [[_sample_page_logits(q_rows[s], kbuf_ref.at[part, s * n_pages + p]) for p in range(n_pages)]
                for s in range(per_step)]

    @pl.when(step == 0)
    def _():
        for t in range(min(key_ring, n_steps)):
            start_keys(t)
        for t in range(min(2, n_steps)):
            start_values(t)
        wait_part(ck_ref, kbuf_ref, key_sem, 0)
        first = key_logits([_sample_queries(qs_ref[s]) for s in range(per_step)], 0)
        for s in range(per_step):
            slg_ref[s] = jnp.concatenate(first[s], axis=-1)
        if n_steps > key_ring:
            start_keys(key_ring)

    @pl.when((step >= 1) & (step + key_ring < n_steps))
    def _():
        start_keys(step + key_ring)

    @pl.when((step >= 1) & (step + 1 < n_steps))
    def _():
        start_values(step + 1)

    @pl.when(step + 1 < n_steps)
    def _():
        wait_part(ck_ref, kbuf_ref, key_sem, (step + 1) % key_ring)

    wait_part(cv_ref, vbuf_ref, val_sem, step % val_ring)
    slot = step % val_ring
    next_keys = (step + 1) % key_ring


    km_rows = jnp.concatenate([km_ref[0]] * N_HEADS, axis=0)
    row_head = lax.broadcasted_iota(jnp.int32, km_rows.shape, 0) // nb
    lane_head = lax.broadcasted_iota(jnp.int32, km_rows.shape, 1) // HEAD_DIM
    km_heads = jnp.where(row_head == lane_head, km_rows, 0.0).astype(BF16)
    s = jnp.dot(km_heads, qt_ref[0], preferred_element_type=F32).reshape(N_HEADS, nb, tq)
    blk = lax.broadcasted_iota(jnp.int32, s.shape, 1)
    sel = _topk_member(s, blk < i, blk, axis=1).astype(F32)

    for h in range(N_HEADS):
        for jb in range(nb):
            take_ref[h, jb] = jnp.broadcast_to(sel[h, jb:jb + 1, :], (SUBLANES, tq))

    def head_rows(h):
        return slice(h * HEAD_DIM, (h + 1) * HEAD_DIM)

    def stage_logits(h, j):
        off = pl.multiple_of(j * MOBA_BLOCK, MOBA_BLOCK)
        lg_ref[h % LOGIT_SLOTS] = jnp.dot(kb_ref[0, pl.ds(off, MOBA_BLOCK), head_rows(h)],
                                          qt_ref[0, head_rows(h), :], preferred_element_type=F32)

    def weighted_values(h, j, p3):
        p = p3.reshape(MOBA_BLOCK, tq).astype(BF16)
        pv = jnp.dot(vtb_ref[0, j, head_rows(h), :], p, preferred_element_type=F32)
        return pv.reshape(HEAD_DIM // SUBLANES, SUBLANES, tq)

    def tiles(t):
        return t.reshape(t.shape[0] // SUBLANES, SUBLANES, tq)

    def block_pass(j, j_next, own, between=None):
        for h in range(N_HEADS):
            ahead = h + LOGIT_AHEAD
            if ahead < N_HEADS:
                stage_logits(ahead, j)
            else:
                stage_logits(ahead - N_HEADS, j_next)
            if between is not None:
                between(h)
            lg = lg_ref[h % LOGIT_SLOTS]
            if own:
                kk = lax.broadcasted_iota(jnp.int32, lg.shape, 0)
                qq = lax.broadcasted_iota(jnp.int32, lg.shape, 1)
                lg3 = tiles(jnp.where(kk <= qq, lg, NEG))
                m = _all_sublanes(jnp.max(lg3, axis=0), jnp.maximum)
                p3 = jnp.exp2(lg3 - m[None])
                m_ref[h] = m
                l_ref[h] = _all_sublanes(jnp.sum(p3, axis=0), jnp.add)
                acc_ref[head_rows(h), :] = weighted_values(h, j, p3).reshape(HEAD_DIM, tq)
            else:
                lg3 = tiles(lg)
                take = take_ref[h, j] > 0.0
                m_old = m_ref[h]
                l_old = l_ref[h]
                acc_old = tiles(acc_ref[head_rows(h), :])
                m = jnp.maximum(m_old, _all_sublanes(jnp.max(lg3, axis=0), jnp.maximum))
                a = jnp.exp2(m_old - m)
                p3 = jnp.exp2(lg3 - m[None])
                l = a * l_old + _all_sublanes(jnp.sum(p3, axis=0), jnp.add)
                acc3 = a[None] * acc_old + weighted_values(h, j, p3)
                m_ref[h] = jnp.where(take, m, m_old)
                l_ref[h] = jnp.where(take, l, l_old)
                acc_ref[head_rows(h), :] = jnp.where(take[None], acc3, acc_old).reshape(HEAD_DIM, tq)

    last_past = jnp.maximum(i - 1, 0)
    for h in range(LOGIT_AHEAD):
        stage_logits(h, i)

    for s in range(per_step):
        v_pages = [vbuf_ref.at[slot, s * n_pages + p] for p in range(n_pages)]
        os_ref[s] = _sample_finish(_sample_queries(qs_ref[s]), slg_ref[s], kn_ref[s], vn_ref[s], v_pages).astype(BF16)

    next_q = [_sample_queries(qn_ref[s]) for s in range(per_step)]
    jobs = [(s, p) for s in range(per_step) for p in range(n_pages)]
    per_head = -(-len(jobs) // N_HEADS)
    page_logits = {}

    def sample_page_logits(h):
        for s, p in jobs[h * per_head:(h + 1) * per_head]:
            page_logits[s, p] = _sample_page_logits(next_q[s], kbuf_ref.at[next_keys, s * n_pages + p])

    block_pass(i, 0, own=True, between=sample_page_logits)
    for s in range(per_step):
        slg_ref[s] = jnp.concatenate([page_logits[s, p] for p in range(n_pages)], axis=-1)

    def two_past_blocks(jj, carry):
        j = 2 * jj
        block_pass(j, j + 1, own=False)
        block_pass(j + 1, jnp.minimum(j + 2, last_past), own=False)
        return carry

    lax.fori_loop(0, i // 2, two_past_blocks, 0)

    @pl.when(i % 2 == 1)
    def _():
        block_pass(last_past, last_past, own=False)

    for h in range(N_HEADS):
        rows = slice(h * HEAD_DIM, (h + 1) * HEAD_DIM)
        acc_ref[rows, :] = (tiles(acc_ref[rows, :]) / l_ref[h][None]).reshape(HEAD_DIM, tq)
    o_ref[0] = acc_ref[...].T.astype(BF16)


def _moba(qtb, kb, vtb, kmean, page_table, q_s, k_s, v_s, cache_kt, cache_vt):
    b, w, s = qtb.shape
    nq = s // MOBA_BLOCK
    bs, n_pages = page_table.shape
    page = cache_kt.shape[2]
    steps = b * nq
    assert bs % steps == 0
    per_step = bs // steps
    step_seq = lambda bi, i, pt: (bi * nq + i, 0, 0)
    tok = pl.BlockSpec((per_step, 1, w), step_seq)
    tok_next = pl.BlockSpec((per_step, 1, w), lambda bi, i, pt: (jnp.minimum(bi * nq + i + 1, steps - 1), 0, 0))
    in_hbm = pl.BlockSpec(memory_space=pl.ANY)
    page_ring = lambda parts: pltpu.VMEM((parts, per_step * n_pages, w, page), cache_kt.dtype)
    grid_spec = pltpu.PrefetchScalarGridSpec(
        num_scalar_prefetch=1,
        grid=(b, nq),
        in_specs=[
            pl.BlockSpec((1, w, MOBA_BLOCK), lambda bi, i, pt: (bi, 0, _query_block(i, nq))),
            pl.BlockSpec((1, s, w), lambda bi, i, pt: (bi, 0, 0)),
            pl.BlockSpec((1, nq, w, MOBA_BLOCK), lambda bi, i, pt: (bi, 0, 0, 0)),
            pl.BlockSpec((1, nq, w), lambda bi, i, pt: (bi, 0, 0)),
            tok, tok_next, tok, tok, in_hbm, in_hbm,
        ],
        out_specs=[pl.BlockSpec((1, MOBA_BLOCK, w), lambda bi, i, pt: (bi, _query_block(i, nq), 0)), tok],
        scratch_shapes=[pltpu.VMEM((w, MOBA_BLOCK), F32),
                        pltpu.VMEM((N_HEADS, SUBLANES, MOBA_BLOCK), F32),
                        pltpu.VMEM((N_HEADS, SUBLANES, MOBA_BLOCK), F32),
                        pltpu.VMEM((N_HEADS, nq, SUBLANES, MOBA_BLOCK), F32),
                        pltpu.VMEM((LOGIT_SLOTS, MOBA_BLOCK, MOBA_BLOCK), F32),
                        page_ring(KEY_RING), page_ring(VALUE_RING),
                        pltpu.VMEM((per_step, N_HEADS, n_pages * page), F32),
                        pltpu.SemaphoreType.DMA((KEY_RING,)), pltpu.SemaphoreType.DMA((VALUE_RING,))],
    )
    return pl.pallas_call(
        functools.partial(_moba_kernel, n_pages=n_pages, per_step=per_step),
        grid_spec=grid_spec,
        out_shape=[jax.ShapeDtypeStruct((b, s, w), BF16), jax.ShapeDtypeStruct((bs, 1, w), BF16)],
        compiler_params=_params("arbitrary", "arbitrary"),
        name="moba",
    )(page_table, qtb, kb, vtb, kmean, q_s, q_s, k_s, v_s, cache_kt, cache_vt)


def _mem_kv_kernel(m_ref, g_ref, wk_ref, wv_ref, mk_ref, mv_ref, mkb_ref, mvb_ref):
    h = _rms(m_ref[...], g_ref[...]).astype(BF16)
    mk = jnp.dot(h, wk_ref[...], preferred_element_type=F32)
    mv = jnp.dot(h, wv_ref[...], preferred_element_type=F32)
    mk_ref[...] = mk
    mv_ref[...] = mv
    mkb_ref[...] = mk.astype(BF16)
    mvb_ref[...] = mv.astype(BF16)


def _mem_kv(mem, g, w_ck, w_cv):
    rows, d = mem.shape
    tm = 512
    row = lambda i: (i, 0)
    return pl.pallas_call(
        _mem_kv_kernel,
        grid=(rows // tm,),
        in_specs=[pl.BlockSpec((tm, d), row), _resident((1, d)), _resident(w_ck.shape), _resident(w_cv.shape)],
        out_specs=[pl.BlockSpec((tm, d), row)] * 4,
        out_shape=[jax.ShapeDtypeStruct((rows, d), F32)] * 2 + [jax.ShapeDtypeStruct((rows, d), BF16)] * 2,
        compiler_params=_params("arbitrary"),
        name="mem_kv",
    )(mem, g, w_ck, w_cv)


MIX_CHAINS = 2


def _mix_out_kernel(x_ref, po_ref, at_ref, wo_ref, g_ref, wq_ref, *rest):
    with_memory = len(rest) == 4
    if with_memory:
        mk_ref, mv_ref, x1_ref, out_ref = rest
    else:
        x1_ref, out_ref = rest
    tm = x_ref.shape[0]
    chains = MIX_CHAINS if with_memory else 1
    rows_of = lambda c: slice(c * (tm // chains), (c + 1) * (tm // chains))

    def residual(c):
        rows = rows_of(c)
        cat = jnp.concatenate([po_ref[rows, :], at_ref[rows, :]], axis=-1)
        x1 = x_ref[rows, :] + jnp.dot(cat, wo_ref[...], preferred_element_type=F32)
        x1_ref[rows, :] = x1
        return _rms(x1, g_ref[...]).astype(BF16)

    def queries(hq):
        return (jnp.dot(hq, wq_ref[...], preferred_element_type=F32) * (MEM_HEAD_DIM ** -0.5)).astype(BF16)

    def attend(c, q):
        for h in range(MEM_HEADS):
            sl = slice(h * MEM_HEAD_DIM, (h + 1) * MEM_HEAD_DIM)
            lg = lax.dot_general(q[:, sl], mk_ref[0, :, sl], NT_DIMS, preferred_element_type=F32)
            p = jnp.exp(lg - jnp.max(lg, axis=-1, keepdims=True))
            l = jnp.sum(p, axis=-1, keepdims=True)
            o = jnp.dot(p.astype(BF16), mv_ref[0, :, sl], preferred_element_type=F32)
            out_ref[rows_of(c), sl] = (o / l).astype(BF16)

    if not with_memory:
        out_ref[...] = queries(residual(0))
        return
    q = queries(residual(0))
    for c in range(chains):
        hq_next = residual(c + 1) if c + 1 < chains else None
        attend(c, q)
        if hq_next is not None:
            q = queries(hq_next)


def _mix_out(x, po, at, w_out, g_cross, w_cq, memory=None):
    rows, d = x.shape
    tm = min(512, rows)
    row = lambda i: (i, 0)
    in_specs = [pl.BlockSpec((tm, d), row), pl.BlockSpec((tm, POOL_WIDTH), row), pl.BlockSpec((tm, ATTN_WIDTH), row),
                _resident(w_out.shape), _resident((1, d)), _resident(w_cq.shape)]
    args = [x, po, at, w_out, g_cross, w_cq]
    if memory is not None:
        mk, mv = memory
        tiles_per_batch = rows // mk.shape[0] // tm
        assert tiles_per_batch * tm * mk.shape[0] == rows and tm % MIX_CHAINS == 0
        of_batch = pl.BlockSpec((1,) + mk.shape[1:], lambda i: (i // tiles_per_batch, 0, 0))
        in_specs += [of_batch, of_batch]
        args += [mk, mv]
    return pl.pallas_call(
        _mix_out_kernel,
        grid=(rows // tm,),
        in_specs=in_specs,
        out_specs=[pl.BlockSpec((tm, d), row), pl.BlockSpec((tm, d), row)],
        out_shape=[jax.ShapeDtypeStruct((rows, d), F32), jax.ShapeDtypeStruct((rows, d), BF16)],
        compiler_params=_params("arbitrary"),
        name="mix_out",
    )(*args)


MEM_LANES = 128
MEM_PARTS = MEM_HEAD_DIM // MEM_LANES
MEM_ROWS = MEM_PARTS * MEM_HEADS


def _cross_sample_products(q, mk_ref):
    return lax.dot_general(q, mk_ref[...].astype(BF16), NT_DIMS, preferred_element_type=F32)


def _cross_sample_attend(r, mv_ref):
    n = r.shape[1]
    row = lax.broadcasted_iota(jnp.int32, (MEM_ROWS, n), 0)
    lane_row = lax.broadcasted_iota(jnp.int32, (MEM_ROWS, n), 1) % MEM_ROWS
    own = lane_row == row
    valid = (row < MEM_HEADS) & own

    def other_part(t, lane_shift):
        return pltpu.roll(pltpu.roll(t, MEM_HEADS, 0), lane_shift, 1)

    r = jnp.where(own, r, 0.0)
    lg = jnp.where(valid, r + other_part(r, n - MEM_HEADS), NEG)
    p = jnp.where(valid, jnp.exp(lg - jnp.max(lg, axis=-1, keepdims=True)), 0.0)
    p = p + other_part(p, MEM_HEADS)
    l = jnp.sum(p, axis=-1, keepdims=True)
    return jnp.dot(p.astype(BF16), mv_ref[...].astype(BF16), preferred_element_type=F32) / l


def _stored_rows(qc):
    rows = qc.shape[0]
    return jnp.transpose(qc.reshape(rows, MEM_HEADS, MEM_PARTS, MEM_LANES), (0, 2, 1, 3)).reshape(rows, MEM_ROWS, MEM_LANES)


def _stored_slots(m):
    b, slots = m.shape[:2]
    m = m.reshape(b, slots, MEM_HEADS, MEM_PARTS, MEM_LANES)
    return jnp.transpose(m, (0, 1, 3, 2, 4)).reshape(b, slots * MEM_ROWS, MEM_LANES)


def _unstored_rows(o):
    rows = o.shape[0]
    return jnp.transpose(o.reshape(rows, MEM_PARTS, MEM_HEADS, MEM_LANES), (0, 2, 1, 3)).reshape(rows, -1)


def _mlp_kernel(x1_ref, o_ref, wco_ref, gm_ref, wu_ref, wd_ref, gf_ref, *rest):
    if len(rest) == 1:
        (y_ref,) = rest
        n_seq = 0
    else:
        qs_ref, mk_ref, mv_ref, y_ref, os_ref = rest
        n_seq = qs_ref.shape[0]
    products = [_cross_sample_products(qs_ref[s], mk_ref.at[s]) for s in range(n_seq)]
    x2 = x1_ref[...] + jnp.dot(o_ref[...], wco_ref[...], preferred_element_type=F32)
    hm = _rms(x2, gm_ref[...]).astype(BF16)
    acc = x2
    for c in range(D_FF // FF_CHUNK):
        if c == 1:
            for s in range(n_seq):
                os_ref[s] = _cross_sample_attend(products[s], mv_ref.at[s]).astype(BF16)
        cs = slice(c * FF_CHUNK, (c + 1) * FF_CHUNK)
        a = jnp.dot(hm, wu_ref[:, cs], preferred_element_type=F32)
        a = jnp.square(jnp.maximum(a, 0.0)).astype(BF16)
        acc = acc + jnp.dot(a, wd_ref[cs, :], preferred_element_type=F32)
    y_ref[...] = _rms(acc, gf_ref[...])


def _mlp(x1, o, w_co, g_mlp, w_up, w_down, g_final, sample_cross=None):
    rows, d = x1.shape
    tm = min(512, rows)
    steps = rows // tm
    row = lambda i: (i, 0)
    in_specs = [pl.BlockSpec((tm, d), row), pl.BlockSpec((tm, d), row), _resident(w_co.shape),
                _resident((1, d)), _resident(w_up.shape), _resident(w_down.shape), _resident((1, d))]
    args = [x1, o, w_co, g_mlp, w_up, w_down, g_final]
    out_specs = [pl.BlockSpec((tm, d), row)]
    out_shape = [jax.ShapeDtypeStruct((rows, d), F32)]
    if sample_cross is not None:
        qc, mem_k, mem_v = sample_cross
        b, slots = mem_k.shape[:2]
        assert b % steps == 0
        seq = lambda n: pl.BlockSpec((b // steps, n, MEM_LANES), lambda i: (i, 0, 0))
        in_specs += [seq(MEM_ROWS), seq(slots * MEM_ROWS), seq(slots * MEM_ROWS)]
        args += [_stored_rows(qc), _stored_slots(mem_k), _stored_slots(mem_v)]
        out_specs.append(seq(MEM_ROWS))
        out_shape.append(jax.ShapeDtypeStruct((b, MEM_ROWS, MEM_LANES), BF16))
    outs = pl.pallas_call(
        _mlp_kernel,
        grid=(steps,),
        in_specs=in_specs,
        out_specs=out_specs,
        out_shape=out_shape,
        compiler_params=_params("arbitrary"),
        name="mlp",
    )(*args)
    if sample_cross is None:
        return outs[0]
    return outs[0], _unstored_rows(outs[1])


def _proj_sample_kernel(x_ref, g_ref, w_ref, cos_ref, sin_ref, st_ref, wp_ref, sp_ref,
                        q_ref, k_ref, v_ref, kt_ref, vt_ref, u_ref, po_ref, *, n_prev):
    h = _rms(x_ref[...], g_ref[...]).astype(BF16)
    proj = jnp.dot(h, w_ref[...], preferred_element_type=F32)
    u = proj[:, :POOL_WIDTH]
    cos = cos_ref[...]
    sin = sin_ref[...]
    q_ref[...] = _rope(proj[:, POOL_WIDTH:POOL_WIDTH + ATTN_WIDTH], cos, sin) * (HEAD_DIM ** -0.5)
    k = _rope(proj[:, POOL_WIDTH + ATTN_WIDTH:POOL_WIDTH + 2 * ATTN_WIDTH], cos, sin)
    v = proj[:, POOL_WIDTH + 2 * ATTN_WIDTH:]
    k_ref[...] = k
    v_ref[...] = v
    kt_ref[...] = k.T
    vt_ref[...] = v.T
    u_ref[...] = u
    for g, w in enumerate(POOL_WINDOWS):
        cols = slice(g * POOL_GC, (g + 1) * POOL_GC)
        ug = u[:, cols]
        wsum = ug
        for j in range(1, w):
            wsum = wsum + st_ref[POOL_STATE - j, :, cols]
        d = wsum / float(min(w, n_prev + 1)) - ug
        mixed = jnp.dot(d.astype(BF16), wp_ref[g], preferred_element_type=F32)
        po_ref[:, cols] = (mixed * sp_ref[:, cols]).astype(BF16)


def _proj_sample(x, g, w_in, cos, sin, state_t, w_pool, s_pool, n_prev):
    rows, d = x.shape
    wide = jax.ShapeDtypeStruct((rows, ATTN_WIDTH), F32)
    tall = jax.ShapeDtypeStruct((ATTN_WIDTH, rows), F32)
    full = lambda a: pl.BlockSpec(a.shape, lambda i, nd=len(a.shape): (0,) * nd)
    args = (x, g, w_in, cos, sin, state_t, w_pool, s_pool)
    outs = [wide, wide, wide, tall, tall, wide, jax.ShapeDtypeStruct((rows, POOL_WIDTH), BF16)]
    return pl.pallas_call(
        functools.partial(_proj_sample_kernel, n_prev=n_prev),
        grid=(1,),
        in_specs=[full(a) for a in args],
        out_specs=[full(o) for o in outs],
        out_shape=outs,
        compiler_params=_params("arbitrary"),
        name="proj_sample",
    )(*args)


def _own_lanes():
    head_of_lane = lax.broadcasted_iota(jnp.int32, (N_HEADS, ATTN_WIDTH), 1) // HEAD_DIM
    head = lax.broadcasted_iota(jnp.int32, (N_HEADS, ATTN_WIDTH), 0)
    return head_of_lane == head


def _sample_queries(q):
    return jnp.where(_own_lanes(), q, 0.0)


def _sample_page_logits(q_heads, k_page):
    return jnp.dot(q_heads.astype(BF16), k_page[...].astype(BF16), preferred_element_type=F32)


def _sample_finish(q_heads, lg, k_new, v_new, v_pages):
    page = v_pages[0].shape[1]
    past = lg.shape[1]
    nb = past // MOBA_BLOCK
    mine = _own_lanes()

    s = jnp.concatenate(
        [jnp.sum(lg[:, j * MOBA_BLOCK:(j + 1) * MOBA_BLOCK], axis=-1, keepdims=True) for j in range(nb)], axis=-1)
    blk_id = lax.broadcasted_iota(jnp.int32, (N_HEADS, nb), 1)
    sel = _topk_member(s, blk_id >= 0, blk_id, axis=1)
    key_blk = lax.broadcasted_iota(jnp.int32, (N_HEADS, past), 1) // MOBA_BLOCK
    chosen = jnp.zeros((N_HEADS, past), jnp.bool_)
    for j in range(nb):
        chosen = chosen | ((key_blk == j) & sel[:, j:j + 1])
    lg = jnp.where(chosen, lg, NEG)

    lg_new = jnp.sum(q_heads * k_new, axis=-1, keepdims=True)
    m = jnp.maximum(jnp.max(lg, axis=-1, keepdims=True), lg_new)
    p = jnp.exp(lg - m)
    p_new = jnp.exp(lg_new - m)
    l = jnp.sum(p, axis=-1, keepdims=True) + p_new

    lanes = page
    sub = lax.broadcasted_iota(jnp.int32, (HEAD_DIM, lanes), 0)
    lane = lax.broadcasted_iota(jnp.int32, (HEAD_DIM, lanes), 1)
    tiles = []
    for first in range(0, N_HEADS, lanes // HEAD_DIM):
        row = jnp.zeros((1, lanes), F32)
        for h in range(first, first + lanes // HEAD_DIM):
            acc = jnp.zeros((HEAD_DIM, lanes), F32)
            for n, vp in enumerate(v_pages):
                w = jnp.broadcast_to(p[h:h + 1, n * page:(n + 1) * page], (HEAD_DIM, page))
                acc = acc + vp[h * HEAD_DIM:(h + 1) * HEAD_DIM, :] * w
            total = jnp.sum(acc, axis=1, keepdims=True)
            on_diag = lane == sub + (h - first) * HEAD_DIM
            row = row + jnp.sum(jnp.where(on_diag, total, 0.0), axis=0, keepdims=True)
        tiles.append(row)
    pv = jnp.concatenate(tiles, axis=-1)
    per_lane = lambda t: jnp.sum(jnp.where(mine, t, 0.0), axis=0, keepdims=True)
    return (pv + per_lane(p_new) * v_new) / per_lane(l)


def _rope_tables(pos, heads):
    half = HEAD_DIM // 2
    inv_freq = ROPE_THETA ** (-jnp.arange(half, dtype=F32) / half)
    ang = pos.astype(F32)[:, None] * inv_freq[None, :]
    cos = jnp.cos(ang)
    sin = jnp.sin(ang)
    cos_full = jnp.tile(jnp.concatenate([cos, cos], axis=-1), (1, heads))
    sin_signed = jnp.tile(jnp.concatenate([-sin, sin], axis=-1), (1, heads))
    return cos_full, sin_signed


def kernel(x_prompt, x_sample, cache_k, cache_v, cache_mem_k, cache_mem_v, state_pool, page_table, mem_prompt,
           g_mix, w_in, w_pool, s_pool, w_out, g_cross, g_mem, w_cq, w_ck, w_cv, w_co, g_mlp, w_up, w_down,
           g_final):
    depth = w_in.shape[0]
    assert depth == 1, "one decoder layer"
    bp, seq, d = x_prompt.shape
    bs, dec_seq, _ = x_sample.shape
    assert dec_seq == 1 and seq % MOBA_BLOCK == 0
    n_phys, page = cache_k.shape[1], cache_k.shape[2]
    n_pages = page_table.shape[1]
    past_len = n_pages * page
    assert past_len % MOBA_BLOCK == 0 and past_len >= POOL_STATE
    mem_len = mem_prompt.shape[1]

    l = 0
    row = lambda a: a.reshape(1, -1)
    lp = dict(w_out=w_out[l].astype(BF16), g_cross=row(g_cross[l]), w_cq=w_cq[l].astype(BF16),
              w_co=w_co[l].astype(BF16), g_mlp=row(g_mlp[l]), w_up=w_up[l].astype(BF16),
              w_down=w_down[l].astype(BF16), g_final=row(g_final))
    w_in_b = w_in[l].astype(BF16)
    w_pool_b = w_pool[l].astype(BF16)
    g_mix_r = row(g_mix[l])
    s_pool_r = row(s_pool[l])

    lane_heads = 128 // HEAD_DIM
    cos_p, sin_p = _rope_tables(jnp.arange(seq), lane_heads)
    qtb, kt_p, vt_p, kb, vtb, kmean, po_p, last_p = _proj_prompt(
        x_prompt, g_mix_r, w_in_b, cos_p, sin_p, cos_p[:, :HEAD_DIM].T, sin_p[:, :HEAD_DIM].T, w_pool_b, s_pool_r)
    cos_s, sin_s = _rope_tables(jnp.full((1,), past_len), N_HEADS)
    state_t = jnp.transpose(state_pool[l], (1, 0, 2))
    q_s, k_s, v_s, kt_s, vt_s, u_s, po_s = _proj_sample(x_sample.reshape(bs, d), g_mix_r, w_in_b, cos_s, sin_s,
                                                        state_t, w_pool_b, s_pool_r, past_len)

    tok = lambda a: a.reshape(bs, 1, -1)
    pages_t = lambda c: jnp.transpose(c, (0, 2, 3, 1)).reshape(n_phys, ATTN_WIDTH, page)
    at_p, at_s = _moba(qtb, kb, vtb, kmean.reshape(bp, seq // MOBA_BLOCK, ATTN_WIDTH), page_table, tok(q_s),
                       tok(k_s), tok(v_s), pages_t(cache_k[l]), pages_t(cache_v[l]))

    mk_p, mv_p, mkb, mvb = _mem_kv(mem_prompt.reshape(bp * mem_len, d), row(g_mem[l]), w_ck[l].astype(BF16),
                                   w_cv[l].astype(BF16))
    flat = lambda a: a.reshape(-1, a.shape[-1])
    mix = lambda x, po, at, memory=None: _mix_out(flat(x), flat(po), flat(at), lp["w_out"], lp["g_cross"],
                                                  lp["w_cq"], memory)
    tail = (lp["w_co"], lp["g_mlp"], lp["w_up"], lp["w_down"], lp["g_final"])
    x1_p, o_p = mix(x_prompt, po_p, at_p, (mkb.reshape(bp, mem_len, d), mvb.reshape(bp, mem_len, d)))
    x1_s, qc_s = mix(x_sample, po_s, at_s)
    y_prompt, o_s = _mlp(x1_p, o_p, *tail, sample_cross=(qc_s, cache_mem_k[l], cache_mem_v[l]))
    y_prompt = y_prompt.reshape(bp, seq, d)
    y_sample = _mlp(x1_s, o_s, *tail).reshape(bs, 1, d)

    def heads(t, b_):
        return jnp.transpose(t.reshape(b_, N_HEADS, HEAD_DIM, -1), (0, 3, 1, 2))[None]

    mem_heads = lambda a: a.reshape(1, bp, mem_len, MEM_HEADS, MEM_HEAD_DIM)
    pool_sample = jnp.transpose(jnp.concatenate([state_t[1:], u_s[None]], axis=0), (1, 0, 2))[None]
    return (y_prompt, y_sample, heads(kt_p, bp), heads(vt_p, bp), last_p[:, 1:][None], mem_heads(mk_p),
            mem_heads(mv_p), heads(kt_s.T.reshape(bs, ATTN_WIDTH, 1), bs), heads(vt_s.T.reshape(bs, ATTN_WIDTH, 1), bs),
            pool_sample)
```

```python
import functools

import jax
import jax.numpy as jnp
from jax import lax
from jax.experimental import pallas as pl
from jax.experimental.pallas import tpu as pltpu

D_MODEL = 1024
POOL_WIDTH = 512
POOL_WINDOWS = (2, 4, 8, 16)
POOL_GC = POOL_WIDTH // len(POOL_WINDOWS)
POOL_STATE = max(POOL_WINDOWS) - 1
POOL_HIST = POOL_STATE + 1
ATTN_WIDTH = D_MODEL - POOL_WIDTH
HEAD_DIM = 64
N_HEADS = ATTN_WIDTH // HEAD_DIM
MOBA_BLOCK = 256
MOBA_TOPK = 3
ROPE_THETA = 10000.0
MEM_HEADS = 4
MEM_HEAD_DIM = D_MODEL // MEM_HEADS
D_FF = 4 * D_MODEL
FF_CHUNK = 1024
EPS = 1e-6
NEG = -1e30

VMEM_LIMIT_BYTES = 56 * 1024 * 1024

F32 = jnp.float32
BF16 = jnp.bfloat16
NT_DIMS = (((1,), (1,)), ((), ()))


def _params(*semantics):
    return pltpu.CompilerParams(dimension_semantics=semantics, vmem_limit_bytes=VMEM_LIMIT_BYTES)


def _resident(shape):
    zeros = (0,) * len(shape)
    return pl.BlockSpec(shape, lambda *_: zeros, pipeline_mode=pl.Buffered(1))


def _rms(x, g):
    return x * lax.rsqrt(jnp.mean(x * x, axis=-1, keepdims=True) + EPS) * g


def _rope(t, cos, sin_signed):
    width = t.shape[-1]
    lane = lax.broadcasted_iota(jnp.int32, t.shape, t.ndim - 1)
    first_half = (lane % HEAD_DIM) < (HEAD_DIM // 2)
    partner = jnp.where(first_half,
                        pltpu.roll(t, width - HEAD_DIM // 2, t.ndim - 1),
                        pltpu.roll(t, HEAD_DIM // 2, t.ndim - 1))
    return t * cos + partner * sin_signed


def _rope_t(t, cos, sin_signed):
    rows = t.shape[0]
    row = lax.broadcasted_iota(jnp.int32, t.shape, 0)
    first_half = (row % HEAD_DIM) < (HEAD_DIM // 2)
    partner = jnp.where(first_half, pltpu.roll(t, rows - HEAD_DIM // 2, 0), pltpu.roll(t, HEAD_DIM // 2, 0))
    return t * cos + partner * sin_signed


def _topk_member(s, valid, idx, axis):
    n = s.shape[axis]
    s = jnp.where(valid, s, NEG)
    rank = jnp.zeros(s.shape, jnp.int32)
    for j in range(n):
        sj = lax.slice_in_dim(s, j, j + 1, axis=axis)
        beats = (sj > s) | ((sj == s) & (j < idx))
        rank = rank + beats.astype(jnp.int32)
    return (rank < MOBA_TOPK) & valid


def _proj_prompt_kernel(x_ref, g_ref, w_ref, cos_ref, sin_ref, cost_ref, sint_ref, wp_ref, sp_ref,
                        qt_ref, kt_ref, vt_ref, kb_ref, vtb_ref, km_ref, po_ref, last_ref, ext_ref):
    i = pl.program_id(1)
    tm = x_ref.shape[1]

    @pl.when(i == 0)
    def _():
        ext_ref[0:POOL_HIST, :] = jnp.zeros((POOL_HIST, POOL_WIDTH), F32)

    h = _rms(x_ref[0], g_ref[...]).astype(BF16)

    def project(first):
        return jnp.dot(h, w_ref[:, first:first + ATTN_WIDTH], preferred_element_type=F32)

    lane_reps = ATTN_WIDTH // cos_ref.shape[1]
    u = project(0)
    k = _rope(project(POOL_WIDTH + ATTN_WIDTH), jnp.tile(cos_ref[...], (1, lane_reps)),
              jnp.tile(sin_ref[...], (1, lane_reps)))
    q = project(POOL_WIDTH)
    v = project(POOL_WIDTH + 2 * ATTN_WIDTH)
    qt = _rope_t(q.T, jnp.tile(cost_ref[...], (N_HEADS, 1)), jnp.tile(sint_ref[...], (N_HEADS, 1)))
    qt_ref[0] = (qt * (HEAD_DIM ** -0.5 * LOG2_E)).astype(BF16)
    vt = v.T
    kt_ref[0] = k.T
    vt_ref[0] = vt
    kb_ref[0] = k.astype(BF16)
    vtb_ref[0, 0] = vt.astype(BF16)
    km_ref[0, 0] = jnp.mean(k, axis=0, keepdims=True)

    pos = i * tm + lax.broadcasted_iota(jnp.int32, (tm, 1), 0)
    for g, w in enumerate(POOL_WINDOWS):
        cols = slice(g * POOL_GC, (g + 1) * POOL_GC)
        ug = u[:, cols]
        wsum = jnp.concatenate([ext_ref[:, cols], ug], axis=0)
        shift = 1
        while shift < w:
            wsum = wsum + pltpu.roll(wsum, shift, 0)
            shift *= 2
        cnt = jnp.minimum(w, pos + 1).astype(F32)
        d = wsum[POOL_HIST:, :] / cnt - ug
        mixed = jnp.dot(d.astype(BF16), wp_ref[g], preferred_element_type=F32)
        po_ref[0, :, cols] = (mixed * sp_ref[:, cols]).astype(BF16)
    tail = u[tm - POOL_HIST:, :]
    ext_ref[...] = tail
    last_ref[0] = tail


def _proj_prompt(x, g, w_in, cos, sin, cos_t, sin_t, w_pool, s_pool):
    b, s, d = x.shape
    tm = MOBA_BLOCK
    nq = s // tm
    row = lambda bi, i: (bi, i, 0)
    col = lambda bi, i: (bi, 0, i)
    blk = lambda bi, i: (bi, i, 0, 0)
    return pl.pallas_call(
        _proj_prompt_kernel,
        grid=(b, nq),
        in_specs=[
            pl.BlockSpec((1, tm, d), row),
            _resident((1, d)),
            _resident(w_in.shape),
            pl.BlockSpec((tm, cos.shape[1]), lambda bi, i: (i, 0)),
            pl.BlockSpec((tm, cos.shape[1]), lambda bi, i: (i, 0)),
            pl.BlockSpec((HEAD_DIM, tm), lambda bi, i: (0, i)),
            pl.BlockSpec((HEAD_DIM, tm), lambda bi, i: (0, i)),
            _resident(w_pool.shape),
            _resident((1, POOL_WIDTH)),
        ],
        out_specs=[
            pl.BlockSpec((1, ATTN_WIDTH, tm), col),
            pl.BlockSpec((1, ATTN_WIDTH, tm), col),
            pl.BlockSpec((1, ATTN_WIDTH, tm), col),
            pl.BlockSpec((1, tm, ATTN_WIDTH), row),
            pl.BlockSpec((1, 1, ATTN_WIDTH, tm), blk),
            pl.BlockSpec((1, 1, 1, ATTN_WIDTH), blk),
            pl.BlockSpec((1, tm, POOL_WIDTH), row),
            pl.BlockSpec((1, POOL_HIST, POOL_WIDTH), lambda bi, i: (bi, 0, 0)),
        ],
        out_shape=[
            jax.ShapeDtypeStruct((b, ATTN_WIDTH, s), BF16),
            jax.ShapeDtypeStruct((b, ATTN_WIDTH, s), F32),
            jax.ShapeDtypeStruct((b, ATTN_WIDTH, s), F32),
            jax.ShapeDtypeStruct((b, s, ATTN_WIDTH), BF16),
            jax.ShapeDtypeStruct((b, nq, ATTN_WIDTH, tm), BF16),
            jax.ShapeDtypeStruct((b, nq, 1, ATTN_WIDTH), F32),
            jax.ShapeDtypeStruct((b, s, POOL_WIDTH), BF16),
            jax.ShapeDtypeStruct((b, POOL_HIST, POOL_WIDTH), F32),
        ],
        scratch_shapes=[pltpu.VMEM((POOL_HIST, POOL_WIDTH), F32)],
        compiler_params=_params("arbitrary", "arbitrary"),
        name="proj_prompt",
    )(x, g, w_in, cos, sin, cos_t, sin_t, w_pool, s_pool)


SUBLANES = 8
LOGIT_AHEAD = 2
LOGIT_SLOTS = 4
LOG2_E = 1.4426950408889634
KEY_RING = 3
VALUE_RING = 2


def _query_block(visit, n):
    return jnp.where(visit % 2 == 0, visit // 2, n - 1 - visit // 2)


def _all_sublanes(t, combine):
    for s in (4, 2, 1):
        t = combine(t, pltpu.roll(t, s, 0))
    return t


def _moba_kernel(pt_ref, qt_ref, kb_ref, vtb_ref, km_ref, qs_ref, qn_ref, kn_ref, vn_ref, ck_ref, cv_ref,
                 o_ref, os_ref, acc_ref, m_ref, l_ref, take_ref, lg_ref, kbuf_ref, vbuf_ref, slg_ref, key_sem, val_sem,
                 *, n_pages, per_step):
    nq = pl.num_programs(1)
    i = _query_block(pl.program_id(1), nq)
    step = pl.program_id(0) * nq + pl.program_id(1)
    n_steps = pl.num_programs(0) * nq
    key_ring = kbuf_ref.shape[0]
    val_ring = vbuf_ref.shape[0]
    tq = qt_ref.shape[2]
    nb = km_ref.shape[1]

    def page_copies(cache_ref, buf_ref, sem, page_id, part):
        return [pltpu.make_async_copy(cache_ref.at[page_id(s, p)], buf_ref.at[part, s * n_pages + p], sem.at[part])
                for s in range(per_step) for p in range(n_pages)]

    def start_keys(t):
        for c in page_copies(ck_ref, kbuf_ref, key_sem, lambda s, p: pt_ref[t * per_step + s, p], t % key_ring):
            c.start()

    def start_values(t):
        for c in page_copies(cv_ref, vbuf_ref, val_sem, lambda s, p: pt_ref[t * per_step + s, p], t % val_ring):
            c.start()

    def wait_part(cache_ref, buf_ref, sem, part):
        pltpu.make_async_copy(cache_ref.at[pl.ds(0, per_step * n_pages)], buf_ref.at[part], sem.at[part]).wait()

    def key_logits(q_rows, part):
        return [[_sample_page_logits(q_rows[s], kbuf_ref.at[part, s * n_pages + p]) for p in range(n_pages)]
                for s in range(per_step)]

    @pl.when(step == 0)
    def _():
        start_keys(0)
        for t in range(min(2, n_steps)):
            start_values(t)
        for t in range(1, min(key_ring, n_steps)):
            start_keys(t)
        wait_part(ck_ref, kbuf_ref, key_sem, 0)
        first = key_logits([_sample_queries(qs_ref[s]) for s in range(per_step)], 0)
        for s in range(per_step):
            slg_ref[s] = jnp.concatenate(first[s], axis=-1)
        if n_steps > key_ring:
            start_keys(key_ring)

    @pl.when((step >= 1) & (step + 1 < n_steps))
    def _():
        start_values(step + 1)

    @pl.when((step >= 1) & (step + key_ring < n_steps))
    def _():
        start_keys(step + key_ring)

    @pl.when(step + 1 < n_steps)
    def _():
        wait_part(ck_ref, kbuf_ref, key_sem, (step + 1) % key_ring)

    wait_part(cv_ref, vbuf_ref, val_sem, step % val_ring)
    slot = step % val_ring
    next_keys = (step + 1) % key_ring


    km_rows = jnp.concatenate([km_ref[0]] * N_HEADS, axis=0)
    row_head = lax.broadcasted_iota(jnp.int32, km_rows.shape, 0) // nb
    lane_head = lax.broadcasted_iota(jnp.int32, km_rows.shape, 1) // HEAD_DIM
    km_heads = jnp.where(row_head == lane_head, km_rows, 0.0).astype(BF16)
    s = jnp.dot(km_heads, qt_ref[0], preferred_element_type=F32).reshape(N_HEADS, nb, tq)
    blk = lax.broadcasted_iota(jnp.int32, s.shape, 1)
    sel = _topk_member(s, blk < i, blk, axis=1).astype(F32)

    for h in range(N_HEADS):
        for jb in range(nb):
            take_ref[h, jb] = jnp.broadcast_to(sel[h, jb:jb + 1, :], (SUBLANES, tq))

    def head_rows(h):
        return slice(h * HEAD_DIM, (h + 1) * HEAD_DIM)

    def stage_logits(h, j):
        off = pl.multiple_of(j * MOBA_BLOCK, MOBA_BLOCK)
        lg_ref[h % LOGIT_SLOTS] = jnp.dot(kb_ref[0, pl.ds(off, MOBA_BLOCK), head_rows(h)],
                                          qt_ref[0, head_rows(h), :], preferred_element_type=F32)

    def weighted_values(h, j, p3):
        p = p3.reshape(MOBA_BLOCK, tq).astype(BF16)
        pv = jnp.dot(vtb_ref[0, j, head_rows(h), :], p, preferred_element_type=F32)
        return pv.reshape(HEAD_DIM // SUBLANES, SUBLANES, tq)

    def tiles(t):
        return t.reshape(t.shape[0] // SUBLANES, SUBLANES, tq)

    def block_pass(j, j_next, own, between=None):
        for h in range(N_HEADS):
            ahead = h + LOGIT_AHEAD
            if ahead < N_HEADS:
                stage_logits(ahead, j)
            else:
                stage_logits(ahead - N_HEADS, j_next)
            if between is not None:
                between(h)
            lg = lg_ref[h % LOGIT_SLOTS]
            if own:
                kk = lax.broadcasted_iota(jnp.int32, lg.shape, 0)
                qq = lax.broadcasted_iota(jnp.int32, lg.shape, 1)
                lg3 = tiles(jnp.where(kk <= qq, lg, NEG))
                m = _all_sublanes(jnp.max(lg3, axis=0), jnp.maximum)
                p3 = jnp.exp2(lg3 - m[None])
                m_ref[h] = m
                l_ref[h] = _all_sublanes(jnp.sum(p3, axis=0), jnp.add)
                acc_ref[head_rows(h), :] = weighted_values(h, j, p3).reshape(HEAD_DIM, tq)
            else:
                lg3 = tiles(lg)
                take = take_ref[h, j] > 0.0
                m_old = m_ref[h]
                l_old = l_ref[h]
                acc_old = tiles(acc_ref[head_rows(h), :])
                m = jnp.maximum(m_old, _all_sublanes(jnp.max(lg3, axis=0), jnp.maximum))
                a = jnp.exp2(m_old - m)
                p3 = jnp.exp2(lg3 - m[None])
                l = a * l_old + _all_sublanes(jnp.sum(p3, axis=0), jnp.add)
                acc3 = a[None] * acc_old + weighted_values(h, j, p3)
                m_ref[h] = jnp.where(take, m, m_old)
                l_ref[h] = jnp.where(take, l, l_old)
                acc_ref[head_rows(h), :] = jnp.where(take[None], acc3, acc_old).reshape(HEAD_DIM, tq)

    last_past = jnp.maximum(i - 1, 0)
    for h in range(LOGIT_AHEAD):
        stage_logits(h, i)

    for s in range(per_step):
        v_pages = [vbuf_ref.at[slot, s * n_pages + p] for p in range(n_pages)]
        os_ref[s] = _sample_finish(_sample_queries(qs_ref[s]), slg_ref[s], kn_ref[s], vn_ref[s], v_pages).astype(BF16)

    next_q = [_sample_queries(qn_ref[s]) for s in range(per_step)]
    jobs = [(s, p) for s in range(per_step) for p in range(n_pages)]
    per_head = -(-len(jobs) // N_HEADS)
    page_logits = {}

    def sample_page_logits(h):
        for s, p in jobs[h * per_head:(h + 1) * per_head]:
            page_logits[s, p] = _sample_page_logits(next_q[s], kbuf_ref.at[next_keys, s * n_pages + p])

    block_pass(i, 0, own=True, between=sample_page_logits)
    for s in range(per_step):
        slg_ref[s] = jnp.concatenate([page_logits[s, p] for p in range(n_pages)], axis=-1)

    def two_past_blocks(jj, carry):
        j = 2 * jj
        block_pass(j, j + 1, own=False)
        block_pass(j + 1, jnp.minimum(j + 2, last_past), own=False)
        return carry

    lax.fori_loop(0, i // 2, two_past_blocks, 0)

    @pl.when(i % 2 == 1)
    def _():
        block_pass(last_past, last_past, own=False)

    for h in range(N_HEADS):
        rows = slice(h * HEAD_DIM, (h + 1) * HEAD_DIM)
        acc_ref[rows, :] = (tiles(acc_ref[rows, :]) / l_ref[h][None]).reshape(HEAD_DIM, tq)
    o_ref[0] = acc_ref[...].T.astype(BF16)


def _moba(qtb, kb, vtb, kmean, page_table, q_s, k_s, v_s, cache_kt, cache_vt):
    b, w, s = qtb.shape
    nq = s // MOBA_BLOCK
    bs, n_pages = page_table.shape
    page = cache_kt.shape[2]
    steps = b * nq
    assert bs % steps == 0
    per_step = bs // steps
    step_seq = lambda bi, i, pt: (bi * nq + i, 0, 0)
    tok = pl.BlockSpec((per_step, 1, w), step_seq)
    tok_next = pl.BlockSpec((per_step, 1, w), lambda bi, i, pt: (jnp.minimum(bi * nq + i + 1, steps - 1), 0, 0))
    in_hbm = pl.BlockSpec(memory_space=pl.ANY)
    page_ring = lambda parts: pltpu.VMEM((parts, per_step * n_pages, w, page), cache_kt.dtype)
    grid_spec = pltpu.PrefetchScalarGridSpec(
        num_scalar_prefetch=1,
        grid=(b, nq),
        in_specs=[
            pl.BlockSpec((1, w, MOBA_BLOCK), lambda bi, i, pt: (bi, 0, _query_block(i, nq))),
            pl.BlockSpec((1, s, w), lambda bi, i, pt: (bi, 0, 0)),
            pl.BlockSpec((1, nq, w, MOBA_BLOCK), lambda bi, i, pt: (bi, 0, 0, 0)),
            pl.BlockSpec((1, nq, w), lambda bi, i, pt: (bi, 0, 0)),
            tok, tok_next, tok, tok, in_hbm, in_hbm,
        ],
        out_specs=[pl.BlockSpec((1, MOBA_BLOCK, w), lambda bi, i, pt: (bi, _query_block(i, nq), 0)), tok],
        scratch_shapes=[pltpu.VMEM((w, MOBA_BLOCK), F32),
                        pltpu.VMEM((N_HEADS, SUBLANES, MOBA_BLOCK), F32),
                        pltpu.VMEM((N_HEADS, SUBLANES, MOBA_BLOCK), F32),
                        pltpu.VMEM((N_HEADS, nq, SUBLANES, MOBA_BLOCK), F32),
                        pltpu.VMEM((LOGIT_SLOTS, MOBA_BLOCK, MOBA_BLOCK), F32),
                        page_ring(KEY_RING), page_ring(VALUE_RING),
                        pltpu.VMEM((per_step, N_HEADS, n_pages * page), F32),
                        pltpu.SemaphoreType.DMA((KEY_RING,)), pltpu.SemaphoreType.DMA((VALUE_RING,))],
    )
    return pl.pallas_call(
        functools.partial(_moba_kernel, n_pages=n_pages, per_step=per_step),
        grid_spec=grid_spec,
        out_shape=[jax.ShapeDtypeStruct((b, s, w), BF16), jax.ShapeDtypeStruct((bs, 1, w), BF16)],
        compiler_params=_params("arbitrary", "arbitrary"),
        name="moba",
    )(page_table, qtb, kb, vtb, kmean, q_s, q_s, k_s, v_s, cache_kt, cache_vt)


def _mem_kv_kernel(m_ref, g_ref, wk_ref, wv_ref, mk_ref, mv_ref, mkb_ref, mvb_ref):
    h = _rms(m_ref[...], g_ref[...]).astype(BF16)
    mk = jnp.dot(h, wk_ref[...], preferred_element_type=F32)
    mv = jnp.dot(h, wv_ref[...], preferred_element_type=F32)
    mk_ref[...] = mk
    mv_ref[...] = mv
    mkb_ref[...] = mk.astype(BF16)
    mvb_ref[...] = mv.astype(BF16)


def _mem_kv(mem, g, w_ck, w_cv):
    rows, d = mem.shape
    tm = 512
    row = lambda i: (i, 0)
    return pl.pallas_call(
        _mem_kv_kernel,
        grid=(rows // tm,),
        in_specs=[pl.BlockSpec((tm, d), row), _resident((1, d)), _resident(w_ck.shape), _resident(w_cv.shape)],
        out_specs=[pl.BlockSpec((tm, d), row)] * 4,
        out_shape=[jax.ShapeDtypeStruct((rows, d), F32)] * 2 + [jax.ShapeDtypeStruct((rows, d), BF16)] * 2,
        compiler_params=_params("arbitrary"),
        name="mem_kv",
    )(mem, g, w_ck, w_cv)


MIX_CHAINS = 2


def _mix_out_kernel(x_ref, po_ref, at_ref, wo_ref, g_ref, wq_ref, *rest):
    with_memory = len(rest) == 4
    if with_memory:
        mk_ref, mv_ref, x1_ref, out_ref = rest
    else:
        x1_ref, out_ref = rest
    tm = x_ref.shape[0]
    chains = MIX_CHAINS if with_memory else 1
    rows_of = lambda c: slice(c * (tm // chains), (c + 1) * (tm // chains))

    def residual(c):
        rows = rows_of(c)
        cat = jnp.concatenate([po_ref[rows, :], at_ref[rows, :]], axis=-1)
        x1 = x_ref[rows, :] + jnp.dot(cat, wo_ref[...], preferred_element_type=F32)
        x1_ref[rows, :] = x1
        return _rms(x1, g_ref[...]).astype(BF16)

    def queries(hq):
        return (jnp.dot(hq, wq_ref[...], preferred_element_type=F32) * (MEM_HEAD_DIM ** -0.5)).astype(BF16)

    def attend(c, q):
        for h in range(MEM_HEADS):
            sl = slice(h * MEM_HEAD_DIM, (h + 1) * MEM_HEAD_DIM)
            lg = lax.dot_general(q[:, sl], mk_ref[0, :, sl], NT_DIMS, preferred_element_type=F32)
            p = jnp.exp(lg - jnp.max(lg, axis=-1, keepdims=True))
            l = jnp.sum(p, axis=-1, keepdims=True)
            o = jnp.dot(p.astype(BF16), mv_ref[0, :, sl], preferred_element_type=F32)
            out_ref[rows_of(c), sl] = (o / l).astype(BF16)

    if not with_memory:
        out_ref[...] = queries(residual(0))
        return
    q = queries(residual(0))
    for c in range(chains):
        hq_next = residual(c + 1) if c + 1 < chains else None
        attend(c, q)
        if hq_next is not None:
            q = queries(hq_next)


def _mix_out(x, po, at, w_out, g_cross, w_cq, memory=None):
    rows, d = x.shape
    tm = min(512, rows)
    row = lambda i: (i, 0)
    in_specs = [pl.BlockSpec((tm, d), row), pl.BlockSpec((tm, POOL_WIDTH), row), pl.BlockSpec((tm, ATTN_WIDTH), row),
                _resident(w_out.shape), _resident((1, d)), _resident(w_cq.shape)]
    args = [x, po, at, w_out, g_cross, w_cq]
    if memory is not None:
        mk, mv = memory
        tiles_per_batch = rows // mk.shape[0] // tm
        assert tiles_per_batch * tm * mk.shape[0] == rows and tm % MIX_CHAINS == 0
        of_batch = pl.BlockSpec((1,) + mk.shape[1:], lambda i: (i // tiles_per_batch, 0, 0))
        in_specs += [of_batch, of_batch]
        args += [mk, mv]
    return pl.pallas_call(
        _mix_out_kernel,
        grid=(rows // tm,),
        in_specs=in_specs,
        out_specs=[pl.BlockSpec((tm, d), row), pl.BlockSpec((tm, d), row)],
        out_shape=[jax.ShapeDtypeStruct((rows, d), F32), jax.ShapeDtypeStruct((rows, d), BF16)],
        compiler_params=_params("arbitrary"),
        name="mix_out",
    )(*args)


MEM_LANES = 128
MEM_PARTS = MEM_HEAD_DIM // MEM_LANES
MEM_ROWS = MEM_PARTS * MEM_HEADS


def _cross_sample_products(q, mk_ref):
    return lax.dot_general(q, mk_ref[...].astype(BF16), NT_DIMS, preferred_element_type=F32)


def _cross_sample_attend(r, mv_ref):
    n = r.shape[1]
    row = lax.broadcasted_iota(jnp.int32, (MEM_ROWS, n), 0)
    lane_row = lax.broadcasted_iota(jnp.int32, (MEM_ROWS, n), 1) % MEM_ROWS
    own = lane_row == row
    valid = (row < MEM_HEADS) & own

    def other_part(t, lane_shift):
        return pltpu.roll(pltpu.roll(t, MEM_HEADS, 0), lane_shift, 1)

    r = jnp.where(own, r, 0.0)
    lg = jnp.where(valid, r + other_part(r, n - MEM_HEADS), NEG)
    p = jnp.where(valid, jnp.exp(lg - jnp.max(lg, axis=-1, keepdims=True)), 0.0)
    p = p + other_part(p, MEM_HEADS)
    l = jnp.sum(p, axis=-1, keepdims=True)
    return jnp.dot(p.astype(BF16), mv_ref[...].astype(BF16), preferred_element_type=F32) / l


def _stored_rows(qc):
    rows = qc.shape[0]
    return jnp.transpose(qc.reshape(rows, MEM_HEADS, MEM_PARTS, MEM_LANES), (0, 2, 1, 3)).reshape(rows, MEM_ROWS, MEM_LANES)


def _stored_slots(m):
    b, slots = m.shape[:2]
    m = m.reshape(b, slots, MEM_HEADS, MEM_PARTS, MEM_LANES)
    return jnp.transpose(m, (0, 1, 3, 2, 4)).reshape(b, slots * MEM_ROWS, MEM_LANES)


def _unstored_rows(o):
    rows = o.shape[0]
    return jnp.transpose(o.reshape(rows, MEM_PARTS, MEM_HEADS, MEM_LANES), (0, 2, 1, 3)).reshape(rows, -1)


def _mlp_kernel(x1_ref, o_ref, wco_ref, gm_ref, wu_ref, wd_ref, gf_ref, *rest):
    if len(rest) == 1:
        (y_ref,) = rest
        n_seq = 0
    else:
        qs_ref, mk_ref, mv_ref, y_ref, os_ref = rest
        n_seq = qs_ref.shape[0]
    products = [_cross_sample_products(qs_ref[s], mk_ref.at[s]) for s in range(n_seq)]
    x2 = x1_ref[...] + jnp.dot(o_ref[...], wco_ref[...], preferred_element_type=F32)
    hm = _rms(x2, gm_ref[...]).astype(BF16)
    acc = x2
    for c in range(D_FF // FF_CHUNK):
        if c == 1:
            for s in range(n_seq):
                os_ref[s] = _cross_sample_attend(products[s], mv_ref.at[s]).astype(BF16)
        cs = slice(c * FF_CHUNK, (c + 1) * FF_CHUNK)
        a = jnp.dot(hm, wu_ref[:, cs], preferred_element_type=F32)
        a = jnp.square(jnp.maximum(a, 0.0)).astype(BF16)
        acc = acc + jnp.dot(a, wd_ref[cs, :], preferred_element_type=F32)
    y_ref[...] = _rms(acc, gf_ref[...])


def _mlp(x1, o, w_co, g_mlp, w_up, w_down, g_final, sample_cross=None):
    rows, d = x1.shape
    tm = min(512, rows)
    steps = rows // tm
    row = lambda i: (i, 0)
    in_specs = [pl.BlockSpec((tm, d), row), pl.BlockSpec((tm, d), row), _resident(w_co.shape),
                _resident((1, d)), _resident(w_up.shape), _resident(w_down.shape), _resident((1, d))]
    args = [x1, o, w_co, g_mlp, w_up, w_down, g_final]
    out_specs = [pl.BlockSpec((tm, d), row)]
    out_shape = [jax.ShapeDtypeStruct((rows, d), F32)]
    if sample_cross is not None:
        qc, mem_k, mem_v = sample_cross
        b, slots = mem_k.shape[:2]
        assert b % steps == 0
        seq = lambda n: pl.BlockSpec((b // steps, n, MEM_LANES), lambda i: (i, 0, 0))
        in_specs += [seq(MEM_ROWS), seq(slots * MEM_ROWS), seq(slots * MEM_ROWS)]
        args += [_stored_rows(qc), _stored_slots(mem_k), _stored_slots(mem_v)]
        out_specs.append(seq(MEM_ROWS))
        out_shape.append(jax.ShapeDtypeStruct((b, MEM_ROWS, MEM_LANES), BF16))
    outs = pl.pallas_call(
        _mlp_kernel,
        grid=(steps,),
        in_specs=in_specs,
        out_specs=out_specs,
        out_shape=out_shape,
        compiler_params=_params("arbitrary"),
        name="mlp",
    )(*args)
    if sample_cross is None:
        return outs[0]
    return outs[0], _unstored_rows(outs[1])


def _proj_sample_kernel(x_ref, g_ref, w_ref, cos_ref, sin_ref, st_ref, wp_ref, sp_ref,
                        q_ref, k_ref, v_ref, kt_ref, vt_ref, u_ref, po_ref, *, n_prev):
    h = _rms(x_ref[...], g_ref[...]).astype(BF16)
    proj = jnp.dot(h, w_ref[...], preferred_element_type=F32)
    u = proj[:, :POOL_WIDTH]
    cos = cos_ref[...]
    sin = sin_ref[...]
    q_ref[...] = _rope(proj[:, POOL_WIDTH:POOL_WIDTH + ATTN_WIDTH], cos, sin) * (HEAD_DIM ** -0.5)
    k = _rope(proj[:, POOL_WIDTH + ATTN_WIDTH:POOL_WIDTH + 2 * ATTN_WIDTH], cos, sin)
    v = proj[:, POOL_WIDTH + 2 * ATTN_WIDTH:]
    k_ref[...] = k
    v_ref[...] = v
    kt_ref[...] = k.T
    vt_ref[...] = v.T
    u_ref[...] = u
    for g, w in enumerate(POOL_WINDOWS):
        cols = slice(g * POOL_GC, (g + 1) * POOL_GC)
        ug = u[:, cols]
        wsum = ug
        for j in range(1, w):
            wsum = wsum + st_ref[POOL_STATE - j, :, cols]
        d = wsum / float(min(w, n_prev + 1)) - ug
        mixed = jnp.dot(d.astype(BF16), wp_ref[g], preferred_element_type=F32)
        po_ref[:, cols] = (mixed * sp_ref[:, cols]).astype(BF16)


def _proj_sample(x, g, w_in, cos, sin, state_t, w_pool, s_pool, n_prev):
    rows, d = x.shape
    wide = jax.ShapeDtypeStruct((rows, ATTN_WIDTH), F32)
    tall = jax.ShapeDtypeStruct((ATTN_WIDTH, rows), F32)
    full = lambda a: pl.BlockSpec(a.shape, lambda i, nd=len(a.shape): (0,) * nd)
    args = (x, g, w_in, cos, sin, state_t, w_pool, s_pool)
    outs = [wide, wide, wide, tall, tall, wide, jax.ShapeDtypeStruct((rows, POOL_WIDTH), BF16)]
    return pl.pallas_call(
        functools.partial(_proj_sample_kernel, n_prev=n_prev),
        grid=(1,),
        in_specs=[full(a) for a in args],
        out_specs=[full(o) for o in outs],
        out_shape=outs,
        compiler_params=_params("arbitrary"),
        name="proj_sample",
    )(*args)


def _own_lanes():
    head_of_lane = lax.broadcasted_iota(jnp.int32, (N_HEADS, ATTN_WIDTH), 1) // HEAD_DIM
    head = lax.broadcasted_iota(jnp.int32, (N_HEADS, ATTN_WIDTH), 0)
    return head_of_lane == head


def _sample_queries(q):
    return jnp.where(_own_lanes(), q, 0.0)


def _sample_page_logits(q_heads, k_page):
    return jnp.dot(q_heads.astype(BF16), k_page[...].astype(BF16), preferred_element_type=F32)


def _sample_finish(q_heads, lg, k_new, v_new, v_pages):
    page = v_pages[0].shape[1]
    past = lg.shape[1]
    nb = past // MOBA_BLOCK
    mine = _own_lanes()

    s = jnp.concatenate(
        [jnp.sum(lg[:, j * MOBA_BLOCK:(j + 1) * MOBA_BLOCK], axis=-1, keepdims=True) for j in range(nb)], axis=-1)
    blk_id = lax.broadcasted_iota(jnp.int32, (N_HEADS, nb), 1)
    sel = _topk_member(s, blk_id >= 0, blk_id, axis=1)
    key_blk = lax.broadcasted_iota(jnp.int32, (N_HEADS, past), 1) // MOBA_BLOCK
    chosen = jnp.zeros((N_HEADS, past), jnp.bool_)
    for j in range(nb):
        chosen = chosen | ((key_blk == j) & sel[:, j:j + 1])
    lg = jnp.where(chosen, lg, NEG)

    lg_new = jnp.sum(q_heads * k_new, axis=-1, keepdims=True)
    m = jnp.maximum(jnp.max(lg, axis=-1, keepdims=True), lg_new)
    p = jnp.exp(lg - m)
    p_new = jnp.exp(lg_new - m)
    l = jnp.sum(p, axis=-1, keepdims=True) + p_new

    lanes = page
    sub = lax.broadcasted_iota(jnp.int32, (HEAD_DIM, lanes), 0)
    lane = lax.broadcasted_iota(jnp.int32, (HEAD_DIM, lanes), 1)
    tiles = []
    for first in range(0, N_HEADS, lanes // HEAD_DIM):
        row = jnp.zeros((1, lanes), F32)
        for h in range(first, first + lanes // HEAD_DIM):
            acc = jnp.zeros((HEAD_DIM, lanes), F32)
            for n, vp in enumerate(v_pages):
                w = jnp.broadcast_to(p[h:h + 1, n * page:(n + 1) * page], (HEAD_DIM, page))
                acc = acc + vp[h * HEAD_DIM:(h + 1) * HEAD_DIM, :] * w
            total = jnp.sum(acc, axis=1, keepdims=True)
            on_diag = lane == sub + (h - first) * HEAD_DIM
            row = row + jnp.sum(jnp.where(on_diag, total, 0.0), axis=0, keepdims=True)
        tiles.append(row)
    pv = jnp.concatenate(tiles, axis=-1)
    per_lane = lambda t: jnp.sum(jnp.where(mine, t, 0.0), axis=0, keepdims=True)
    return (pv + per_lane(p_new) * v_new) / per_lane(l)


def _rope_tables(pos, heads):
    half = HEAD_DIM // 2
    inv_freq = ROPE_THETA ** (-jnp.arange(half, dtype=F32) / half)
    ang = pos.astype(F32)[:, None] * inv_freq[None, :]
    cos = jnp.cos(ang)
    sin = jnp.sin(ang)
    cos_full = jnp.tile(jnp.concatenate([cos, cos], axis=-1), (1, heads))
    sin_signed = jnp.tile(jnp.concatenate([-sin, sin], axis=-1), (1, heads))
    return cos_full, sin_signed


def kernel(x_prompt, x_sample, cache_k, cache_v, cache_mem_k, cache_mem_v, state_pool, page_table, mem_prompt,
           g_mix, w_in, w_pool, s_pool, w_out, g_cross, g_mem, w_cq, w_ck, w_cv, w_co, g_mlp, w_up, w_down,
           g_final):
    depth = w_in.shape[0]
    assert depth == 1, "one decoder layer"
    bp, seq, d = x_prompt.shape
    bs, dec_seq, _ = x_sample.shape
    assert dec_seq == 1 and seq % MOBA_BLOCK == 0
    n_phys, page = cache_k.shape[1], cache_k.shape[2]
    n_pages = page_table.shape[1]
    past_len = n_pages * page
    assert past_len % MOBA_BLOCK == 0 and past_len >= POOL_STATE
    mem_len = mem_prompt.shape[1]

    l = 0
    row = lambda a: a.reshape(1, -1)
    lp = dict(w_out=w_out[l].astype(BF16), g_cross=row(g_cross[l]), w_cq=w_cq[l].astype(BF16),
              w_co=w_co[l].astype(BF16), g_mlp=row(g_mlp[l]), w_up=w_up[l].astype(BF16),
              w_down=w_down[l].astype(BF16), g_final=row(g_final))
    w_in_b = w_in[l].astype(BF16)
    w_pool_b = w_pool[l].astype(BF16)
    g_mix_r = row(g_mix[l])
    s_pool_r = row(s_pool[l])

    lane_heads = 128 // HEAD_DIM
    cos_p, sin_p = _rope_tables(jnp.arange(seq), lane_heads)
    qtb, kt_p, vt_p, kb, vtb, kmean, po_p, last_p = _proj_prompt(
        x_prompt, g_mix_r, w_in_b, cos_p, sin_p, cos_p[:, :HEAD_DIM].T, sin_p[:, :HEAD_DIM].T, w_pool_b, s_pool_r)
    cos_s, sin_s = _rope_tables(jnp.full((1,), past_len), N_HEADS)
    state_t = jnp.transpose(state_pool[l], (1, 0, 2))
    q_s, k_s, v_s, kt_s, vt_s, u_s, po_s = _proj_sample(x_sample.reshape(bs, d), g_mix_r, w_in_b, cos_s, sin_s,
                                                        state_t, w_pool_b, s_pool_r, past_len)

    tok = lambda a: a.reshape(bs, 1, -1)
    pages_t = lambda c: jnp.transpose(c, (0, 2, 3, 1)).reshape(n_phys, ATTN_WIDTH, page)
    at_p, at_s = _moba(qtb, kb, vtb, kmean.reshape(bp, seq // MOBA_BLOCK, ATTN_WIDTH), page_table, tok(q_s),
                       tok(k_s), tok(v_s), pages_t(cache_k[l]), pages_t(cache_v[l]))

    mk_p, mv_p, mkb, mvb = _mem_kv(mem_prompt.reshape(bp * mem_len, d), row(g_mem[l]), w_ck[l].astype(BF16),
                                   w_cv[l].astype(BF16))
    flat = lambda a: a.reshape(-1, a.shape[-1])
    mix = lambda x, po, at, memory=None: _mix_out(flat(x), flat(po), flat(at), lp["w_out"], lp["g_cross"],
                                                  lp["w_cq"], memory)
    tail = (lp["w_co"], lp["g_mlp"], lp["w_up"], lp["w_down"], lp["g_final"])
    x1_p, o_p = mix(x_prompt, po_p, at_p, (mkb.reshape(bp, mem_len, d), mvb.reshape(bp, mem_len, d)))
    x1_s, qc_s = mix(x_sample, po_s, at_s)
    y_prompt, o_s = _mlp(x1_p, o_p, *tail, sample_cross=(qc_s, cache_mem_k[l], cache_mem_v[l]))
    y_prompt = y_prompt.reshape(bp, seq, d)
    y_sample = _mlp(x1_s, o_s, *tail).reshape(bs, 1, d)

    def heads(t, b_):
        return jnp.transpose(t.reshape(b_, N_HEADS, HEAD_DIM, -1), (0, 3, 1, 2))[None]

    mem_heads = lambda a: a.reshape(1, bp, mem_len, MEM_HEADS, MEM_HEAD_DIM)
    pool_sample = jnp.transpose(jnp.concatenate([state_t[1:], u_s[None]], axis=0), (1, 0, 2))[None]
    return (y_prompt, y_sample, heads(kt_p, bp), heads(vt_p, bp), last_p[:, 1:][None], mem_heads(mk_p),
            mem_heads(mv_p), heads(kt_s.T.reshape(bs, ATTN_WIDTH, 1), bs), heads(vt_s.T.reshape(bs, ATTN_WIDTH, 1), bs),
            pool_sample)
```

```python
import functools

import jax
import jax.numpy as jnp
from jax import lax
from jax.experimental import pallas as pl
from jax.experimental.pallas import tpu as pltpu

D_MODEL = 1024
POOL_WIDTH = 512
POOL_WINDOWS = (2, 4, 8, 16)
POOL_GC = POOL_WIDTH // len(POOL_WINDOWS)
POOL_STATE = max(POOL_WINDOWS) - 1
POOL_HIST = POOL_STATE + 1
ATTN_WIDTH = D_MODEL - POOL_WIDTH
HEAD_DIM = 64
N_HEADS = ATTN_WIDTH // HEAD_DIM
MOBA_BLOCK = 256
MOBA_TOPK = 3
ROPE_THETA = 10000.0
MEM_HEADS = 4
MEM_HEAD_DIM = D_MODEL // MEM_HEADS
D_FF = 4 * D_MODEL
FF_CHUNK = 1024
EPS = 1e-6
NEG = -1e30

VMEM_LIMIT_BYTES = 56 * 1024 * 1024

F32 = jnp.float32
BF16 = jnp.bfloat16
NT_DIMS = (((1,), (1,)), ((), ()))


def _params(*semantics):
    return pltpu.CompilerParams(dimension_semantics=semantics, vmem_limit_bytes=VMEM_LIMIT_BYTES)


def _resident(shape):
    zeros = (0,) * len(shape)
    return pl.BlockSpec(shape, lambda *_: zeros, pipeline_mode=pl.Buffered(1))


def _rms(x, g):
    return x * lax.rsqrt(jnp.mean(x * x, axis=-1, keepdims=True) + EPS) * g


def _rope(t, cos, sin_signed):
    width = t.shape[-1]
    lane = lax.broadcasted_iota(jnp.int32, t.shape, t.ndim - 1)
    first_half = (lane % HEAD_DIM) < (HEAD_DIM // 2)
    partner = jnp.where(first_half,
                        pltpu.roll(t, width - HEAD_DIM // 2, t.ndim - 1),
                        pltpu.roll(t, HEAD_DIM // 2, t.ndim - 1))
    return t * cos + partner * sin_signed


def _rope_t(t, cos, sin_signed):
    rows = t.shape[0]
    row = lax.broadcasted_iota(jnp.int32, t.shape, 0)
    first_half = (row % HEAD_DIM) < (HEAD_DIM // 2)
    partner = jnp.where(first_half, pltpu.roll(t, rows - HEAD_DIM // 2, 0), pltpu.roll(t, HEAD_DIM // 2, 0))
    return t * cos + partner * sin_signed


def _topk_member(s, valid, idx, axis):
    n = s.shape[axis]
    s = jnp.where(valid, s, NEG)
    rank = jnp.zeros(s.shape, jnp.int32)
    for j in range(n):
        sj = lax.slice_in_dim(s, j, j + 1, axis=axis)
        beats = (sj > s) | ((sj == s) & (j < idx))
        rank = rank + beats.astype(jnp.int32)
    return (rank < MOBA_TOPK) & valid


def _proj_prompt_kernel(x_ref, g_ref, w_ref, cos_ref, sin_ref, cost_ref, sint_ref, wp_ref, sp_ref,
                        qt_ref, kt_ref, vt_ref, kb_ref, vtb_ref, km_ref, po_ref, last_ref, ext_ref):
    i = pl.program_id(1)
    tm = x_ref.shape[1]

    @pl.when(i == 0)
    def _():
        ext_ref[0:POOL_HIST, :] = jnp.zeros((POOL_HIST, POOL_WIDTH), F32)

    h = _rms(x_ref[0], g_ref[...]).astype(BF16)

    def project(first):
        return jnp.dot(h, w_ref[:, first:first + ATTN_WIDTH], preferred_element_type=F32)

    lane_reps = ATTN_WIDTH // cos_ref.shape[1]
    u = project(0)
    k = _rope(project(POOL_WIDTH + ATTN_WIDTH), jnp.tile(cos_ref[...], (1, lane_reps)),
              jnp.tile(sin_ref[...], (1, lane_reps)))
    q = project(POOL_WIDTH)
    v = project(POOL_WIDTH + 2 * ATTN_WIDTH)
    qt = _rope_t(q.T, jnp.tile(cost_ref[...], (N_HEADS, 1)), jnp.tile(sint_ref[...], (N_HEADS, 1)))
    qt_ref[0] = (qt * (HEAD_DIM ** -0.5 * LOG2_E)).astype(BF16)
    vt = v.T
    kt_ref[0] = k.T
    vt_ref[0] = vt
    kb_ref[0] = k.astype(BF16)
    vtb_ref[0, 0] = vt.astype(BF16)
    km_ref[0, 0] = jnp.mean(k, axis=0, keepdims=True)

    pos = i * tm + lax.broadcasted_iota(jnp.int32, (tm, 1), 0)
    for g, w in enumerate(POOL_WINDOWS):
        cols = slice(g * POOL_GC, (g + 1) * POOL_GC)
        ug = u[:, cols]
        wsum = jnp.concatenate([ext_ref[:, cols], ug], axis=0)
        shift = 1
        while shift < w:
            wsum = wsum + pltpu.roll(wsum, shift, 0)
            shift *= 2
        cnt = jnp.minimum(w, pos + 1).astype(F32)
        d = wsum[POOL_HIST:, :] / cnt - ug
        mixed = jnp.dot(d.astype(BF16), wp_ref[g], preferred_element_type=F32)
        po_ref[0, :, cols] = (mixed * sp_ref[:, cols]).astype(BF16)
    tail = u[tm - POOL_HIST:, :]
    ext_ref[...] = tail
    last_ref[0] = tail


def _proj_prompt(x, g, w_in, cos, sin, cos_t, sin_t, w_pool, s_pool):
    b, s, d = x.shape
    tm = MOBA_BLOCK
    nq = s // tm
    row = lambda bi, i: (bi, i, 0)
    col = lambda bi, i: (bi, 0, i)
    blk = lambda bi, i: (bi, i, 0, 0)
    return pl.pallas_call(
        _proj_prompt_kernel,
        grid=(b, nq),
        in_specs=[
            pl.BlockSpec((1, tm, d), row),
            _resident((1, d)),
            _resident(w_in.shape),
            pl.BlockSpec((tm, cos.shape[1]), lambda bi, i: (i, 0)),
            pl.BlockSpec((tm, cos.shape[1]), lambda bi, i: (i, 0)),
            pl.BlockSpec((HEAD_DIM, tm), lambda bi, i: (0, i)),
            pl.BlockSpec((HEAD_DIM, tm), lambda bi, i: (0, i)),
            _resident(w_pool.shape),
            _resident((1, POOL_WIDTH)),
        ],
        out_specs=[
            pl.BlockSpec((1, ATTN_WIDTH, tm), col),
            pl.BlockSpec((1, ATTN_WIDTH, tm), col),
            pl.BlockSpec((1, ATTN_WIDTH, tm), col),
            pl.BlockSpec((1, tm, ATTN_WIDTH), row),
            pl.BlockSpec((1, 1, ATTN_WIDTH, tm), blk),
            pl.BlockSpec((1, 1, 1, ATTN_WIDTH), blk),
            pl.BlockSpec((1, tm, POOL_WIDTH), row),
            pl.BlockSpec((1, POOL_HIST, POOL_WIDTH), lambda bi, i: (bi, 0, 0)),
        ],
        out_shape=[
            jax.ShapeDtypeStruct((b, ATTN_WIDTH, s), BF16),
            jax.ShapeDtypeStruct((b, ATTN_WIDTH, s), F32),
            jax.ShapeDtypeStruct((b, ATTN_WIDTH, s), F32),
            jax.ShapeDtypeStruct((b, s, ATTN_WIDTH), BF16),
            jax.ShapeDtypeStruct((b, nq, ATTN_WIDTH, tm), BF16),
            jax.ShapeDtypeStruct((b, nq, 1, ATTN_WIDTH), F32),
            jax.ShapeDtypeStruct((b, s, POOL_WIDTH), BF16),
            jax.ShapeDtypeStruct((b, POOL_HIST, POOL_WIDTH), F32),
        ],
        scratch_shapes=[pltpu.VMEM((POOL_HIST, POOL_WIDTH), F32)],
        compiler_params=_params("arbitrary", "arbitrary"),
        name="proj_prompt",
    )(x, g, w_in, cos, sin, cos_t, sin_t, w_pool, s_pool)


SUBLANES = 8
LOGIT_AHEAD = 2
LOGIT_SLOTS = 4
LOG2_E = 1.4426950408889634
KEY_RING = 3
VALUE_RING = 2


def _query_block(visit, n):
    return jnp.where(visit % 2 == 0, visit // 2, n - 1 - visit // 2)


def _all_sublanes(t, combine):
    for s in (4, 2, 1):
        t = combine(t, pltpu.roll(t, s, 0))
    return t


def _moba_kernel(pt_ref, qt_ref, kb_ref, vtb_ref, km_ref, qs_ref, qn_ref, kn_ref, vn_ref, ck_ref, cv_ref,
                 o_ref, os_ref, acc_ref, m_ref, l_ref, take_ref, lg_ref, kbuf_ref, vbuf_ref, slg_ref, key_sem, val_sem,
                 *, n_pages, per_step):
    nq = pl.num_programs(1)
    i = _query_block(pl.program_id(1), nq)
    step = pl.program_id(0) * nq + pl.program_id(1)
    n_steps = pl.num_programs(0) * nq
    key_ring = kbuf_ref.shape[0]
    val_ring = vbuf_ref.shape[0]
    tq = qt_ref.shape[2]
    nb = km_ref.shape[1]

    def page_copies(cache_ref, buf_ref, sem, page_id, part):
        return [pltpu.make_async_copy(cache_ref.at[page_id(s, p)], buf_ref.at[part, s * n_pages + p], sem.at[part])
                for s in range(per_step) for p in range(n_pages)]

    def start_keys(t):
        for c in page_copies(ck_ref, kbuf_ref, key_sem, lambda s, p: pt_ref[t * per_step + s, p], t % key_ring):
            c.start()

    def start_values(t):
        for c in page_copies(cv_ref, vbuf_ref, val_sem, lambda s, p: pt_ref[t * per_step + s, p], t % val_ring):
            c.start()

    def wait_part(cache_ref, buf_ref, sem, part):
        pltpu.make_async_copy(cache_ref.at[pl.ds(0, per_step * n_pages)], buf_ref.at[part], sem.at[part]).wait()

    def key_logits(q_rows, part):
        return [[_sample_page_logits(q_rows[s], kbuf_ref.at[part, s * n_pages + p]) for p in range(n_pages)]
                for s in range(per_step)]

    @pl.when(step == 0)
    def _():
        start_keys(0)
        for t in range(min(2, n_steps)):
            start_values(t)
        for t in range(1, min(key_ring, n_steps)):
            start_keys(t)
        wait_part(ck_ref, kbuf_ref, key_sem, 0)
        first = key_logits([_sample_queries(qs_ref[s]) for s in range(per_step)], 0)
        for s in range(per_step):
            slg_ref[s] = jnp.concatenate(first[s], axis=-1)
        if n_steps > key_ring:
            start_keys(key_ring)

    @pl.when((step >= 1) & (step + 1 < n_steps))
    def _():
        start_values(step + 1)

    @pl.when((step >= 1) & (step + key_ring < n_steps))
    def _():
        start_keys(step + key_ring)

    @pl.when(step + 1 < n_steps)
    def _():
        wait_part(ck_ref, kbuf_ref, key_sem, (step + 1) % key_ring)

    wait_part(cv_ref, vbuf_ref, val_sem, step % val_ring)
    slot = step % val_ring
    next_keys = (step + 1) % key_ring


    km_rows = jnp.concatenate([km_ref[0]] * N_HEADS, axis=0)
    row_head = lax.broadcasted_iota(jnp.int32, km_rows.shape, 0) // nb
    lane_head = lax.broadcasted_iota(jnp.int32, km_rows.shape, 1) // HEAD_DIM
    km_heads = jnp.where(row_head == lane_head, km_rows, 0.0).astype(BF16)
    s = jnp.dot(km_heads, qt_ref[0], preferred_element_type=F32).reshape(N_HEADS, nb, tq)
    blk = lax.broadcasted_iota(jnp.int32, s.shape, 1)
    sel = _topk_member(s, blk < i, blk, axis=1).astype(F32)

    for h in range(N_HEADS):
        for jb in range(nb):
            take_ref[h, jb] = jnp.broadcast_to(sel[h, jb:jb + 1, :], (SUBLANES, tq))

    def head_rows(h):
        return slice(h * HEAD_DIM, (h + 1) * HEAD_DIM)

    def stage_logits(h, j):
        off = pl.multiple_of(j * MOBA_BLOCK, MOBA_BLOCK)
        lg_ref[h % LOGIT_SLOTS] = jnp.dot(kb_ref[0, pl.ds(off, MOBA_BLOCK), head_rows(h)],
                                          qt_ref[0, head_rows(h), :], preferred_element_type=F32)

    def weighted_values(h, j, p3):
        p = p3.reshape(MOBA_BLOCK, tq).astype(BF16)
        pv = jnp.dot(vtb_ref[0, j, head_rows(h), :], p, preferred_element_type=F32)
        return pv.reshape(HEAD_DIM // SUBLANES, SUBLANES, tq)

    def tiles(t):
        return t.reshape(t.shape[0] // SUBLANES, SUBLANES, tq)

    def block_pass(j, j_next, own, between=None):
        for h in range(N_HEADS):
            ahead = h + LOGIT_AHEAD
            if ahead < N_HEADS:
                stage_logits(ahead, j)
            else:
                stage_logits(ahead - N_HEADS, j_next)
            if between is not None:
                between(h)
            lg = lg_ref[h % LOGIT_SLOTS]
            if own:
                kk = lax.broadcasted_iota(jnp.int32, lg.shape, 0)
                qq = lax.broadcasted_iota(jnp.int32, lg.shape, 1)
                lg3 = tiles(jnp.where(kk <= qq, lg, NEG))
                m = _all_sublanes(jnp.max(lg3, axis=0), jnp.maximum)
                p3 = jnp.exp2(lg3 - m[None])
                m_ref[h] = m
                l_ref[h] = _all_sublanes(jnp.sum(p3, axis=0), jnp.add)
                acc_ref[head_rows(h), :] = weighted_values(h, j, p3).reshape(HEAD_DIM, tq)
            else:
                lg3 = tiles(lg)
                take = take_ref[h, j] > 0.0
                m_old = m_ref[h]
                l_old = l_ref[h]
                acc_old = tiles(acc_ref[head_rows(h), :])
                m = jnp.maximum(m_old, _all_sublanes(jnp.max(lg3, axis=0), jnp.maximum))
                a = jnp.exp2(m_old - m)
                p3 = jnp.exp2(lg3 - m[None])
                l = a * l_old + _all_sublanes(jnp.sum(p3, axis=0), jnp.add)
                acc3 = a[None] * acc_old + weighted_values(h, j, p3)
                m_ref[h] = jnp.where(take, m, m_old)
                l_ref[h] = jnp.where(take, l, l_old)
                acc_ref[head_rows(h), :] = jnp.where(take[None], acc3, acc_old).reshape(HEAD_DIM, tq)

    last_past = jnp.maximum(i - 1, 0)
    for h in range(LOGIT_AHEAD):
        stage_logits(h, i)

    for s in range(per_step):
        v_pages = [vbuf_ref.at[slot, s * n_pages + p] for p in range(n_pages)]
        os_ref[s] = _sample_finish(_sample_queries(qs_ref[s]), slg_ref[s], kn_ref[s], vn_ref[s], v_pages).astype(BF16)

    next_q = [_sample_queries(qn_ref[s]) for s in range(per_step)]
    jobs = [(s, p) for s in range(per_step) for p in range(n_pages)]
    per_head = -(-len(jobs) // N_HEADS)
    page_logits = {}

    def sample_page_logits(h):
        for s, p in jobs[h * per_head:(h + 1) * per_head]:
            page_logits[s, p] = _sample_page_logits(next_q[s], kbuf_ref.at[next_keys, s * n_pages + p])

    block_pass(i, 0, own=True, between=sample_page_logits)
    for s in range(per_step):
        slg_ref[s] = jnp.concatenate([page_logits[s, p] for p in range(n_pages)], axis=-1)

    def two_past_blocks(jj, carry):
        j = 2 * jj
        block_pass(j, j + 1, own=False)
        block_pass(j + 1, jnp.minimum(j + 2, last_past), own=False)
        return carry

    lax.fori_loop(0, i // 2, two_past_blocks, 0)

    @pl.when(i % 2 == 1)
    def _():
        block_pass(last_past, last_past, own=False)

    for h in range(N_HEADS):
        rows = slice(h * HEAD_DIM, (h + 1) * HEAD_DIM)
        acc_ref[rows, :] = (tiles(acc_ref[rows, :]) / l_ref[h][None]).reshape(HEAD_DIM, tq)
    o_ref[0] = acc_ref[...].T.astype(BF16)


def _moba(qtb, kb, vtb, kmean, page_table, q_s, k_s, v_s, cache_kt, cache_vt):
    b, w, s = qtb.shape
    nq = s // MOBA_BLOCK
    bs, n_pages = page_table.shape
    page = cache_kt.shape[2]
    steps = b * nq
    assert bs % steps == 0
    per_step = bs // steps
    step_seq = lambda bi, i, pt: (bi * nq + i, 0, 0)
    tok = pl.BlockSpec((per_step, 1, w), step_seq)
    tok_next = pl.BlockSpec((per_step, 1, w), lambda bi, i, pt: (jnp.minimum(bi * nq + i + 1, steps - 1), 0, 0))
    in_hbm = pl.BlockSpec(memory_space=pl.ANY)
    page_ring = lambda parts: pltpu.VMEM((parts, per_step * n_pages, w, page), cache_kt.dtype)
    grid_spec = pltpu.PrefetchScalarGridSpec(
        num_scalar_prefetch=1,
        grid=(b, nq),
        in_specs=[
            pl.BlockSpec((1, w, MOBA_BLOCK), lambda bi, i, pt: (bi, 0, _query_block(i, nq))),
            pl.BlockSpec((1, s, w), lambda bi, i, pt: (bi, 0, 0)),
            pl.BlockSpec((1, nq, w, MOBA_BLOCK), lambda bi, i, pt: (bi, 0, 0, 0)),
            pl.BlockSpec((1, nq, w), lambda bi, i, pt: (bi, 0, 0)),
            tok, tok_next, tok, tok, in_hbm, in_hbm,
        ],
        out_specs=[pl.BlockSpec((1, MOBA_BLOCK, w), lambda bi, i, pt: (bi, _query_block(i, nq), 0)), tok],
        scratch_shapes=[pltpu.VMEM((w, MOBA_BLOCK), F32),
                        pltpu.VMEM((N_HEADS, SUBLANES, MOBA_BLOCK), F32),
                        pltpu.VMEM((N_HEADS, SUBLANES, MOBA_BLOCK), F32),
                        pltpu.VMEM((N_HEADS, nq, SUBLANES, MOBA_BLOCK), F32),
                        pltpu.VMEM((LOGIT_SLOTS, MOBA_BLOCK, MOBA_BLOCK), F32),
                        page_ring(KEY_RING), page_ring(VALUE_RING),
                        pltpu.VMEM((per_step, N_HEADS, n_pages * page), F32),
                        pltpu.SemaphoreType.DMA((KEY_RING,)), pltpu.SemaphoreType.DMA((VALUE_RING,))],
    )
    return pl.pallas_call(
        functools.partial(_moba_kernel, n_pages=n_pages, per_step=per_step),
        grid_spec=grid_spec,
        out_shape=[jax.ShapeDtypeStruct((b, s, w), BF16), jax.ShapeDtypeStruct((bs, 1, w), BF16)],
        compiler_params=_params("arbitrary", "arbitrary"),
        name="moba",
    )(page_table, qtb, kb, vtb, kmean, q_s, q_s, k_s, v_s, cache_kt, cache_vt)


def _mem_kv_kernel(m_ref, g_ref, wk_ref, wv_ref, mk_ref, mv_ref, mkb_ref, mvb_ref):
    h = _rms(m_ref[...], g_ref[...]).astype(BF16)
    mk = jnp.dot(h, wk_ref[...], preferred_element_type=F32)
    mv = jnp.dot(h, wv_ref[...], preferred_element_type=F32)
    mk_ref[...] = mk
    mv_ref[...] = mv
    mkb_ref[...] = mk.astype(BF16)
    mvb_ref[...] = mv.astype(BF16)


def _mem_kv(mem, g, w_ck, w_cv):
    rows, d = mem.shape
    tm = 512
    row = lambda i: (i, 0)
    return pl.pallas_call(
        _mem_kv_kernel,
        grid=(rows // tm,),
        in_specs=[pl.BlockSpec((tm, d), row), _resident((1, d)), _resident(w_ck.shape), _resident(w_cv.shape)],
        out_specs=[pl.BlockSpec((tm, d), row)] * 4,
        out_shape=[jax.ShapeDtypeStruct((rows, d), F32)] * 2 + [jax.ShapeDtypeStruct((rows, d), BF16)] * 2,
        compiler_params=_params("arbitrary"),
        name="mem_kv",
    )(mem, g, w_ck, w_cv)


MIX_CHAINS = 2


def _mix_out_kernel(x_ref, po_ref, at_ref, wo_ref, g_ref, wq_ref, *rest):
    with_memory = len(rest) == 4
    if with_memory:
        mk_ref, mv_ref, x1_ref, out_ref = rest
    else:
        x1_ref, out_ref = rest
    tm = x_ref.shape[0]
    chains = MIX_CHAINS if with_memory else 1
    rows_of = lambda c: slice(c * (tm // chains), (c + 1) * (tm // chains))

    def residual(c):
        rows = rows_of(c)
        cat = jnp.concatenate([po_ref[rows, :], at_ref[rows, :]], axis=-1)
        x1 = x_ref[rows, :] + jnp.dot(cat, wo_ref[...], preferred_element_type=F32)
        x1_ref[rows, :] = x1
        return _rms(x1, g_ref[...]).astype(BF16)

    def queries(hq):
        return (jnp.dot(hq, wq_ref[...], preferred_element_type=F32) * (MEM_HEAD_DIM ** -0.5)).astype(BF16)

    def attend(c, q):
        for h in range(MEM_HEADS):
            sl = slice(h * MEM_HEAD_DIM, (h + 1) * MEM_HEAD_DIM)
            lg = lax.dot_general(q[:, sl], mk_ref[0, :, sl], NT_DIMS, preferred_element_type=F32)
            p = jnp.exp(lg - jnp.max(lg, axis=-1, keepdims=True))
            l = jnp.sum(p, axis=-1, keepdims=True)
            o = jnp.dot(p.astype(BF16), mv_ref[0, :, sl], preferred_element_type=F32)
            out_ref[rows_of(c), sl] = (o / l).astype(BF16)

    if not with_memory:
        out_ref[...] = queries(residual(0))
        return
    q = queries(residual(0))
    for c in range(chains):
        hq_next = residual(c + 1) if c + 1 < chains else None
        attend(c, q)
        if hq_next is not None:
            q = queries(hq_next)


def _mix_out(x, po, at, w_out, g_cross, w_cq, memory=None):
    rows, d = x.shape
    tm = min(512, rows)
    row = lambda i: (i, 0)
    in_specs = [pl.BlockSpec((tm, d), row), pl.BlockSpec((tm, POOL_WIDTH), row), pl.BlockSpec((tm, ATTN_WIDTH), row),
                _resident(w_out.shape), _resident((1, d)), _resident(w_cq.shape)]
    args = [x, po, at, w_out, g_cross, w_cq]
    if memory is not None:
        mk, mv = memory
        tiles_per_batch = rows // mk.shape[0] // tm
        assert tiles_per_batch * tm * mk.shape[0] == rows and tm % MIX_CHAINS == 0
        of_batch = pl.BlockSpec((1,) + mk.shape[1:], lambda i: (i // tiles_per_batch, 0, 0))
        in_specs += [of_batch, of_batch]
        args += [mk, mv]
    return pl.pallas_call(
        _mix_out_kernel,
        grid=(rows // tm,),
        in_specs=in_specs,
        out_specs=[pl.BlockSpec((tm, d), row), pl.BlockSpec((tm, d), row)],
        out_shape=[jax.ShapeDtypeStruct((rows, d), F32), jax.ShapeDtypeStruct((rows, d), BF16)],
        compiler_params=_params("arbitrary"),
        name="mix_out",
    )(*args)


MEM_LANES = 128
MEM_PARTS = MEM_HEAD_DIM // MEM_LANES
MEM_ROWS = MEM_PARTS * MEM_HEADS


def _tree(combine, items):
    while len(items) > 1:
        items = [combine(items[k], items[k + 1]) if k + 1 < len(items) else items[k] for k in range(0, len(items), 2)]
    return items[0]


def _cross_sample_attend(q, mk_ref, mv_ref):
    slots = mk_ref.shape[0] // MEM_ROWS
    tile = lambda ref, m: ref[m * MEM_ROWS:(m + 1) * MEM_ROWS, :]
    logits = []
    for m in range(slots):
        prod = tile(mk_ref, m) * q
        prod = prod + pltpu.roll(prod, MEM_HEADS, 0)
        logits.append(jnp.sum(prod, axis=-1, keepdims=True))
    top = _tree(jnp.maximum, logits)
    weights = [jnp.exp(lg - top) for lg in logits]
    total = _tree(jnp.add, weights)
    partial = [None] * 4
    for m in range(slots):
        term = weights[m] * tile(mv_ref, m)
        k = m % len(partial)
        partial[k] = term if partial[k] is None else partial[k] + term
    return _tree(jnp.add, partial) / total


def _stored_rows(qc):
    rows = qc.shape[0]
    return jnp.transpose(qc.reshape(rows, MEM_HEADS, MEM_PARTS, MEM_LANES), (0, 2, 1, 3)).reshape(rows, MEM_ROWS, MEM_LANES)


def _stored_slots(m):
    b, slots = m.shape[:2]
    m = m.reshape(b, slots, MEM_HEADS, MEM_PARTS, MEM_LANES)
    return jnp.transpose(m, (0, 1, 3, 2, 4)).reshape(b, slots * MEM_ROWS, MEM_LANES)


def _unstored_rows(o):
    rows = o.shape[0]
    return jnp.transpose(o.reshape(rows, MEM_PARTS, MEM_HEADS, MEM_LANES), (0, 2, 1, 3)).reshape(rows, -1)


def _mlp_kernel(x1_ref, o_ref, wco_ref, gm_ref, wu_ref, wd_ref, gf_ref, *rest):
    if len(rest) == 1:
        (y_ref,) = rest
        n_seq = 0
    else:
        qs_ref, mk_ref, mv_ref, y_ref, os_ref = rest
        n_seq = qs_ref.shape[0]
    chunks = D_FF // FF_CHUNK
    ties = {}
    for s in range(n_seq):
        out = _cross_sample_attend(qs_ref[s].astype(F32), mk_ref.at[s], mv_ref.at[s])
        os_ref[s] = out.astype(BF16)
        ties.setdefault(1 + s * chunks // n_seq, []).append(out)

    def after_ties(stage, t):
        for out in ties.get(stage, []):
            zero = ((pltpu.bitcast(out, jnp.uint32) >> 16) >> 16).astype(F32)
            t = t + jnp.tile(zero[0:1, :], (1, t.shape[1] // zero.shape[1])).astype(t.dtype)
        return t

    x2 = x1_ref[...] + jnp.dot(o_ref[...], wco_ref[...], preferred_element_type=F32)
    hm = _rms(x2, gm_ref[...]).astype(BF16)
    acc = x2
    for c in range(chunks):
        cs = slice(c * FF_CHUNK, (c + 1) * FF_CHUNK)
        a = jnp.dot(after_ties(c, hm), wu_ref[:, cs], preferred_element_type=F32)
        a = jnp.square(jnp.maximum(a, 0.0)).astype(BF16)
        acc = acc + jnp.dot(a, wd_ref[cs, :], preferred_element_type=F32)
    y_ref[...] = _rms(after_ties(chunks, acc), gf_ref[...])


def _mlp(x1, o, w_co, g_mlp, w_up, w_down, g_final, sample_cross=None):
    rows, d = x1.shape
    tm = min(512, rows)
    steps = rows // tm
    row = lambda i: (i, 0)
    in_specs = [pl.BlockSpec((tm, d), row), pl.BlockSpec((tm, d), row), _resident(w_co.shape),
                _resident((1, d)), _resident(w_up.shape), _resident(w_down.shape), _resident((1, d))]
    args = [x1, o, w_co, g_mlp, w_up, w_down, g_final]
    out_specs = [pl.BlockSpec((tm, d), row)]
    out_shape = [jax.ShapeDtypeStruct((rows, d), F32)]
    if sample_cross is not None:
        qc, mem_k, mem_v = sample_cross
        b, slots = mem_k.shape[:2]
        assert b % steps == 0
        seq = lambda n: pl.BlockSpec((b // steps, n, MEM_LANES), lambda i: (i, 0, 0))
        in_specs += [seq(MEM_ROWS), seq(slots * MEM_ROWS), seq(slots * MEM_ROWS)]
        args += [_stored_rows(qc), _stored_slots(mem_k), _stored_slots(mem_v)]
        out_specs.append(seq(MEM_ROWS))
        out_shape.append(jax.ShapeDtypeStruct((b, MEM_ROWS, MEM_LANES), BF16))
    outs = pl.pallas_call(
        _mlp_kernel,
        grid=(steps,),
        in_specs=in_specs,
        out_specs=out_specs,
        out_shape=out_shape,
        compiler_params=_params("arbitrary"),
        name="mlp",
    )(*args)
    if sample_cross is None:
        return outs[0]
    return outs[0], _unstored_rows(outs[1])


def _proj_sample_kernel(x_ref, g_ref, w_ref, cos_ref, sin_ref, st_ref, wp_ref, sp_ref,
                        q_ref, k_ref, v_ref, kt_ref, vt_ref, u_ref, po_ref, *, n_prev):
    h = _rms(x_ref[...], g_ref[...]).astype(BF16)
    proj = jnp.dot(h, w_ref[...], preferred_element_type=F32)
    u = proj[:, :POOL_WIDTH]
    cos = cos_ref[...]
    sin = sin_ref[...]
    q_ref[...] = _rope(proj[:, POOL_WIDTH:POOL_WIDTH + ATTN_WIDTH], cos, sin) * (HEAD_DIM ** -0.5)
    k = _rope(proj[:, POOL_WIDTH + ATTN_WIDTH:POOL_WIDTH + 2 * ATTN_WIDTH], cos, sin)
    v = proj[:, POOL_WIDTH + 2 * ATTN_WIDTH:]
    k_ref[...] = k
    v_ref[...] = v
    kt_ref[...] = k.T
    vt_ref[...] = v.T
    u_ref[...] = u
    for g, w in enumerate(POOL_WINDOWS):
        cols = slice(g * POOL_GC, (g + 1) * POOL_GC)
        ug = u[:, cols]
        wsum = ug
        for j in range(1, w):
            wsum = wsum + st_ref[POOL_STATE - j, :, cols]
        d = wsum / float(min(w, n_prev + 1)) - ug
        mixed = jnp.dot(d.astype(BF16), wp_ref[g], preferred_element_type=F32)
        po_ref[:, cols] = (mixed * sp_ref[:, cols]).astype(BF16)


def _proj_sample(x, g, w_in, cos, sin, state_t, w_pool, s_pool, n_prev):
    rows, d = x.shape
    wide = jax.ShapeDtypeStruct((rows, ATTN_WIDTH), F32)
    tall = jax.ShapeDtypeStruct((ATTN_WIDTH, rows), F32)
    full = lambda a: pl.BlockSpec(a.shape, lambda i, nd=len(a.shape): (0,) * nd)
    args = (x, g, w_in, cos, sin, state_t, w_pool, s_pool)
    outs = [wide, wide, wide, tall, tall, wide, jax.ShapeDtypeStruct((rows, POOL_WIDTH), BF16)]
    return pl.pallas_call(
        functools.partial(_proj_sample_kernel, n_prev=n_prev),
        grid=(1,),
        in_specs=[full(a) for a in args],
        out_specs=[full(o) for o in outs],
        out_shape=outs,
        compiler_params=_params("arbitrary"),
        name="proj_sample",
    )(*args)


def _own_lanes():
    head_of_lane = lax.broadcasted_iota(jnp.int32, (N_HEADS, ATTN_WIDTH), 1) // HEAD_DIM
    head = lax.broadcasted_iota(jnp.int32, (N_HEADS, ATTN_WIDTH), 0)
    return head_of_lane == head


def _sample_queries(q):
    return jnp.where(_own_lanes(), q, 0.0)


def _sample_page_logits(q_heads, k_page):
    return jnp.dot(q_heads.astype(BF16), k_page[...].astype(BF16), preferred_element_type=F32)


def _sample_finish(q_heads, lg, k_new, v_new, v_pages):
    page = v_pages[0].shape[1]
    past = lg.shape[1]
    nb = past // MOBA_BLOCK
    mine = _own_lanes()

    s = jnp.concatenate(
        [jnp.sum(lg[:, j * MOBA_BLOCK:(j + 1) * MOBA_BLOCK], axis=-1, keepdims=True) for j in range(nb)], axis=-1)
    blk_id = lax.broadcasted_iota(jnp.int32, (N_HEADS, nb), 1)
    sel = _topk_member(s, blk_id >= 0, blk_id, axis=1)
    key_blk = lax.broadcasted_iota(jnp.int32, (N_HEADS, past), 1) // MOBA_BLOCK
    chosen = jnp.zeros((N_HEADS, past), jnp.bool_)
    for j in range(nb):
        chosen = chosen | ((key_blk == j) & sel[:, j:j + 1])
    lg = jnp.where(chosen, lg, NEG)

    lg_new = jnp.sum(q_heads * k_new, axis=-1, keepdims=True)
    m = jnp.maximum(jnp.max(lg, axis=-1, keepdims=True), lg_new)
    p = jnp.exp(lg - m)
    p_new = jnp.exp(lg_new - m)
    l = jnp.sum(p, axis=-1, keepdims=True) + p_new

    lanes = page
    sub = lax.broadcasted_iota(jnp.int32, (HEAD_DIM, lanes), 0)
    lane = lax.broadcasted_iota(jnp.int32, (HEAD_DIM, lanes), 1)
    tiles = []
    for first in range(0, N_HEADS, lanes // HEAD_DIM):
        row = jnp.zeros((1, lanes), F32)
        for h in range(first, first + lanes // HEAD_DIM):
            acc = jnp.zeros((HEAD_DIM, lanes), F32)
            for n, vp in enumerate(v_pages):
                w = jnp.broadcast_to(p[h:h + 1, n * page:(n + 1) * page], (HEAD_DIM, page))
                acc = acc + vp[h * HEAD_DIM:(h + 1) * HEAD_DIM, :] * w
            total = jnp.sum(acc, axis=1, keepdims=True)
            on_diag = lane == sub + (h - first) * HEAD_DIM
            row = row + jnp.sum(jnp.where(on_diag, total, 0.0), axis=0, keepdims=True)
        tiles.append(row)
    pv = jnp.concatenate(tiles, axis=-1)
    per_lane = lambda t: jnp.sum(jnp.where(mine, t, 0.0), axis=0, keepdims=True)
    return (pv + per_lane(p_new) * v_new) / per_lane(l)


def _rope_tables(pos, heads):
    half = HEAD_DIM // 2
    inv_freq = ROPE_THETA ** (-jnp.arange(half, dtype=F32) / half)
    ang = pos.astype(F32)[:, None] * inv_freq[None, :]
    cos = jnp.cos(ang)
    sin = jnp.sin(ang)
    cos_full = jnp.tile(jnp.concatenate([cos, cos], axis=-1), (1, heads))
    sin_signed = jnp.tile(jnp.concatenate([-sin, sin], axis=-1), (1, heads))
    return cos_full, sin_signed


def kernel(x_prompt, x_sample, cache_k, cache_v, cache_mem_k, cache_mem_v, state_pool, page_table, mem_prompt,
           g_mix, w_in, w_pool, s_pool, w_out, g_cross, g_mem, w_cq, w_ck, w_cv, w_co, g_mlp, w_up, w_down,
           g_final):
    depth = w_in.shape[0]
    assert depth == 1, "one decoder layer"
    bp, seq, d = x_prompt.shape
    bs, dec_seq, _ = x_sample.shape
    assert dec_seq == 1 and seq % MOBA_BLOCK == 0
    n_phys, page = cache_k.shape[1], cache_k.shape[2]
    n_pages = page_table.shape[1]
    past_len = n_pages * page
    assert past_len % MOBA_BLOCK == 0 and past_len >= POOL_STATE
    mem_len = mem_prompt.shape[1]

    l = 0
    row = lambda a: a.reshape(1, -1)
    lp = dict(w_out=w_out[l].astype(BF16), g_cross=row(g_cross[l]), w_cq=w_cq[l].astype(BF16),
              w_co=w_co[l].astype(BF16), g_mlp=row(g_mlp[l]), w_up=w_up[l].astype(BF16),
              w_down=w_down[l].astype(BF16), g_final=row(g_final))
    w_in_b = w_in[l].astype(BF16)
    w_pool_b = w_pool[l].astype(BF16)
    g_mix_r = row(g_mix[l])
    s_pool_r = row(s_pool[l])

    lane_heads = 128 // HEAD_DIM
    cos_p, sin_p = _rope_tables(jnp.arange(seq), lane_heads)
    qtb, kt_p, vt_p, kb, vtb, kmean, po_p, last_p = _proj_prompt(
        x_prompt, g_mix_r, w_in_b, cos_p, sin_p, cos_p[:, :HEAD_DIM].T, sin_p[:, :HEAD_DIM].T, w_pool_b, s_pool_r)
    cos_s, sin_s = _rope_tables(jnp.full((1,), past_len), N_HEADS)
    state_t = jnp.transpose(state_pool[l], (1, 0, 2))
    q_s, k_s, v_s, kt_s, vt_s, u_s, po_s = _proj_sample(x_sample.reshape(bs, d), g_mix_r, w_in_b, cos_s, sin_s,
                                                        state_t, w_pool_b, s_pool_r, past_len)

    tok = lambda a: a.reshape(bs, 1, -1)
    pages_t = lambda c: jnp.transpose(c, (0, 2, 3, 1)).reshape(n_phys, ATTN_WIDTH, page)
    at_p, at_s = _moba(qtb, kb, vtb, kmean.reshape(bp, seq // MOBA_BLOCK, ATTN_WIDTH), page_table, tok(q_s),
                       tok(k_s), tok(v_s), pages_t(cache_k[l]), pages_t(cache_v[l]))

    mk_p, mv_p, mkb, mvb = _mem_kv(mem_prompt.reshape(bp * mem_len, d), row(g_mem[l]), w_ck[l].astype(BF16),
                                   w_cv[l].astype(BF16))
    flat = lambda a: a.reshape(-1, a.shape[-1])
    mix = lambda x, po, at, memory=None: _mix_out(flat(x), flat(po), flat(at), lp["w_out"], lp["g_cross"],
                                                  lp["w_cq"], memory)
    tail = (lp["w_co"], lp["g_mlp"], lp["w_up"], lp["w_down"], lp["g_final"])
    x1_p, o_p = mix(x_prompt, po_p, at_p, (mkb.reshape(bp, mem_len, d), mvb.reshape(bp, mem_len, d)))
    x1_s, qc_s = mix(x_sample, po_s, at_s)
    y_prompt, o_s = _mlp(x1_p, o_p, *tail, sample_cross=(qc_s, cache_mem_k[l], cache_mem_v[l]))
    y_prompt = y_prompt.reshape(bp, seq, d)
    y_sample = _mlp(x1_s, o_s, *tail).reshape(bs, 1, d)

    def heads(t, b_):
        return jnp.transpose(t.reshape(b_, N_HEADS, HEAD_DIM, -1), (0, 3, 1, 2))[None]

    mem_heads = lambda a: a.reshape(1, bp, mem_len, MEM_HEADS, MEM_HEAD_DIM)
    pool_sample = jnp.transpose(jnp.concatenate([state_t[1:], u_s[None]], axis=0), (1, 0, 2))[None]
    return (y_prompt, y_sample, heads(kt_p, bp), heads(vt_p, bp), last_p[:, 1:][None], mem_heads(mk_p),
            mem_heads(mv_p), heads(kt_s.T.reshape(bs, ATTN_WIDTH, 1), bs), heads(vt_s.T.reshape(bs, ATTN_WIDTH, 1), bs),
            pool_sample)
```

```python
import functools

import jax
import jax.numpy as jnp
from jax import lax
from jax.experimental import pallas as pl
from jax.experimental.pallas import tpu as pltpu

D_MODEL = 1024
POOL_WIDTH = 512
POOL_WINDOWS = (2, 4, 8, 16)
POOL_GC = POOL_WIDTH // len(POOL_WINDOWS)
POOL_STATE = max(POOL_WINDOWS) - 1
POOL_HIST = POOL_STATE + 1
ATTN_WIDTH = D_MODEL - POOL_WIDTH
HEAD_DIM = 64
N_HEADS = ATTN_WIDTH // HEAD_DIM
MOBA_BLOCK = 256
MOBA_TOPK = 3
ROPE_THETA = 10000.0
MEM_HEADS = 4
MEM_HEAD_DIM = D_MODEL // MEM_HEADS
D_FF = 4 * D_MODEL
FF_CHUNK = 1024
EPS = 1e-6
NEG = -1e30

VMEM_LIMIT_BYTES = 56 * 1024 * 1024
LANES = 128

F32 = jnp.float32
BF16 = jnp.bfloat16
NT_DIMS = (((1,), (1,)), ((), ()))


def _params(*semantics):
    return pltpu.CompilerParams(dimension_semantics=semantics, vmem_limit_bytes=VMEM_LIMIT_BYTES)


def _resident(shape):
    zeros = (0,) * len(shape)
    return pl.BlockSpec(shape, lambda *_: zeros, pipeline_mode=pl.Buffered(1))


def _rms(x, g):
    return x * lax.rsqrt(jnp.mean(x * x, axis=-1, keepdims=True) + EPS) * g


def _rope(t, cos, sin_signed):
    width = t.shape[-1]
    lane = lax.broadcasted_iota(jnp.int32, t.shape, t.ndim - 1)
    first_half = (lane % HEAD_DIM) < (HEAD_DIM // 2)
    partner = jnp.where(first_half,
                        pltpu.roll(t, width - HEAD_DIM // 2, t.ndim - 1),
                        pltpu.roll(t, HEAD_DIM // 2, t.ndim - 1))
    return t * cos + partner * sin_signed


def _rope_t(t, cos, sin_signed):
    rows = t.shape[0]
    row = lax.broadcasted_iota(jnp.int32, t.shape, 0)
    first_half = (row % HEAD_DIM) < (HEAD_DIM // 2)
    partner = jnp.where(first_half, pltpu.roll(t, rows - HEAD_DIM // 2, 0), pltpu.roll(t, HEAD_DIM // 2, 0))
    return t * cos + partner * sin_signed


def _topk_member(s, valid, idx, axis):
    n = s.shape[axis]
    s = jnp.where(valid, s, NEG)
    rank = jnp.zeros(s.shape, jnp.int32)
    for j in range(n):
        sj = lax.slice_in_dim(s, j, j + 1, axis=axis)
        beats = (sj > s) | ((sj == s) & (j < idx))
        rank = rank + beats.astype(jnp.int32)
    return (rank < MOBA_TOPK) & valid


PROJ_BLOCKS = 2
def _proj_prompt_kernel(x_ref, g_ref, w_ref, cos_ref, sin_ref, cost_ref, sint_ref, wp_ref, sp_ref,
                        qt_ref, kt_ref, vt_ref, kb_ref, vtb_ref, km_ref, po_ref, last_ref, ext_ref):
    i = pl.program_id(1)
    tm = x_ref.shape[1]

    @pl.when(i == 0)
    def _():
        ext_ref[0:POOL_HIST, :] = jnp.zeros((POOL_HIST, POOL_WIDTH), F32)

    h = _rms(x_ref[0], g_ref[...]).astype(BF16)

    def project(first):
        return jnp.dot(h, w_ref[:, first:first + ATTN_WIDTH], preferred_element_type=F32)

    lane_reps = ATTN_WIDTH // cos_ref.shape[1]
    u = project(0)
    k = _rope(project(POOL_WIDTH + ATTN_WIDTH), jnp.tile(cos_ref[...], (1, lane_reps)),
              jnp.tile(sin_ref[...], (1, lane_reps)))
    q = project(POOL_WIDTH)
    v = project(POOL_WIDTH + 2 * ATTN_WIDTH)
    qt = _rope_t(q.T, jnp.tile(cost_ref[...], (N_HEADS, 1)), jnp.tile(sint_ref[...], (N_HEADS, 1)))
    qt_ref[0] = (qt * (HEAD_DIM ** -0.5 * LOG2_E)).astype(BF16)
    vt = v.T
    kt_ref[0] = k.T
    vt_ref[0] = vt
    kb_ref[0] = k.astype(BF16)
    for c in range(tm // MOBA_BLOCK):
        block = slice(c * MOBA_BLOCK, (c + 1) * MOBA_BLOCK)
        vtb_ref[0, c] = vt[:, block].astype(BF16)
        km_ref[0, c] = jnp.mean(k[block, :], axis=0, keepdims=True)

    pos = i * tm + lax.broadcasted_iota(jnp.int32, (tm, 1), 0)
    for g, w in enumerate(POOL_WINDOWS):
        cols = slice(g * POOL_GC, (g + 1) * POOL_GC)
        ug = u[:, cols]
        wsum = jnp.concatenate([ext_ref[:, cols], ug], axis=0)
        shift = 1
        while shift < w:
            wsum = wsum + pltpu.roll(wsum, shift, 0)
            shift *= 2
        cnt = jnp.minimum(w, pos + 1).astype(F32)
        d = wsum[POOL_HIST:, :] / cnt - ug
        mixed = jnp.dot(d.astype(BF16), wp_ref[g], preferred_element_type=F32)
        po_ref[0, :, cols] = (mixed * sp_ref[:, cols]).astype(BF16)
    tail = u[tm - POOL_HIST:, :]
    ext_ref[...] = tail
    last_ref[0] = tail


def _proj_prompt(x, g, w_in, cos, sin, cos_t, sin_t, w_pool, s_pool):
    b, s, d = x.shape
    per_tile = PROJ_BLOCKS if s % (PROJ_BLOCKS * MOBA_BLOCK) == 0 else 1
    tm = per_tile * MOBA_BLOCK
    nq = s // MOBA_BLOCK
    row = lambda bi, i: (bi, i, 0)
    col = lambda bi, i: (bi, 0, i)
    blk = lambda bi, i: (bi, i, 0, 0)
    return pl.pallas_call(
        _proj_prompt_kernel,
        grid=(b, s // tm),
        in_specs=[
            pl.BlockSpec((1, tm, d), row),
            _resident((1, d)),
            _resident(w_in.shape),
            pl.BlockSpec((tm, cos.shape[1]), lambda bi, i: (i, 0)),
            pl.BlockSpec((tm, cos.shape[1]), lambda bi, i: (i, 0)),
            pl.BlockSpec((HEAD_DIM, tm), lambda bi, i: (0, i)),
            pl.BlockSpec((HEAD_DIM, tm), lambda bi, i: (0, i)),
            _resident(w_pool.shape),
            _resident((1, POOL_WIDTH)),
        ],
        out_specs=[
            pl.BlockSpec((1, ATTN_WIDTH, tm), col),
            pl.BlockSpec((1, ATTN_WIDTH, tm), col),
            pl.BlockSpec((1, ATTN_WIDTH, tm), col),
            pl.BlockSpec((1, tm, ATTN_WIDTH), row),
            pl.BlockSpec((1, per_tile, ATTN_WIDTH, MOBA_BLOCK), blk),
            pl.BlockSpec((1, per_tile, 1, ATTN_WIDTH), blk),
            pl.BlockSpec((1, tm, POOL_WIDTH), row),
            pl.BlockSpec((1, POOL_HIST, POOL_WIDTH), lambda bi, i: (bi, 0, 0)),
        ],
        out_shape=[
            jax.ShapeDtypeStruct((b, ATTN_WIDTH, s), BF16),
            jax.ShapeDtypeStruct((b, ATTN_WIDTH, s), F32),
            jax.ShapeDtypeStruct((b, ATTN_WIDTH, s), F32),
            jax.ShapeDtypeStruct((b, s, ATTN_WIDTH), BF16),
            jax.ShapeDtypeStruct((b, nq, ATTN_WIDTH, MOBA_BLOCK), BF16),
            jax.ShapeDtypeStruct((b, nq, 1, ATTN_WIDTH), F32),
            jax.ShapeDtypeStruct((b, s, POOL_WIDTH), BF16),
            jax.ShapeDtypeStruct((b, POOL_HIST, POOL_WIDTH), F32),
        ],
        scratch_shapes=[pltpu.VMEM((POOL_HIST, POOL_WIDTH), F32)],
        compiler_params=_params("arbitrary", "arbitrary"),
        name="proj_prompt",
    )(x, g, w_in, cos, sin, cos_t, sin_t, w_pool, s_pool)


SUBLANES = 8
LOGIT_AHEAD = 2
LOGIT_SLOTS = 4
LOG2_E = 1.4426950408889634
KEY_RING = 3
VALUE_RING = 2


def _query_block(visit, n):
    return jnp.where(visit % 2 == 0, visit // 2, n - 1 - visit // 2)


def _all_sublanes(t, combine):
    for s in (4, 2, 1):
        t = combine(t, pltpu.roll(t, s, 0))
    return t


def _moba_kernel(pt_ref, qt_ref, kb_ref, vtb_ref, km_ref, qs_ref, qn_ref, kn_ref, vn_ref, ck_ref, cv_ref,
                 o_ref, os_ref, acc_ref, m_ref, l_ref, take_ref, lg_ref, kbuf_ref, vbuf_ref, slg_ref, key_sem, val_sem,
                 *, n_pages, per_step):
    nq = pl.num_programs(1)
    i = _query_block(pl.program_id(1), nq)
    step = pl.program_id(0) * nq + pl.program_id(1)
    n_steps = pl.num_programs(0) * nq
    key_ring = kbuf_ref.shape[0]
    val_ring = vbuf_ref.shape[0]
    tq = qt_ref.shape[2]
    nb = km_ref.shape[1]

    def page_copies(cache_ref, buf_ref, sem, page_id, part):
        return [pltpu.make_async_copy(cache_ref.at[page_id(s, p)], buf_ref.at[part, s * n_pages + p], sem.at[part])
                for s in range(per_step) for p in range(n_pages)]

    def start_keys(t):
        for c in page_copies(ck_ref, kbuf_ref, key_sem, lambda s, p: pt_ref[t * per_step + s, p], t % key_ring):
            c.start()

    def start_values(t):
        for c in page_copies(cv_ref, vbuf_ref, val_sem, lambda s, p: pt_ref[t * per_step + s, p], t % val_ring):
            c.start()

    def wait_part(cache_ref, buf_ref, sem, part):
        pltpu.make_async_copy(cache_ref.at[pl.ds(0, per_step * n_pages)], buf_ref.at[part], sem.at[part]).wait()

    def key_logits(q_rows, part):
        return [[_sample_page_logits(q_rows[s], kbuf_ref.at[part, s * n_pages + p]) for p in range(n_pages)]
                for s in range(per_step)]

    @pl.when(step == 0)
    def _():
        start_keys(0)
        for t in range(min(2, n_steps)):
            start_values(t)
        for t in range(1, min(key_ring, n_steps)):
            start_keys(t)
        wait_part(ck_ref, kbuf_ref, key_sem, 0)
        first = key_logits([_sample_queries(qs_ref[s]) for s in range(per_step)], 0)
        for s in range(per_step):
            slg_ref[s] = jnp.concatenate(first[s], axis=-1)
        if n_steps > key_ring:
            start_keys(key_ring)

    @pl.when((step >= 1) & (step + 1 < n_steps))
    def _():
        start_values(step + 1)

    @pl.when((step >= 1) & (step + key_ring < n_steps))
    def _():
        start_keys(step + key_ring)

    @pl.when(step + 1 < n_steps)
    def _():
        wait_part(ck_ref, kbuf_ref, key_sem, (step + 1) % key_ring)

    wait_part(cv_ref, vbuf_ref, val_sem, step % val_ring)
    slot = step % val_ring
    next_keys = (step + 1) % key_ring


    km_rows = jnp.concatenate([km_ref[0]] * N_HEADS, axis=0)
    row_head = lax.broadcasted_iota(jnp.int32, km_rows.shape, 0) // nb
    lane_head = lax.broadcasted_iota(jnp.int32, km_rows.shape, 1) // HEAD_DIM
    km_heads = jnp.where(row_head == lane_head, km_rows, 0.0).astype(BF16)
    s = jnp.dot(km_heads, qt_ref[0], preferred_element_type=F32).reshape(N_HEADS, nb, tq)
    blk = lax.broadcasted_iota(jnp.int32, s.shape, 1)
    sel = _topk_member(s, blk < i, blk, axis=1).astype(F32)

    for h in range(N_HEADS):
        for jb in range(nb):
            take_ref[h, jb] = jnp.broadcast_to(sel[h, jb:jb + 1, :], (SUBLANES, tq))

    def head_rows(h):
        return slice(h * HEAD_DIM, (h + 1) * HEAD_DIM)

    def stage_logits(h, j):
        off = pl.multiple_of(j * MOBA_BLOCK, MOBA_BLOCK)
        lg_ref[h % LOGIT_SLOTS] = jnp.dot(kb_ref[0, pl.ds(off, MOBA_BLOCK), head_rows(h)],
                                          qt_ref[0, head_rows(h), :], preferred_element_type=F32)

    def weighted_values(h, j, p3):
        p = p3.reshape(MOBA_BLOCK, tq).astype(BF16)
        pv = jnp.dot(vtb_ref[0, j, head_rows(h), :], p, preferred_element_type=F32)
        return pv.reshape(HEAD_DIM // SUBLANES, SUBLANES, tq)

    def tiles(t):
        return t.reshape(t.shape[0] // SUBLANES, SUBLANES, tq)

    def block_pass(j, j_next, own, between=None):
        for h in range(N_HEADS):
            ahead = h + LOGIT_AHEAD
            if ahead < N_HEADS:
                stage_logits(ahead, j)
            else:
                stage_logits(ahead - N_HEADS, j_next)
            if between is not None:
                between(h)
            lg = lg_ref[h % LOGIT_SLOTS]
            if own:
                kk = lax.broadcasted_iota(jnp.int32, lg.shape, 0)
                qq = lax.broadcasted_iota(jnp.int32, lg.shape, 1)
                lg3 = tiles(jnp.where(kk <= qq, lg, NEG))
                m = _all_sublanes(jnp.max(lg3, axis=0), jnp.maximum)
                p3 = jnp.exp2(lg3 - m[None])
                m_ref[h] = m
                l_ref[h] = _all_sublanes(jnp.sum(p3, axis=0), jnp.add)
                acc_ref[head_rows(h), :] = weighted_values(h, j, p3).reshape(HEAD_DIM, tq)
            else:
                lg3 = tiles(lg)
                take = take_ref[h, j] > 0.0
                m_old = m_ref[h]
                l_old = l_ref[h]
                acc_old = tiles(acc_ref[head_rows(h), :])
                m = jnp.maximum(m_old, _all_sublanes(jnp.max(lg3, axis=0), jnp.maximum))
                a = jnp.exp2(m_old - m)
                p3 = jnp.exp2(lg3 - m[None])
                l = a * l_old + _all_sublanes(jnp.sum(p3, axis=0), jnp.add)
                acc3 = a[None] * acc_old + weighted_values(h, j, p3)
                m_ref[h] = jnp.where(take, m, m_old)
                l_ref[h] = jnp.where(take, l, l_old)
                acc_ref[head_rows(h), :] = jnp.where(take[None], acc3, acc_old).reshape(HEAD_DIM, tq)

    last_past = jnp.maximum(i - 1, 0)
    for h in range(LOGIT_AHEAD):
        stage_logits(h, i)

    for s in range(per_step):
        v_pages = [vbuf_ref.at[slot, s * n_pages + p] for p in range(n_pages)]
        os_ref[s] = _sample_finish(_sample_queries(qs_ref[s]), slg_ref[s], kn_ref[s], vn_ref[s], v_pages).astype(BF16)

    next_q = [_sample_queries(qn_ref[s]) for s in range(per_step)]
    jobs = [(s, p) for s in range(per_step) for p in range(n_pages)]
    per_head = -(-len(jobs) // N_HEADS)
    page_logits = {}

    def sample_page_logits(h):
        for s, p in jobs[h * per_head:(h + 1) * per_head]:
            page_logits[s, p] = _sample_page_logits(next_q[s], kbuf_ref.at[next_keys, s * n_pages + p])

    block_pass(i, 0, own=True, between=sample_page_logits)
    for s in range(per_step):
        slg_ref[s] = jnp.concatenate([page_logits[s, p] for p in range(n_pages)], axis=-1)

    def two_past_blocks(jj, carry):
        j = 2 * jj
        block_pass(j, j + 1, own=False)
        block_pass(j + 1, jnp.minimum(j + 2, last_past), own=False)
        return carry

    lax.fori_loop(0, i // 2, two_past_blocks, 0)

    @pl.when(i % 2 == 1)
    def _():
        block_pass(last_past, last_past, own=False)

    for h in range(N_HEADS):
        rows = slice(h * HEAD_DIM, (h + 1) * HEAD_DIM)
        acc_ref[rows, :] = (tiles(acc_ref[rows, :]) / l_ref[h][None]).reshape(HEAD_DIM, tq)
    o_ref[0] = acc_ref[...].T.astype(BF16)


def _moba(qtb, kb, vtb, kmean, page_table, q_s, k_s, v_s, cache_kt, cache_vt):
    b, w, s = qtb.shape
    nq = s // MOBA_BLOCK
    bs, n_pages = page_table.shape
    page = cache_kt.shape[2]
    steps = b * nq
    assert bs % steps == 0
    per_step = bs // steps
    step_seq = lambda bi, i, pt: (bi * nq + i, 0, 0)
    tok = pl.BlockSpec((per_step, 1, w), step_seq)
    tok_next = pl.BlockSpec((per_step, 1, w), lambda bi, i, pt: (jnp.minimum(bi * nq + i + 1, steps - 1), 0, 0))
    in_hbm = pl.BlockSpec(memory_space=pl.ANY)
    page_ring = lambda parts: pltpu.VMEM((parts, per_step * n_pages, w, page), cache_kt.dtype)
    grid_spec = pltpu.PrefetchScalarGridSpec(
        num_scalar_prefetch=1,
        grid=(b, nq),
        in_specs=[
            pl.BlockSpec((1, w, MOBA_BLOCK), lambda bi, i, pt: (bi, 0, _query_block(i, nq))),
            pl.BlockSpec((1, s, w), lambda bi, i, pt: (bi, 0, 0)),
            pl.BlockSpec((1, nq, w, MOBA_BLOCK), lambda bi, i, pt: (bi, 0, 0, 0)),
            pl.BlockSpec((1, nq, w), lambda bi, i, pt: (bi, 0, 0)),
            tok, tok_next, tok, tok, in_hbm, in_hbm,
        ],
        out_specs=[pl.BlockSpec((1, MOBA_BLOCK, w), lambda bi, i, pt: (bi, _query_block(i, nq), 0)), tok],
        scratch_shapes=[pltpu.VMEM((w, MOBA_BLOCK), F32),
                        pltpu.VMEM((N_HEADS, SUBLANES, MOBA_BLOCK), F32),
                        pltpu.VMEM((N_HEADS, SUBLANES, MOBA_BLOCK), F32),
                        pltpu.VMEM((N_HEADS, nq, SUBLANES, MOBA_BLOCK), F32),
                        pltpu.VMEM((LOGIT_SLOTS, MOBA_BLOCK, MOBA_BLOCK), F32),
                        page_ring(KEY_RING), page_ring(VALUE_RING),
                        pltpu.VMEM((per_step, N_HEADS, n_pages * page), F32),
                        pltpu.SemaphoreType.DMA((KEY_RING,)), pltpu.SemaphoreType.DMA((VALUE_RING,))],
    )
    return pl.pallas_call(
        functools.partial(_moba_kernel, n_pages=n_pages, per_step=per_step),
        grid_spec=grid_spec,
        out_shape=[jax.ShapeDtypeStruct((b, s, w), BF16), jax.ShapeDtypeStruct((bs, 1, w), BF16)],
        compiler_params=_params("arbitrary", "arbitrary"),
        name="moba",
    )(page_table, qtb, kb, vtb, kmean, q_s, q_s, k_s, v_s, cache_kt, cache_vt)


def _mem_kv_kernel(m_ref, g_ref, wk_ref, wv_ref, mk_ref, mv_ref, mkb_ref, mvb_ref):
    h = _rms(m_ref[...], g_ref[...]).astype(BF16)
    mk = jnp.dot(h, wk_ref[...], preferred_element_type=F32)
    mv = jnp.dot(h, wv_ref[...], preferred_element_type=F32)
    mk_ref[...] = mk
    mv_ref[...] = mv
    mkb_ref[...] = mk.astype(BF16)
    mvb_ref[...] = mv.astype(BF16)


def _mem_kv(mem, g, w_ck, w_cv):
    rows, d = mem.shape
    tm = 512
    row = lambda i: (i, 0)
    return pl.pallas_call(
        _mem_kv_kernel,
        grid=(rows // tm,),
        in_specs=[pl.BlockSpec((tm, d), row), _resident((1, d)), _resident(w_ck.shape), _resident(w_cv.shape)],
        out_specs=[pl.BlockSpec((tm, d), row)] * 4,
        out_shape=[jax.ShapeDtypeStruct((rows, d), F32)] * 2 + [jax.ShapeDtypeStruct((rows, d), BF16)] * 2,
        compiler_params=_params("arbitrary"),
        name="mem_kv",
    )(mem, g, w_ck, w_cv)


MIX_CHAINS = 2


def _mix_out_kernel(x_ref, po_ref, at_ref, wo_ref, g_ref, wq_ref, *rest):
    with_memory = len(rest) == 4
    if with_memory:
        mk_ref, mv_ref, x1_ref, out_ref = rest
    else:
        x1_ref, out_ref = rest
    tm = x_ref.shape[0]
    chains = MIX_CHAINS if with_memory else 1
    rows_of = lambda c: slice(c * (tm // chains), (c + 1) * (tm // chains))

    def residual(c):
        rows = rows_of(c)
        cat = jnp.concatenate([po_ref[rows, :], at_ref[rows, :]], axis=-1)
        x1 = x_ref[rows, :] + jnp.dot(cat, wo_ref[...], preferred_element_type=F32)
        x1_ref[rows, :] = x1
        return _rms(x1, g_ref[...]).astype(BF16)

    def queries(hq):
        return (jnp.dot(hq, wq_ref[...], preferred_element_type=F32) * (MEM_HEAD_DIM ** -0.5)).astype(BF16)

    def attend(c, q):
        for h in range(MEM_HEADS):
            sl = slice(h * MEM_HEAD_DIM, (h + 1) * MEM_HEAD_DIM)
            lg = lax.dot_general(q[:, sl], mk_ref[0, :, sl], NT_DIMS, preferred_element_type=F32)
            p = jnp.exp(lg - jnp.max(lg, axis=-1, keepdims=True))
            l = jnp.sum(p, axis=-1, keepdims=True)
            o = jnp.dot(p.astype(BF16), mv_ref[0, :, sl], preferred_element_type=F32)
            out_ref[rows_of(c), sl] = (o / l).astype(BF16)

    if not with_memory:
        out_ref[...] = queries(residual(0))
        return
    q = queries(residual(0))
    for c in range(chains):
        hq_next = residual(c + 1) if c + 1 < chains else None
        attend(c, q)
        if hq_next is not None:
            q = queries(hq_next)


def _mix_out(x, po, at, w_out, g_cross, w_cq, memory=None):
    rows, d = x.shape
    tm = min(512, rows)
    row = lambda i: (i, 0)
    in_specs = [pl.BlockSpec((tm, d), row), pl.BlockSpec((tm, POOL_WIDTH), row), pl.BlockSpec((tm, ATTN_WIDTH), row),
                _resident(w_out.shape), _resident((1, d)), _resident(w_cq.shape)]
    args = [x, po, at, w_out, g_cross, w_cq]
    if memory is not None:
        mk, mv = memory
        tiles_per_batch = rows // mk.shape[0] // tm
        assert tiles_per_batch * tm * mk.shape[0] == rows and tm % MIX_CHAINS == 0
        of_batch = pl.BlockSpec((1,) + mk.shape[1:], lambda i: (i // tiles_per_batch, 0, 0))
        in_specs += [of_batch, of_batch]
        args += [mk, mv]
    return pl.pallas_call(
        _mix_out_kernel,
        grid=(rows // tm,),
        in_specs=in_specs,
        out_specs=[pl.BlockSpec((tm, d), row), pl.BlockSpec((tm, d), row)],
        out_shape=[jax.ShapeDtypeStruct((rows, d), F32), jax.ShapeDtypeStruct((rows, d), BF16)],
        compiler_params=_params("arbitrary"),
        name="mix_out",
    )(*args)


MEM_LANES = LANES
MEM_PARTS = MEM_HEAD_DIM // MEM_LANES
MEM_ROWS = MEM_PARTS * MEM_HEADS


def _cross_sample_products(q, mk_ref):
    return lax.dot_general(q, mk_ref[...].astype(BF16), NT_DIMS, preferred_element_type=F32)


def _cross_sample_attend(r, mv_ref):
    n = r.shape[1]
    row = lax.broadcasted_iota(jnp.int32, (MEM_ROWS, n), 0)
    lane_row = lax.broadcasted_iota(jnp.int32, (MEM_ROWS, n), 1) % MEM_ROWS
    own = lane_row == row
    valid = (row < MEM_HEADS) & own

    def other_part(t, lane_shift):
        return pltpu.roll(pltpu.roll(t, MEM_HEADS, 0), lane_shift, 1)

    r = jnp.where(own, r, 0.0)
    lg = jnp.where(valid, r + other_part(r, n - MEM_HEADS), NEG)
    p = jnp.where(valid, jnp.exp(lg - jnp.max(lg, axis=-1, keepdims=True)), 0.0)
    p = p + other_part(p, MEM_HEADS)
    l = jnp.sum(p, axis=-1, keepdims=True)
    return jnp.dot(p.astype(BF16), mv_ref[...].astype(BF16), preferred_element_type=F32) / l


def _stored_rows(qc):
    rows = qc.shape[0]
    return jnp.transpose(qc.reshape(rows, MEM_HEADS, MEM_PARTS, MEM_LANES), (0, 2, 1, 3)).reshape(rows, MEM_ROWS, MEM_LANES)


def _stored_slots(m):
    b, slots = m.shape[:2]
    m = m.reshape(b, slots, MEM_HEADS, MEM_PARTS, MEM_LANES)
    return jnp.transpose(m, (0, 1, 3, 2, 4)).reshape(b, slots * MEM_ROWS, MEM_LANES)


def _unstored_rows(o):
    rows = o.shape[0]
    return jnp.transpose(o.reshape(rows, MEM_PARTS, MEM_HEADS, MEM_LANES), (0, 2, 1, 3)).reshape(rows, -1)


def _mlp_kernel(x1_ref, o_ref, wco_ref, gm_ref, wu_ref, wd_ref, gf_ref, *rest):
    if len(rest) == 1:
        (y_ref,) = rest
        n_seq = 0
    else:
        qs_ref, mk_ref, mv_ref, y_ref, os_ref = rest
        n_seq = qs_ref.shape[0]
    products = [_cross_sample_products(qs_ref[s], mk_ref.at[s]) for s in range(n_seq)]
    x2 = x1_ref[...] + jnp.dot(o_ref[...], wco_ref[...], preferred_element_type=F32)
    hm = _rms(x2, gm_ref[...]).astype(BF16)
    acc = x2
    for c in range(D_FF // FF_CHUNK):
        if c == 1:
            for s in range(n_seq):
                os_ref[s] = _cross_sample_attend(products[s], mv_ref.at[s]).astype(BF16)
        cs = slice(c * FF_CHUNK, (c + 1) * FF_CHUNK)
        a = jnp.dot(hm, wu_ref[:, cs], preferred_element_type=F32)
        a = jnp.square(jnp.maximum(a, 0.0)).astype(BF16)
        acc = acc + jnp.dot(a, wd_ref[cs, :], preferred_element_type=F32)
    y_ref[...] = _rms(acc, gf_ref[...])


def _mlp(x1, o, w_co, g_mlp, w_up, w_down, g_final, sample_cross=None):
    rows, d = x1.shape
    tm = min(512, rows)
    steps = rows // tm
    row = lambda i: (i, 0)
    in_specs = [pl.BlockSpec((tm, d), row), pl.BlockSpec((tm, d), row), _resident(w_co.shape),
                _resident((1, d)), _resident(w_up.shape), _resident(w_down.shape), _resident((1, d))]
    args = [x1, o, w_co, g_mlp, w_up, w_down, g_final]
    out_specs = [pl.BlockSpec((tm, d), row)]
    out_shape = [jax.ShapeDtypeStruct((rows, d), F32)]
    if sample_cross is not None:
        qc, mem_k, mem_v = sample_cross
        b, slots = mem_k.shape[:2]
        assert b % steps == 0
        seq = lambda n: pl.BlockSpec((b // steps, n, MEM_LANES), lambda i: (i, 0, 0))
        in_specs += [seq(MEM_ROWS), seq(slots * MEM_ROWS), seq(slots * MEM_ROWS)]
        args += [_stored_rows(qc), _stored_slots(mem_k), _stored_slots(mem_v)]
        out_specs.append(seq(MEM_ROWS))
        out_shape.append(jax.ShapeDtypeStruct((b, MEM_ROWS, MEM_LANES), BF16))
    outs = pl.pallas_call(
        _mlp_kernel,
        grid=(steps,),
        in_specs=in_specs,
        out_specs=out_specs,
        out_shape=out_shape,
        compiler_params=_params("arbitrary"),
        name="mlp",
    )(*args)
    if sample_cross is None:
        return outs[0]
    return outs[0], _unstored_rows(outs[1])


def _proj_sample_kernel(x_ref, g_ref, w_ref, cos_ref, sin_ref, st_ref, wp_ref, sp_ref,
                        q_ref, k_ref, v_ref, kt_ref, vt_ref, u_ref, po_ref, *, n_prev):
    h = _rms(x_ref[...], g_ref[...]).astype(BF16)
    proj = jnp.dot(h, w_ref[...], preferred_element_type=F32)
    u = proj[:, :POOL_WIDTH]
    cos = cos_ref[...]
    sin = sin_ref[...]
    q_ref[...] = _rope(proj[:, POOL_WIDTH:POOL_WIDTH + ATTN_WIDTH], cos, sin) * (HEAD_DIM ** -0.5)
    k = _rope(proj[:, POOL_WIDTH + ATTN_WIDTH:POOL_WIDTH + 2 * ATTN_WIDTH], cos, sin)
    v = proj[:, POOL_WIDTH + 2 * ATTN_WIDTH:]
    k_ref[...] = k
    v_ref[...] = v
    kt_ref[...] = k.T
    vt_ref[...] = v.T
    u_ref[...] = u
    for g, w in enumerate(POOL_WINDOWS):
        cols = slice(g * POOL_GC, (g + 1) * POOL_GC)
        ug = u[:, cols]
        wsum = ug
        for j in range(1, w):
            wsum = wsum + st_ref[POOL_STATE - j, :, cols]
        d = wsum / float(min(w, n_prev + 1)) - ug
        mixed = jnp.dot(d.astype(BF16), wp_ref[g], preferred_element_type=F32)
        po_ref[:, cols] = (mixed * sp_ref[:, cols]).astype(BF16)


def _proj_sample(x, g, w_in, cos, sin, state_t, w_pool, s_pool, n_prev):
    rows, d = x.shape
    wide = jax.ShapeDtypeStruct((rows, ATTN_WIDTH), F32)
    tall = jax.ShapeDtypeStruct((ATTN_WIDTH, rows), F32)
    full = lambda a: pl.BlockSpec(a.shape, lambda i, nd=len(a.shape): (0,) * nd)
    args = (x, g, w_in, cos, sin, state_t, w_pool, s_pool)
    outs = [wide, wide, wide, tall, tall, wide, jax.ShapeDtypeStruct((rows, POOL_WIDTH), BF16)]
    return pl.pallas_call(
        functools.partial(_proj_sample_kernel, n_prev=n_prev),
        grid=(1,),
        in_specs=[full(a) for a in args],
        out_specs=[full(o) for o in outs],
        out_shape=outs,
        compiler_params=_params("arbitrary"),
        name="proj_sample",
    )(*args)


def _own_lanes():
    head_of_lane = lax.broadcasted_iota(jnp.int32, (N_HEADS, ATTN_WIDTH), 1) // HEAD_DIM
    head = lax.broadcasted_iota(jnp.int32, (N_HEADS, ATTN_WIDTH), 0)
    return head_of_lane == head


def _sample_queries(q):
    return jnp.where(_own_lanes(), q, 0.0)


def _sample_page_logits(q_heads, k_page):
    return jnp.dot(q_heads.astype(BF16), k_page[...].astype(BF16), preferred_element_type=F32)


def _sample_finish(q_heads, lg, k_new, v_new, v_pages):
    page = v_pages[0].shape[1]
    past = lg.shape[1]
    nb = past // MOBA_BLOCK
    mine = _own_lanes()

    s = jnp.concatenate(
        [jnp.sum(lg[:, j * MOBA_BLOCK:(j + 1) * MOBA_BLOCK], axis=-1, keepdims=True) for j in range(nb)], axis=-1)
    blk_id = lax.broadcasted_iota(jnp.int32, (N_HEADS, nb), 1)
    sel = _topk_member(s, blk_id >= 0, blk_id, axis=1)
    key_blk = lax.broadcasted_iota(jnp.int32, (N_HEADS, past), 1) // MOBA_BLOCK
    chosen = jnp.zeros((N_HEADS, past), jnp.bool_)
    for j in range(nb):
        chosen = chosen | ((key_blk == j) & sel[:, j:j + 1])
    lg = jnp.where(chosen, lg, NEG)

    lg_new = jnp.sum(q_heads * k_new, axis=-1, keepdims=True)
    m = jnp.maximum(jnp.max(lg, axis=-1, keepdims=True), lg_new)
    p = jnp.exp(lg - m)
    p_new = jnp.exp(lg_new - m)
    l = jnp.sum(p, axis=-1, keepdims=True) + p_new

    lanes = page
    sub = lax.broadcasted_iota(jnp.int32, (HEAD_DIM, lanes), 0)
    lane = lax.broadcasted_iota(jnp.int32, (HEAD_DIM, lanes), 1)
    tiles = []
    for first in range(0, N_HEADS, lanes // HEAD_DIM):
        row = jnp.zeros((1, lanes), F32)
        for h in range(first, first + lanes // HEAD_DIM):
            acc = jnp.zeros((HEAD_DIM, lanes), F32)
            for n, vp in enumerate(v_pages):
                w = jnp.broadcast_to(p[h:h + 1, n * page:(n + 1) * page], (HEAD_DIM, page))
                acc = acc + vp[h * HEAD_DIM:(h + 1) * HEAD_DIM, :] * w
            total = jnp.sum(acc, axis=1, keepdims=True)
            on_diag = lane == sub + (h - first) * HEAD_DIM
            row = row + jnp.sum(jnp.where(on_diag, total, 0.0), axis=0, keepdims=True)
        tiles.append(row)
    pv = jnp.concatenate(tiles, axis=-1)
    per_lane = lambda t: jnp.sum(jnp.where(mine, t, 0.0), axis=0, keepdims=True)
    return (pv + per_lane(p_new) * v_new) / per_lane(l)


def _rope_tables(pos, heads):
    half = HEAD_DIM // 2
    inv_freq = ROPE_THETA ** (-jnp.arange(half, dtype=F32) / half)
    ang = pos.astype(F32)[:, None] * inv_freq[None, :]
    cos = jnp.cos(ang)
    sin = jnp.sin(ang)
    cos_full = jnp.tile(jnp.concatenate([cos, cos], axis=-1), (1, heads))
    sin_signed = jnp.tile(jnp.concatenate([-sin, sin], axis=-1), (1, heads))
    return cos_full, sin_signed


def kernel(x_prompt, x_sample, cache_k, cache_v, cache_mem_k, cache_mem_v, state_pool, page_table, mem_prompt,
           g_mix, w_in, w_pool, s_pool, w_out, g_cross, g_mem, w_cq, w_ck, w_cv, w_co, g_mlp, w_up, w_down,
           g_final):
    depth = w_in.shape[0]
    assert depth == 1, "one decoder layer"
    bp, seq, d = x_prompt.shape
    bs, dec_seq, _ = x_sample.shape
    assert dec_seq == 1 and seq % MOBA_BLOCK == 0
    n_phys, page = cache_k.shape[1], cache_k.shape[2]
    n_pages = page_table.shape[1]
    past_len = n_pages * page
    assert past_len % MOBA_BLOCK == 0 and past_len >= POOL_STATE
    mem_len = mem_prompt.shape[1]

    l = 0
    row = lambda a: a.reshape(1, -1)
    lp = dict(w_out=w_out[l].astype(BF16), g_cross=row(g_cross[l]), w_cq=w_cq[l].astype(BF16),
              w_co=w_co[l].astype(BF16), g_mlp=row(g_mlp[l]), w_up=w_up[l].astype(BF16),
              w_down=w_down[l].astype(BF16), g_final=row(g_final))
    w_in_b = w_in[l].astype(BF16)
    w_pool_b = w_pool[l].astype(BF16)
    g_mix_r = row(g_mix[l])
    s_pool_r = row(s_pool[l])

    lane_heads = LANES // HEAD_DIM
    cos_p, sin_p = _rope_tables(jnp.arange(seq), lane_heads)
    qtb, kt_p, vt_p, kb, vtb, kmean, po_p, last_p = _proj_prompt(
        x_prompt, g_mix_r, w_in_b, cos_p, sin_p, cos_p[:, :HEAD_DIM].T, sin_p[:, :HEAD_DIM].T, w_pool_b, s_pool_r)
    cos_s, sin_s = _rope_tables(jnp.full((1,), past_len), N_HEADS)
    state_t = jnp.transpose(state_pool[l], (1, 0, 2))
    q_s, k_s, v_s, kt_s, vt_s, u_s, po_s = _proj_sample(x_sample.reshape(bs, d), g_mix_r, w_in_b, cos_s, sin_s,
                                                        state_t, w_pool_b, s_pool_r, past_len)

    tok = lambda a: a.reshape(bs, 1, -1)
    pages_t = lambda c: jnp.transpose(c, (0, 2, 3, 1)).reshape(n_phys, ATTN_WIDTH, page)
    at_p, at_s = _moba(qtb, kb, vtb, kmean.reshape(bp, seq // MOBA_BLOCK, ATTN_WIDTH), page_table, tok(q_s),
                       tok(k_s), tok(v_s), pages_t(cache_k[l]), pages_t(cache_v[l]))

    mk_p, mv_p, mkb, mvb = _mem_kv(mem_prompt.reshape(bp * mem_len, d), row(g_mem[l]), w_ck[l].astype(BF16),
                                   w_cv[l].astype(BF16))
    flat = lambda a: a.reshape(-1, a.shape[-1])
    mix = lambda x, po, at, memory=None: _mix_out(flat(x), flat(po), flat(at), lp["w_out"], lp["g_cross"],
                                                  lp["w_cq"], memory)
    tail = (lp["w_co"], lp["g_mlp"], lp["w_up"], lp["w_down"], lp["g_final"])
    x1_p, o_p = mix(x_prompt, po_p, at_p, (mkb.reshape(bp, mem_len, d), mvb.reshape(bp, mem_len, d)))
    x1_s, qc_s = mix(x_sample, po_s, at_s)
    y_prompt, o_s = _mlp(x1_p, o_p, *tail, sample_cross=(qc_s, cache_mem_k[l], cache_mem_v[l]))
    y_prompt = y_prompt.reshape(bp, seq, d)
    y_sample = _mlp(x1_s, o_s, *tail).reshape(bs, 1, d)

    def heads(t, b_):
        return jnp.transpose(t.reshape(b_, N_HEADS, HEAD_DIM, -1), (0, 3, 1, 2))[None]

    mem_heads = lambda a: a.reshape(1, bp, mem_len, MEM_HEADS, MEM_HEAD_DIM)
    pool_sample = jnp.transpose(jnp.concatenate([state_t[1:], u_s[None]], axis=0), (1, 0, 2))[None]
    return (y_prompt, y_sample, heads(kt_p, bp), heads(vt_p, bp), last_p[:, 1:][None], mem_heads(mk_p),
            mem_heads(mv_p), heads(kt_s.T.reshape(bs, ATTN_WIDTH, 1), bs), heads(vt_s.T.reshape(bs, ATTN_WIDTH, 1), bs),
            pool_sample)
```

```python
import functools

import jax
import jax.numpy as jnp
from jax import lax
from jax.experimental import pallas as pl
from jax.experimental.pallas import tpu as pltpu

D_MODEL = 1024
POOL_WIDTH = 512
POOL_WINDOWS = (2, 4, 8, 16)
POOL_GC = POOL_WIDTH // len(POOL_WINDOWS)
POOL_STATE = max(POOL_WINDOWS) - 1
POOL_HIST = POOL_STATE + 1
ATTN_WIDTH = D_MODEL - POOL_WIDTH
HEAD_DIM = 64
N_HEADS = ATTN_WIDTH // HEAD_DIM
MOBA_BLOCK = 256
MOBA_TOPK = 3
ROPE_THETA = 10000.0
MEM_HEADS = 4
MEM_HEAD_DIM = D_MODEL // MEM_HEADS
D_FF = 4 * D_MODEL
FF_CHUNK = 1024
EPS = 1e-6
NEG = -1e30

VMEM_LIMIT_BYTES = 56 * 1024 * 1024
LANES = 128

F32 = jnp.float32
BF16 = jnp.bfloat16
NT_DIMS = (((1,), (1,)), ((), ()))


def _params(*semantics):
    return pltpu.CompilerParams(dimension_semantics=semantics, vmem_limit_bytes=VMEM_LIMIT_BYTES)


def _resident(shape):
    zeros = (0,) * len(shape)
    return pl.BlockSpec(shape, lambda *_: zeros, pipeline_mode=pl.Buffered(1))


def _rms(x, g):
    return x * lax.rsqrt(jnp.mean(x * x, axis=-1, keepdims=True) + EPS) * g


def _rope(t, cos, sin_signed):
    width = t.shape[-1]
    lane = lax.broadcasted_iota(jnp.int32, t.shape, t.ndim - 1)
    first_half = (lane % HEAD_DIM) < (HEAD_DIM // 2)
    partner = jnp.where(first_half,
                        pltpu.roll(t, width - HEAD_DIM // 2, t.ndim - 1),
                        pltpu.roll(t, HEAD_DIM // 2, t.ndim - 1))
    return t * cos + partner * sin_signed


def _rope_t(t, cos, sin_signed):
    rows = t.shape[0]
    row = lax.broadcasted_iota(jnp.int32, t.shape, 0)
    first_half = (row % HEAD_DIM) < (HEAD_DIM // 2)
    partner = jnp.where(first_half, pltpu.roll(t, rows - HEAD_DIM // 2, 0), pltpu.roll(t, HEAD_DIM // 2, 0))
    return t * cos + partner * sin_signed


def _topk_member(s, valid, idx, axis):
    n = s.shape[axis]
    s = jnp.where(valid, s, NEG)
    rank = jnp.zeros(s.shape, jnp.int32)
    for j in range(n):
        sj = lax.slice_in_dim(s, j, j + 1, axis=axis)
        beats = (sj > s) | ((sj == s) & (j < idx))
        rank = rank + beats.astype(jnp.int32)
    return (rank < MOBA_TOPK) & valid


PROJ_BLOCKS = 2
def _proj_prompt_kernel(x_ref, g_ref, w_ref, cos_ref, sin_ref, cost_ref, sint_ref, wp_ref, sp_ref,
                        qt_ref, kt_ref, vt_ref, kb_ref, vtb_ref, km_ref, po_ref, last_ref, ext_ref):
    i = pl.program_id(1)
    tm = x_ref.shape[1]

    @pl.when(i == 0)
    def _():
        ext_ref[0:POOL_HIST, :] = jnp.zeros((POOL_HIST, POOL_WIDTH), F32)

    h = _rms(x_ref[0], g_ref[...]).astype(BF16)

    def project(first):
        return jnp.dot(h, w_ref[:, first:first + ATTN_WIDTH], preferred_element_type=F32)

    lane_reps = ATTN_WIDTH // cos_ref.shape[1]
    u = project(0)
    k = _rope(project(POOL_WIDTH + ATTN_WIDTH), jnp.tile(cos_ref[...], (1, lane_reps)),
              jnp.tile(sin_ref[...], (1, lane_reps)))
    q = project(POOL_WIDTH)
    v = project(POOL_WIDTH + 2 * ATTN_WIDTH)
    qt = _rope_t(q.T, jnp.tile(cost_ref[...], (N_HEADS, 1)), jnp.tile(sint_ref[...], (N_HEADS, 1)))
    qt_ref[0] = (qt * (HEAD_DIM ** -0.5 * LOG2_E)).astype(BF16)
    vt = v.T
    kt_ref[0] = k.T
    vt_ref[0] = vt
    kb_ref[0] = k.astype(BF16)
    for c in range(tm // MOBA_BLOCK):
        block = slice(c * MOBA_BLOCK, (c + 1) * MOBA_BLOCK)
        vtb_ref[0, c] = vt[:, block].astype(BF16)
        km_ref[0, c] = jnp.mean(k[block, :], axis=0, keepdims=True)

    pos = i * tm + lax.broadcasted_iota(jnp.int32, (tm, 1), 0)
    for g, w in enumerate(POOL_WINDOWS):
        cols = slice(g * POOL_GC, (g + 1) * POOL_GC)
        ug = u[:, cols]
        wsum = jnp.concatenate([ext_ref[:, cols], ug], axis=0)
        shift = 1
        while shift < w:
            wsum = wsum + pltpu.roll(wsum, shift, 0)
            shift *= 2
        cnt = jnp.minimum(w, pos + 1).astype(F32)
        d = wsum[POOL_HIST:, :] / cnt - ug
        mixed = jnp.dot(d.astype(BF16), wp_ref[g], preferred_element_type=F32)
        po_ref[0, :, cols] = (mixed * sp_ref[:, cols]).astype(BF16)
    tail = u[tm - POOL_HIST:, :]
    ext_ref[...] = tail
    last_ref[0] = tail


def _proj_prompt(x, g, w_in, cos, sin, cos_t, sin_t, w_pool, s_pool):
    b, s, d = x.shape
    per_tile = PROJ_BLOCKS if s % (PROJ_BLOCKS * MOBA_BLOCK) == 0 else 1
    tm = per_tile * MOBA_BLOCK
    nq = s // MOBA_BLOCK
    row = lambda bi, i: (bi, i, 0)
    col = lambda bi, i: (bi, 0, i)
    blk = lambda bi, i: (bi, i, 0, 0)
    return pl.pallas_call(
        _proj_prompt_kernel,
        grid=(b, s // tm),
        in_specs=[
            pl.BlockSpec((1, tm, d), row),
            _resident((1, d)),
            _resident(w_in.shape),
            pl.BlockSpec((tm, cos.shape[1]), lambda bi, i: (i, 0)),
            pl.BlockSpec((tm, cos.shape[1]), lambda bi, i: (i, 0)),
            pl.BlockSpec((HEAD_DIM, tm), lambda bi, i: (0, i)),
            pl.BlockSpec((HEAD_DIM, tm), lambda bi, i: (0, i)),
            _resident(w_pool.shape),
            _resident((1, POOL_WIDTH)),
        ],
        out_specs=[
            pl.BlockSpec((1, ATTN_WIDTH, tm), col),
            pl.BlockSpec((1, ATTN_WIDTH, tm), col),
            pl.BlockSpec((1, ATTN_WIDTH, tm), col),
            pl.BlockSpec((1, tm, ATTN_WIDTH), row),
            pl.BlockSpec((1, per_tile, ATTN_WIDTH, MOBA_BLOCK), blk),
            pl.BlockSpec((1, per_tile, 1, ATTN_WIDTH), blk),
            pl.BlockSpec((1, tm, POOL_WIDTH), row),
            pl.BlockSpec((1, POOL_HIST, POOL_WIDTH), lambda bi, i: (bi, 0, 0)),
        ],
        out_shape=[
            jax.ShapeDtypeStruct((b, ATTN_WIDTH, s), BF16),
            jax.ShapeDtypeStruct((b, ATTN_WIDTH, s), F32),
            jax.ShapeDtypeStruct((b, ATTN_WIDTH, s), F32),
            jax.ShapeDtypeStruct((b, s, ATTN_WIDTH), BF16),
            jax.ShapeDtypeStruct((b, nq, ATTN_WIDTH, MOBA_BLOCK), BF16),
            jax.ShapeDtypeStruct((b, nq, 1, ATTN_WIDTH), F32),
            jax.ShapeDtypeStruct((b, s, POOL_WIDTH), BF16),
            jax.ShapeDtypeStruct((b, POOL_HIST, POOL_WIDTH), F32),
        ],
        scratch_shapes=[pltpu.VMEM((POOL_HIST, POOL_WIDTH), F32)],
        compiler_params=_params("arbitrary", "arbitrary"),
        name="proj_prompt",
    )(x, g, w_in, cos, sin, cos_t, sin_t, w_pool, s_pool)


SUBLANES = 8
LOGIT_AHEAD = 4
LOGIT_SLOTS = 8
LOG2_E = 1.4426950408889634
KEY_RING = 3
VALUE_RING = 2


def _query_block(visit, n):
    return jnp.where(visit % 2 == 0, visit // 2, n - 1 - visit // 2)


def _all_sublanes(t, combine):
    for s in (4, 2, 1):
        t = combine(t, pltpu.roll(t, s, 0))
    return t


def _moba_kernel(pt_ref, qt_ref, kb_ref, vtb_ref, km_ref, qs_ref, qn_ref, kn_ref, vn_ref, ck_ref, cv_ref,
                 o_ref, os_ref, acc_ref, m_ref, l_ref, take_ref, lg_ref, kbuf_ref, vbuf_ref, slg_ref, key_sem, val_sem,
                 *, n_pages, per_step):
    nq = pl.num_programs(1)
    i = _query_block(pl.program_id(1), nq)
    step = pl.program_id(0) * nq + pl.program_id(1)
    n_steps = pl.num_programs(0) * nq
    key_ring = kbuf_ref.shape[0]
    val_ring = vbuf_ref.shape[0]
    tq = qt_ref.shape[2]
    nb = km_ref.shape[1]

    def page_copies(cache_ref, buf_ref, sem, page_id, part):
        return [pltpu.make_async_copy(cache_ref.at[page_id(s, p)], buf_ref.at[part, s * n_pages + p], sem.at[part])
                for s in range(per_step) for p in range(n_pages)]

    def start_keys(t):
        for c in page_copies(ck_ref, kbuf_ref, key_sem, lambda s, p: pt_ref[t * per_step + s, p], t % key_ring):
            c.start()

    def start_values(t):
        for c in page_copies(cv_ref, vbuf_ref, val_sem, lambda s, p: pt_ref[t * per_step + s, p], t % val_ring):
            c.start()

    def wait_part(cache_ref, buf_ref, sem, part):
        pltpu.make_async_copy(cache_ref.at[pl.ds(0, per_step * n_pages)], buf_ref.at[part], sem.at[part]).wait()

    def key_logits(q_rows, part):
        return [[_sample_page_logits(q_rows[s], kbuf_ref.at[part, s * n_pages + p]) for p in range(n_pages)]
                for s in range(per_step)]

    @pl.when(step == 0)
    def _():
        start_keys(0)
        for t in range(min(2, n_steps)):
            start_values(t)
        for t in range(1, min(key_ring, n_steps)):
            start_keys(t)
        wait_part(ck_ref, kbuf_ref, key_sem, 0)
        first = key_logits([_sample_queries(qs_ref[s]) for s in range(per_step)], 0)
        for s in range(per_step):
            slg_ref[s] = jnp.concatenate(first[s], axis=-1)
        if n_steps > key_ring:
            start_keys(key_ring)

    @pl.when((step >= 1) & (step + 1 < n_steps))
    def _():
        start_values(step + 1)

    @pl.when((step >= 1) & (step + key_ring < n_steps))
    def _():
        start_keys(step + key_ring)

    @pl.when(step + 1 < n_steps)
    def _():
        wait_part(ck_ref, kbuf_ref, key_sem, (step + 1) % key_ring)

    wait_part(cv_ref, vbuf_ref, val_sem, step % val_ring)
    slot = step % val_ring
    next_keys = (step + 1) % key_ring


    km_rows = jnp.concatenate([km_ref[0]] * N_HEADS, axis=0)
    row_head = lax.broadcasted_iota(jnp.int32, km_rows.shape, 0) // nb
    lane_head = lax.broadcasted_iota(jnp.int32, km_rows.shape, 1) // HEAD_DIM
    km_heads = jnp.where(row_head == lane_head, km_rows, 0.0).astype(BF16)
    s = jnp.dot(km_heads, qt_ref[0], preferred_element_type=F32).reshape(N_HEADS, nb, tq)
    blk = lax.broadcasted_iota(jnp.int32, s.shape, 1)
    sel = _topk_member(s, blk < i, blk, axis=1).astype(F32)

    for h in range(N_HEADS):
        for jb in range(nb):
            take_ref[h, jb] = jnp.broadcast_to(sel[h, jb:jb + 1, :], (SUBLANES, tq))

    def head_rows(h):
        return slice(h * HEAD_DIM, (h + 1) * HEAD_DIM)

    def stage_logits(h, j):
        off = pl.multiple_of(j * MOBA_BLOCK, MOBA_BLOCK)
        lg_ref[h % LOGIT_SLOTS] = jnp.dot(kb_ref[0, pl.ds(off, MOBA_BLOCK), head_rows(h)],
                                          qt_ref[0, head_rows(h), :], preferred_element_type=F32)

    def weighted_values(h, j, p3):
        p = p3.reshape(MOBA_BLOCK, tq).astype(BF16)
        pv = jnp.dot(vtb_ref[0, j, head_rows(h), :], p, preferred_element_type=F32)
        return pv.reshape(HEAD_DIM // SUBLANES, SUBLANES, tq)

    def tiles(t):
        return t.reshape(t.shape[0] // SUBLANES, SUBLANES, tq)

    def block_pass(j, j_next, own, between=None):
        for h in range(N_HEADS):
            ahead = h + LOGIT_AHEAD
            if ahead < N_HEADS:
                stage_logits(ahead, j)
            else:
                stage_logits(ahead - N_HEADS, j_next)
            if between is not None:
                between(h)
            lg = lg_ref[h % LOGIT_SLOTS]
            if own:
                kk = lax.broadcasted_iota(jnp.int32, lg.shape, 0)
                qq = lax.broadcasted_iota(jnp.int32, lg.shape, 1)
                lg3 = tiles(jnp.where(kk <= qq, lg, NEG))
                m = _all_sublanes(jnp.max(lg3, axis=0), jnp.maximum)
                p3 = jnp.exp2(lg3 - m[None])
                m_ref[h] = m
                l_ref[h] = _all_sublanes(jnp.sum(p3, axis=0), jnp.add)
                acc_ref[head_rows(h), :] = weighted_values(h, j, p3).reshape(HEAD_DIM, tq)
            else:
                lg3 = tiles(lg)
                take = take_ref[h, j] > 0.0
                m_old = m_ref[h]
                l_old = l_ref[h]
                acc_old = tiles(acc_ref[head_rows(h), :])
                m = jnp.maximum(m_old, _all_sublanes(jnp.max(lg3, axis=0), jnp.maximum))
                a = jnp.exp2(m_old - m)
                p3 = jnp.exp2(lg3 - m[None])
                l = a * l_old + _all_sublanes(jnp.sum(p3, axis=0), jnp.add)
                acc3 = a[None] * acc_old + weighted_values(h, j, p3)
                m_ref[h] = jnp.where(take, m, m_old)
                l_ref[h] = jnp.where(take, l, l_old)
                acc_ref[head_rows(h), :] = jnp.where(take[None], acc3, acc_old).reshape(HEAD_DIM, tq)

    last_past = jnp.maximum(i - 1, 0)
    for h in range(LOGIT_AHEAD):
        stage_logits(h, i)

    for s in range(per_step):
        v_pages = [vbuf_ref.at[slot, s * n_pages + p] for p in range(n_pages)]
        os_ref[s] = _sample_finish(_sample_queries(qs_ref[s]), slg_ref[s], kn_ref[s], vn_ref[s], v_pages).astype(BF16)

    next_q = [_sample_queries(qn_ref[s]) for s in range(per_step)]
    jobs = [(s, p) for s in range(per_step) for p in range(n_pages)]
    per_head = -(-len(jobs) // N_HEADS)
    page_logits = {}

    def sample_page_logits(h):
        for s, p in jobs[h * per_head:(h + 1) * per_head]:
            page_logits[s, p] = _sample_page_logits(next_q[s], kbuf_ref.at[next_keys, s * n_pages + p])

    block_pass(i, 0, own=True, between=sample_page_logits)
    for s in range(per_step):
        slg_ref[s] = jnp.concatenate([page_logits[s, p] for p in range(n_pages)], axis=-1)

    def two_past_blocks(jj, carry):
        j = 2 * jj
        block_pass(j, j + 1, own=False)
        block_pass(j + 1, jnp.minimum(j + 2, last_past), own=False)
        return carry

    lax.fori_loop(0, i // 2, two_past_blocks, 0)

    @pl.when(i % 2 == 1)
    def _():
        block_pass(last_past, last_past, own=False)

    for h in range(N_HEADS):
        rows = slice(h * HEAD_DIM, (h + 1) * HEAD_DIM)
        acc_ref[rows, :] = (tiles(acc_ref[rows, :]) / l_ref[h][None]).reshape(HEAD_DIM, tq)
    o_ref[0] = acc_ref[...].T.astype(BF16)


def _moba(qtb, kb, vtb, kmean, page_table, q_s, k_s, v_s, cache_kt, cache_vt):
    b, w, s = qtb.shape
    nq = s // MOBA_BLOCK
    bs, n_pages = page_table.shape
    page = cache_kt.shape[2]
    steps = b * nq
    assert bs % steps == 0
    per_step = bs // steps
    step_seq = lambda bi, i, pt: (bi * nq + i, 0, 0)
    tok = pl.BlockSpec((per_step, 1, w), step_seq)
    tok_next = pl.BlockSpec((per_step, 1, w), lambda bi, i, pt: (jnp.minimum(bi * nq + i + 1, steps - 1), 0, 0))
    in_hbm = pl.BlockSpec(memory_space=pl.ANY)
    page_ring = lambda parts: pltpu.VMEM((parts, per_step * n_pages, w, page), cache_kt.dtype)
    grid_spec = pltpu.PrefetchScalarGridSpec(
        num_scalar_prefetch=1,
        grid=(b, nq),
        in_specs=[
            pl.BlockSpec((1, w, MOBA_BLOCK), lambda bi, i, pt: (bi, 0, _query_block(i, nq))),
            pl.BlockSpec((1, s, w), lambda bi, i, pt: (bi, 0, 0)),
            pl.BlockSpec((1, nq, w, MOBA_BLOCK), lambda bi, i, pt: (bi, 0, 0, 0)),
            pl.BlockSpec((1, nq, w), lambda bi, i, pt: (bi, 0, 0)),
            tok, tok_next, tok, tok, in_hbm, in_hbm,
        ],
        out_specs=[pl.BlockSpec((1, MOBA_BLOCK, w), lambda bi, i, pt: (bi, _query_block(i, nq), 0)), tok],
        scratch_shapes=[pltpu.VMEM((w, MOBA_BLOCK), F32),
                        pltpu.VMEM((N_HEADS, SUBLANES, MOBA_BLOCK), F32),
                        pltpu.VMEM((N_HEADS, SUBLANES, MOBA_BLOCK), F32),
                        pltpu.VMEM((N_HEADS, nq, SUBLANES, MOBA_BLOCK), F32),
                        pltpu.VMEM((LOGIT_SLOTS, MOBA_BLOCK, MOBA_BLOCK), F32),
                        page_ring(KEY_RING), page_ring(VALUE_RING),
                        pltpu.VMEM((per_step, N_HEADS, n_pages * page), F32),
                        pltpu.SemaphoreType.DMA((KEY_RING,)), pltpu.SemaphoreType.DMA((VALUE_RING,))],
    )
    return pl.pallas_call(
        functools.partial(_moba_kernel, n_pages=n_pages, per_step=per_step),
        grid_spec=grid_spec,
        out_shape=[jax.ShapeDtypeStruct((b, s, w), BF16), jax.ShapeDtypeStruct((bs, 1, w), BF16)],
        compiler_params=_params("arbitrary", "arbitrary"),
        name="moba",
    )(page_table, qtb, kb, vtb, kmean, q_s, q_s, k_s, v_s, cache_kt, cache_vt)


def _mem_kv_kernel(m_ref, g_ref, wk_ref, wv_ref, mk_ref, mv_ref, mkb_ref, mvb_ref):
    h = _rms(m_ref[...], g_ref[...]).astype(BF16)
    mk = jnp.dot(h, wk_ref[...], preferred_element_type=F32)
    mv = jnp.dot(h, wv_ref[...], preferred_element_type=F32)
    mk_ref[...] = mk
    mv_ref[...] = mv
    mkb_ref[...] = mk.astype(BF16)
    mvb_ref[...] = mv.astype(BF16)


def _mem_kv(mem, g, w_ck, w_cv):
    rows, d = mem.shape
    tm = 512
    row = lambda i: (i, 0)
    return pl.pallas_call(
        _mem_kv_kernel,
        grid=(rows // tm,),
        in_specs=[pl.BlockSpec((tm, d), row), _resident((1, d)), _resident(w_ck.shape), _resident(w_cv.shape)],
        out_specs=[pl.BlockSpec((tm, d), row)] * 4,
        out_shape=[jax.ShapeDtypeStruct((rows, d), F32)] * 2 + [jax.ShapeDtypeStruct((rows, d), BF16)] * 2,
        compiler_params=_params("arbitrary"),
        name="mem_kv",
    )(mem, g, w_ck, w_cv)


MIX_CHAINS = 4
MIX_CHAIN_ROWS = 256


def _mix_out_kernel(x_ref, po_ref, at_ref, wo_ref, g_ref, wq_ref, *rest):
    with_memory = len(rest) == 4
    if with_memory:
        mk_ref, mv_ref, x1_ref, out_ref = rest
    else:
        x1_ref, out_ref = rest
    tm = x_ref.shape[0]
    chains = MIX_CHAINS if with_memory else 1
    rows_of = lambda c: slice(c * (tm // chains), (c + 1) * (tm // chains))

    def residual(c):
        rows = rows_of(c)
        cat = jnp.concatenate([po_ref[rows, :], at_ref[rows, :]], axis=-1)
        x1 = x_ref[rows, :] + jnp.dot(cat, wo_ref[...], preferred_element_type=F32)
        x1_ref[rows, :] = x1
        return _rms(x1, g_ref[...]).astype(BF16)

    def queries(hq):
        return (jnp.dot(hq, wq_ref[...], preferred_element_type=F32) * (MEM_HEAD_DIM ** -0.5)).astype(BF16)

    def attend(c, q):
        for h in range(MEM_HEADS):
            sl = slice(h * MEM_HEAD_DIM, (h + 1) * MEM_HEAD_DIM)
            lg = lax.dot_general(q[:, sl], mk_ref[0, :, sl], NT_DIMS, preferred_element_type=F32)
            p = jnp.exp(lg - jnp.max(lg, axis=-1, keepdims=True))
            l = jnp.sum(p, axis=-1, keepdims=True)
            o = jnp.dot(p.astype(BF16), mv_ref[0, :, sl], preferred_element_type=F32)
            out_ref[rows_of(c), sl] = (o / l).astype(BF16)

    if not with_memory:
        out_ref[...] = queries(residual(0))
        return
    q = queries(residual(0))
    for c in range(chains):
        hq_next = residual(c + 1) if c + 1 < chains else None
        attend(c, q)
        if hq_next is not None:
            q = queries(hq_next)


def _mix_out(x, po, at, w_out, g_cross, w_cq, memory=None):
    rows, d = x.shape
    tm = min(MIX_CHAINS * MIX_CHAIN_ROWS if memory is not None else MIX_CHAIN_ROWS, rows)
    row = lambda i: (i, 0)
    in_specs = [pl.BlockSpec((tm, d), row), pl.BlockSpec((tm, POOL_WIDTH), row), pl.BlockSpec((tm, ATTN_WIDTH), row),
                _resident(w_out.shape), _resident((1, d)), _resident(w_cq.shape)]
    args = [x, po, at, w_out, g_cross, w_cq]
    if memory is not None:
        mk, mv = memory
        tiles_per_batch = rows // mk.shape[0] // tm
        assert tiles_per_batch * tm * mk.shape[0] == rows and tm % MIX_CHAINS == 0
        of_batch = pl.BlockSpec((1,) + mk.shape[1:], lambda i: (i // tiles_per_batch, 0, 0))
        in_specs += [of_batch, of_batch]
        args += [mk, mv]
    return pl.pallas_call(
        _mix_out_kernel,
        grid=(rows // tm,),
        in_specs=in_specs,
        out_specs=[pl.BlockSpec((tm, d), row), pl.BlockSpec((tm, d), row)],
        out_shape=[jax.ShapeDtypeStruct((rows, d), F32), jax.ShapeDtypeStruct((rows, d), BF16)],
        compiler_params=_params("arbitrary"),
        name="mix_out",
    )(*args)


MEM_LANES = LANES
MEM_PARTS = MEM_HEAD_DIM // MEM_LANES
MEM_ROWS = MEM_PARTS * MEM_HEADS


def _cross_sample_products(q, mk_ref):
    return lax.dot_general(q, mk_ref[...].astype(BF16), NT_DIMS, preferred_element_type=F32)


def _cross_sample_attend(r, mv_ref):
    n = r.shape[1]
    row = lax.broadcasted_iota(jnp.int32, (MEM_ROWS, n), 0)
    lane_row = lax.broadcasted_iota(jnp.int32, (MEM_ROWS, n), 1) % MEM_ROWS
    own = lane_row == row
    valid = (row < MEM_HEADS) & own

    def other_part(t, lane_shift):
        return pltpu.roll(pltpu.roll(t, MEM_HEADS, 0), lane_shift, 1)

    r = jnp.where(own, r, 0.0)
    lg = jnp.where(valid, r + other_part(r, n - MEM_HEADS), NEG)
    p = jnp.where(valid, jnp.exp(lg - jnp.max(lg, axis=-1, keepdims=True)), 0.0)
    p = p + other_part(p, MEM_HEADS)
    l = jnp.sum(p, axis=-1, keepdims=True)
    return jnp.dot(p.astype(BF16), mv_ref[...].astype(BF16), preferred_element_type=F32) / l


def _stored_rows(qc):
    rows = qc.shape[0]
    return jnp.transpose(qc.reshape(rows, MEM_HEADS, MEM_PARTS, MEM_LANES), (0, 2, 1, 3)).reshape(rows, MEM_ROWS, MEM_LANES)


def _stored_slots(m):
    b, slots = m.shape[:2]
    m = m.reshape(b, slots, MEM_HEADS, MEM_PARTS, MEM_LANES)
    return jnp.transpose(m, (0, 1, 3, 2, 4)).reshape(b, slots * MEM_ROWS, MEM_LANES)


def _unstored_rows(o):
    rows = o.shape[0]
    return jnp.transpose(o.reshape(rows, MEM_PARTS, MEM_HEADS, MEM_LANES), (0, 2, 1, 3)).reshape(rows, -1)


def _mlp_kernel(x1_ref, o_ref, wco_ref, gm_ref, wu_ref, wd_ref, gf_ref, *rest):
    if len(rest) == 1:
        (y_ref,) = rest
        n_seq = 0
    else:
        qs_ref, mk_ref, mv_ref, y_ref, os_ref = rest
        n_seq = qs_ref.shape[0]
    products = [_cross_sample_products(qs_ref[s], mk_ref.at[s]) for s in range(n_seq)]
    x2 = x1_ref[...] + jnp.dot(o_ref[...], wco_ref[...], preferred_element_type=F32)
    hm = _rms(x2, gm_ref[...]).astype(BF16)
    acc = x2
    for c in range(D_FF // FF_CHUNK):
        if c == 1:
            for s in range(n_seq):
                os_ref[s] = _cross_sample_attend(products[s], mv_ref.at[s]).astype(BF16)
        cs = slice(c * FF_CHUNK, (c + 1) * FF_CHUNK)
        a = jnp.dot(hm, wu_ref[:, cs], preferred_element_type=F32)
        a = jnp.square(jnp.maximum(a, 0.0)).astype(BF16)
        acc = acc + jnp.dot(a, wd_ref[cs, :], preferred_element_type=F32)
    y_ref[...] = _rms(acc, gf_ref[...])


def _mlp(x1, o, w_co, g_mlp, w_up, w_down, g_final, sample_cross=None):
    rows, d = x1.shape
    tm = min(512, rows)
    steps = rows // tm
    row = lambda i: (i, 0)
    in_specs = [pl.BlockSpec((tm, d), row), pl.BlockSpec((tm, d), row), _resident(w_co.shape),
                _resident((1, d)), _resident(w_up.shape), _resident(w_down.shape), _resident((1, d))]
    args = [x1, o, w_co, g_mlp, w_up, w_down, g_final]
    out_specs = [pl.BlockSpec((tm, d), row)]
    out_shape = [jax.ShapeDtypeStruct((rows, d), F32)]
    if sample_cross is not None:
        qc, mem_k, mem_v = sample_cross
        b, slots = mem_k.shape[:2]
        assert b % steps == 0
        seq = lambda n: pl.BlockSpec((b // steps, n, MEM_LANES), lambda i: (i, 0, 0))
        in_specs += [seq(MEM_ROWS), seq(slots * MEM_ROWS), seq(slots * MEM_ROWS)]
        args += [_stored_rows(qc), _stored_slots(mem_k), _stored_slots(mem_v)]
        out_specs.append(seq(MEM_ROWS))
        out_shape.append(jax.ShapeDtypeStruct((b, MEM_ROWS, MEM_LANES), BF16))
    outs = pl.pallas_call(
        _mlp_kernel,
        grid=(steps,),
        in_specs=in_specs,
        out_specs=out_specs,
        out_shape=out_shape,
        compiler_params=_params("arbitrary"),
        name="mlp",
    )(*args)
    if sample_cross is None:
        return outs[0]
    return outs[0], _unstored_rows(outs[1])


def _proj_sample_kernel(x_ref, g_ref, w_ref, cos_ref, sin_ref, st_ref, wp_ref, sp_ref,
                        q_ref, k_ref, v_ref, kt_ref, vt_ref, u_ref, po_ref, *, n_prev):
    h = _rms(x_ref[...], g_ref[...]).astype(BF16)
    proj = jnp.dot(h, w_ref[...], preferred_element_type=F32)
    u = proj[:, :POOL_WIDTH]
    cos = cos_ref[...]
    sin = sin_ref[...]
    q_ref[...] = _rope(proj[:, POOL_WIDTH:POOL_WIDTH + ATTN_WIDTH], cos, sin) * (HEAD_DIM ** -0.5)
    k = _rope(proj[:, POOL_WIDTH + ATTN_WIDTH:POOL_WIDTH + 2 * ATTN_WIDTH], cos, sin)
    v = proj[:, POOL_WIDTH + 2 * ATTN_WIDTH:]
    k_ref[...] = k
    v_ref[...] = v
    kt_ref[...] = k.T
    vt_ref[...] = v.T
    u_ref[...] = u
    for g, w in enumerate(POOL_WINDOWS):
        cols = slice(g * POOL_GC, (g + 1) * POOL_GC)
        ug = u[:, cols]
        wsum = ug
        for j in range(1, w):
            wsum = wsum + st_ref[POOL_STATE - j, :, cols]
        d = wsum / float(min(w, n_prev + 1)) - ug
        mixed = jnp.dot(d.astype(BF16), wp_ref[g], preferred_element_type=F32)
        po_ref[:, cols] = (mixed * sp_ref[:, cols]).astype(BF16)


def _proj_sample(x, g, w_in, cos, sin, state_t, w_pool, s_pool, n_prev):
    rows, d = x.shape
    wide = jax.ShapeDtypeStruct((rows, ATTN_WIDTH), F32)
    tall = jax.ShapeDtypeStruct((ATTN_WIDTH, rows), F32)
    full = lambda a: pl.BlockSpec(a.shape, lambda i, nd=len(a.shape): (0,) * nd)
    args = (x, g, w_in, cos, sin, state_t, w_pool, s_pool)
    outs = [wide, wide, wide, tall, tall, wide, jax.ShapeDtypeStruct((rows, POOL_WIDTH), BF16)]
    return pl.pallas_call(
        functools.partial(_proj_sample_kernel, n_prev=n_prev),
        grid=(1,),
        in_specs=[full(a) for a in args],
        out_specs=[full(o) for o in outs],
        out_shape=outs,
        compiler_params=_params("arbitrary"),
        name="proj_sample",
    )(*args)


def _own_lanes():
    head_of_lane = lax.broadcasted_iota(jnp.int32, (N_HEADS, ATTN_WIDTH), 1) // HEAD_DIM
    head = lax.broadcasted_iota(jnp.int32, (N_HEADS, ATTN_WIDTH), 0)
    return head_of_lane == head


def _sample_queries(q):
    return jnp.where(_own_lanes(), q, 0.0)


def _sample_page_logits(q_heads, k_page):
    return jnp.dot(q_heads.astype(BF16), k_page[...].astype(BF16), preferred_element_type=F32)


def _sample_finish(q_heads, lg, k_new, v_new, v_pages):
    page = v_pages[0].shape[1]
    past = lg.shape[1]
    nb = past // MOBA_BLOCK
    mine = _own_lanes()

    s = jnp.concatenate(
        [jnp.sum(lg[:, j * MOBA_BLOCK:(j + 1) * MOBA_BLOCK], axis=-1, keepdims=True) for j in range(nb)], axis=-1)
    blk_id = lax.broadcasted_iota(jnp.int32, (N_HEADS, nb), 1)
    sel = _topk_member(s, blk_id >= 0, blk_id, axis=1)
    key_blk = lax.broadcasted_iota(jnp.int32, (N_HEADS, past), 1) // MOBA_BLOCK
    chosen = jnp.zeros((N_HEADS, past), jnp.bool_)
    for j in range(nb):
        chosen = chosen | ((key_blk == j) & sel[:, j:j + 1])
    lg = jnp.where(chosen, lg, NEG)

    lg_new = jnp.sum(q_heads * k_new, axis=-1, keepdims=True)
    m = jnp.maximum(jnp.max(lg, axis=-1, keepdims=True), lg_new)
    p = jnp.exp(lg - m)
    p_new = jnp.exp(lg_new - m)
    l = jnp.sum(p, axis=-1, keepdims=True) + p_new

    lanes = page
    sub = lax.broadcasted_iota(jnp.int32, (HEAD_DIM, lanes), 0)
    lane = lax.broadcasted_iota(jnp.int32, (HEAD_DIM, lanes), 1)
    tiles = []
    for first in range(0, N_HEADS, lanes // HEAD_DIM):
        row = jnp.zeros((1, lanes), F32)
        for h in range(first, first + lanes // HEAD_DIM):
            acc = jnp.zeros((HEAD_DIM, lanes), F32)
            for n, vp in enumerate(v_pages):
                w = jnp.broadcast_to(p[h:h + 1, n * page:(n + 1) * page], (HEAD_DIM, page))
                acc = acc + vp[h * HEAD_DIM:(h + 1) * HEAD_DIM, :] * w
            total = jnp.sum(acc, axis=1, keepdims=True)
            on_diag = lane == sub + (h - first) * HEAD_DIM
            row = row + jnp.sum(jnp.where(on_diag, total, 0.0), axis=0, keepdims=True)
        tiles.append(row)
    pv = jnp.concatenate(tiles, axis=-1)
    per_lane = lambda t: jnp.sum(jnp.where(mine, t, 0.0), axis=0, keepdims=True)
    return (pv + per_lane(p_new) * v_new) / per_lane(l)


def _rope_tables(pos, heads):
    half = HEAD_DIM // 2
    inv_freq = ROPE_THETA ** (-jnp.arange(half, dtype=F32) / half)
    ang = pos.astype(F32)[:, None] * inv_freq[None, :]
    cos = jnp.cos(ang)
    sin = jnp.sin(ang)
    cos_full = jnp.tile(jnp.concatenate([cos, cos], axis=-1), (1, heads))
    sin_signed = jnp.tile(jnp.concatenate([-sin, sin], axis=-1), (1, heads))
    return cos_full, sin_signed


def kernel(x_prompt, x_sample, cache_k, cache_v, cache_mem_k, cache_mem_v, state_pool, page_table, mem_prompt,
           g_mix, w_in, w_pool, s_pool, w_out, g_cross, g_mem, w_cq, w_ck, w_cv, w_co, g_mlp, w_up, w_down,
           g_final):
    depth = w_in.shape[0]
    assert depth == 1, "one decoder layer"
    bp, seq, d = x_prompt.shape
    bs, dec_seq, _ = x_sample.shape
    assert dec_seq == 1 and seq % MOBA_BLOCK == 0
    n_phys, page = cache_k.shape[1], cache_k.shape[2]
    n_pages = page_table.shape[1]
    past_len = n_pages * page
    assert past_len % MOBA_BLOCK == 0 and past_len >= POOL_STATE
    mem_len = mem_prompt.shape[1]

    l = 0
    row = lambda a: a.reshape(1, -1)
    lp = dict(w_out=w_out[l].astype(BF16), g_cross=row(g_cross[l]), w_cq=w_cq[l].astype(BF16),
              w_co=w_co[l].astype(BF16), g_mlp=row(g_mlp[l]), w_up=w_up[l].astype(BF16),
              w_down=w_down[l].astype(BF16), g_final=row(g_final))
    w_in_b = w_in[l].astype(BF16)
    w_pool_b = w_pool[l].astype(BF16)
    g_mix_r = row(g_mix[l])
    s_pool_r = row(s_pool[l])

    lane_heads = LANES // HEAD_DIM
    cos_p, sin_p = _rope_tables(jnp.arange(seq), lane_heads)
    qtb, kt_p, vt_p, kb, vtb, kmean, po_p, last_p = _proj_prompt(
        x_prompt, g_mix_r, w_in_b, cos_p, sin_p, cos_p[:, :HEAD_DIM].T, sin_p[:, :HEAD_DIM].T, w_pool_b, s_pool_r)
    cos_s, sin_s = _rope_tables(jnp.full((1,), past_len), N_HEADS)
    state_t = jnp.transpose(state_pool[l], (1, 0, 2))
    q_s, k_s, v_s, kt_s, vt_s, u_s, po_s = _proj_sample(x_sample.reshape(bs, d), g_mix_r, w_in_b, cos_s, sin_s,
                                                        state_t, w_pool_b, s_pool_r, past_len)

    tok = lambda a: a.reshape(bs, 1, -1)
    pages_t = lambda c: jnp.transpose(c, (0, 2, 3, 1)).reshape(n_phys, ATTN_WIDTH, page)
    at_p, at_s = _moba(qtb, kb, vtb, kmean.reshape(bp, seq // MOBA_BLOCK, ATTN_WIDTH), page_table, tok(q_s),
                       tok(k_s), tok(v_s), pages_t(cache_k[l]), pages_t(cache_v[l]))

    mk_p, mv_p, mkb, mvb = _mem_kv(mem_prompt.reshape(bp * mem_len, d), row(g_mem[l]), w_ck[l].astype(BF16),
                                   w_cv[l].astype(BF16))
    flat = lambda a: a.reshape(-1, a.shape[-1])
    mix = lambda x, po, at, memory=None: _mix_out(flat(x), flat(po), flat(at), lp["w_out"], lp["g_cross"],
                                                  lp["w_cq"], memory)
    tail = (lp["w_co"], lp["g_mlp"], lp["w_up"], lp["w_down"], lp["g_final"])
    x1_p, o_p = mix(x_prompt, po_p, at_p, (mkb.reshape(bp, mem_len, d), mvb.reshape(bp, mem_len, d)))
    x1_s, qc_s = mix(x_sample, po_s, at_s)
    y_prompt, o_s = _mlp(x1_p, o_p, *tail, sample_cross=(qc_s, cache_mem_k[l], cache_mem_v[l]))
    y_prompt = y_prompt.reshape(bp, seq, d)
    y_sample = _mlp(x1_s, o_s, *tail).reshape(bs, 1, d)

    def heads(t, b_):
        return jnp.transpose(t.reshape(b_, N_HEADS, HEAD_DIM, -1), (0, 3, 1, 2))[None]

    mem_heads = lambda a: a.reshape(1, bp, mem_len, MEM_HEADS, MEM_HEAD_DIM)
    pool_sample = jnp.transpose(jnp.concatenate([state_t[1:], u_s[None]], axis=0), (1, 0, 2))[None]
    return (y_prompt, y_sample, heads(kt_p, bp), heads(vt_p, bp), last_p[:, 1:][None], mem_heads(mk_p),
            mem_heads(mv_p), heads(kt_s.T.reshape(bs, ATTN_WIDTH, 1), bs), heads(vt_s.T.reshape(bs, ATTN_WIDTH, 1), bs),
            pool_sample)
```

```python
import functools

import jax
import jax.numpy as jnp
from jax import lax
from jax.experimental import pallas as pl
from jax.experimental.pallas import tpu as pltpu

D_MODEL = 1024
POOL_WIDTH = 512
POOL_WINDOWS = (2, 4, 8, 16)
POOL_GC = POOL_WIDTH // len(POOL_WINDOWS)
POOL_STATE = max(POOL_WINDOWS) - 1
POOL_HIST = POOL_STATE + 1
ATTN_WIDTH = D_MODEL - POOL_WIDTH
HEAD_DIM = 64
N_HEADS = ATTN_WIDTH // HEAD_DIM
MOBA_BLOCK = 256
MOBA_TOPK = 3
ROPE_THETA = 10000.0
MEM_HEADS = 4
MEM_HEAD_DIM = D_MODEL // MEM_HEADS
D_FF = 4 * D_MODEL
FF_CHUNK = 1024
EPS = 1e-6
NEG = -1e30

VMEM_LIMIT_BYTES = 56 * 1024 * 1024
LANES = 128

F32 = jnp.float32
BF16 = jnp.bfloat16
NT_DIMS = (((1,), (1,)), ((), ()))


def _params(*semantics):
    return pltpu.CompilerParams(dimension_semantics=semantics, vmem_limit_bytes=VMEM_LIMIT_BYTES)


def _resident(shape):
    zeros = (0,) * len(shape)
    return pl.BlockSpec(shape, lambda *_: zeros, pipeline_mode=pl.Buffered(1))


def _rms(x, g):
    return x * lax.rsqrt(jnp.mean(x * x, axis=-1, keepdims=True) + EPS) * g


def _rope(t, cos, sin_signed):
    width = t.shape[-1]
    lane = lax.broadcasted_iota(jnp.int32, t.shape, t.ndim - 1)
    first_half = (lane % HEAD_DIM) < (HEAD_DIM // 2)
    partner = jnp.where(first_half,
                        pltpu.roll(t, width - HEAD_DIM // 2, t.ndim - 1),
                        pltpu.roll(t, HEAD_DIM // 2, t.ndim - 1))
    return t * cos + partner * sin_signed


def _rope_t(t, cos, sin_signed):
    rows = t.shape[0]
    row = lax.broadcasted_iota(jnp.int32, t.shape, 0)
    first_half = (row % HEAD_DIM) < (HEAD_DIM // 2)
    partner = jnp.where(first_half, pltpu.roll(t, rows - HEAD_DIM // 2, 0), pltpu.roll(t, HEAD_DIM // 2, 0))
    return t * cos + partner * sin_signed


def _topk_member(s, valid, idx, axis):
    n = s.shape[axis]
    s = jnp.where(valid, s, NEG)
    rank = jnp.zeros(s.shape, jnp.int32)
    for j in range(n):
        sj = lax.slice_in_dim(s, j, j + 1, axis=axis)
        beats = (sj > s) | ((sj == s) & (j < idx))
        rank = rank + beats.astype(jnp.int32)
    return (rank < MOBA_TOPK) & valid


PROJ_BLOCKS = 2
def _proj_prompt_kernel(x_ref, g_ref, w_ref, cos_ref, sin_ref, cost_ref, sint_ref, wp_ref, sp_ref,
                        qt_ref, kt_ref, vt_ref, kb_ref, vtb_ref, km_ref, po_ref, last_ref, ext_ref):
    i = pl.program_id(1)
    tm = x_ref.shape[1]

    @pl.when(i == 0)
    def _():
        ext_ref[0:POOL_HIST, :] = jnp.zeros((POOL_HIST, POOL_WIDTH), F32)

    h = _rms(x_ref[0], g_ref[...]).astype(BF16)

    def project(first):
        return jnp.dot(h, w_ref[:, first:first + ATTN_WIDTH], preferred_element_type=F32)

    lane_reps = ATTN_WIDTH // cos_ref.shape[1]
    u = project(0)
    k = _rope(project(POOL_WIDTH + ATTN_WIDTH), jnp.tile(cos_ref[...], (1, lane_reps)),
              jnp.tile(sin_ref[...], (1, lane_reps)))
    q = project(POOL_WIDTH)
    v = project(POOL_WIDTH + 2 * ATTN_WIDTH)
    qt = _rope_t(q.T, jnp.tile(cost_ref[...], (N_HEADS, 1)), jnp.tile(sint_ref[...], (N_HEADS, 1)))
    qt_ref[0] = (qt * (HEAD_DIM ** -0.5 * LOG2_E)).astype(BF16)
    vt = v.T
    kt_ref[0] = k.T
    vt_ref[0] = vt
    kb_ref[0] = k.astype(BF16)
    for c in range(tm // MOBA_BLOCK):
        block = slice(c * MOBA_BLOCK, (c + 1) * MOBA_BLOCK)
        vtb_ref[0, c] = vt[:, block].astype(BF16)
        km_ref[0, c] = jnp.mean(k[block, :], axis=0, keepdims=True)

    pos = i * tm + lax.broadcasted_iota(jnp.int32, (tm, 1), 0)
    for g, w in enumerate(POOL_WINDOWS):
        cols = slice(g * POOL_GC, (g + 1) * POOL_GC)
        ug = u[:, cols]
        wsum = jnp.concatenate([ext_ref[:, cols], ug], axis=0)
        shift = 1
        while shift < w:
            wsum = wsum + pltpu.roll(wsum, shift, 0)
            shift *= 2
        cnt = jnp.minimum(w, pos + 1).astype(F32)
        d = wsum[POOL_HIST:, :] / cnt - ug
        mixed = jnp.dot(d.astype(BF16), wp_ref[g], preferred_element_type=F32)
        po_ref[0, :, cols] = (mixed * sp_ref[:, cols]).astype(BF16)
    tail = u[tm - POOL_HIST:, :]
    ext_ref[...] = tail
    last_ref[0] = tail


def _proj_prompt(x, g, w_in, cos, sin, cos_t, sin_t, w_pool, s_pool):
    b, s, d = x.shape
    per_tile = PROJ_BLOCKS if s % (PROJ_BLOCKS * MOBA_BLOCK) == 0 else 1
    tm = per_tile * MOBA_BLOCK
    nq = s // MOBA_BLOCK
    row = lambda bi, i: (bi, i, 0)
    col = lambda bi, i: (bi, 0, i)
    blk = lambda bi, i: (bi, i, 0, 0)
    return pl.pallas_call(
        _proj_prompt_kernel,
        grid=(b, s // tm),
        in_specs=[
            pl.BlockSpec((1, tm, d), row),
            _resident((1, d)),
            _resident(w_in.shape),
            pl.BlockSpec((tm, cos.shape[1]), lambda bi, i: (i, 0)),
            pl.BlockSpec((tm, cos.shape[1]), lambda bi, i: (i, 0)),
            pl.BlockSpec((HEAD_DIM, tm), lambda bi, i: (0, i)),
            pl.BlockSpec((HEAD_DIM, tm), lambda bi, i: (0, i)),
            _resident(w_pool.shape),
            _resident((1, POOL_WIDTH)),
        ],
        out_specs=[
            pl.BlockSpec((1, ATTN_WIDTH, tm), col),
            pl.BlockSpec((1, ATTN_WIDTH, tm), col),
            pl.BlockSpec((1, ATTN_WIDTH, tm), col),
            pl.BlockSpec((1, tm, ATTN_WIDTH), row),
            pl.BlockSpec((1, per_tile, ATTN_WIDTH, MOBA_BLOCK), blk),
            pl.BlockSpec((1, per_tile, 1, ATTN_WIDTH), blk),
            pl.BlockSpec((1, tm, POOL_WIDTH), row),
            pl.BlockSpec((1, POOL_HIST, POOL_WIDTH), lambda bi, i: (bi, 0, 0)),
        ],
        out_shape=[
            jax.ShapeDtypeStruct((b, ATTN_WIDTH, s), BF16),
            jax.ShapeDtypeStruct((b, ATTN_WIDTH, s), F32),
            jax.ShapeDtypeStruct((b, ATTN_WIDTH, s), F32),
            jax.ShapeDtypeStruct((b, s, ATTN_WIDTH), BF16),
            jax.ShapeDtypeStruct((b, nq, ATTN_WIDTH, MOBA_BLOCK), BF16),
            jax.ShapeDtypeStruct((b, nq, 1, ATTN_WIDTH), F32),
            jax.ShapeDtypeStruct((b, s, POOL_WIDTH), BF16),
            jax.ShapeDtypeStruct((b, POOL_HIST, POOL_WIDTH), F32),
        ],
        scratch_shapes=[pltpu.VMEM((POOL_HIST, POOL_WIDTH), F32)],
        compiler_params=_params("arbitrary", "arbitrary"),
        name="proj_prompt",
    )(x, g, w_in, cos, sin, cos_t, sin_t, w_pool, s_pool)


SUBLANES = 8
LOGIT_AHEAD = 4
LOGIT_SLOTS = 8
LOG2_E = 1.4426950408889634
KEY_RING = 3
VALUE_RING = 2


def _query_block(visit, n):
    return jnp.where(visit % 2 == 0, visit // 2, n - 1 - visit // 2)


def _all_sublanes(t, combine):
    for s in (4, 2, 1):
        t = combine(t, pltpu.roll(t, s, 0))
    return t


def _moba_kernel(pt_ref, qt_ref, kb_ref, vtb_ref, km_ref, qs_ref, qn_ref, kn_ref, vn_ref, ck_ref, cv_ref,
                 o_ref, os_ref, acc_ref, m_ref, l_ref, take_ref, lg_ref, kbuf_ref, vbuf_ref, slg_ref, key_sem, val_sem,
                 *, n_pages, per_step):
    nq = pl.num_programs(1)
    i = _query_block(pl.program_id(1), nq)
    step = pl.program_id(0) * nq + pl.program_id(1)
    n_steps = pl.num_programs(0) * nq
    key_ring = kbuf_ref.shape[0]
    val_ring = vbuf_ref.shape[0]
    tq = qt_ref.shape[2]
    nb = km_ref.shape[1]

    def page_copies(cache_ref, buf_ref, sem, page_id, part):
        return [pltpu.make_async_copy(cache_ref.at[page_id(s, p)], buf_ref.at[part, s * n_pages + p], sem.at[part])
                for s in range(per_step) for p in range(n_pages)]

    def start_keys(t):
        for c in page_copies(ck_ref, kbuf_ref, key_sem, lambda s, p: pt_ref[t * per_step + s, p], t % key_ring):
            c.start(priority=1)

    def start_values(t):
        for c in page_copies(cv_ref, vbuf_ref, val_sem, lambda s, p: pt_ref[t * per_step + s, p], t % val_ring):
            c.start()

    def wait_part(cache_ref, buf_ref, sem, part):
        pltpu.make_async_copy(cache_ref.at[pl.ds(0, per_step * n_pages)], buf_ref.at[part], sem.at[part]).wait()

    def key_logits(q_rows, part):
        return [[_sample_page_logits(q_rows[s], kbuf_ref.at[part, s * n_pages + p]) for p in range(n_pages)]
                for s in range(per_step)]

    @pl.when(step == 0)
    def _():
        start_keys(0)
        for t in range(min(2, n_steps)):
            start_values(t)
        for t in range(1, min(key_ring, n_steps)):
            start_keys(t)
        wait_part(ck_ref, kbuf_ref, key_sem, 0)
        first = key_logits([_sample_queries(qs_ref[s]) for s in range(per_step)], 0)
        for s in range(per_step):
            slg_ref[s] = jnp.concatenate(first[s], axis=-1)
        if n_steps > key_ring:
            start_keys(key_ring)

    @pl.when((step >= 1) & (step + 1 < n_steps))
    def _():
        start_values(step + 1)

    @pl.when((step >= 1) & (step + key_ring < n_steps))
    def _():
        start_keys(step + key_ring)

    @pl.when(step + 1 < n_steps)
    def _():
        wait_part(ck_ref, kbuf_ref, key_sem, (step + 1) % key_ring)

    wait_part(cv_ref, vbuf_ref, val_sem, step % val_ring)
    slot = step % val_ring
    next_keys = (step + 1) % key_ring


    km_rows = jnp.concatenate([km_ref[0]] * N_HEADS, axis=0)
    row_head = lax.broadcasted_iota(jnp.int32, km_rows.shape, 0) // nb
    lane_head = lax.broadcasted_iota(jnp.int32, km_rows.shape, 1) // HEAD_DIM
    km_heads = jnp.where(row_head == lane_head, km_rows, 0.0).astype(BF16)
    s = jnp.dot(km_heads, qt_ref[0], preferred_element_type=F32).reshape(N_HEADS, nb, tq)
    blk = lax.broadcasted_iota(jnp.int32, s.shape, 1)
    sel = _topk_member(s, blk < i, blk, axis=1).astype(F32)

    for h in range(N_HEADS):
        for jb in range(nb):
            take_ref[h, jb] = jnp.broadcast_to(sel[h, jb:jb + 1, :], (SUBLANES, tq))

    def head_rows(h):
        return slice(h * HEAD_DIM, (h + 1) * HEAD_DIM)

    def stage_logits(h, j):
        off = pl.multiple_of(j * MOBA_BLOCK, MOBA_BLOCK)
        lg_ref[h % LOGIT_SLOTS] = jnp.dot(kb_ref[0, pl.ds(off, MOBA_BLOCK), head_rows(h)],
                                          qt_ref[0, head_rows(h), :], preferred_element_type=F32)

    def weighted_values(h, j, p3):
        p = p3.reshape(MOBA_BLOCK, tq).astype(BF16)
        pv = jnp.dot(vtb_ref[0, j, head_rows(h), :], p, preferred_element_type=F32)
        return pv.reshape(HEAD_DIM // SUBLANES, SUBLANES, tq)

    def tiles(t):
        return t.reshape(t.shape[0] // SUBLANES, SUBLANES, tq)

    def block_pass(j, j_next, own, between=None):
        for h in range(N_HEADS):
            ahead = h + LOGIT_AHEAD
            if ahead < N_HEADS:
                stage_logits(ahead, j)
            else:
                stage_logits(ahead - N_HEADS, j_next)
            if between is not None:
                between(h)
            lg = lg_ref[h % LOGIT_SLOTS]
            if own:
                kk = lax.broadcasted_iota(jnp.int32, lg.shape, 0)
                qq = lax.broadcasted_iota(jnp.int32, lg.shape, 1)
                lg3 = tiles(jnp.where(kk <= qq, lg, NEG))
                m = _all_sublanes(jnp.max(lg3, axis=0), jnp.maximum)
                p3 = jnp.exp2(lg3 - m[None])
                m_ref[h] = m
                l_ref[h] = _all_sublanes(jnp.sum(p3, axis=0), jnp.add)
                acc_ref[head_rows(h), :] = weighted_values(h, j, p3).reshape(HEAD_DIM, tq)
            else:
                lg3 = tiles(lg)
                take = take_ref[h, j] > 0.0
                m_old = m_ref[h]
                l_old = l_ref[h]
                acc_old = tiles(acc_ref[head_rows(h), :])
                m = jnp.maximum(m_old, _all_sublanes(jnp.max(lg3, axis=0), jnp.maximum))
                a = jnp.exp2(m_old - m)
                p3 = jnp.exp2(lg3 - m[None])
                l = a * l_old + _all_sublanes(jnp.sum(p3, axis=0), jnp.add)
                acc3 = a[None] * acc_old + weighted_values(h, j, p3)
                m_ref[h] = jnp.where(take, m, m_old)
                l_ref[h] = jnp.where(take, l, l_old)
                acc_ref[head_rows(h), :] = jnp.where(take[None], acc3, acc_old).reshape(HEAD_DIM, tq)

    last_past = jnp.maximum(i - 1, 0)
    for h in range(LOGIT_AHEAD):
        stage_logits(h, i)

    for s in range(per_step):
        v_pages = [vbuf_ref.at[slot, s * n_pages + p] for p in range(n_pages)]
        os_ref[s] = _sample_finish(_sample_queries(qs_ref[s]), slg_ref[s], kn_ref[s], vn_ref[s], v_pages).astype(BF16)

    next_q = [_sample_queries(qn_ref[s]) for s in range(per_step)]
    jobs = [(s, p) for s in range(per_step) for p in range(n_pages)]
    per_head = -(-len(jobs) // N_HEADS)
    page_logits = {}

    def sample_page_logits(h):
        for s, p in jobs[h * per_head:(h + 1) * per_head]:
            page_logits[s, p] = _sample_page_logits(next_q[s], kbuf_ref.at[next_keys, s * n_pages + p])

    block_pass(i, 0, own=True, between=sample_page_logits)
    for s in range(per_step):
        slg_ref[s] = jnp.concatenate([page_logits[s, p] for p in range(n_pages)], axis=-1)

    def two_past_blocks(jj, carry):
        j = 2 * jj
        block_pass(j, j + 1, own=False)
        block_pass(j + 1, jnp.minimum(j + 2, last_past), own=False)
        return carry

    lax.fori_loop(0, i // 2, two_past_blocks, 0)

    @pl.when(i % 2 == 1)
    def _():
        block_pass(last_past, last_past, own=False)

    for h in range(N_HEADS):
        rows = slice(h * HEAD_DIM, (h + 1) * HEAD_DIM)
        acc_ref[rows, :] = (tiles(acc_ref[rows, :]) / l_ref[h][None]).reshape(HEAD_DIM, tq)
    o_ref[0] = acc_ref[...].T.astype(BF16)


def _moba(qtb, kb, vtb, kmean, page_table, q_s, k_s, v_s, cache_kt, cache_vt):
    b, w, s = qtb.shape
    nq = s // MOBA_BLOCK
    bs, n_pages = page_table.shape
    page = cache_kt.shape[2]
    steps = b * nq
    assert bs % steps == 0
    per_step = bs // steps
    step_seq = lambda bi, i, pt: (bi * nq + i, 0, 0)
    tok = pl.BlockSpec((per_step, 1, w), step_seq)
    tok_next = pl.BlockSpec((per_step, 1, w), lambda bi, i, pt: (jnp.minimum(bi * nq + i + 1, steps - 1), 0, 0))
    in_hbm = pl.BlockSpec(memory_space=pl.ANY)
    page_ring = lambda parts: pltpu.VMEM((parts, per_step * n_pages, w, page), cache_kt.dtype)
    grid_spec = pltpu.PrefetchScalarGridSpec(
        num_scalar_prefetch=1,
        grid=(b, nq),
        in_specs=[
            pl.BlockSpec((1, w, MOBA_BLOCK), lambda bi, i, pt: (bi, 0, _query_block(i, nq))),
            pl.BlockSpec((1, s, w), lambda bi, i, pt: (bi, 0, 0)),
            pl.BlockSpec((1, nq, w, MOBA_BLOCK), lambda bi, i, pt: (bi, 0, 0, 0)),
            pl.BlockSpec((1, nq, w), lambda bi, i, pt: (bi, 0, 0)),
            tok, tok_next, tok, tok, in_hbm, in_hbm,
        ],
        out_specs=[pl.BlockSpec((1, MOBA_BLOCK, w), lambda bi, i, pt: (bi, _query_block(i, nq), 0)), tok],
        scratch_shapes=[pltpu.VMEM((w, MOBA_BLOCK), F32),
                        pltpu.VMEM((N_HEADS, SUBLANES, MOBA_BLOCK), F32),
                        pltpu.VMEM((N_HEADS, SUBLANES, MOBA_BLOCK), F32),
                        pltpu.VMEM((N_HEADS, nq, SUBLANES, MOBA_BLOCK), F32),
                        pltpu.VMEM((LOGIT_SLOTS, MOBA_BLOCK, MOBA_BLOCK), F32),
                        page_ring(KEY_RING), page_ring(VALUE_RING),
                        pltpu.VMEM((per_step, N_HEADS, n_pages * page), F32),
                        pltpu.SemaphoreType.DMA((KEY_RING,)), pltpu.SemaphoreType.DMA((VALUE_RING,))],
    )
    return pl.pallas_call(
        functools.partial(_moba_kernel, n_pages=n_pages, per_step=per_step),
        grid_spec=grid_spec,
        out_shape=[jax.ShapeDtypeStruct((b, s, w), BF16), jax.ShapeDtypeStruct((bs, 1, w), BF16)],
        compiler_params=_params("arbitrary", "arbitrary"),
        name="moba",
    )(page_table, qtb, kb, vtb, kmean, q_s, q_s, k_s, v_s, cache_kt, cache_vt)


def _mem_kv_kernel(m_ref, g_ref, wk_ref, wv_ref, mk_ref, mv_ref, mkb_ref, mvb_ref):
    h = _rms(m_ref[...], g_ref[...]).astype(BF16)
    mk = jnp.dot(h, wk_ref[...], preferred_element_type=F32)
    mv = jnp.dot(h, wv_ref[...], preferred_element_type=F32)
    mk_ref[...] = mk
    mv_ref[...] = mv
    mkb_ref[...] = mk.astype(BF16)
    mvb_ref[...] = mv.astype(BF16)


def _mem_kv(mem, g, w_ck, w_cv):
    rows, d = mem.shape
    tm = 512
    row = lambda i: (i, 0)
    return pl.pallas_call(
        _mem_kv_kernel,
        grid=(rows // tm,),
        in_specs=[pl.BlockSpec((tm, d), row), _resident((1, d)), _resident(w_ck.shape), _resident(w_cv.shape)],
        out_specs=[pl.BlockSpec((tm, d), row)] * 4,
        out_shape=[jax.ShapeDtypeStruct((rows, d), F32)] * 2 + [jax.ShapeDtypeStruct((rows, d), BF16)] * 2,
        compiler_params=_params("arbitrary"),
        name="mem_kv",
    )(mem, g, w_ck, w_cv)


MIX_CHAINS = 2
MIX_CHAIN_ROWS = 512


def _mix_out_kernel(x_ref, po_ref, at_ref, wo_ref, g_ref, wq_ref, *rest):
    with_memory = len(rest) == 4
    if with_memory:
        mk_ref, mv_ref, x1_ref, out_ref = rest
    else:
        x1_ref, out_ref = rest
    tm = x_ref.shape[0]
    chains = MIX_CHAINS if with_memory else 1
    rows_of = lambda c: slice(c * (tm // chains), (c + 1) * (tm // chains))

    def residual(c):
        rows = rows_of(c)
        cat = jnp.concatenate([po_ref[rows, :], at_ref[rows, :]], axis=-1)
        x1 = x_ref[rows, :] + jnp.dot(cat, wo_ref[...], preferred_element_type=F32)
        x1_ref[rows, :] = x1
        return _rms(x1, g_ref[...]).astype(BF16)

    def queries(hq):
        return (jnp.dot(hq, wq_ref[...], preferred_element_type=F32) * (MEM_HEAD_DIM ** -0.5)).astype(BF16)

    def attend(c, q):
        for h in range(MEM_HEADS):
            sl = slice(h * MEM_HEAD_DIM, (h + 1) * MEM_HEAD_DIM)
            lg = lax.dot_general(q[:, sl], mk_ref[0, :, sl], NT_DIMS, preferred_element_type=F32)
            p = jnp.exp(lg - jnp.max(lg, axis=-1, keepdims=True))
            l = jnp.sum(p, axis=-1, keepdims=True)
            o = jnp.dot(p.astype(BF16), mv_ref[0, :, sl], preferred_element_type=F32)
            out_ref[rows_of(c), sl] = (o / l).astype(BF16)

    if not with_memory:
        out_ref[...] = queries(residual(0))
        return
    q = queries(residual(0))
    for c in range(chains):
        hq_next = residual(c + 1) if c + 1 < chains else None
        attend(c, q)
        if hq_next is not None:
            q = queries(hq_next)


def _mix_out(x, po, at, w_out, g_cross, w_cq, memory=None):
    rows, d = x.shape
    tm = min(MIX_CHAINS * MIX_CHAIN_ROWS if memory is not None else MIX_CHAIN_ROWS, rows)
    row = lambda i: (i, 0)
    in_specs = [pl.BlockSpec((tm, d), row), pl.BlockSpec((tm, POOL_WIDTH), row), pl.BlockSpec((tm, ATTN_WIDTH), row),
                _resident(w_out.shape), _resident((1, d)), _resident(w_cq.shape)]
    args = [x, po, at, w_out, g_cross, w_cq]
    if memory is not None:
        mk, mv = memory
        tiles_per_batch = rows // mk.shape[0] // tm
        assert tiles_per_batch * tm * mk.shape[0] == rows and tm % MIX_CHAINS == 0
        of_batch = pl.BlockSpec((1,) + mk.shape[1:], lambda i: (i // tiles_per_batch, 0, 0))
        in_specs += [of_batch, of_batch]
        args += [mk, mv]
    return pl.pallas_call(
        _mix_out_kernel,
        grid=(rows // tm,),
        in_specs=in_specs,
        out_specs=[pl.BlockSpec((tm, d), row), pl.BlockSpec((tm, d), row)],
        out_shape=[jax.ShapeDtypeStruct((rows, d), F32), jax.ShapeDtypeStruct((rows, d), BF16)],
        compiler_params=_params("arbitrary"),
        name="mix_out",
    )(*args)


MEM_LANES = LANES
MEM_PARTS = MEM_HEAD_DIM // MEM_LANES
MEM_ROWS = MEM_PARTS * MEM_HEADS


def _cross_sample_products(q, mk_ref):
    return lax.dot_general(q, mk_ref[...].astype(BF16), NT_DIMS, preferred_element_type=F32)


def _cross_sample_attend(r, mv_ref):
    n = r.shape[1]
    row = lax.broadcasted_iota(jnp.int32, (MEM_ROWS, n), 0)
    lane_row = lax.broadcasted_iota(jnp.int32, (MEM_ROWS, n), 1) % MEM_ROWS
    own = lane_row == row
    valid = (row < MEM_HEADS) & own

    def other_part(t, lane_shift):
        return pltpu.roll(pltpu.roll(t, MEM_HEADS, 0), lane_shift, 1)

    r = jnp.where(own, r, 0.0)
    lg = jnp.where(valid, r + other_part(r, n - MEM_HEADS), NEG)
    p = jnp.where(valid, jnp.exp(lg - jnp.max(lg, axis=-1, keepdims=True)), 0.0)
    p = p + other_part(p, MEM_HEADS)
    l = jnp.sum(p, axis=-1, keepdims=True)
    return jnp.dot(p.astype(BF16), mv_ref[...].astype(BF16), preferred_element_type=F32) / l


def _stored_rows(qc):
    rows = qc.shape[0]
    return jnp.transpose(qc.reshape(rows, MEM_HEADS, MEM_PARTS, MEM_LANES), (0, 2, 1, 3)).reshape(rows, MEM_ROWS, MEM_LANES)


def _stored_slots(m):
    b, slots = m.shape[:2]
    m = m.reshape(b, slots, MEM_HEADS, MEM_PARTS, MEM_LANES)
    return jnp.transpose(m, (0, 1, 3, 2, 4)).reshape(b, slots * MEM_ROWS, MEM_LANES)


def _unstored_rows(o):
    rows = o.shape[0]
    return jnp.transpose(o.reshape(rows, MEM_PARTS, MEM_HEADS, MEM_LANES), (0, 2, 1, 3)).reshape(rows, -1)


def _mlp_kernel(x1_ref, o_ref, wco_ref, gm_ref, wu_ref, wd_ref, gf_ref, *rest):
    if len(rest) == 1:
        (y_ref,) = rest
        n_seq = 0
    else:
        qs_ref, mk_ref, mv_ref, y_ref, os_ref = rest
        n_seq = qs_ref.shape[0]
    products = [_cross_sample_products(qs_ref[s], mk_ref.at[s]) for s in range(n_seq)]
    x2 = x1_ref[...] + jnp.dot(o_ref[...], wco_ref[...], preferred_element_type=F32)
    hm = _rms(x2, gm_ref[...]).astype(BF16)
    acc = x2
    for c in range(D_FF // FF_CHUNK):
        if c == 1:
            for s in range(n_seq):
                os_ref[s] = _cross_sample_attend(products[s], mv_ref.at[s]).astype(BF16)
        cs = slice(c * FF_CHUNK, (c + 1) * FF_CHUNK)
        a = jnp.dot(hm, wu_ref[:, cs], preferred_element_type=F32)
        a = jnp.square(jnp.maximum(a, 0.0)).astype(BF16)
        acc = acc + jnp.dot(a, wd_ref[cs, :], preferred_element_type=F32)
    y_ref[...] = _rms(acc, gf_ref[...])


def _mlp(x1, o, w_co, g_mlp, w_up, w_down, g_final, sample_cross=None):
    rows, d = x1.shape
    tm = min(512, rows)
    steps = rows // tm
    row = lambda i: (i, 0)
    in_specs = [pl.BlockSpec((tm, d), row), pl.BlockSpec((tm, d), row), _resident(w_co.shape),
                _resident((1, d)), _resident(w_up.shape), _resident(w_down.shape), _resident((1, d))]
    args = [x1, o, w_co, g_mlp, w_up, w_down, g_final]
    out_specs = [pl.BlockSpec((tm, d), row)]
    out_shape = [jax.ShapeDtypeStruct((rows, d), F32)]
    if sample_cross is not None:
        qc, mem_k, mem_v = sample_cross
        b, slots = mem_k.shape[:2]
        assert b % steps == 0
        seq = lambda n: pl.BlockSpec((b // steps, n, MEM_LANES), lambda i: (i, 0, 0))
        in_specs += [seq(MEM_ROWS), seq(slots * MEM_ROWS), seq(slots * MEM_ROWS)]
        args += [_stored_rows(qc), _stored_slots(mem_k), _stored_slots(mem_v)]
        out_specs.append(seq(MEM_ROWS))
        out_shape.append(jax.ShapeDtypeStruct((b, MEM_ROWS, MEM_LANES), BF16))
    outs = pl.pallas_call(
        _mlp_kernel,
        grid=(steps,),
        in_specs=in_specs,
        out_specs=out_specs,
        out_shape=out_shape,
        compiler_params=_params("arbitrary"),
        name="mlp",
    )(*args)
    if sample_cross is None:
        return outs[0]
    return outs[0], _unstored_rows(outs[1])


def _proj_sample_kernel(x_ref, g_ref, w_ref, cos_ref, sin_ref, st_ref, wp_ref, sp_ref,
                        q_ref, k_ref, v_ref, kt_ref, vt_ref, u_ref, po_ref, *, n_prev):
    h = _rms(x_ref[...], g_ref[...]).astype(BF16)
    proj = jnp.dot(h, w_ref[...], preferred_element_type=F32)
    u = proj[:, :POOL_WIDTH]
    cos = cos_ref[...]
    sin = sin_ref[...]
    q_ref[...] = _rope(proj[:, POOL_WIDTH:POOL_WIDTH + ATTN_WIDTH], cos, sin) * (HEAD_DIM ** -0.5)
    k = _rope(proj[:, POOL_WIDTH + ATTN_WIDTH:POOL_WIDTH + 2 * ATTN_WIDTH], cos, sin)
    v = proj[:, POOL_WIDTH + 2 * ATTN_WIDTH:]
    k_ref[...] = k
    v_ref[...] = v
    kt_ref[...] = k.T
    vt_ref[...] = v.T
    u_ref[...] = u
    for g, w in enumerate(POOL_WINDOWS):
        cols = slice(g * POOL_GC, (g + 1) * POOL_GC)
        ug = u[:, cols]
        wsum = ug
        for j in range(1, w):
            wsum = wsum + st_ref[POOL_STATE - j, :, cols]
        d = wsum / float(min(w, n_prev + 1)) - ug
        mixed = jnp.dot(d.astype(BF16), wp_ref[g], preferred_element_type=F32)
        po_ref[:, cols] = (mixed * sp_ref[:, cols]).astype(BF16)


def _proj_sample(x, g, w_in, cos, sin, state_t, w_pool, s_pool, n_prev):
    rows, d = x.shape
    wide = jax.ShapeDtypeStruct((rows, ATTN_WIDTH), F32)
    tall = jax.ShapeDtypeStruct((ATTN_WIDTH, rows), F32)
    full = lambda a: pl.BlockSpec(a.shape, lambda i, nd=len(a.shape): (0,) * nd)
    args = (x, g, w_in, cos, sin, state_t, w_pool, s_pool)
    outs = [wide, wide, wide, tall, tall, wide, jax.ShapeDtypeStruct((rows, POOL_WIDTH), BF16)]
    return pl.pallas_call(
        functools.partial(_proj_sample_kernel, n_prev=n_prev),
        grid=(1,),
        in_specs=[full(a) for a in args],
        out_specs=[full(o) for o in outs],
        out_shape=outs,
        compiler_params=_params("arbitrary"),
        name="proj_sample",
    )(*args)


def _own_lanes():
    head_of_lane = lax.broadcasted_iota(jnp.int32, (N_HEADS, ATTN_WIDTH), 1) // HEAD_DIM
    head = lax.broadcasted_iota(jnp.int32, (N_HEADS, ATTN_WIDTH), 0)
    return head_of_lane == head


def _sample_queries(q):
    return jnp.where(_own_lanes(), q, 0.0)


def _sample_page_logits(q_heads, k_page):
    return jnp.dot(q_heads.astype(BF16), k_page[...].astype(BF16), preferred_element_type=F32)


def _sample_finish(q_heads, lg, k_new, v_new, v_pages):
    page = v_pages[0].shape[1]
    past = lg.shape[1]
    nb = past // MOBA_BLOCK
    mine = _own_lanes()

    s = jnp.concatenate(
        [jnp.sum(lg[:, j * MOBA_BLOCK:(j + 1) * MOBA_BLOCK], axis=-1, keepdims=True) for j in range(nb)], axis=-1)
    blk_id = lax.broadcasted_iota(jnp.int32, (N_HEADS, nb), 1)
    sel = _topk_member(s, blk_id >= 0, blk_id, axis=1)
    key_blk = lax.broadcasted_iota(jnp.int32, (N_HEADS, past), 1) // MOBA_BLOCK
    chosen = jnp.zeros((N_HEADS, past), jnp.bool_)
    for j in range(nb):
        chosen = chosen | ((key_blk == j) & sel[:, j:j + 1])
    lg = jnp.where(chosen, lg, NEG)

    lg_new = jnp.sum(q_heads * k_new, axis=-1, keepdims=True)
    m = jnp.maximum(jnp.max(lg, axis=-1, keepdims=True), lg_new)
    p = jnp.exp(lg - m)
    p_new = jnp.exp(lg_new - m)
    l = jnp.sum(p, axis=-1, keepdims=True) + p_new

    lanes = page
    sub = lax.broadcasted_iota(jnp.int32, (HEAD_DIM, lanes), 0)
    lane = lax.broadcasted_iota(jnp.int32, (HEAD_DIM, lanes), 1)
    tiles = []
    for first in range(0, N_HEADS, lanes // HEAD_DIM):
        row = jnp.zeros((1, lanes), F32)
        for h in range(first, first + lanes // HEAD_DIM):
            acc = jnp.zeros((HEAD_DIM, lanes), F32)
            for n, vp in enumerate(v_pages):
                w = jnp.broadcast_to(p[h:h + 1, n * page:(n + 1) * page], (HEAD_DIM, page))
                acc = acc + vp[h * HEAD_DIM:(h + 1) * HEAD_DIM, :] * w
            total = jnp.sum(acc, axis=1, keepdims=True)
            on_diag = lane == sub + (h - first) * HEAD_DIM
            row = row + jnp.sum(jnp.where(on_diag, total, 0.0), axis=0, keepdims=True)
        tiles.append(row)
    pv = jnp.concatenate(tiles, axis=-1)
    per_lane = lambda t: jnp.sum(jnp.where(mine, t, 0.0), axis=0, keepdims=True)
    return (pv + per_lane(p_new) * v_new) / per_lane(l)


def _rope_tables(pos, heads):
    half = HEAD_DIM // 2
    inv_freq = ROPE_THETA ** (-jnp.arange(half, dtype=F32) / half)
    ang = pos.astype(F32)[:, None] * inv_freq[None, :]
    cos = jnp.cos(ang)
    sin = jnp.sin(ang)
    cos_full = jnp.tile(jnp.concatenate([cos, cos], axis=-1), (1, heads))
    sin_signed = jnp.tile(jnp.concatenate([-sin, sin], axis=-1), (1, heads))
    return cos_full, sin_signed


def kernel(x_prompt, x_sample, cache_k, cache_v, cache_mem_k, cache_mem_v, state_pool, page_table, mem_prompt,
           g_mix, w_in, w_pool, s_pool, w_out, g_cross, g_mem, w_cq, w_ck, w_cv, w_co, g_mlp, w_up, w_down,
           g_final):
    depth = w_in.shape[0]
    assert depth == 1, "one decoder layer"
    bp, seq, d = x_prompt.shape
    bs, dec_seq, _ = x_sample.shape
    assert dec_seq == 1 and seq % MOBA_BLOCK == 0
    n_phys, page = cache_k.shape[1], cache_k.shape[2]
    n_pages = page_table.shape[1]
    past_len = n_pages * page
    assert past_len % MOBA_BLOCK == 0 and past_len >= POOL_STATE
    mem_len = mem_prompt.shape[1]

    l = 0
    row = lambda a: a.reshape(1, -1)
    lp = dict(w_out=w_out[l].astype(BF16), g_cross=row(g_cross[l]), w_cq=w_cq[l].astype(BF16),
              w_co=w_co[l].astype(BF16), g_mlp=row(g_mlp[l]), w_up=w_up[l].astype(BF16),
              w_down=w_down[l].astype(BF16), g_final=row(g_final))
    w_in_b = w_in[l].astype(BF16)
    w_pool_b = w_pool[l].astype(BF16)
    g_mix_r = row(g_mix[l])
    s_pool_r = row(s_pool[l])

    lane_heads = LANES // HEAD_DIM
    cos_p, sin_p = _rope_tables(jnp.arange(seq), lane_heads)
    qtb, kt_p, vt_p, kb, vtb, kmean, po_p, last_p = _proj_prompt(
        x_prompt, g_mix_r, w_in_b, cos_p, sin_p, cos_p[:, :HEAD_DIM].T, sin_p[:, :HEAD_DIM].T, w_pool_b, s_pool_r)
    cos_s, sin_s = _rope_tables(jnp.full((1,), past_len), N_HEADS)
    state_t = jnp.transpose(state_pool[l], (1, 0, 2))
    q_s, k_s, v_s, kt_s, vt_s, u_s, po_s = _proj_sample(x_sample.reshape(bs, d), g_mix_r, w_in_b, cos_s, sin_s,
                                                        state_t, w_pool_b, s_pool_r, past_len)

    tok = lambda a: a.reshape(bs, 1, -1)
    pages_t = lambda c: jnp.transpose(c, (0, 2, 3, 1)).reshape(n_phys, ATTN_WIDTH, page)
    at_p, at_s = _moba(qtb, kb, vtb, kmean.reshape(bp, seq // MOBA_BLOCK, ATTN_WIDTH), page_table, tok(q_s),
                       tok(k_s), tok(v_s), pages_t(cache_k[l]), pages_t(cache_v[l]))

    mk_p, mv_p, mkb, mvb = _mem_kv(mem_prompt.reshape(bp * mem_len, d), row(g_mem[l]), w_ck[l].astype(BF16),
                                   w_cv[l].astype(BF16))
    flat = lambda a: a.reshape(-1, a.shape[-1])
    mix = lambda x, po, at, memory=None: _mix_out(flat(x), flat(po), flat(at), lp["w_out"], lp["g_cross"],
                                                  lp["w_cq"], memory)
    tail = (lp["w_co"], lp["g_mlp"], lp["w_up"], lp["w_down"], lp["g_final"])
    x1_p, o_p = mix(x_prompt, po_p, at_p, (mkb.reshape(bp, mem_len, d), mvb.reshape(bp, mem_len, d)))
    x1_s, qc_s = mix(x_sample, po_s, at_s)
    y_prompt, o_s = _mlp(x1_p, o_p, *tail, sample_cross=(qc_s, cache_mem_k[l], cache_mem_v[l]))
    y_prompt = y_prompt.reshape(bp, seq, d)
    y_sample = _mlp(x1_s, o_s, *tail).reshape(bs, 1, d)

    def heads(t, b_):
        return jnp.transpose(t.reshape(b_, N_HEADS, HEAD_DIM, -1), (0, 3, 1, 2))[None]

    mem_heads = lambda a: a.reshape(1, bp, mem_len, MEM_HEADS, MEM_HEAD_DIM)
    pool_sample = jnp.transpose(jnp.concatenate([state_t[1:], u_s[None]], axis=0), (1, 0, 2))[None]
    return (y_prompt, y_sample, heads(kt_p, bp), heads(vt_p, bp), last_p[:, 1:][None], mem_heads(mk_p),
            mem_heads(mv_p), heads(kt_s.T.reshape(bs, ATTN_WIDTH, 1), bs), heads(vt_s.T.reshape(bs, ATTN_WIDTH, 1), bs),
            pool_sample)
```

```python
import functools

import jax
import jax.numpy as jnp
from jax import lax
from jax.experimental import pallas as pl
from jax.experimental.pallas import tpu as pltpu

D_MODEL = 1024
POOL_WIDTH = 512
POOL_WINDOWS = (2, 4, 8, 16)
POOL_GC = POOL_WIDTH // len(POOL_WINDOWS)
POOL_STATE = max(POOL_WINDOWS) - 1
POOL_HIST = POOL_STATE + 1
ATTN_WIDTH = D_MODEL - POOL_WIDTH
HEAD_DIM = 64
N_HEADS = ATTN_WIDTH // HEAD_DIM
MOBA_BLOCK = 256
MOBA_TOPK = 3
ROPE_THETA = 10000.0
MEM_HEADS = 4
MEM_HEAD_DIM = D_MODEL // MEM_HEADS
D_FF = 4 * D_MODEL
FF_CHUNK = 1024
EPS = 1e-6
NEG = -1e30

VMEM_LIMIT_BYTES = 56 * 1024 * 1024
LANES = 128

F32 = jnp.float32
BF16 = jnp.bfloat16
NT_DIMS = (((1,), (1,)), ((), ()))


def _params(*semantics):
    return pltpu.CompilerParams(dimension_semantics=semantics, vmem_limit_bytes=VMEM_LIMIT_BYTES)


def _resident(shape):
    zeros = (0,) * len(shape)
    return pl.BlockSpec(shape, lambda *_: zeros, pipeline_mode=pl.Buffered(1))


def _rms(x, g):
    return x * lax.rsqrt(jnp.mean(x * x, axis=-1, keepdims=True) + EPS) * g


def _rope(t, cos, sin_signed):
    width = t.shape[-1]
    lane = lax.broadcasted_iota(jnp.int32, t.shape, t.ndim - 1)
    first_half = (lane % HEAD_DIM) < (HEAD_DIM // 2)
    partner = jnp.where(first_half,
                        pltpu.roll(t, width - HEAD_DIM // 2, t.ndim - 1),
                        pltpu.roll(t, HEAD_DIM // 2, t.ndim - 1))
    return t * cos + partner * sin_signed


def _rope_t(t, cos, sin_signed):
    rows = t.shape[0]
    row = lax.broadcasted_iota(jnp.int32, t.shape, 0)
    first_half = (row % HEAD_DIM) < (HEAD_DIM // 2)
    partner = jnp.where(first_half, pltpu.roll(t, rows - HEAD_DIM // 2, 0), pltpu.roll(t, HEAD_DIM // 2, 0))
    return t * cos + partner * sin_signed


def _topk_member(s, valid, idx, axis):
    n = s.shape[axis]
    s = jnp.where(valid, s, NEG)
    rank = jnp.zeros(s.shape, jnp.int32)
    for j in range(n):
        sj = lax.slice_in_dim(s, j, j + 1, axis=axis)
        beats = (sj > s) | ((sj == s) & (j < idx))
        rank = rank + beats.astype(jnp.int32)
    return (rank < MOBA_TOPK) & valid


PROJ_BLOCKS = 2
def _proj_prompt_kernel(x_ref, g_ref, w_ref, cos_ref, sin_ref, cost_ref, sint_ref, wp_ref, sp_ref,
                        qt_ref, kt_ref, vt_ref, kb_ref, vtb_ref, km_ref, po_ref, last_ref, ext_ref):
    i = pl.program_id(1)
    tm = x_ref.shape[1]

    @pl.when(i == 0)
    def _():
        ext_ref[0:POOL_HIST, :] = jnp.zeros((POOL_HIST, POOL_WIDTH), F32)

    h = _rms(x_ref[0], g_ref[...]).astype(BF16)

    def project(first):
        return jnp.dot(h, w_ref[:, first:first + ATTN_WIDTH], preferred_element_type=F32)

    lane_reps = ATTN_WIDTH // cos_ref.shape[1]
    u = project(0)
    k = _rope(project(POOL_WIDTH + ATTN_WIDTH), jnp.tile(cos_ref[...], (1, lane_reps)),
              jnp.tile(sin_ref[...], (1, lane_reps)))
    q = project(POOL_WIDTH)
    v = project(POOL_WIDTH + 2 * ATTN_WIDTH)
    qt = _rope_t(q.T, jnp.tile(cost_ref[...], (N_HEADS, 1)), jnp.tile(sint_ref[...], (N_HEADS, 1)))
    qt_ref[0] = (qt * (HEAD_DIM ** -0.5 * LOG2_E)).astype(BF16)
    vt = v.T
    kt_ref[0] = k.T
    vt_ref[0] = vt
    kb_ref[0] = k.astype(BF16)
    for c in range(tm // MOBA_BLOCK):
        block = slice(c * MOBA_BLOCK, (c + 1) * MOBA_BLOCK)
        vtb_ref[0, c] = vt[:, block].astype(BF16)
        km_ref[0, c] = jnp.mean(k[block, :], axis=0, keepdims=True)

    pos = i * tm + lax.broadcasted_iota(jnp.int32, (tm, 1), 0)
    for g, w in enumerate(POOL_WINDOWS):
        cols = slice(g * POOL_GC, (g + 1) * POOL_GC)
        ug = u[:, cols]
        wsum = jnp.concatenate([ext_ref[:, cols], ug], axis=0)
        shift = 1
        while shift < w:
            wsum = wsum + pltpu.roll(wsum, shift, 0)
            shift *= 2
        cnt = jnp.minimum(w, pos + 1).astype(F32)
        d = wsum[POOL_HIST:, :] / cnt - ug
        mixed = jnp.dot(d.astype(BF16), wp_ref[g], preferred_element_type=F32)
        po_ref[0, :, cols] = (mixed * sp_ref[:, cols]).astype(BF16)
    tail = u[tm - POOL_HIST:, :]
    ext_ref[...] = tail
    last_ref[0] = tail


def _proj_prompt(x, g, w_in, cos, sin, cos_t, sin_t, w_pool, s_pool):
    b, s, d = x.shape
    per_tile = PROJ_BLOCKS if s % (PROJ_BLOCKS * MOBA_BLOCK) == 0 else 1
    tm = per_tile * MOBA_BLOCK
    nq = s // MOBA_BLOCK
    row = lambda bi, i: (bi, i, 0)
    col = lambda bi, i: (bi, 0, i)
    blk = lambda bi, i: (bi, i, 0, 0)
    return pl.pallas_call(
        _proj_prompt_kernel,
        grid=(b, s // tm),
        in_specs=[
            pl.BlockSpec((1, tm, d), row),
            _resident((1, d)),
            _resident(w_in.shape),
            pl.BlockSpec((tm, cos.shape[1]), lambda bi, i: (i, 0)),
            pl.BlockSpec((tm, cos.shape[1]), lambda bi, i: (i, 0)),
            pl.BlockSpec((HEAD_DIM, tm), lambda bi, i: (0, i)),
            pl.BlockSpec((HEAD_DIM, tm), lambda bi, i: (0, i)),
            _resident(w_pool.shape),
            _resident((1, POOL_WIDTH)),
        ],
        out_specs=[
            pl.BlockSpec((1, ATTN_WIDTH, tm), col),
            pl.BlockSpec((1, ATTN_WIDTH, tm), col),
            pl.BlockSpec((1, ATTN_WIDTH, tm), col),
            pl.BlockSpec((1, tm, ATTN_WIDTH), row),
            pl.BlockSpec((1, per_tile, ATTN_WIDTH, MOBA_BLOCK), blk),
            pl.BlockSpec((1, per_tile, 1, ATTN_WIDTH), blk),
            pl.BlockSpec((1, tm, POOL_WIDTH), row),
            pl.BlockSpec((1, POOL_HIST, POOL_WIDTH), lambda bi, i: (bi, 0, 0)),
        ],
        out_shape=[
            jax.ShapeDtypeStruct((b, ATTN_WIDTH, s), BF16),
            jax.ShapeDtypeStruct((b, ATTN_WIDTH, s), F32),
            jax.ShapeDtypeStruct((b, ATTN_WIDTH, s), F32),
            jax.ShapeDtypeStruct((b, s, ATTN_WIDTH), BF16),
            jax.ShapeDtypeStruct((b, nq, ATTN_WIDTH, MOBA_BLOCK), BF16),
            jax.ShapeDtypeStruct((b, nq, 1, ATTN_WIDTH), F32),
            jax.ShapeDtypeStruct((b, s, POOL_WIDTH), BF16),
            jax.ShapeDtypeStruct((b, POOL_HIST, POOL_WIDTH), F32),
        ],
        scratch_shapes=[pltpu.VMEM((POOL_HIST, POOL_WIDTH), F32)],
        compiler_params=_params("arbitrary", "arbitrary"),
        name="proj_prompt",
    )(x, g, w_in, cos, sin, cos_t, sin_t, w_pool, s_pool)


SUBLANES = 8
LOGIT_AHEAD = 4
LOGIT_SLOTS = 8
LOG2_E = 1.4426950408889634
KEY_RING = 3
VALUE_RING = 2


def _query_block(visit, n):
    return jnp.where(visit % 2 == 0, visit // 2, n - 1 - visit // 2)


def _all_sublanes(t, combine):
    for s in (4, 2, 1):
        t = combine(t, pltpu.roll(t, s, 0))
    return t


def _moba_kernel(pt_ref, qt_ref, kb_ref, vtb_ref, km_ref, qs_ref, qn_ref, kn_ref, vn_ref, ck_ref, cv_ref,
                 o_ref, os_ref, acc_ref, m_ref, l_ref, take_ref, lg_ref, kbuf_ref, vbuf_ref, slg_ref, key_sem, val_sem,
                 *, n_pages, per_step):
    nq = pl.num_programs(1)
    i = _query_block(pl.program_id(1), nq)
    step = pl.program_id(0) * nq + pl.program_id(1)
    n_steps = pl.num_programs(0) * nq
    key_ring = kbuf_ref.shape[0]
    val_ring = vbuf_ref.shape[0]
    tq = qt_ref.shape[2]
    nb = km_ref.shape[1]

    def page_copies(cache_ref, buf_ref, sem, page_id, part):
        return [pltpu.make_async_copy(cache_ref.at[page_id(s, p)], buf_ref.at[part, s * n_pages + p], sem.at[part])
                for s in range(per_step) for p in range(n_pages)]

    def start_keys(t):
        for c in page_copies(ck_ref, kbuf_ref, key_sem, lambda s, p: pt_ref[t * per_step + s, p], t % key_ring):
            c.start()

    def start_values(t):
        for c in page_copies(cv_ref, vbuf_ref, val_sem, lambda s, p: pt_ref[t * per_step + s, p], t % val_ring):
            c.start()

    def wait_part(cache_ref, buf_ref, sem, part):
        pltpu.make_async_copy(cache_ref.at[pl.ds(0, per_step * n_pages)], buf_ref.at[part], sem.at[part]).wait()

    def key_logits(q_rows, part):
        return [[_sample_page_logits(q_rows[s], kbuf_ref.at[part, s * n_pages + p]) for p in range(n_pages)]
                for s in range(per_step)]

    @pl.when(step == 0)
    def _():
        start_keys(0)
        for t in range(min(2, n_steps)):
            start_values(t)
        for t in range(1, min(key_ring, n_steps)):
            start_keys(t)
        wait_part(ck_ref, kbuf_ref, key_sem, 0)
        first = key_logits([_sample_queries(qs_ref[s]) for s in range(per_step)], 0)
        for s in range(per_step):
            slg_ref[s] = jnp.concatenate(first[s], axis=-1)
        if n_steps > key_ring:
            start_keys(key_ring)

    @pl.when((step >= 1) & (step + 1 < n_steps))
    def _():
        start_values(step + 1)

    @pl.when((step >= 1) & (step + key_ring < n_steps))
    def _():
        start_keys(step + key_ring)

    @pl.when(step + 1 < n_steps)
    def _():
        wait_part(ck_ref, kbuf_ref, key_sem, (step + 1) % key_ring)

    wait_part(cv_ref, vbuf_ref, val_sem, step % val_ring)
    slot = step % val_ring
    next_keys = (step + 1) % key_ring


    km_rows = jnp.concatenate([km_ref[0]] * N_HEADS, axis=0)
    row_head = lax.broadcasted_iota(jnp.int32, km_rows.shape, 0) // nb
    lane_head = lax.broadcasted_iota(jnp.int32, km_rows.shape, 1) // HEAD_DIM
    km_heads = jnp.where(row_head == lane_head, km_rows, 0.0).astype(BF16)
    s = jnp.dot(km_heads, qt_ref[0], preferred_element_type=F32).reshape(N_HEADS, nb, tq)
    blk = lax.broadcasted_iota(jnp.int32, s.shape, 1)
    sel = _topk_member(s, blk < i, blk, axis=1).astype(F32)

    for h in range(N_HEADS):
        for jb in range(nb):
            take_ref[h, jb] = jnp.broadcast_to(sel[h, jb:jb + 1, :], (SUBLANES, tq))

    def head_rows(h):
        return slice(h * HEAD_DIM, (h + 1) * HEAD_DIM)

    def stage_logits(h, j):
        off = pl.multiple_of(j * MOBA_BLOCK, MOBA_BLOCK)
        lg_ref[h % LOGIT_SLOTS] = jnp.dot(kb_ref[0, pl.ds(off, MOBA_BLOCK), head_rows(h)],
                                          qt_ref[0, head_rows(h), :], preferred_element_type=F32)

    def weighted_values(h, j, p3):
        p = p3.reshape(MOBA_BLOCK, tq).astype(BF16)
        pv = jnp.dot(vtb_ref[0, j, head_rows(h), :], p, preferred_element_type=F32)
        return pv.reshape(HEAD_DIM // SUBLANES, SUBLANES, tq)

    def tiles(t):
        return t.reshape(t.shape[0] // SUBLANES, SUBLANES, tq)

    def block_pass(j, j_next, own, between=None):
        for h in range(N_HEADS):
            ahead = h + LOGIT_AHEAD
            if ahead < N_HEADS:
                stage_logits(ahead, j)
            else:
                stage_logits(ahead - N_HEADS, j_next)
            if between is not None:
                between(h)
            lg = lg_ref[h % LOGIT_SLOTS]
            if own:
                kk = lax.broadcasted_iota(jnp.int32, lg.shape, 0)
                qq = lax.broadcasted_iota(jnp.int32, lg.shape, 1)
                lg3 = tiles(jnp.where(kk <= qq, lg, NEG))
                m = _all_sublanes(jnp.max(lg3, axis=0), jnp.maximum)
                p3 = jnp.exp2(lg3 - m[None])
                m_ref[h] = m
                l_ref[h] = _all_sublanes(jnp.sum(p3, axis=0), jnp.add)
                acc_ref[head_rows(h), :] = weighted_values(h, j, p3).reshape(HEAD_DIM, tq)
            else:
                lg3 = tiles(lg)
                take = take_ref[h, j] > 0.0
                m_old = m_ref[h]
                l_old = l_ref[h]
                acc_old = tiles(acc_ref[head_rows(h), :])
                m = jnp.maximum(m_old, _all_sublanes(jnp.max(lg3, axis=0), jnp.maximum))
                a = jnp.exp2(m_old - m)
                p3 = jnp.exp2(lg3 - m[None])
                l = a * l_old + _all_sublanes(jnp.sum(p3, axis=0), jnp.add)
                acc3 = a[None] * acc_old + weighted_values(h, j, p3)
                m_ref[h] = jnp.where(take, m, m_old)
                l_ref[h] = jnp.where(take, l, l_old)
                acc_ref[head_rows(h), :] = jnp.where(take[None], acc3, acc_old).reshape(HEAD_DIM, tq)

    last_past = jnp.maximum(i - 1, 0)
    for h in range(LOGIT_AHEAD):
        stage_logits(h, i)

    for s in range(per_step):
        v_pages = [vbuf_ref.at[slot, s * n_pages + p] for p in range(n_pages)]
        os_ref[s] = _sample_finish(_sample_queries(qs_ref[s]), slg_ref[s], kn_ref[s], vn_ref[s], v_pages).astype(BF16)

    next_q = [_sample_queries(qn_ref[s]) for s in range(per_step)]
    jobs = [(s, p) for s in range(per_step) for p in range(n_pages)]
    per_head = -(-len(jobs) // N_HEADS)
    page_logits = {}

    def sample_page_logits(h):
        for s, p in jobs[h * per_head:(h + 1) * per_head]:
            page_logits[s, p] = _sample_page_logits(next_q[s], kbuf_ref.at[next_keys, s * n_pages + p])

    block_pass(i, 0, own=True, between=sample_page_logits)
    for s in range(per_step):
        slg_ref[s] = jnp.concatenate([page_logits[s, p] for p in range(n_pages)], axis=-1)

    def two_past_blocks(jj, carry):
        j = 2 * jj
        block_pass(j, j + 1, own=False)
        block_pass(j + 1, jnp.minimum(j + 2, last_past), own=False)
        return carry

    lax.fori_loop(0, i // 2, two_past_blocks, 0)

    @pl.when(i % 2 == 1)
    def _():
        block_pass(last_past, last_past, own=False)

    for h in range(N_HEADS):
        rows = slice(h * HEAD_DIM, (h + 1) * HEAD_DIM)
        acc_ref[rows, :] = (tiles(acc_ref[rows, :]) / l_ref[h][None]).reshape(HEAD_DIM, tq)
    o_ref[0] = acc_ref[...].T.astype(BF16)


def _moba(qtb, kb, vtb, kmean, page_table, q_s, k_s, v_s, cache_kt, cache_vt):
    b, w, s = qtb.shape
    nq = s // MOBA_BLOCK
    bs, n_pages = page_table.shape
    page = cache_kt.shape[2]
    steps = b * nq
    assert bs % steps == 0
    per_step = bs // steps
    step_seq = lambda bi, i, pt: (bi * nq + i, 0, 0)
    tok = pl.BlockSpec((per_step, 1, w), step_seq)
    tok_next = pl.BlockSpec((per_step, 1, w), lambda bi, i, pt: (jnp.minimum(bi * nq + i + 1, steps - 1), 0, 0))
    in_hbm = pl.BlockSpec(memory_space=pl.ANY)
    page_ring = lambda parts: pltpu.VMEM((parts, per_step * n_pages, w, page), cache_kt.dtype)
    grid_spec = pltpu.PrefetchScalarGridSpec(
        num_scalar_prefetch=1,
        grid=(b, nq),
        in_specs=[
            pl.BlockSpec((1, w, MOBA_BLOCK), lambda bi, i, pt: (bi, 0, _query_block(i, nq))),
            pl.BlockSpec((1, s, w), lambda bi, i, pt: (bi, 0, 0)),
            pl.BlockSpec((1, nq, w, MOBA_BLOCK), lambda bi, i, pt: (bi, 0, 0, 0)),
            pl.BlockSpec((1, nq, w), lambda bi, i, pt: (bi, 0, 0)),
            tok, tok_next, tok, tok, in_hbm, in_hbm,
        ],
        out_specs=[pl.BlockSpec((1, MOBA_BLOCK, w), lambda bi, i, pt: (bi, _query_block(i, nq), 0)), tok],
        scratch_shapes=[pltpu.VMEM((w, MOBA_BLOCK), F32),
                        pltpu.VMEM((N_HEADS, SUBLANES, MOBA_BLOCK), F32),
                        pltpu.VMEM((N_HEADS, SUBLANES, MOBA_BLOCK), F32),
                        pltpu.VMEM((N_HEADS, nq, SUBLANES, MOBA_BLOCK), F32),
                        pltpu.VMEM((LOGIT_SLOTS, MOBA_BLOCK, MOBA_BLOCK), F32),
                        page_ring(KEY_RING), page_ring(VALUE_RING),
                        pltpu.VMEM((per_step, N_HEADS, n_pages * page), F32),
                        pltpu.SemaphoreType.DMA((KEY_RING,)), pltpu.SemaphoreType.DMA((VALUE_RING,))],
    )
    return pl.pallas_call(
        functools.partial(_moba_kernel, n_pages=n_pages, per_step=per_step),
        grid_spec=grid_spec,
        out_shape=[jax.ShapeDtypeStruct((b, s, w), BF16), jax.ShapeDtypeStruct((bs, 1, w), BF16)],
        compiler_params=_params("arbitrary", "arbitrary"),
        name="moba",
    )(page_table, qtb, kb, vtb, kmean, q_s, q_s, k_s, v_s, cache_kt, cache_vt)


def _mem_kv_kernel(m_ref, g_ref, wk_ref, wv_ref, mk_ref, mv_ref, mkb_ref, mvb_ref):
    h = _rms(m_ref[...], g_ref[...]).astype(BF16)
    mk = jnp.dot(h, wk_ref[...], preferred_element_type=F32)
    mv = jnp.dot(h, wv_ref[...], preferred_element_type=F32)
    mk_ref[...] = mk
    mv_ref[...] = mv
    mkb_ref[...] = mk.astype(BF16)
    mvb_ref[...] = mv.astype(BF16)


def _mem_kv(mem, g, w_ck, w_cv):
    rows, d = mem.shape
    tm = min(1024, rows)
    row = lambda i: (i, 0)
    return pl.pallas_call(
        _mem_kv_kernel,
        grid=(rows // tm,),
        in_specs=[pl.BlockSpec((tm, d), row), _resident((1, d)), _resident(w_ck.shape), _resident(w_cv.shape)],
        out_specs=[pl.BlockSpec((tm, d), row)] * 4,
        out_shape=[jax.ShapeDtypeStruct((rows, d), F32)] * 2 + [jax.ShapeDtypeStruct((rows, d), BF16)] * 2,
        compiler_params=_params("arbitrary"),
        name="mem_kv",
    )(mem, g, w_ck, w_cv)


MIX_CHAINS = 2
MIX_CHAIN_ROWS = 512


def _mix_out_kernel(x_ref, po_ref, at_ref, wo_ref, g_ref, wq_ref, *rest):
    with_memory = len(rest) == 4
    if with_memory:
        mk_ref, mv_ref, x1_ref, out_ref = rest
    else:
        x1_ref, out_ref = rest
    tm = x_ref.shape[0]
    chains = MIX_CHAINS if with_memory else 1
    rows_of = lambda c: slice(c * (tm // chains), (c + 1) * (tm // chains))

    def residual(c):
        rows = rows_of(c)
        cat = jnp.concatenate([po_ref[rows, :], at_ref[rows, :]], axis=-1)
        x1 = x_ref[rows, :] + jnp.dot(cat, wo_ref[...], preferred_element_type=F32)
        x1_ref[rows, :] = x1
        return _rms(x1, g_ref[...]).astype(BF16)

    def queries(hq):
        return (jnp.dot(hq, wq_ref[...], preferred_element_type=F32) * (MEM_HEAD_DIM ** -0.5)).astype(BF16)

    def attend(c, q):
        for h in range(MEM_HEADS):
            sl = slice(h * MEM_HEAD_DIM, (h + 1) * MEM_HEAD_DIM)
            lg = lax.dot_general(q[:, sl], mk_ref[0, :, sl], NT_DIMS, preferred_element_type=F32)
            p = jnp.exp(lg - jnp.max(lg, axis=-1, keepdims=True))
            l = jnp.sum(p, axis=-1, keepdims=True)
            o = jnp.dot(p.astype(BF16), mv_ref[0, :, sl], preferred_element_type=F32)
            out_ref[rows_of(c), sl] = (o / l).astype(BF16)

    if not with_memory:
        out_ref[...] = queries(residual(0))
        return
    q = queries(residual(0))
    for c in range(chains):
        hq_next = residual(c + 1) if c + 1 < chains else None
        attend(c, q)
        if hq_next is not None:
            q = queries(hq_next)


def _mix_out(x, po, at, w_out, g_cross, w_cq, memory=None):
    rows, d = x.shape
    tm = min(MIX_CHAINS * MIX_CHAIN_ROWS if memory is not None else MIX_CHAIN_ROWS, rows)
    row = lambda i: (i, 0)
    in_specs = [pl.BlockSpec((tm, d), row), pl.BlockSpec((tm, POOL_WIDTH), row), pl.BlockSpec((tm, ATTN_WIDTH), row),
                _resident(w_out.shape), _resident((1, d)), _resident(w_cq.shape)]
    args = [x, po, at, w_out, g_cross, w_cq]
    if memory is not None:
        mk, mv = memory
        tiles_per_batch = rows // mk.shape[0] // tm
        assert tiles_per_batch * tm * mk.shape[0] == rows and tm % MIX_CHAINS == 0
        of_batch = pl.BlockSpec((1,) + mk.shape[1:], lambda i: (i // tiles_per_batch, 0, 0))
        in_specs += [of_batch, of_batch]
        args += [mk, mv]
    return pl.pallas_call(
        _mix_out_kernel,
        grid=(rows // tm,),
        in_specs=in_specs,
        out_specs=[pl.BlockSpec((tm, d), row), pl.BlockSpec((tm, d), row)],
        out_shape=[jax.ShapeDtypeStruct((rows, d), F32), jax.ShapeDtypeStruct((rows, d), BF16)],
        compiler_params=_params("arbitrary"),
        name="mix_out",
    )(*args)


MEM_LANES = LANES
MEM_PARTS = MEM_HEAD_DIM // MEM_LANES
MEM_ROWS = MEM_PARTS * MEM_HEADS


def _cross_sample_products(q, mk_ref):
    return lax.dot_general(q, mk_ref[...].astype(BF16), NT_DIMS, preferred_element_type=F32)


def _cross_sample_attend(r, mv_ref):
    n = r.shape[1]
    row = lax.broadcasted_iota(jnp.int32, (MEM_ROWS, n), 0)
    lane_row = lax.broadcasted_iota(jnp.int32, (MEM_ROWS, n), 1) % MEM_ROWS
    own = lane_row == row
    valid = (row < MEM_HEADS) & own

    def other_part(t, lane_shift):
        return pltpu.roll(pltpu.roll(t, MEM_HEADS, 0), lane_shift, 1)

    r = jnp.where(own, r, 0.0)
    lg = jnp.where(valid, r + other_part(r, n - MEM_HEADS), NEG)
    p = jnp.where(valid, jnp.exp(lg - jnp.max(lg, axis=-1, keepdims=True)), 0.0)
    p = p + other_part(p, MEM_HEADS)
    l = jnp.sum(p, axis=-1, keepdims=True)
    return jnp.dot(p.astype(BF16), mv_ref[...].astype(BF16), preferred_element_type=F32) / l


def _stored_rows(qc):
    rows = qc.shape[0]
    return jnp.transpose(qc.reshape(rows, MEM_HEADS, MEM_PARTS, MEM_LANES), (0, 2, 1, 3)).reshape(rows, MEM_ROWS, MEM_LANES)


def _stored_slots(m):
    b, slots = m.shape[:2]
    m = m.reshape(b, slots, MEM_HEADS, MEM_PARTS, MEM_LANES)
    return jnp.transpose(m, (0, 1, 3, 2, 4)).reshape(b, slots * MEM_ROWS, MEM_LANES)


def _unstored_rows(o):
    rows = o.shape[0]
    return jnp.transpose(o.reshape(rows, MEM_PARTS, MEM_HEADS, MEM_LANES), (0, 2, 1, 3)).reshape(rows, -1)


def _mlp_kernel(x1_ref, o_ref, wco_ref, gm_ref, wu_ref, wd_ref, gf_ref, *rest):
    if len(rest) == 1:
        (y_ref,) = rest
        n_seq = 0
    else:
        qs_ref, mk_ref, mv_ref, y_ref, os_ref = rest
        n_seq = qs_ref.shape[0]
    products = [_cross_sample_products(qs_ref[s], mk_ref.at[s]) for s in range(n_seq)]
    x2 = x1_ref[...] + jnp.dot(o_ref[...], wco_ref[...], preferred_element_type=F32)
    hm = _rms(x2, gm_ref[...]).astype(BF16)
    acc = x2
    for c in range(D_FF // FF_CHUNK):
        if c == 1:
            for s in range(n_seq):
                os_ref[s] = _cross_sample_attend(products[s], mv_ref.at[s]).astype(BF16)
        cs = slice(c * FF_CHUNK, (c + 1) * FF_CHUNK)
        a = jnp.dot(hm, wu_ref[:, cs], preferred_element_type=F32)
        a = jnp.square(jnp.maximum(a, 0.0)).astype(BF16)
        acc = acc + jnp.dot(a, wd_ref[cs, :], preferred_element_type=F32)
    y_ref[...] = _rms(acc, gf_ref[...])


def _mlp(x1, o, w_co, g_mlp, w_up, w_down, g_final, sample_cross=None):
    rows, d = x1.shape
    tm = min(512, rows)
    steps = rows // tm
    row = lambda i: (i, 0)
    in_specs = [pl.BlockSpec((tm, d), row), pl.BlockSpec((tm, d), row), _resident(w_co.shape),
                _resident((1, d)), _resident(w_up.shape), _resident(w_down.shape), _resident((1, d))]
    args = [x1, o, w_co, g_mlp, w_up, w_down, g_final]
    out_specs = [pl.BlockSpec((tm, d), row)]
    out_shape = [jax.ShapeDtypeStruct((rows, d), F32)]
    if sample_cross is not None:
        qc, mem_k, mem_v = sample_cross
        b, slots = mem_k.shape[:2]
        assert b % steps == 0
        seq = lambda n: pl.BlockSpec((b // steps, n, MEM_LANES), lambda i: (i, 0, 0))
        in_specs += [seq(MEM_ROWS), seq(slots * MEM_ROWS), seq(slots * MEM_ROWS)]
        args += [_stored_rows(qc), _stored_slots(mem_k), _stored_slots(mem_v)]
        out_specs.append(seq(MEM_ROWS))
        out_shape.append(jax.ShapeDtypeStruct((b, MEM_ROWS, MEM_LANES), BF16))
    outs = pl.pallas_call(
        _mlp_kernel,
        grid=(steps,),
        in_specs=in_specs,
        out_specs=out_specs,
        out_shape=out_shape,
        compiler_params=_params("arbitrary"),
        name="mlp",
    )(*args)
    if sample_cross is None:
        return outs[0]
    return outs[0], _unstored_rows(outs[1])


def _proj_sample_kernel(x_ref, g_ref, w_ref, cos_ref, sin_ref, st_ref, wp_ref, sp_ref,
                        q_ref, k_ref, v_ref, kt_ref, vt_ref, u_ref, po_ref, *, n_prev):
    h = _rms(x_ref[...], g_ref[...]).astype(BF16)
    proj = jnp.dot(h, w_ref[...], preferred_element_type=F32)
    u = proj[:, :POOL_WIDTH]
    cos = cos_ref[...]
    sin = sin_ref[...]
    q_ref[...] = _rope(proj[:, POOL_WIDTH:POOL_WIDTH + ATTN_WIDTH], cos, sin) * (HEAD_DIM ** -0.5)
    k = _rope(proj[:, POOL_WIDTH + ATTN_WIDTH:POOL_WIDTH + 2 * ATTN_WIDTH], cos, sin)
    v = proj[:, POOL_WIDTH + 2 * ATTN_WIDTH:]
    k_ref[...] = k
    v_ref[...] = v
    kt_ref[...] = k.T
    vt_ref[...] = v.T
    u_ref[...] = u
    for g, w in enumerate(POOL_WINDOWS):
        cols = slice(g * POOL_GC, (g + 1) * POOL_GC)
        ug = u[:, cols]
        wsum = ug
        for j in range(1, w):
            wsum = wsum + st_ref[POOL_STATE - j, :, cols]
        d = wsum / float(min(w, n_prev + 1)) - ug
        mixed = jnp.dot(d.astype(BF16), wp_ref[g], preferred_element_type=F32)
        po_ref[:, cols] = (mixed * sp_ref[:, cols]).astype(BF16)


def _proj_sample(x, g, w_in, cos, sin, state_t, w_pool, s_pool, n_prev):
    rows, d = x.shape
    wide = jax.ShapeDtypeStruct((rows, ATTN_WIDTH), F32)
    tall = jax.ShapeDtypeStruct((ATTN_WIDTH, rows), F32)
    full = lambda a: pl.BlockSpec(a.shape, lambda i, nd=len(a.shape): (0,) * nd)
    args = (x, g, w_in, cos, sin, state_t, w_pool, s_pool)
    outs = [wide, wide, wide, tall, tall, wide, jax.ShapeDtypeStruct((rows, POOL_WIDTH), BF16)]
    return pl.pallas_call(
        functools.partial(_proj_sample_kernel, n_prev=n_prev),
        grid=(1,),
        in_specs=[full(a) for a in args],
        out_specs=[full(o) for o in outs],
        out_shape=outs,
        compiler_params=_params("arbitrary"),
        name="proj_sample",
    )(*args)


def _own_lanes():
    head_of_lane = lax.broadcasted_iota(jnp.int32, (N_HEADS, ATTN_WIDTH), 1) // HEAD_DIM
    head = lax.broadcasted_iota(jnp.int32, (N_HEADS, ATTN_WIDTH), 0)
    return head_of_lane == head


def _sample_queries(q):
    return jnp.where(_own_lanes(), q, 0.0)


def _sample_page_logits(q_heads, k_page):
    return jnp.dot(q_heads.astype(BF16), k_page[...].astype(BF16), preferred_element_type=F32)


def _sample_finish(q_heads, lg, k_new, v_new, v_pages):
    page = v_pages[0].shape[1]
    past = lg.shape[1]
    nb = past // MOBA_BLOCK
    mine = _own_lanes()

    s = jnp.concatenate(
        [jnp.sum(lg[:, j * MOBA_BLOCK:(j + 1) * MOBA_BLOCK], axis=-1, keepdims=True) for j in range(nb)], axis=-1)
    blk_id = lax.broadcasted_iota(jnp.int32, (N_HEADS, nb), 1)
    sel = _topk_member(s, blk_id >= 0, blk_id, axis=1)
    key_blk = lax.broadcasted_iota(jnp.int32, (N_HEADS, past), 1) // MOBA_BLOCK
    chosen = jnp.zeros((N_HEADS, past), jnp.bool_)
    for j in range(nb):
        chosen = chosen | ((key_blk == j) & sel[:, j:j + 1])
    lg = jnp.where(chosen, lg, NEG)

    lg_new = jnp.sum(q_heads * k_new, axis=-1, keepdims=True)
    m = jnp.maximum(jnp.max(lg, axis=-1, keepdims=True), lg_new)
    p = jnp.exp(lg - m)
    p_new = jnp.exp(lg_new - m)
    l = jnp.sum(p, axis=-1, keepdims=True) + p_new

    lanes = page
    sub = lax.broadcasted_iota(jnp.int32, (HEAD_DIM, lanes), 0)
    lane = lax.broadcasted_iota(jnp.int32, (HEAD_DIM, lanes), 1)
    tiles = []
    for first in range(0, N_HEADS, lanes // HEAD_DIM):
        row = jnp.zeros((1, lanes), F32)
        for h in range(first, first + lanes // HEAD_DIM):
            acc = jnp.zeros((HEAD_DIM, lanes), F32)
            for n, vp in enumerate(v_pages):
                w = jnp.broadcast_to(p[h:h + 1, n * page:(n + 1) * page], (HEAD_DIM, page))
                acc = acc + vp[h * HEAD_DIM:(h + 1) * HEAD_DIM, :] * w
            total = jnp.sum(acc, axis=1, keepdims=True)
            on_diag = lane == sub + (h - first) * HEAD_DIM
            row = row + jnp.sum(jnp.where(on_diag, total, 0.0), axis=0, keepdims=True)
        tiles.append(row)
    pv = jnp.concatenate(tiles, axis=-1)
    per_lane = lambda t: jnp.sum(jnp.where(mine, t, 0.0), axis=0, keepdims=True)
    return (pv + per_lane(p_new) * v_new) / per_lane(l)


def _rope_tables(pos, heads):
    half = HEAD_DIM // 2
    inv_freq = ROPE_THETA ** (-jnp.arange(half, dtype=F32) / half)
    ang = pos.astype(F32)[:, None] * inv_freq[None, :]
    cos = jnp.cos(ang)
    sin = jnp.sin(ang)
    cos_full = jnp.tile(jnp.concatenate([cos, cos], axis=-1), (1, heads))
    sin_signed = jnp.tile(jnp.concatenate([-sin, sin], axis=-1), (1, heads))
    return cos_full, sin_signed


def kernel(x_prompt, x_sample, cache_k, cache_v, cache_mem_k, cache_mem_v, state_pool, page_table, mem_prompt,
           g_mix, w_in, w_pool, s_pool, w_out, g_cross, g_mem, w_cq, w_ck, w_cv, w_co, g_mlp, w_up, w_down,
           g_final):
    depth = w_in.shape[0]
    assert depth == 1, "one decoder layer"
    bp, seq, d = x_prompt.shape
    bs, dec_seq, _ = x_sample.shape
    assert dec_seq == 1 and seq % MOBA_BLOCK == 0
    n_phys, page = cache_k.shape[1], cache_k.shape[2]
    n_pages = page_table.shape[1]
    past_len = n_pages * page
    assert past_len % MOBA_BLOCK == 0 and past_len >= POOL_STATE
    mem_len = mem_prompt.shape[1]

    l = 0
    row = lambda a: a.reshape(1, -1)
    lp = dict(w_out=w_out[l].astype(BF16), g_cross=row(g_cross[l]), w_cq=w_cq[l].astype(BF16),
              w_co=w_co[l].astype(BF16), g_mlp=row(g_mlp[l]), w_up=w_up[l].astype(BF16),
              w_down=w_down[l].astype(BF16), g_final=row(g_final))
    w_in_b = w_in[l].astype(BF16)
    w_pool_b = w_pool[l].astype(BF16)
    g_mix_r = row(g_mix[l])
    s_pool_r = row(s_pool[l])

    lane_heads = LANES // HEAD_DIM
    cos_p, sin_p = _rope_tables(jnp.arange(seq), lane_heads)
    qtb, kt_p, vt_p, kb, vtb, kmean, po_p, last_p = _proj_prompt(
        x_prompt, g_mix_r, w_in_b, cos_p, sin_p, cos_p[:, :HEAD_DIM].T, sin_p[:, :HEAD_DIM].T, w_pool_b, s_pool_r)
    cos_s, sin_s = _rope_tables(jnp.full((1,), past_len), N_HEADS)
    state_t = jnp.transpose(state_pool[l], (1, 0, 2))
    q_s, k_s, v_s, kt_s, vt_s, u_s, po_s = _proj_sample(x_sample.reshape(bs, d), g_mix_r, w_in_b, cos_s, sin_s,
                                                        state_t, w_pool_b, s_pool_r, past_len)

    tok = lambda a: a.reshape(bs, 1, -1)
    pages_t = lambda c: jnp.transpose(c, (0, 2, 3, 1)).reshape(n_phys, ATTN_WIDTH, page)
    at_p, at_s = _moba(qtb, kb, vtb, kmean.reshape(bp, seq // MOBA_BLOCK, ATTN_WIDTH), page_table, tok(q_s),
                       tok(k_s), tok(v_s), pages_t(cache_k[l]), pages_t(cache_v[l]))

    mk_p, mv_p, mkb, mvb = _mem_kv(mem_prompt.reshape(bp * mem_len, d), row(g_mem[l]), w_ck[l].astype(BF16),
                                   w_cv[l].astype(BF16))
    flat = lambda a: a.reshape(-1, a.shape[-1])
    mix = lambda x, po, at, memory=None: _mix_out(flat(x), flat(po), flat(at), lp["w_out"], lp["g_cross"],
                                                  lp["w_cq"], memory)
    tail = (lp["w_co"], lp["g_mlp"], lp["w_up"], lp["w_down"], lp["g_final"])
    x1_p, o_p = mix(x_prompt, po_p, at_p, (mkb.reshape(bp, mem_len, d), mvb.reshape(bp, mem_len, d)))
    x1_s, qc_s = mix(x_sample, po_s, at_s)
    y_prompt, o_s = _mlp(x1_p, o_p, *tail, sample_cross=(qc_s, cache_mem_k[l], cache_mem_v[l]))
    y_prompt = y_prompt.reshape(bp, seq, d)
    y_sample = _mlp(x1_s, o_s, *tail).reshape(bs, 1, d)

    def heads(t, b_):
        return jnp.transpose(t.reshape(b_, N_HEADS, HEAD_DIM, -1), (0, 3, 1, 2))[None]

    mem_heads = lambda a: a.reshape(1, bp, mem_len, MEM_HEADS, MEM_HEAD_DIM)
    pool_sample = jnp.transpose(jnp.concatenate([state_t[1:], u_s[None]], axis=0), (1, 0, 2))[None]
    return (y_prompt, y_sample, heads(kt_p, bp), heads(vt_p, bp), last_p[:, 1:][None], mem_heads(mk_p),
            mem_heads(mv_p), heads(kt_s.T.reshape(bs, ATTN_WIDTH, 1), bs), heads(vt_s.T.reshape(bs, ATTN_WIDTH, 1), bs),
            pool_sample)
```

```python
import functools

import jax
import jax.numpy as jnp
from jax import lax
from jax.experimental import pallas as pl
from jax.experimental.pallas import tpu as pltpu

D_MODEL = 1024
POOL_WIDTH = 512
POOL_WINDOWS = (2, 4, 8, 16)
POOL_GC = POOL_WIDTH // len(POOL_WINDOWS)
POOL_STATE = max(POOL_WINDOWS) - 1
POOL_HIST = POOL_STATE + 1
ATTN_WIDTH = D_MODEL - POOL_WIDTH
HEAD_DIM = 64
N_HEADS = ATTN_WIDTH // HEAD_DIM
MOBA_BLOCK = 256
MOBA_TOPK = 3
ROPE_THETA = 10000.0
MEM_HEADS = 4
MEM_HEAD_DIM = D_MODEL // MEM_HEADS
D_FF = 4 * D_MODEL
FF_CHUNK = 1024
EPS = 1e-6
NEG = -1e30

VMEM_LIMIT_BYTES = 56 * 1024 * 1024
LANES = 128

F32 = jnp.float32
BF16 = jnp.bfloat16
NT_DIMS = (((1,), (1,)), ((), ()))


def _params(*semantics):
    return pltpu.CompilerParams(dimension_semantics=semantics, vmem_limit_bytes=VMEM_LIMIT_BYTES)


def _resident(shape):
    zeros = (0,) * len(shape)
    return pl.BlockSpec(shape, lambda *_: zeros, pipeline_mode=pl.Buffered(1))


def _rms(x, g):
    return x * lax.rsqrt(jnp.mean(x * x, axis=-1, keepdims=True) + EPS) * g


def _rope(t, cos, sin_signed):
    width = t.shape[-1]
    lane = lax.broadcasted_iota(jnp.int32, t.shape, t.ndim - 1)
    first_half = (lane % HEAD_DIM) < (HEAD_DIM // 2)
    partner = jnp.where(first_half,
                        pltpu.roll(t, width - HEAD_DIM // 2, t.ndim - 1),
                        pltpu.roll(t, HEAD_DIM // 2, t.ndim - 1))
    return t * cos + partner * sin_signed


def _rope_t(t, cos, sin_signed):
    rows = t.shape[0]
    row = lax.broadcasted_iota(jnp.int32, t.shape, 0)
    first_half = (row % HEAD_DIM) < (HEAD_DIM // 2)
    partner = jnp.where(first_half, pltpu.roll(t, rows - HEAD_DIM // 2, 0), pltpu.roll(t, HEAD_DIM // 2, 0))
    return t * cos + partner * sin_signed


def _topk_member(s, valid, idx, axis):
    n = s.shape[axis]
    s = jnp.where(valid, s, NEG)
    rank = jnp.zeros(s.shape, jnp.int32)
    for j in range(n):
        sj = lax.slice_in_dim(s, j, j + 1, axis=axis)
        beats = (sj > s) | ((sj == s) & (j < idx))
        rank = rank + beats.astype(jnp.int32)
    return (rank < MOBA_TOPK) & valid


PROJ_BLOCKS = 2
def _proj_prompt_kernel(x_ref, g_ref, w_ref, cos_ref, sin_ref, cost_ref, sint_ref, wp_ref, sp_ref,
                        qt_ref, kt_ref, vt_ref, kb_ref, vtb_ref, km_ref, po_ref, last_ref, ext_ref):
    i = pl.program_id(1)
    tm = x_ref.shape[1]

    @pl.when(i == 0)
    def _():
        ext_ref[0:POOL_HIST, :] = jnp.zeros((POOL_HIST, POOL_WIDTH), F32)

    h = _rms(x_ref[0], g_ref[...]).astype(BF16)

    def project(first):
        return jnp.dot(h, w_ref[:, first:first + ATTN_WIDTH], preferred_element_type=F32)

    lane_reps = ATTN_WIDTH // cos_ref.shape[1]
    u = project(0)
    k = _rope(project(POOL_WIDTH + ATTN_WIDTH), jnp.tile(cos_ref[...], (1, lane_reps)),
              jnp.tile(sin_ref[...], (1, lane_reps)))
    q = project(POOL_WIDTH)
    v = project(POOL_WIDTH + 2 * ATTN_WIDTH)
    qt = _rope_t(q.T, jnp.tile(cost_ref[...], (N_HEADS, 1)), jnp.tile(sint_ref[...], (N_HEADS, 1)))
    qt_ref[0] = (qt * (HEAD_DIM ** -0.5 * LOG2_E)).astype(BF16)
    vt = v.T
    kt_ref[0] = k.T
    vt_ref[0] = vt
    kb_ref[0] = k.astype(BF16)
    for c in range(tm // MOBA_BLOCK):
        block = slice(c * MOBA_BLOCK, (c + 1) * MOBA_BLOCK)
        vtb_ref[0, c] = vt[:, block].astype(BF16)
        km_ref[0, c] = jnp.mean(k[block, :], axis=0, keepdims=True)

    pos = i * tm + lax.broadcasted_iota(jnp.int32, (tm, 1), 0)
    for g, w in enumerate(POOL_WINDOWS):
        cols = slice(g * POOL_GC, (g + 1) * POOL_GC)
        ug = u[:, cols]
        wsum = jnp.concatenate([ext_ref[:, cols], ug], axis=0)
        shift = 1
        while shift < w:
            wsum = wsum + pltpu.roll(wsum, shift, 0)
            shift *= 2
        cnt = jnp.minimum(w, pos + 1).astype(F32)
        d = wsum[POOL_HIST:, :] / cnt - ug
        mixed = jnp.dot(d.astype(BF16), wp_ref[g], preferred_element_type=F32)
        po_ref[0, :, cols] = (mixed * sp_ref[:, cols]).astype(BF16)
    tail = u[tm - POOL_HIST:, :]
    ext_ref[...] = tail
    last_ref[0] = tail


def _proj_prompt(x, g, w_in, cos, sin, cos_t, sin_t, w_pool, s_pool):
    b, s, d = x.shape
    per_tile = PROJ_BLOCKS if s % (PROJ_BLOCKS * MOBA_BLOCK) == 0 else 1
    tm = per_tile * MOBA_BLOCK
    nq = s // MOBA_BLOCK
    row = lambda bi, i: (bi, i, 0)
    col = lambda bi, i: (bi, 0, i)
    blk = lambda bi, i: (bi, i, 0, 0)
    return pl.pallas_call(
        _proj_prompt_kernel,
        grid=(b, s // tm),
        in_specs=[
            pl.BlockSpec((1, tm, d), row),
            _resident((1, d)),
            _resident(w_in.shape),
            pl.BlockSpec((tm, cos.shape[1]), lambda bi, i: (i, 0)),
            pl.BlockSpec((tm, cos.shape[1]), lambda bi, i: (i, 0)),
            pl.BlockSpec((HEAD_DIM, tm), lambda bi, i: (0, i)),
            pl.BlockSpec((HEAD_DIM, tm), lambda bi, i: (0, i)),
            _resident(w_pool.shape),
            _resident((1, POOL_WIDTH)),
        ],
        out_specs=[
            pl.BlockSpec((1, ATTN_WIDTH, tm), col),
            pl.BlockSpec((1, ATTN_WIDTH, tm), col),
            pl.BlockSpec((1, ATTN_WIDTH, tm), col),
            pl.BlockSpec((1, tm, ATTN_WIDTH), row),
            pl.BlockSpec((1, per_tile, ATTN_WIDTH, MOBA_BLOCK), blk),
            pl.BlockSpec((1, per_tile, 1, ATTN_WIDTH), blk),
            pl.BlockSpec((1, tm, POOL_WIDTH), row),
            pl.BlockSpec((1, POOL_HIST, POOL_WIDTH), lambda bi, i: (bi, 0, 0)),
        ],
        out_shape=[
            jax.ShapeDtypeStruct((b, ATTN_WIDTH, s), BF16),
            jax.ShapeDtypeStruct((b, ATTN_WIDTH, s), F32),
            jax.ShapeDtypeStruct((b, ATTN_WIDTH, s), F32),
            jax.ShapeDtypeStruct((b, s, ATTN_WIDTH), BF16),
            jax.ShapeDtypeStruct((b, nq, ATTN_WIDTH, MOBA_BLOCK), BF16),
            jax.ShapeDtypeStruct((b, nq, 1, ATTN_WIDTH), F32),
            jax.ShapeDtypeStruct((b, s, POOL_WIDTH), BF16),
            jax.ShapeDtypeStruct((b, POOL_HIST, POOL_WIDTH), F32),
        ],
        scratch_shapes=[pltpu.VMEM((POOL_HIST, POOL_WIDTH), F32)],
        compiler_params=_params("arbitrary", "arbitrary"),
        name="proj_prompt",
    )(x, g, w_in, cos, sin, cos_t, sin_t, w_pool, s_pool)


SUBLANES = 8
LOGIT_AHEAD = 4
LOGIT_SLOTS = 8
LOG2_E = 1.4426950408889634
KEY_RING = 3
VALUE_RING = 2


def _query_block(visit, n):
    return jnp.where(visit % 2 == 0, visit // 2, n - 1 - visit // 2)


def _all_sublanes(t, combine):
    for s in (4, 2, 1):
        t = combine(t, pltpu.roll(t, s, 0))
    return t


def _moba_kernel(pt_ref, qt_ref, kb_ref, vtb_ref, km_ref, qs_ref, qn_ref, kn_ref, vn_ref, ck_ref, cv_ref,
                 o_ref, os_ref, acc_ref, m_ref, l_ref, take_ref, lg_ref, kbuf_ref, vbuf_ref, slg_ref, key_sem, val_sem,
                 *, n_pages, per_step):
    nq = pl.num_programs(1)
    i = _query_block(pl.program_id(1), nq)
    step = pl.program_id(0) * nq + pl.program_id(1)
    n_steps = pl.num_programs(0) * nq
    key_ring = kbuf_ref.shape[0]
    val_ring = vbuf_ref.shape[0]
    tq = qt_ref.shape[2]
    nb = km_ref.shape[1]

    def page_copies(cache_ref, buf_ref, sem, page_id, part):
        return [pltpu.make_async_copy(cache_ref.at[page_id(s, p)], buf_ref.at[part, s * n_pages + p], sem.at[part])
                for s in range(per_step) for p in range(n_pages)]

    def start_keys(t):
        for n, c in enumerate(page_copies(ck_ref, kbuf_ref, key_sem, lambda s, p: pt_ref[t * per_step + s, p],
                                          t % key_ring)):
            c.start(priority=n % 2)

    def start_values(t):
        for n, c in enumerate(page_copies(cv_ref, vbuf_ref, val_sem, lambda s, p: pt_ref[t * per_step + s, p],
                                          t % val_ring)):
            c.start(priority=n % 2)

    def wait_part(cache_ref, buf_ref, sem, part):
        pltpu.make_async_copy(cache_ref.at[pl.ds(0, per_step * n_pages)], buf_ref.at[part], sem.at[part]).wait()

    def key_logits(q_rows, part):
        return [[_sample_page_logits(q_rows[s], kbuf_ref.at[part, s * n_pages + p]) for p in range(n_pages)]
                for s in range(per_step)]

    @pl.when(step == 0)
    def _():
        start_keys(0)
        for t in range(min(2, n_steps)):
            start_values(t)
        for t in range(1, min(key_ring, n_steps)):
            start_keys(t)
        wait_part(ck_ref, kbuf_ref, key_sem, 0)
        first = key_logits([_sample_queries(qs_ref[s]) for s in range(per_step)], 0)
        for s in range(per_step):
            slg_ref[s] = jnp.concatenate(first[s], axis=-1)
        if n_steps > key_ring:
            start_keys(key_ring)

    @pl.when((step >= 1) & (step + 1 < n_steps))
    def _():
        start_values(step + 1)

    @pl.when((step >= 1) & (step + key_ring < n_steps))
    def _():
        start_keys(step + key_ring)

    @pl.when(step + 1 < n_steps)
    def _():
        wait_part(ck_ref, kbuf_ref, key_sem, (step + 1) % key_ring)

    wait_part(cv_ref, vbuf_ref, val_sem, step % val_ring)
    slot = step % val_ring
    next_keys = (step + 1) % key_ring


    km_rows = jnp.concatenate([km_ref[0]] * N_HEADS, axis=0)
    row_head = lax.broadcasted_iota(jnp.int32, km_rows.shape, 0) // nb
    lane_head = lax.broadcasted_iota(jnp.int32, km_rows.shape, 1) // HEAD_DIM
    km_heads = jnp.where(row_head == lane_head, km_rows, 0.0).astype(BF16)
    s = jnp.dot(km_heads, qt_ref[0], preferred_element_type=F32).reshape(N_HEADS, nb, tq)
    blk = lax.broadcasted_iota(jnp.int32, s.shape, 1)
    sel = _topk_member(s, blk < i, blk, axis=1).astype(F32)

    for h in range(N_HEADS):
        for jb in range(nb):
            take_ref[h, jb] = jnp.broadcast_to(sel[h, jb:jb + 1, :], (SUBLANES, tq))

    def head_rows(h):
        return slice(h * HEAD_DIM, (h + 1) * HEAD_DIM)

    def stage_logits(h, j):
        off = pl.multiple_of(j * MOBA_BLOCK, MOBA_BLOCK)
        lg_ref[h % LOGIT_SLOTS] = jnp.dot(kb_ref[0, pl.ds(off, MOBA_BLOCK), head_rows(h)],
                                          qt_ref[0, head_rows(h), :], preferred_element_type=F32)

    def weighted_values(h, j, p3):
        p = p3.reshape(MOBA_BLOCK, tq).astype(BF16)
        pv = jnp.dot(vtb_ref[0, j, head_rows(h), :], p, preferred_element_type=F32)
        return pv.reshape(HEAD_DIM // SUBLANES, SUBLANES, tq)

    def tiles(t):
        return t.reshape(t.shape[0] // SUBLANES, SUBLANES, tq)

    def block_pass(j, j_next, own, between=None):
        for h in range(N_HEADS):
            ahead = h + LOGIT_AHEAD
            if ahead < N_HEADS:
                stage_logits(ahead, j)
            else:
                stage_logits(ahead - N_HEADS, j_next)
            if between is not None:
                between(h)
            lg = lg_ref[h % LOGIT_SLOTS]
            if own:
                kk = lax.broadcasted_iota(jnp.int32, lg.shape, 0)
                qq = lax.broadcasted_iota(jnp.int32, lg.shape, 1)
                lg3 = tiles(jnp.where(kk <= qq, lg, NEG))
                m = _all_sublanes(jnp.max(lg3, axis=0), jnp.maximum)
                p3 = jnp.exp2(lg3 - m[None])
                m_ref[h] = m
                l_ref[h] = _all_sublanes(jnp.sum(p3, axis=0), jnp.add)
                acc_ref[head_rows(h), :] = weighted_values(h, j, p3).reshape(HEAD_DIM, tq)
            else:
                lg3 = tiles(lg)
                take = take_ref[h, j] > 0.0
                m_old = m_ref[h]
                l_old = l_ref[h]
                acc_old = tiles(acc_ref[head_rows(h), :])
                m = jnp.maximum(m_old, _all_sublanes(jnp.max(lg3, axis=0), jnp.maximum))
                a = jnp.exp2(m_old - m)
                p3 = jnp.exp2(lg3 - m[None])
                l = a * l_old + _all_sublanes(jnp.sum(p3, axis=0), jnp.add)
                acc3 = a[None] * acc_old + weighted_values(h, j, p3)
                m_ref[h] = jnp.where(take, m, m_old)
                l_ref[h] = jnp.where(take, l, l_old)
                acc_ref[head_rows(h), :] = jnp.where(take[None], acc3, acc_old).reshape(HEAD_DIM, tq)

    last_past = jnp.maximum(i - 1, 0)
    for h in range(LOGIT_AHEAD):
        stage_logits(h, i)

    for s in range(per_step):
        v_pages = [vbuf_ref.at[slot, s * n_pages + p] for p in range(n_pages)]
        os_ref[s] = _sample_finish(_sample_queries(qs_ref[s]), slg_ref[s], kn_ref[s], vn_ref[s], v_pages).astype(BF16)

    next_q = [_sample_queries(qn_ref[s]) for s in range(per_step)]
    jobs = [(s, p) for s in range(per_step) for p in range(n_pages)]
    per_head = -(-len(jobs) // N_HEADS)
    page_logits = {}

    def sample_page_logits(h):
        for s, p in jobs[h * per_head:(h + 1) * per_head]:
            page_logits[s, p] = _sample_page_logits(next_q[s], kbuf_ref.at[next_keys, s * n_pages + p])

    block_pass(i, 0, own=True, between=sample_page_logits)
    for s in range(per_step):
        slg_ref[s] = jnp.concatenate([page_logits[s, p] for p in range(n_pages)], axis=-1)

    def two_past_blocks(jj, carry):
        j = 2 * jj
        block_pass(j, j + 1, own=False)
        block_pass(j + 1, jnp.minimum(j + 2, last_past), own=False)
        return carry

    lax.fori_loop(0, i // 2, two_past_blocks, 0)

    @pl.when(i % 2 == 1)
    def _():
        block_pass(last_past, last_past, own=False)

    for h in range(N_HEADS):
        rows = slice(h * HEAD_DIM, (h + 1) * HEAD_DIM)
        acc_ref[rows, :] = (tiles(acc_ref[rows, :]) / l_ref[h][None]).reshape(HEAD_DIM, tq)
    o_ref[0] = acc_ref[...].T.astype(BF16)


def _moba(qtb, kb, vtb, kmean, page_table, q_s, k_s, v_s, cache_kt, cache_vt):
    b, w, s = qtb.shape
    nq = s // MOBA_BLOCK
    bs, n_pages = page_table.shape
    page = cache_kt.shape[2]
    steps = b * nq
    assert bs % steps == 0
    per_step = bs // steps
    step_seq = lambda bi, i, pt: (bi * nq + i, 0, 0)
    tok = pl.BlockSpec((per_step, 1, w), step_seq)
    tok_next = pl.BlockSpec((per_step, 1, w), lambda bi, i, pt: (jnp.minimum(bi * nq + i + 1, steps - 1), 0, 0))
    in_hbm = pl.BlockSpec(memory_space=pl.ANY)
    page_ring = lambda parts: pltpu.VMEM((parts, per_step * n_pages, w, page), cache_kt.dtype)
    grid_spec = pltpu.PrefetchScalarGridSpec(
        num_scalar_prefetch=1,
        grid=(b, nq),
        in_specs=[
            pl.BlockSpec((1, w, MOBA_BLOCK), lambda bi, i, pt: (bi, 0, _query_block(i, nq))),
            pl.BlockSpec((1, s, w), lambda bi, i, pt: (bi, 0, 0)),
            pl.BlockSpec((1, nq, w, MOBA_BLOCK), lambda bi, i, pt: (bi, 0, 0, 0)),
            pl.BlockSpec((1, nq, w), lambda bi, i, pt: (bi, 0, 0)),
            tok, tok_next, tok, tok, in_hbm, in_hbm,
        ],
        out_specs=[pl.BlockSpec((1, MOBA_BLOCK, w), lambda bi, i, pt: (bi, _query_block(i, nq), 0)), tok],
        scratch_shapes=[pltpu.VMEM((w, MOBA_BLOCK), F32),
                        pltpu.VMEM((N_HEADS, SUBLANES, MOBA_BLOCK), F32),
                        pltpu.VMEM((N_HEADS, SUBLANES, MOBA_BLOCK), F32),
                        pltpu.VMEM((N_HEADS, nq, SUBLANES, MOBA_BLOCK), F32),
                        pltpu.VMEM((LOGIT_SLOTS, MOBA_BLOCK, MOBA_BLOCK), F32),
                        page_ring(KEY_RING), page_ring(VALUE_RING),
                        pltpu.VMEM((per_step, N_HEADS, n_pages * page), F32),
                        pltpu.SemaphoreType.DMA((KEY_RING,)), pltpu.SemaphoreType.DMA((VALUE_RING,))],
    )
    return pl.pallas_call(
        functools.partial(_moba_kernel, n_pages=n_pages, per_step=per_step),
        grid_spec=grid_spec,
        out_shape=[jax.ShapeDtypeStruct((b, s, w), BF16), jax.ShapeDtypeStruct((bs, 1, w), BF16)],
        compiler_params=_params("arbitrary", "arbitrary"),
        name="moba",
    )(page_table, qtb, kb, vtb, kmean, q_s, q_s, k_s, v_s, cache_kt, cache_vt)


def _mem_kv_kernel(m_ref, g_ref, wk_ref, wv_ref, mk_ref, mv_ref, mkb_ref, mvb_ref):
    h = _rms(m_ref[...], g_ref[...]).astype(BF16)
    mk = jnp.dot(h, wk_ref[...], preferred_element_type=F32)
    mv = jnp.dot(h, wv_ref[...], preferred_element_type=F32)
    mk_ref[...] = mk
    mv_ref[...] = mv
    mkb_ref[...] = mk.astype(BF16)
    mvb_ref[...] = mv.astype(BF16)


def _mem_kv(mem, g, w_ck, w_cv):
    rows, d = mem.shape
    tm = min(1024, rows)
    row = lambda i: (i, 0)
    return pl.pallas_call(
        _mem_kv_kernel,
        grid=(rows // tm,),
        in_specs=[pl.BlockSpec((tm, d), row), _resident((1, d)), _resident(w_ck.shape), _resident(w_cv.shape)],
        out_specs=[pl.BlockSpec((tm, d), row)] * 4,
        out_shape=[jax.ShapeDtypeStruct((rows, d), F32)] * 2 + [jax.ShapeDtypeStruct((rows, d), BF16)] * 2,
        compiler_params=_params("arbitrary"),
        name="mem_kv",
    )(mem, g, w_ck, w_cv)


MIX_CHAINS = 2
MIX_CHAIN_ROWS = 512


def _mix_out_kernel(x_ref, po_ref, at_ref, wo_ref, g_ref, wq_ref, *rest):
    with_memory = len(rest) == 4
    if with_memory:
        mk_ref, mv_ref, x1_ref, out_ref = rest
    else:
        x1_ref, out_ref = rest
    tm = x_ref.shape[0]
    chains = MIX_CHAINS if with_memory else 1
    rows_of = lambda c: slice(c * (tm // chains), (c + 1) * (tm // chains))

    def residual(c):
        rows = rows_of(c)
        cat = jnp.concatenate([po_ref[rows, :], at_ref[rows, :]], axis=-1)
        x1 = x_ref[rows, :] + jnp.dot(cat, wo_ref[...], preferred_element_type=F32)
        x1_ref[rows, :] = x1
        return _rms(x1, g_ref[...]).astype(BF16)

    def queries(hq):
        return (jnp.dot(hq, wq_ref[...], preferred_element_type=F32) * (MEM_HEAD_DIM ** -0.5)).astype(BF16)

    def attend(c, q):
        for h in range(MEM_HEADS):
            sl = slice(h * MEM_HEAD_DIM, (h + 1) * MEM_HEAD_DIM)
            lg = lax.dot_general(q[:, sl], mk_ref[0, :, sl], NT_DIMS, preferred_element_type=F32)
            p = jnp.exp(lg - jnp.max(lg, axis=-1, keepdims=True))
            l = jnp.sum(p, axis=-1, keepdims=True)
            o = jnp.dot(p.astype(BF16), mv_ref[0, :, sl], preferred_element_type=F32)
            out_ref[rows_of(c), sl] = (o / l).astype(BF16)

    if not with_memory:
        out_ref[...] = queries(residual(0))
        return
    q = queries(residual(0))
    for c in range(chains):
        hq_next = residual(c + 1) if c + 1 < chains else None
        attend(c, q)
        if hq_next is not None:
            q = queries(hq_next)


def _mix_out(x, po, at, w_out, g_cross, w_cq, memory=None):
    rows, d = x.shape
    tm = min(MIX_CHAINS * MIX_CHAIN_ROWS if memory is not None else MIX_CHAIN_ROWS, rows)
    row = lambda i: (i, 0)
    in_specs = [pl.BlockSpec((tm, d), row), pl.BlockSpec((tm, POOL_WIDTH), row), pl.BlockSpec((tm, ATTN_WIDTH), row),
                _resident(w_out.shape), _resident((1, d)), _resident(w_cq.shape)]
    args = [x, po, at, w_out, g_cross, w_cq]
    if memory is not None:
        mk, mv = memory
        tiles_per_batch = rows // mk.shape[0] // tm
        assert tiles_per_batch * tm * mk.shape[0] == rows and tm % MIX_CHAINS == 0
        of_batch = pl.BlockSpec((1,) + mk.shape[1:], lambda i: (i // tiles_per_batch, 0, 0))
        in_specs += [of_batch, of_batch]
        args += [mk, mv]
    return pl.pallas_call(
        _mix_out_kernel,
        grid=(rows // tm,),
        in_specs=in_specs,
        out_specs=[pl.BlockSpec((tm, d), row), pl.BlockSpec((tm, d), row)],
        out_shape=[jax.ShapeDtypeStruct((rows, d), F32), jax.ShapeDtypeStruct((rows, d), BF16)],
        compiler_params=_params("arbitrary"),
        name="mix_out",
    )(*args)


MEM_LANES = LANES
MEM_PARTS = MEM_HEAD_DIM // MEM_LANES
MEM_ROWS = MEM_PARTS * MEM_HEADS


def _cross_sample_products(q, mk_ref):
    return lax.dot_general(q, mk_ref[...].astype(BF16), NT_DIMS, preferred_element_type=F32)


def _cross_sample_attend(r, mv_ref):
    n = r.shape[1]
    row = lax.broadcasted_iota(jnp.int32, (MEM_ROWS, n), 0)
    lane_row = lax.broadcasted_iota(jnp.int32, (MEM_ROWS, n), 1) % MEM_ROWS
    own = lane_row == row
    valid = (row < MEM_HEADS) & own

    def other_part(t, lane_shift):
        return pltpu.roll(pltpu.roll(t, MEM_HEADS, 0), lane_shift, 1)

    r = jnp.where(own, r, 0.0)
    lg = jnp.where(valid, r + other_part(r, n - MEM_HEADS), NEG)
    p = jnp.where(valid, jnp.exp(lg - jnp.max(lg, axis=-1, keepdims=True)), 0.0)
    p = p + other_part(p, MEM_HEADS)
    l = jnp.sum(p, axis=-1, keepdims=True)
    return jnp.dot(p.astype(BF16), mv_ref[...].astype(BF16), preferred_element_type=F32) / l


def _stored_rows(qc):
    rows = qc.shape[0]
    return jnp.transpose(qc.reshape(rows, MEM_HEADS, MEM_PARTS, MEM_LANES), (0, 2, 1, 3)).reshape(rows, MEM_ROWS, MEM_LANES)


def _stored_slots(m):
    b, slots = m.shape[:2]
    m = m.reshape(b, slots, MEM_HEADS, MEM_PARTS, MEM_LANES)
    return jnp.transpose(m, (0, 1, 3, 2, 4)).reshape(b, slots * MEM_ROWS, MEM_LANES)


def _unstored_rows(o):
    rows = o.shape[0]
    return jnp.transpose(o.reshape(rows, MEM_PARTS, MEM_HEADS, MEM_LANES), (0, 2, 1, 3)).reshape(rows, -1)


def _mlp_kernel(x1_ref, o_ref, wco_ref, gm_ref, wu_ref, wd_ref, gf_ref, *rest):
    if len(rest) == 1:
        (y_ref,) = rest
        n_seq = 0
    else:
        qs_ref, mk_ref, mv_ref, y_ref, os_ref = rest
        n_seq = qs_ref.shape[0]
    products = [_cross_sample_products(qs_ref[s], mk_ref.at[s]) for s in range(n_seq)]
    x2 = x1_ref[...] + jnp.dot(o_ref[...], wco_ref[...], preferred_element_type=F32)
    hm = _rms(x2, gm_ref[...]).astype(BF16)
    acc = x2
    for c in range(D_FF // FF_CHUNK):
        if c == 1:
            for s in range(n_seq):
                os_ref[s] = _cross_sample_attend(products[s], mv_ref.at[s]).astype(BF16)
        cs = slice(c * FF_CHUNK, (c + 1) * FF_CHUNK)
        a = jnp.dot(hm, wu_ref[:, cs], preferred_element_type=F32)
        a = jnp.square(jnp.maximum(a, 0.0)).astype(BF16)
        acc = acc + jnp.dot(a, wd_ref[cs, :], preferred_element_type=F32)
    y_ref[...] = _rms(acc, gf_ref[...])


def _mlp(x1, o, w_co, g_mlp, w_up, w_down, g_final, sample_cross=None):
    rows, d = x1.shape
    tm = min(512, rows)
    steps = rows // tm
    row = lambda i: (i, 0)
    in_specs = [pl.BlockSpec((tm, d), row), pl.BlockSpec((tm, d), row), _resident(w_co.shape),
                _resident((1, d)), _resident(w_up.shape), _resident(w_down.shape), _resident((1, d))]
    args = [x1, o, w_co, g_mlp, w_up, w_down, g_final]
    out_specs = [pl.BlockSpec((tm, d), row)]
    out_shape = [jax.ShapeDtypeStruct((rows, d), F32)]
    if sample_cross is not None:
        qc, mem_k, mem_v = sample_cross
        b, slots = mem_k.shape[:2]
        assert b % steps == 0
        seq = lambda n: pl.BlockSpec((b // steps, n, MEM_LANES), lambda i: (i, 0, 0))
        in_specs += [seq(MEM_ROWS), seq(slots * MEM_ROWS), seq(slots * MEM_ROWS)]
        args += [_stored_rows(qc), _stored_slots(mem_k), _stored_slots(mem_v)]
        out_specs.append(seq(MEM_ROWS))
        out_shape.append(jax.ShapeDtypeStruct((b, MEM_ROWS, MEM_LANES), BF16))
    outs = pl.pallas_call(
        _mlp_kernel,
        grid=(steps,),
        in_specs=in_specs,
        out_specs=out_specs,
        out_shape=out_shape,
        compiler_params=_params("arbitrary"),
        name="mlp",
    )(*args)
    if sample_cross is None:
        return outs[0]
    return outs[0], _unstored_rows(outs[1])


def _proj_sample_kernel(x_ref, g_ref, w_ref, cos_ref, sin_ref, st_ref, wp_ref, sp_ref,
                        q_ref, k_ref, v_ref, kt_ref, vt_ref, u_ref, po_ref, *, n_prev):
    h = _rms(x_ref[...], g_ref[...]).astype(BF16)
    proj = jnp.dot(h, w_ref[...], preferred_element_type=F32)
    u = proj[:, :POOL_WIDTH]
    cos = cos_ref[...]
    sin = sin_ref[...]
    q_ref[...] = _rope(proj[:, POOL_WIDTH:POOL_WIDTH + ATTN_WIDTH], cos, sin) * (HEAD_DIM ** -0.5)
    k = _rope(proj[:, POOL_WIDTH + ATTN_WIDTH:POOL_WIDTH + 2 * ATTN_WIDTH], cos, sin)
    v = proj[:, POOL_WIDTH + 2 * ATTN_WIDTH:]
    k_ref[...] = k
    v_ref[...] = v
    kt_ref[...] = k.T
    vt_ref[...] = v.T
    u_ref[...] = u
    for g, w in enumerate(POOL_WINDOWS):
        cols = slice(g * POOL_GC, (g + 1) * POOL_GC)
        ug = u[:, cols]
        wsum = ug
        for j in range(1, w):
            wsum = wsum + st_ref[POOL_STATE - j, :, cols]
        d = wsum / float(min(w, n_prev + 1)) - ug
        mixed = jnp.dot(d.astype(BF16), wp_ref[g], preferred_element_type=F32)
        po_ref[:, cols] = (mixed * sp_ref[:, cols]).astype(BF16)


def _proj_sample(x, g, w_in, cos, sin, state_t, w_pool, s_pool, n_prev):
    rows, d = x.shape
    wide = jax.ShapeDtypeStruct((rows, ATTN_WIDTH), F32)
    tall = jax.ShapeDtypeStruct((ATTN_WIDTH, rows), F32)
    full = lambda a: pl.BlockSpec(a.shape, lambda i, nd=len(a.shape): (0,) * nd)
    args = (x, g, w_in, cos, sin, state_t, w_pool, s_pool)
    outs = [wide, wide, wide, tall, tall, wide, jax.ShapeDtypeStruct((rows, POOL_WIDTH), BF16)]
    return pl.pallas_call(
        functools.partial(_proj_sample_kernel, n_prev=n_prev),
        grid=(1,),
        in_specs=[full(a) for a in args],
        out_specs=[full(o) for o in outs],
        out_shape=outs,
        compiler_params=_params("arbitrary"),
        name="proj_sample",
    )(*args)


def _own_lanes():
    head_of_lane = lax.broadcasted_iota(jnp.int32, (N_HEADS, ATTN_WIDTH), 1) // HEAD_DIM
    head = lax.broadcasted_iota(jnp.int32, (N_HEADS, ATTN_WIDTH), 0)
    return head_of_lane == head


def _sample_queries(q):
    return jnp.where(_own_lanes(), q, 0.0)


def _sample_page_logits(q_heads, k_page):
    return jnp.dot(q_heads.astype(BF16), k_page[...].astype(BF16), preferred_element_type=F32)


def _sample_finish(q_heads, lg, k_new, v_new, v_pages):
    page = v_pages[0].shape[1]
    past = lg.shape[1]
    nb = past // MOBA_BLOCK
    mine = _own_lanes()

    s = jnp.concatenate(
        [jnp.sum(lg[:, j * MOBA_BLOCK:(j + 1) * MOBA_BLOCK], axis=-1, keepdims=True) for j in range(nb)], axis=-1)
    blk_id = lax.broadcasted_iota(jnp.int32, (N_HEADS, nb), 1)
    sel = _topk_member(s, blk_id >= 0, blk_id, axis=1)
    key_blk = lax.broadcasted_iota(jnp.int32, (N_HEADS, past), 1) // MOBA_BLOCK
    chosen = jnp.zeros((N_HEADS, past), jnp.bool_)
    for j in range(nb):
        chosen = chosen | ((key_blk == j) & sel[:, j:j + 1])
    lg = jnp.where(chosen, lg, NEG)

    lg_new = jnp.sum(q_heads * k_new, axis=-1, keepdims=True)
    m = jnp.maximum(jnp.max(lg, axis=-1, keepdims=True), lg_new)
    p = jnp.exp(lg - m)
    p_new = jnp.exp(lg_new - m)
    l = jnp.sum(p, axis=-1, keepdims=True) + p_new

    lanes = page
    sub = lax.broadcasted_iota(jnp.int32, (HEAD_DIM, lanes), 0)
    lane = lax.broadcasted_iota(jnp.int32, (HEAD_DIM, lanes), 1)
    tiles = []
    for first in range(0, N_HEADS, lanes // HEAD_DIM):
        row = jnp.zeros((1, lanes), F32)
        for h in range(first, first + lanes // HEAD_DIM):
            acc = jnp.zeros((HEAD_DIM, lanes), F32)
            for n, vp in enumerate(v_pages):
                w = jnp.broadcast_to(p[h:h + 1, n * page:(n + 1) * page], (HEAD_DIM, page))
                acc = acc + vp[h * HEAD_DIM:(h + 1) * HEAD_DIM, :] * w
            total = jnp.sum(acc, axis=1, keepdims=True)
            on_diag = lane == sub + (h - first) * HEAD_DIM
            row = row + jnp.sum(jnp.where(on_diag, total, 0.0), axis=0, keepdims=True)
        tiles.append(row)
    pv = jnp.concatenate(tiles, axis=-1)
    per_lane = lambda t: jnp.sum(jnp.where(mine, t, 0.0), axis=0, keepdims=True)
    return (pv + per_lane(p_new) * v_new) / per_lane(l)


def _rope_tables(pos, heads):
    half = HEAD_DIM // 2
    inv_freq = ROPE_THETA ** (-jnp.arange(half, dtype=F32) / half)
    ang = pos.astype(F32)[:, None] * inv_freq[None, :]
    cos = jnp.cos(ang)
    sin = jnp.sin(ang)
    cos_full = jnp.tile(jnp.concatenate([cos, cos], axis=-1), (1, heads))
    sin_signed = jnp.tile(jnp.concatenate([-sin, sin], axis=-1), (1, heads))
    return cos_full, sin_signed


def kernel(x_prompt, x_sample, cache_k, cache_v, cache_mem_k, cache_mem_v, state_pool, page_table, mem_prompt,
           g_mix, w_in, w_pool, s_pool, w_out, g_cross, g_mem, w_cq, w_ck, w_cv, w_co, g_mlp, w_up, w_down,
           g_final):
    depth = w_in.shape[0]
    assert depth == 1, "one decoder layer"
    bp, seq, d = x_prompt.shape
    bs, dec_seq, _ = x_sample.shape
    assert dec_seq == 1 and seq % MOBA_BLOCK == 0
    n_phys, page = cache_k.shape[1], cache_k.shape[2]
    n_pages = page_table.shape[1]
    past_len = n_pages * page
    assert past_len % MOBA_BLOCK == 0 and past_len >= POOL_STATE
    mem_len = mem_prompt.shape[1]

    l = 0
    row = lambda a: a.reshape(1, -1)
    lp = dict(w_out=w_out[l].astype(BF16), g_cross=row(g_cross[l]), w_cq=w_cq[l].astype(BF16),
              w_co=w_co[l].astype(BF16), g_mlp=row(g_mlp[l]), w_up=w_up[l].astype(BF16),
              w_down=w_down[l].astype(BF16), g_final=row(g_final))
    w_in_b = w_in[l].astype(BF16)
    w_pool_b = w_pool[l].astype(BF16)
    g_mix_r = row(g_mix[l])
    s_pool_r = row(s_pool[l])

    lane_heads = LANES // HEAD_DIM
    cos_p, sin_p = _rope_tables(jnp.arange(seq), lane_heads)
    qtb, kt_p, vt_p, kb, vtb, kmean, po_p, last_p = _proj_prompt(
        x_prompt, g_mix_r, w_in_b, cos_p, sin_p, cos_p[:, :HEAD_DIM].T, sin_p[:, :HEAD_DIM].T, w_pool_b, s_pool_r)
    cos_s, sin_s = _rope_tables(jnp.full((1,), past_len), N_HEADS)
    state_t = jnp.transpose(state_pool[l], (1, 0, 2))
    q_s, k_s, v_s, kt_s, vt_s, u_s, po_s = _proj_sample(x_sample.reshape(bs, d), g_mix_r, w_in_b, cos_s, sin_s,
                                                        state_t, w_pool_b, s_pool_r, past_len)

    tok = lambda a: a.reshape(bs, 1, -1)
    pages_t = lambda c: jnp.transpose(c, (0, 2, 3, 1)).reshape(n_phys, ATTN_WIDTH, page)
    at_p, at_s = _moba(qtb, kb, vtb, kmean.reshape(bp, seq // MOBA_BLOCK, ATTN_WIDTH), page_table, tok(q_s),
                       tok(k_s), tok(v_s), pages_t(cache_k[l]), pages_t(cache_v[l]))

    mk_p, mv_p, mkb, mvb = _mem_kv(mem_prompt.reshape(bp * mem_len, d), row(g_mem[l]), w_ck[l].astype(BF16),
                                   w_cv[l].astype(BF16))
    flat = lambda a: a.reshape(-1, a.shape[-1])
    mix = lambda x, po, at, memory=None: _mix_out(flat(x), flat(po), flat(at), lp["w_out"], lp["g_cross"],
                                                  lp["w_cq"], memory)
    tail = (lp["w_co"], lp["g_mlp"], lp["w_up"], lp["w_down"], lp["g_final"])
    x1_p, o_p = mix(x_prompt, po_p, at_p, (mkb.reshape(bp, mem_len, d), mvb.reshape(bp, mem_len, d)))
    x1_s, qc_s = mix(x_sample, po_s, at_s)
    y_prompt, o_s = _mlp(x1_p, o_p, *tail, sample_cross=(qc_s, cache_mem_k[l], cache_mem_v[l]))
    y_prompt = y_prompt.reshape(bp, seq, d)
    y_sample = _mlp(x1_s, o_s, *tail).reshape(bs, 1, d)

    def heads(t, b_):
        return jnp.transpose(t.reshape(b_, N_HEADS, HEAD_DIM, -1), (0, 3, 1, 2))[None]

    mem_heads = lambda a: a.reshape(1, bp, mem_len, MEM_HEADS, MEM_HEAD_DIM)
    pool_sample = jnp.transpose(jnp.concatenate([state_t[1:], u_s[None]], axis=0), (1, 0, 2))[None]
    return (y_prompt, y_sample, heads(kt_p, bp), heads(vt_p, bp), last_p[:, 1:][None], mem_heads(mk_p),
            mem_heads(mv_p), heads(kt_s.T.reshape(bs, ATTN_WIDTH, 1), bs), heads(vt_s.T.reshape(bs, ATTN_WIDTH, 1), bs),
            pool_sample)
```
